```python
import jax, jax.numpy as jnp
from jax import lax
import numpy as np

D_MODEL = 1024
BATCH = 16
SEQ = 256
DEPTH = 1
DEC_BATCH = 2
DEC_SEQ = 1024
PAST_LEN = 256

GRID_W = 64
HEAD_DIM = 64
N_Q_HEADS = 8
N_KV_HEADS = 2
GROUP = N_Q_HEADS // N_KV_HEADS
ATTN_WIDTH = N_Q_HEADS * HEAD_DIM
KV_WIDTH = N_KV_HEADS * HEAD_DIM
RG_WIDTH = D_MODEL - ATTN_WIDTH
RG_BLOCKS = 8
RG_BLOCK = RG_WIDTH // RG_BLOCKS
CONV_W = 4
C_LRU = 8.0
N_EXPERTS = 16
CAP_FACTOR = 2
D_EXPERT = 1024
Q_BLOCK = 128
ROPE_THETA = 10000.0
NORM_EPS = 1e-6
ALPHA = (2.0 * DEPTH) ** 0.25
BETA = (8.0 * DEPTH) ** -0.25
IN_WIDTH = ATTN_WIDTH + 2 * KV_WIDTH + 2 * RG_WIDTH

kernel_name = "hybrid_gqa_rglru_ec_moe_diffusion_step"


def layer_norm_plain(x):
    xf = x.astype(jnp.float32)
    mu = jnp.mean(xf, -1, keepdims=True)
    var = jnp.mean(jnp.square(xf - mu), -1, keepdims=True)
    return (xf - mu) * lax.rsqrt(var + NORM_EPS)


def layer_norm(x, g, b):
    y = layer_norm_plain(x) * g.astype(jnp.float32) + b.astype(jnp.float32)
    return y.astype(x.dtype)


def rms_norm(x, g):
    xf = x.astype(jnp.float32)
    y = xf * lax.rsqrt(jnp.mean(jnp.square(xf), -1, keepdims=True) + NORM_EPS)
    return (y * g.astype(jnp.float32)).astype(x.dtype)


def modulate(x, shift, scale):
    return (layer_norm_plain(x) * (1.0 + scale.astype(jnp.float32)) + shift.astype(jnp.float32)).astype(x.dtype)


def rope_half(x, pos):
    nf = x.shape[-1] // 2
    inv = ROPE_THETA ** (-jnp.arange(nf, dtype=jnp.float32) / nf)
    ang = pos.astype(jnp.float32)[:, None] * inv[None, :]
    cos = jnp.cos(ang)[None, :, None, :]
    sin = jnp.sin(ang)[None, :, None, :]
    x1, x2 = x[..., :nf], x[..., nf:]
    return jnp.concatenate([x1 * cos - x2 * sin, x1 * sin + x2 * cos], -1)


def axial_rope(x):
    n = x.shape[1]
    rows = n // GRID_W
    row = jnp.broadcast_to(jnp.arange(rows)[:, None], (rows, GRID_W)).reshape(n)
    col = jnp.broadcast_to(jnp.arange(GRID_W)[None, :], (rows, GRID_W)).reshape(n)
    xf = x.astype(jnp.float32)
    half = HEAD_DIM // 2
    y = jnp.concatenate([rope_half(xf[..., :half], row), rope_half(xf[..., half:], col)], -1)
    return y.astype(x.dtype)


def block_attention(q, k, v):
    b, n = q.shape[:2]
    nblk = n // Q_BLOCK
    qb = jnp.moveaxis(q.reshape(b, nblk, Q_BLOCK, N_KV_HEADS, GROUP, HEAD_DIM), 1, 0)
    scale = HEAD_DIM ** -0.5

    def one_block(qblk):
        s = jnp.einsum('bqhgd,bkhd->bhgqk', qblk, k, preferred_element_type=jnp.float32) * scale
        p = jax.nn.softmax(s, axis=-1).astype(v.dtype)
        return jnp.einsum('bhgqk,bkhd->bqhgd', p, v)

    o = lax.map(one_block, qb)
    return jnp.moveaxis(o, 0, 1).reshape(b, n, ATTN_WIDTH)


def centred_conv(x, w, b):
    n = x.shape[1]
    left = CONV_W // 2
    xp = jnp.pad(x, ((0, 0), (left, CONV_W - 1 - left), (0, 0)))
    return sum(xp[:, j:j + n] * w[j] for j in range(CONV_W)) + b


def block_diag(x, w, b):
    xb = x.reshape(x.shape[:-1] + (RG_BLOCKS, RG_BLOCK))
    return jnp.einsum('bnhi,hij->bnhj', xb, w).reshape(x.shape) + b


def rglru_coeffs(x, w_a, b_a, w_x, b_x, lam):
    r = jax.nn.sigmoid(block_diag(x, w_a, b_a).astype(jnp.float32))
    i = jax.nn.sigmoid(block_diag(x, w_x, b_x).astype(jnp.float32))
    log_a = -C_LRU * r * jax.nn.softplus(-lam.astype(jnp.float32))
    a = jnp.exp(log_a)
    bx = jnp.sqrt(-jnp.expm1(2.0 * log_a)) * (i * x.astype(jnp.float32))
    return a, bx


def linear_scan(a, bx, h0, reverse):
    def combine(l, r):
        return l[0] * r[0], r[0] * l[1] + r[1]
    acum, bcum = lax.associative_scan(combine, (a, bx), axis=1, reverse=reverse)
    return acum * h0.astype(jnp.float32)[:, None, :] + bcum


def token_mixer(h, w_in, q_g, k_g, conv_w, conv_b, w_a, b_a, w_x, b_x, lam, w_out, ctx):
    B, N, _ = h.shape
    z = h @ w_in
    q, k, v, xr, gr = jnp.split(z, [ATTN_WIDTH, ATTN_WIDTH + KV_WIDTH, ATTN_WIDTH + 2 * KV_WIDTH,
                                     ATTN_WIDTH + 2 * KV_WIDTH + RG_WIDTH], axis=-1)
    q = rms_norm(q.reshape(B, N, N_Q_HEADS, HEAD_DIM), q_g)
    k = rms_norm(k.reshape(B, N, N_KV_HEADS, HEAD_DIM), k_g)
    v = v.reshape(B, N, N_KV_HEADS, HEAD_DIM)
    xc = centred_conv(xr, conv_w, conv_b)
    if ctx is None:
        keys, vals = k, v
        h0 = jnp.zeros((B, 2, RG_WIDTH), h.dtype)
    else:
        ctx_k, ctx_v, ctx_h = ctx
        q = axial_rope(q)
        keys = jnp.concatenate([axial_rope(k), ctx_k.astype(k.dtype)], axis=1)
        vals = jnp.concatenate([v, ctx_v.astype(v.dtype)], axis=1)
        h0 = ctx_h
    attn = block_attention(q.reshape(B, N, N_KV_HEADS, GROUP, HEAD_DIM), keys, vals)
    a_f, bx_f = rglru_coeffs(xc, w_a[0], b_a[0], w_x[0], b_x[0], lam[0])
    a_b, bx_b = rglru_coeffs(xc, w_a[1], b_a[1], w_x[1], b_x[1], lam[1])
    h_f = linear_scan(a_f, bx_f, h0[:, 0], reverse=False)
    h_b = linear_scan(a_b, bx_b, h0[:, 1], reverse=True)
    rg = (h_f + h_b).astype(h.dtype) * jax.nn.gelu(gr)
    out = jnp.concatenate([attn, rg], axis=-1) @ w_out
    if ctx is None:
        final = jnp.stack([h_f[:, -1], h_b[:, 0]], axis=1).astype(h.dtype)
        return out, (k, v, final)
    return out, None


def expert_choice_ffn(h, w_router, w_gate_up, w_down):
    B, N, D = h.shape
    cap = CAP_FACTOR * N // N_EXPERTS
    aff = jax.nn.softmax((h @ w_router).astype(jnp.float32), axis=-1)
    g, idx = lax.top_k(jnp.swapaxes(aff, 1, 2), cap)
    xs = jax.vmap(lambda hb, ib: hb[ib])(h, idx)
    gu = jnp.einsum('becd,edf->becf', xs, w_gate_up)
    gate, up = jnp.split(gu, 2, axis=-1)
    y = jnp.einsum('becf,efd->becd', jax.nn.silu(gate) * up, w_down) * g[..., None].astype(h.dtype)
    return jax.vmap(lambda ib, yb: jnp.zeros((N, D), yb.dtype).at[ib.reshape(-1)].add(yb.reshape(-1, D)))(idx, y)


def trunk_layer(x, mod, w_in, q_g, k_g, conv_w, conv_b, w_a, b_a, w_x, b_x, lam, w_out,
                ln1_g, ln1_b, w_router, w_gate_up, w_down, ln2_g, ln2_b, ctx):
    shift1, scale1, gate1, shift2, scale2, gate2 = jnp.split(mod, 6, axis=-1)
    mix, new_ctx = token_mixer(modulate(x, shift1, scale1), w_in, q_g, k_g, conv_w, conv_b,
                               w_a, b_a, w_x, b_x, lam, w_out, ctx)
    x = layer_norm(ALPHA * x + gate1 * mix, ln1_g, ln1_b)
    ff = expert_choice_ffn(modulate(x, shift2, scale2), w_router, w_gate_up, w_down)
    x = layer_norm(ALPHA * x + gate2 * ff, ln2_g, ln2_b)
    return x, new_ctx


def setup_inputs(seed: int = 0) -> dict:
    key = jax.random.key(seed)
    ks = jax.random.split(key, 32)
    nrm = lambda k, s, sc: jax.random.normal(k, s, jnp.float32) * sc
    D = D_MODEL
    u = jax.random.uniform(ks[15], (DEPTH, 2, RG_WIDTH), jnp.float32, 0.9, 0.999)
    s = u ** (1.0 / C_LRU)
    rg_lambda = jnp.log(s) - jnp.log1p(-s)
    return {
        'x_prompt': nrm(ks[0], (BATCH, SEQ, D), 1.0),
        'x_sample': nrm(ks[1], (DEC_BATCH, DEC_SEQ, D), 1.0),
        'cache_k': nrm(ks[2], (DEC_BATCH, DEPTH, PAST_LEN, N_KV_HEADS, HEAD_DIM), 1.0),
        'cache_v': nrm(ks[3], (DEC_BATCH, DEPTH, PAST_LEN, N_KV_HEADS, HEAD_DIM), 1.0),
        'state_h': nrm(ks[4], (DEC_BATCH, DEPTH, 2, RG_WIDTH), 0.5),
        'c': nrm(ks[5], (DEC_BATCH, D), 1.0),
        'c_ctx': nrm(ks[6], (D,), 1.0),
        'w_mod': nrm(ks[7], (DEPTH, D, 6 * D), D ** -0.5),
        'b_mod': nrm(ks[8], (DEPTH, 6 * D), 0.02),
        'w_in': nrm(ks[9], (DEPTH, D, IN_WIDTH), D ** -0.5),
        'q_norm_g': 1.0 + nrm(ks[10], (DEPTH, HEAD_DIM), 0.02),
        'k_norm_g': 1.0 + nrm(ks[11], (DEPTH, HEAD_DIM), 0.02),
        'conv_w': nrm(ks[12], (DEPTH, CONV_W, RG_WIDTH), CONV_W ** -0.5),
        'conv_b': nrm(ks[13], (DEPTH, RG_WIDTH), 0.02),
        'w_rg_a': nrm(ks[14], (DEPTH, 2, RG_BLOCKS, RG_BLOCK, RG_BLOCK), RG_BLOCK ** -0.5),
        'b_rg_a': nrm(ks[16], (DEPTH, 2, RG_WIDTH), 0.02),
        'w_rg_x': nrm(ks[17], (DEPTH, 2, RG_BLOCKS, RG_BLOCK, RG_BLOCK), RG_BLOCK ** -0.5),
        'b_rg_x': nrm(ks[18], (DEPTH, 2, RG_WIDTH), 0.02),
        'rg_lambda': rg_lambda,
        'w_out': nrm(ks[19], (DEPTH, D, D), BETA * D ** -0.5),
        'ln1_g': 1.0 + nrm(ks[20], (DEPTH, D), 0.02),
        'ln1_b': nrm(ks[21], (DEPTH, D), 0.02),
        'w_router': nrm(ks[22], (DEPTH, D, N_EXPERTS), D ** -0.5),
        'w_gate_up': nrm(ks[23], (DEPTH, N_EXPERTS, D, 2 * D_EXPERT), D ** -0.5),
        'w_down': nrm(ks[24], (DEPTH, N_EXPERTS, D_EXPERT, D), BETA * D_EXPERT ** -0.5),
        'ln2_g': 1.0 + nrm(ks[25], (DEPTH, D), 0.02),
        'ln2_b': nrm(ks[26], (DEPTH, D), 0.02),
    }


def reference(x_prompt, x_sample, cache_k, cache_v, state_h, c, c_ctx, w_mod, b_mod, w_in,
              q_norm_g, k_norm_g, conv_w, conv_b, w_rg_a, b_rg_a, w_rg_x, b_rg_x, rg_lambda,
              w_out, ln1_g, ln1_b, w_router, w_gate_up, w_down, ln2_g, ln2_b):
    xp = x_prompt
    ks_new, vs_new, hs_new = [], [], []
    for l in range(DEPTH):
        mod = (jax.nn.silu(c_ctx) @ w_mod[l] + b_mod[l])[None, None, :]
        xp, (k_l, v_l, h_l) = trunk_layer(
            xp, mod, w_in[l], q_norm_g[l], k_norm_g[l], conv_w[l], conv_b[l], w_rg_a[l], b_rg_a[l],
            w_rg_x[l], b_rg_x[l], rg_lambda[l], w_out[l], ln1_g[l], ln1_b[l], w_router[l],
            w_gate_up[l], w_down[l], ln2_g[l], ln2_b[l], None)
        ks_new.append(k_l)
        vs_new.append(v_l)
        hs_new.append(h_l)
    new_k = jnp.stack(ks_new, axis=1)
    new_v = jnp.stack(vs_new, axis=1)
    new_h = jnp.stack(hs_new, axis=1)

    xs = x_sample
    for l in range(DEPTH):
        mod = (jax.nn.silu(c) @ w_mod[l] + b_mod[l])[:, None, :]
        xs, _ = trunk_layer(
            xs, mod, w_in[l], q_norm_g[l], k_norm_g[l], conv_w[l], conv_b[l], w_rg_a[l], b_rg_a[l],
            w_rg_x[l], b_rg_x[l], rg_lambda[l], w_out[l], ln1_g[l], ln1_b[l], w_router[l],
            w_gate_up[l], w_down[l], ln2_g[l], ln2_b[l],
            (cache_k[:, l], cache_v[:, l], state_h[:, l]))

    return (xp, xs, new_k, new_v, new_h)
```

```python
import functools

import numpy as np
import jax
import jax.numpy as jnp
from jax import lax
from jax.experimental import pallas as pl
from jax.experimental.pallas import tpu as pltpu

F32 = jnp.float32
BF16 = jnp.bfloat16

D_MODEL = 1024
HEAD_DIM = 64
N_Q_HEADS = 8
N_KV_HEADS = 2
GROUP = N_Q_HEADS // N_KV_HEADS
ATTN_WIDTH = N_Q_HEADS * HEAD_DIM
KV_WIDTH = N_KV_HEADS * HEAD_DIM
RG_WIDTH = D_MODEL - ATTN_WIDTH
RG_BLOCK = 64
CONV_W = 4
C_LRU = 8.0
N_EXPERTS = 16
CAP_FACTOR = 2
D_EXPERT = 1024
GRID_W = 64
ROPE_THETA = 10000.0
NORM_EPS = 1e-6
DEPTH = 1
ALPHA = (2.0 * DEPTH) ** 0.25
IN_WIDTH = ATTN_WIDTH + 2 * KV_WIDTH + 2 * RG_WIDTH

LANES = 128
SUBLANES = 8
MXU_DIM = 256
VMEM_LIMIT_BYTES = 56 * 1024 * 1024

MOD_ROWS = SUBLANES
MOD_TILE = 512
ATTN_Q_TILE = 256
COMBINE_TILE = 256


def _params(n_axes=1):
    return pltpu.CompilerParams(dimension_semantics=("arbitrary",) * n_axes,
                                vmem_limit_bytes=VMEM_LIMIT_BYTES)


def _full(shape):
    zeros = (0,) * len(shape)
    return pl.BlockSpec(shape, lambda *_: zeros)


def _ln_plain(x):
    mu = jnp.mean(x, -1, keepdims=True)
    xc = x - mu
    var = jnp.mean(xc * xc, -1, keepdims=True)
    return xc * lax.rsqrt(var + NORM_EPS)


def _dot(a, b):
    return jnp.dot(a, b, preferred_element_type=F32)


def _dot_nt(a, b):
    return lax.dot_general(a, b, (((1,), (1,)), ((), ())), preferred_element_type=F32)


def _dot_tn(a, b):
    return lax.dot_general(a, b, (((0,), (0,)), ((), ())), preferred_element_type=F32)


def _split_bf16(x):
    hi = x.astype(BF16)
    lo = (x - hi.astype(F32)).astype(BF16)
    return hi, lo


def _cast_kernel(a_ref, b_ref, ao_ref, bo_ref):
    ao_ref[...] = a_ref[...].astype(BF16)
    bo_ref[...] = b_ref[...].astype(BF16)


def _cast_weights(w_in, w_out):
    return pl.pallas_call(
        _cast_kernel,
        grid=(1,),
        in_specs=[_full(w_in.shape), _full(w_out.shape)],
        out_specs=[_full(w_in.shape), _full(w_out.shape)],
        out_shape=[jax.ShapeDtypeStruct(w_in.shape, BF16), jax.ShapeDtypeStruct(w_out.shape, BF16)],
        compiler_params=_params(),
        name="cast_weights",
    )(w_in, w_out)


def _mod_kernel(cs_ref, w_ref, b_ref, o_ref):
    cs = cs_ref[...]
    s = cs * jax.nn.sigmoid(cs)
    o_ref[...] = _dot(s.astype(BF16), w_ref[...].astype(BF16)) + b_ref[...]


def _mod_vectors(cs, w_mod, b_mod):
    width = w_mod.shape[1]
    return pl.pallas_call(
        _mod_kernel,
        grid=(width // MOD_TILE,),
        in_specs=[_full(cs.shape),
                  pl.BlockSpec((D_MODEL, MOD_TILE), lambda j: (0, j)),
                  pl.BlockSpec((1, MOD_TILE), lambda j: (0, j))],
        out_specs=pl.BlockSpec((MOD_ROWS, MOD_TILE), lambda j: (0, j)),
        out_shape=jax.ShapeDtypeStruct((MOD_ROWS, width), F32),
        compiler_params=_params(),
        name="mod_vectors",
    )(cs, w_mod, b_mod)


def _head_mean_sq(x, bd):
    hi, lo = _split_bf16(x * x)
    return _dot(hi, bd) + _dot(lo, bd)


def _rope_lanes(x, cos, sin_signed, hi_half):
    partner = jnp.where(hi_half, pltpu.roll(x, 16, axis=1), pltpu.roll(x, LANES - 16, axis=1))
    return x * cos + partner * sin_signed


def _front_kernel(*refs, n, mod_row0, mod_row_step, sample):
    if sample:
        (x_ref, mod_ref, win_ref, bd_ref, qg_ref, kg_ref, cw_ref, cb_ref, wg_ref, bg_ref, lam_ref, h0_ref,
         cos_ref, sin_ref, q_ref, k_ref, v_ref, rg_ref, af_ref, bf_ref, ab_ref, bb_ref) = refs
    else:
        (x_ref, mod_ref, win_ref, bd_ref, qg_ref, kg_ref, cw_ref, cb_ref, wg_ref, bg_ref, lam_ref, h0_ref,
         q_ref, k_ref, v_ref, rg_ref, nk_ref, nv_ref, nh_ref, af_ref, bf_ref, ab_ref, bb_ref) = refs

    row = mod_row0 + mod_row_step * pl.program_id(0)
    m = mod_ref[pl.ds(row, 1), :]
    shift1 = m[:, 0:D_MODEL]
    scale1 = m[:, D_MODEL:2 * D_MODEL]
    h = (_ln_plain(x_ref[...]) * (1.0 + scale1) + shift1).astype(BF16)

    def proj(lo, hi):
        return _dot(h, win_ref[:, lo:hi])

    bd = bd_ref[...]
    q = proj(0, ATTN_WIDTH)
    q = q * lax.rsqrt(_head_mean_sq(q, bd) + NORM_EPS) * qg_ref[...]
    k = proj(ATTN_WIDTH, ATTN_WIDTH + KV_WIDTH)
    k = k * lax.rsqrt(_head_mean_sq(k, bd[:KV_WIDTH, :KV_WIDTH]) + NORM_EPS) * kg_ref[...]
    v = proj(ATTN_WIDTH + KV_WIDTH, ATTN_WIDTH + 2 * KV_WIDTH)

    if sample:
        cos = cos_ref[...]
        sin_signed = sin_ref[...]
        lane = lax.broadcasted_iota(jnp.int32, (n, LANES), 1)
        hi_half = (lane & 16) != 0
        q = jnp.concatenate(
            [_rope_lanes(q[:, j * LANES:(j + 1) * LANES], cos, sin_signed, hi_half)
             for j in range(ATTN_WIDTH // LANES)], axis=1)
        k = _rope_lanes(k, cos, sin_signed, hi_half)
    else:
        nk_ref[...] = k
        nv_ref[...] = v
    q_ref[...] = (q * (HEAD_DIM ** -0.5)).astype(BF16)
    k_ref[...] = k.astype(BF16)
    v_ref[...] = v.astype(BF16)

    xr = proj(ATTN_WIDTH + 2 * KV_WIDTH, ATTN_WIDTH + 2 * KV_WIDTH + RG_WIDTH)
    t_idx = lax.broadcasted_iota(jnp.int32, (n, 1), 0)
    cw = cw_ref[...]
    xc = jnp.where(t_idx >= 2, pltpu.roll(xr, 2, axis=0), 0.0) * cw[0:1, :]
    xc = xc + jnp.where(t_idx >= 1, pltpu.roll(xr, 1, axis=0), 0.0) * cw[1:2, :]
    xc = xc + xr * cw[2:3, :]
    xc = xc + jnp.where(t_idx < n - 1, pltpu.roll(xr, n - 1, axis=0), 0.0) * cw[3:4, :]
    xc = xc + cb_ref[...]

    xcb = xc.astype(BF16)
    halves = [_dot(xcb[:, c * MXU_DIM:(c + 1) * MXU_DIM], wg_ref[c].astype(BF16))
              for c in range(RG_WIDTH // MXU_DIM)]

    def gate_pre(idx):
        return jnp.concatenate([hv[:, idx * MXU_DIM:(idx + 1) * MXU_DIM] for hv in halves], axis=1) \
            + bg_ref[idx:idx + 1, :]

    lam = lam_ref[...]
    neg = -lam
    softplus = jnp.maximum(neg, 0.0) + jnp.log1p(jnp.exp(-jnp.abs(neg)))

    def coeffs(d):
        r = jax.nn.sigmoid(gate_pre(2 * d))
        i = jax.nn.sigmoid(gate_pre(2 * d + 1))
        log_a = (-C_LRU * r) * softplus[d:d + 1, :]
        a = jnp.exp(log_a)
        bx = jnp.sqrt(jnp.tanh(-log_a) * (a * a + 1.0)) * (i * xc)
        return a, bx

    rmod = t_idx & (SUBLANES - 1)
    a, bx = coeffs(0)
    for s in (1, 2, 4):
        ok = rmod >= s
        a_sh = jnp.where(ok, pltpu.roll(a, s, axis=0), 1.0)
        b_sh = jnp.where(ok, pltpu.roll(bx, s, axis=0), 0.0)
        bx = a * b_sh + bx
        a = a * a_sh
    af_ref[...] = a
    bf_ref[...] = bx
    a, bx = coeffs(1)
    for s in (1, 2, 4):
        ok = rmod < SUBLANES - s
        a_sh = jnp.where(ok, pltpu.roll(a, n - s, axis=0), 1.0)
        b_sh = jnp.where(ok, pltpu.roll(bx, n - s, axis=0), 0.0)
        bx = a * b_sh + bx
        a = a * a_sh
    ab_ref[...] = a
    bb_ref[...] = bx

    groups = n // SUBLANES
    h0 = h0_ref[0]

    def step(g, carry):
        hf, hb = carry
        rf = pl.multiple_of(g * SUBLANES, SUBLANES)
        rb = pl.multiple_of((groups - 1 - g) * SUBLANES, SUBLANES)
        new_f = af_ref[pl.ds(rf, SUBLANES), :] * hf + bf_ref[pl.ds(rf, SUBLANES), :]
        new_b = ab_ref[pl.ds(rb, SUBLANES), :] * hb + bb_ref[pl.ds(rb, SUBLANES), :]
        bf_ref[pl.ds(rf, SUBLANES), :] = new_f
        bb_ref[pl.ds(rb, SUBLANES), :] = new_b
        return (jnp.broadcast_to(new_f[SUBLANES - 1:SUBLANES, :], (SUBLANES, RG_WIDTH)),
                jnp.broadcast_to(new_b[0:1, :], (SUBLANES, RG_WIDTH)))

    hf, hb = lax.fori_loop(
        0, groups, step,
        (jnp.broadcast_to(h0[0:1, :], (SUBLANES, RG_WIDTH)), jnp.broadcast_to(h0[1:2, :], (SUBLANES, RG_WIDTH))))
    if not sample:
        nh_ref[0] = jnp.concatenate([hf[0:1, :], hb[0:1, :]], axis=0)

    gr = proj(ATTN_WIDTH + 2 * KV_WIDTH + RG_WIDTH, IN_WIDTH)
    rg_ref[...] = ((bf_ref[...] + bb_ref[...]) * jax.nn.gelu(gr)).astype(BF16)


def _front(x2d, mod, win_bf, consts, h0, rope, *, n, nb, sample):
    bd, qg, kg, cw, cb, wg, bg, lam = consts
    tokens = nb * n
    seq = lambda w: pl.BlockSpec((n, w), lambda b: (b, 0))
    in_specs = [seq(D_MODEL), _full(mod.shape), _full(win_bf.shape), _full(bd.shape), _full(qg.shape),
                _full(kg.shape), _full(cw.shape), _full(cb.shape), _full(wg.shape), _full(bg.shape),
                _full(lam.shape), pl.BlockSpec((1, 2, RG_WIDTH), lambda b: (b, 0, 0))]
    args = [x2d, mod, win_bf, bd, qg, kg, cw, cb, wg, bg, lam, h0]
    out_specs = [seq(ATTN_WIDTH), seq(KV_WIDTH), seq(KV_WIDTH), seq(RG_WIDTH)]
    out_shape = [jax.ShapeDtypeStruct((tokens, ATTN_WIDTH), BF16), jax.ShapeDtypeStruct((tokens, KV_WIDTH), BF16),
                 jax.ShapeDtypeStruct((tokens, KV_WIDTH), BF16), jax.ShapeDtypeStruct((tokens, RG_WIDTH), BF16)]
    if sample:
        in_specs += [_full(rope[0].shape), _full(rope[1].shape)]
        args += list(rope)
    else:
        out_specs += [seq(KV_WIDTH), seq(KV_WIDTH), pl.BlockSpec((1, 2, RG_WIDTH), lambda b: (b, 0, 0))]
        out_shape += [jax.ShapeDtypeStruct((tokens, KV_WIDTH), F32), jax.ShapeDtypeStruct((tokens, KV_WIDTH), F32),
                      jax.ShapeDtypeStruct((nb, 2, RG_WIDTH), F32)]
    return pl.pallas_call(
        functools.partial(_front_kernel, n=n, mod_row0=1 if sample else 0, mod_row_step=1 if sample else 0,
                          sample=sample),
        grid=(nb,),
        in_specs=in_specs,
        out_specs=out_specs,
        out_shape=out_shape,
        scratch_shapes=[pltpu.VMEM((n, RG_WIDTH), F32)] * 4,
        compiler_params=_params(),
        name="front_sample" if sample else "front_prompt",
    )(*args)


def _attn_kernel(*refs, has_cache):
    if has_cache:
        q_ref, k_ref, v_ref, kc_ref, vc_ref, o_ref = refs
        k = jnp.concatenate([k_ref[...], kc_ref[...].astype(BF16)], axis=0)
        v = jnp.concatenate([v_ref[...], vc_ref[...].astype(BF16)], axis=0)
    else:
        q_ref, k_ref, v_ref, o_ref = refs
        k = k_ref[...]
        v = v_ref[...]
    q = q_ref[...]
    outs = []
    for hq in range(N_Q_HEADS):
        kv = hq // GROUP
        qh = q[:, hq * HEAD_DIM:(hq + 1) * HEAD_DIM]
        kh = k[:, kv * HEAD_DIM:(kv + 1) * HEAD_DIM]
        vh = v[:, kv * HEAD_DIM:(kv + 1) * HEAD_DIM]
        s = _dot_nt(qh, kh)
        e = jnp.exp(s - jnp.max(s, axis=-1, keepdims=True))
        p = (e / jnp.sum(e, axis=-1, keepdims=True)).astype(BF16)
        outs.append(_dot(p, vh))
    o_ref[...] = jnp.concatenate(outs, axis=1).astype(BF16)


def _attention(q, k, v, cache, *, n, nb, past):
    tq = min(n, ATTN_Q_TILE)
    in_specs = [pl.BlockSpec((tq, ATTN_WIDTH), lambda b, t: (b * (n // tq) + t, 0)),
                pl.BlockSpec((n, KV_WIDTH), lambda b, t: (b, 0)),
                pl.BlockSpec((n, KV_WIDTH), lambda b, t: (b, 0))]
    args = [q, k, v]
    if cache is not None:
        in_specs += [pl.BlockSpec((past, KV_WIDTH), lambda b, t: (b, 0))] * 2
        args += list(cache)
    return pl.pallas_call(
        functools.partial(_attn_kernel, has_cache=cache is not None),
        grid=(nb, n // tq),
        in_specs=in_specs,
        out_specs=pl.BlockSpec((tq, ATTN_WIDTH), lambda b, t: (b * (n // tq) + t, 0)),
        out_shape=jax.ShapeDtypeStruct((nb * n, ATTN_WIDTH), BF16),
        compiler_params=_params(2),
        name="attn_sample" if cache is not None else "attn_prompt",
    )(*args)


def _post_kernel(attn_ref, rg_ref, x_ref, mod_ref, wout_ref, g1_ref, b1_ref, wrt_ref,
                 x1_ref, h2_ref, afft_ref, *, mod_row0, mod_row_step):
    row = mod_row0 + mod_row_step * pl.program_id(0)
    m = mod_ref[pl.ds(row, 1), :]
    gate1 = m[:, 2 * D_MODEL:3 * D_MODEL]
    shift2 = m[:, 3 * D_MODEL:4 * D_MODEL]
    scale2 = m[:, 4 * D_MODEL:5 * D_MODEL]
    u = jnp.concatenate([attn_ref[...], rg_ref[...]], axis=1)
    mix = _dot(u, wout_ref[...])
    x1 = _ln_plain(ALPHA * x_ref[...] + gate1 * mix) * g1_ref[...] + b1_ref[...]
    x1_ref[...] = x1
    h2 = (_ln_plain(x1) * (1.0 + scale2) + shift2).astype(BF16)
    h2_ref[...] = h2
    logits = _dot_nt(wrt_ref[...].astype(BF16), h2)
    e = jnp.exp(logits - jnp.max(logits, axis=0, keepdims=True))
    afft_ref[...] = e / jnp.sum(e, axis=0, keepdims=True)


def _post(attn, rg, x2d, mod, wout_bf, g1, b1, wrt, *, n, nb, sample):
    tokens = nb * n
    seq = lambda w: pl.BlockSpec((n, w), lambda b: (b, 0))
    return pl.pallas_call(
        functools.partial(_post_kernel, mod_row0=1 if sample else 0, mod_row_step=1 if sample else 0),
        grid=(nb,),
        in_specs=[seq(ATTN_WIDTH), seq(RG_WIDTH), seq(D_MODEL), _full(mod.shape), _full(wout_bf.shape),
                  _full(g1.shape), _full(b1.shape), _full(wrt.shape)],
        out_specs=[seq(D_MODEL), seq(D_MODEL), pl.BlockSpec((N_EXPERTS, n), lambda b: (b, 0))],
        out_shape=[jax.ShapeDtypeStruct((tokens, D_MODEL), F32), jax.ShapeDtypeStruct((tokens, D_MODEL), BF16),
                   jax.ShapeDtypeStruct((nb * N_EXPERTS, n), F32)],
        compiler_params=_params(),
        name="post_sample" if sample else "post_prompt",
    )(attn, rg, x2d, mod, wout_bf, g1, b1, wrt)


def _slot_onehot(mask, pos, cap):
    n = mask.shape[1]
    slot = lax.broadcasted_iota(jnp.int32, (cap, n), 0).astype(F32)
    return [(slot == pos[e:e + 1, :]) & (mask[e:e + 1, :] > 0.5) for e in range(N_EXPERTS)]


def _route_kernel(afft_ref, h2_ref, xs_ref, gs_ref, mask_ref, pos_ref, gate_ref, *, n, cap):
    b = pl.program_id(0)

    @pl.when(b == 0)
    def _():
        aff = afft_ref[...]
        thr = jnp.zeros((aff.shape[0], 1), jnp.int32)
        for bit in range(30, -1, -1):
            cand = thr | (1 << bit)
            cnt = jnp.sum((aff >= lax.bitcast_convert_type(cand, F32)).astype(F32), axis=1, keepdims=True)
            thr = jnp.where(cnt >= cap, cand, thr)
        above = aff >= lax.bitcast_convert_type(thr + 1, F32)
        tied = (aff >= lax.bitcast_convert_type(thr, F32)) & jnp.logical_not(above)
        need = cap - jnp.sum(above.astype(F32), axis=1, keepdims=True)
        before = (lax.broadcasted_iota(jnp.int32, (n, n), 0)
                  < lax.broadcasted_iota(jnp.int32, (n, n), 1)).astype(BF16)
        tie_rank = _dot(tied.astype(BF16), before)
        mask = (above | (tied & (tie_rank < need))).astype(F32)
        mask_ref[...] = mask
        pos_ref[...] = _dot(mask.astype(BF16), before)
        gate_ref[...] = aff * mask

    r0 = pl.multiple_of(b * N_EXPERTS, N_EXPERTS)
    mask = mask_ref[pl.ds(r0, N_EXPERTS), :]
    pos = pos_ref[pl.ds(r0, N_EXPERTS), :]
    gate = gate_ref[pl.ds(r0, N_EXPERTS), :]
    onehots = _slot_onehot(mask, pos, cap)
    sel = jnp.concatenate(onehots, axis=0).astype(BF16)
    xs = _dot(sel, h2_ref[...]).astype(BF16)
    xs_ref[...] = xs.reshape(N_EXPERTS, cap, D_MODEL)
    for e in range(N_EXPERTS):
        g = jnp.sum(jnp.where(onehots[e], gate[e:e + 1, :], 0.0), axis=1, keepdims=True)
        gs_ref[e] = jnp.broadcast_to(g, (cap, LANES))


def _route(afft, h2, *, n, nb, cap):
    rows = nb * N_EXPERTS
    return pl.pallas_call(
        functools.partial(_route_kernel, n=n, cap=cap),
        grid=(nb,),
        in_specs=[_full(afft.shape), pl.BlockSpec((n, D_MODEL), lambda b: (b, 0))],
        out_specs=[pl.BlockSpec((N_EXPERTS, cap, D_MODEL), lambda b: (0, b, 0)),
                   pl.BlockSpec((N_EXPERTS, cap, LANES), lambda b: (0, b, 0)),
                   _full((rows, n)), _full((rows, n))],
        out_shape=[jax.ShapeDtypeStruct((N_EXPERTS, nb * cap, D_MODEL), BF16),
                   jax.ShapeDtypeStruct((N_EXPERTS, nb * cap, LANES), F32),
                   jax.ShapeDtypeStruct((rows, n), F32), jax.ShapeDtypeStruct((rows, n), F32)],
        scratch_shapes=[pltpu.VMEM((rows, n), F32)],
        compiler_params=_params(),
        name="route_n%d" % n,
    )(afft, h2)


def _ffn_kernel(xp_ref, xs_ref, gp_ref, gs_ref, wgu_ref, wd_ref, yp_ref, ys_ref):
    rows_p = xp_ref.shape[1]
    xs = jnp.concatenate([xp_ref[0], xs_ref[0]], axis=0)
    gu = _dot(xs, wgu_ref[0].astype(BF16))
    gate = gu[:, :D_EXPERT]
    up = gu[:, D_EXPERT:]
    act = (gate * jax.nn.sigmoid(gate) * up).astype(BF16)
    y = _dot(act, wd_ref[0].astype(BF16))
    g = jnp.concatenate([gp_ref[0], gs_ref[0]], axis=0)
    y = y * jnp.concatenate([g] * (D_MODEL // LANES), axis=1)
    yp_ref[0] = y[:rows_p]
    ys_ref[0] = y[rows_p:]


def _ffn(xs_p, xs_s, gs_p, gs_s, w_gate_up, w_down):
    per_e = lambda a: pl.BlockSpec((1,) + a.shape[1:], lambda e: (e, 0, 0))
    return pl.pallas_call(
        _ffn_kernel,
        grid=(N_EXPERTS,),
        in_specs=[per_e(xs_p), per_e(xs_s), per_e(gs_p), per_e(gs_s), per_e(w_gate_up), per_e(w_down)],
        out_specs=[per_e(xs_p), per_e(xs_s)],
        out_shape=[jax.ShapeDtypeStruct(xs_p.shape, F32), jax.ShapeDtypeStruct(xs_s.shape, F32)],
        compiler_params=_params(),
        name="expert_ffn",
    )(xs_p, xs_s, gs_p, gs_s, w_gate_up, w_down)


def _combine_kernel(y_ref, mask_ref, pos_ref, x1_ref, mod_ref, g2_ref, b2_ref, o_ref, *, cap, mod_row0, mod_row_step):
    row = mod_row0 + mod_row_step * pl.program_id(0)
    gate2 = mod_ref[pl.ds(row, 1), 5 * D_MODEL:6 * D_MODEL]
    sel = jnp.concatenate(_slot_onehot(mask_ref[...], pos_ref[...], cap), axis=0).astype(BF16)
    y = y_ref[...].reshape(N_EXPERTS * cap, D_MODEL)
    hi, lo = _split_bf16(y)
    ff = _dot_tn(sel, hi) + _dot_tn(sel, lo)
    o_ref[...] = _ln_plain(ALPHA * x1_ref[...] + gate2 * ff) * g2_ref[...] + b2_ref[...]


def _combine(y, mask, pos, x1, mod, g2, b2, *, n, nb, cap, sample):
    tn = min(n, COMBINE_TILE)
    tiles = n // tn
    return pl.pallas_call(
        functools.partial(_combine_kernel, cap=cap, mod_row0=1 if sample else 0, mod_row_step=1 if sample else 0),
        grid=(nb, tiles),
        in_specs=[pl.BlockSpec((N_EXPERTS, cap, D_MODEL), lambda b, t: (0, b, 0)),
                  pl.BlockSpec((N_EXPERTS, tn), lambda b, t: (b, t)),
                  pl.BlockSpec((N_EXPERTS, tn), lambda b, t: (b, t)),
                  pl.BlockSpec((tn, D_MODEL), lambda b, t: (b * tiles + t, 0)),
                  _full(mod.shape), _full(g2.shape), _full(b2.shape)],
        out_specs=pl.BlockSpec((tn, D_MODEL), lambda b, t: (b * tiles + t, 0)),
        out_shape=jax.ShapeDtypeStruct((nb * n, D_MODEL), F32),
        compiler_params=_params(2),
        name="combine_sample" if sample else "combine_prompt",
    )(y, mask, pos, x1, mod, g2, b2)


def _block_diag_256(w):
    per_tile = MXU_DIM // RG_BLOCK
    tiles = []
    for c in range(w.shape[0] // per_tile):
        tiles.append(jax.scipy.linalg.block_diag(*[w[c * per_tile + i] for i in range(per_tile)]))
    return jnp.stack(tiles)


def _rope_tables(n):
    lane = np.arange(LANES)
    within = lane % HEAD_DIM
    freq = (within % 16).astype(np.float32)
    inv = jnp.asarray(ROPE_THETA, F32) ** (-jnp.asarray(freq) / 16.0)
    tok = jnp.arange(n)
    pos = jnp.where(jnp.asarray(within < HEAD_DIM // 2)[None, :], (tok // GRID_W)[:, None], (tok % GRID_W)[:, None])
    ang = pos.astype(F32) * inv[None, :]
    sign = jnp.asarray(np.where(within % 32 < 16, -1.0, 1.0).astype(np.float32))
    return jnp.cos(ang), jnp.sin(ang) * sign[None, :]


def kernel(x_prompt, x_sample, cache_k, cache_v, state_h, c, c_ctx, w_mod, b_mod, w_in, q_norm_g, k_norm_g,
           conv_w, conv_b, w_rg_a, b_rg_a, w_rg_x, b_rg_x, rg_lambda, w_out, ln1_g, ln1_b, w_router,
           w_gate_up, w_down, ln2_g, ln2_b):
    assert w_mod.shape[0] == DEPTH == 1
    nb_p, n_p, _ = x_prompt.shape
    nb_s, n_s, _ = x_sample.shape
    past = cache_k.shape[2]
    cap_p = CAP_FACTOR * n_p // N_EXPERTS
    cap_s = CAP_FACTOR * n_s // N_EXPERTS

    cs = jnp.concatenate([c_ctx[None, :], c, jnp.zeros((MOD_ROWS - 1 - nb_s, D_MODEL), F32)], axis=0)
    row = lambda v: v.reshape(1, -1)
    head_avg = np.kron(np.eye(ATTN_WIDTH // HEAD_DIM, dtype=np.float32),
                       np.full((HEAD_DIM, HEAD_DIM), 1.0 / HEAD_DIM, np.float32))
    bd = jnp.asarray(head_avg, BF16)
    qg = row(jnp.tile(q_norm_g[0], N_Q_HEADS))
    kg = row(jnp.tile(k_norm_g[0], N_KV_HEADS))
    wg = jnp.concatenate([_block_diag_256(w_rg_a[0, 0]), _block_diag_256(w_rg_x[0, 0]),
                          _block_diag_256(w_rg_a[0, 1]), _block_diag_256(w_rg_x[0, 1])], axis=2)
    bg = jnp.stack([b_rg_a[0, 0], b_rg_x[0, 0], b_rg_a[0, 1], b_rg_x[0, 1]])
    consts = (bd, qg, kg, conv_w[0], row(conv_b[0]), wg, bg, rg_lambda[0])
    wrt = w_router[0].T

    win_bf, wout_bf = _cast_weights(w_in[0], w_out[0])
    mod = _mod_vectors(cs, w_mod[0], row(b_mod[0]))

    xp = x_prompt.reshape(nb_p * n_p, D_MODEL)
    xs = x_sample.reshape(nb_s * n_s, D_MODEL)

    q_p, k_p, v_p, rg_p, new_k, new_v, new_h = _front(
        xp, mod, win_bf, consts, jnp.zeros((nb_p, 2, RG_WIDTH), F32), None, n=n_p, nb=nb_p, sample=False)
    q_s, k_s, v_s, rg_s = _front(
        xs, mod, win_bf, consts, state_h[:, 0], _rope_tables(n_s), n=n_s, nb=nb_s, sample=True)

    attn_p = _attention(q_p, k_p, v_p, None, n=n_p, nb=nb_p, past=0)
    cache = (cache_k[:, 0].reshape(nb_s * past, KV_WIDTH), cache_v[:, 0].reshape(nb_s * past, KV_WIDTH))
    attn_s = _attention(q_s, k_s, v_s, cache, n=n_s, nb=nb_s, past=past)

    g1, b1, g2, b2 = row(ln1_g[0]), row(ln1_b[0]), row(ln2_g[0]), row(ln2_b[0])
    x1_p, h2_p, aff_p = _post(attn_p, rg_p, xp, mod, wout_bf, g1, b1, wrt, n=n_p, nb=nb_p, sample=False)
    x1_s, h2_s, aff_s = _post(attn_s, rg_s, xs, mod, wout_bf, g1, b1, wrt, n=n_s, nb=nb_s, sample=True)

    xs_p, gs_p, mask_p, pos_p = _route(aff_p, h2_p, n=n_p, nb=nb_p, cap=cap_p)
    xs_s, gs_s, mask_s, pos_s = _route(aff_s, h2_s, n=n_s, nb=nb_s, cap=cap_s)

    y_p, y_s = _ffn(xs_p, xs_s, gs_p, gs_s, w_gate_up[0], w_down[0])

    out_p = _combine(y_p, mask_p, pos_p, x1_p, mod, g2, b2, n=n_p, nb=nb_p, cap=cap_p, sample=False)
    out_s = _combine(y_s, mask_s, pos_s, x1_s, mod, g2, b2, n=n_s, nb=nb_s, cap=cap_s, sample=True)

    return (out_p.reshape(nb_p, n_p, D_MODEL), out_s.reshape(nb_s, n_s, D_MODEL),
            new_k.reshape(nb_p, DEPTH, n_p, N_KV_HEADS, HEAD_DIM),
            new_v.reshape(nb_p, DEPTH, n_p, N_KV_HEADS, HEAD_DIM),
            new_h.reshape(nb_p, DEPTH, 2, RG_WIDTH))
```

```python
import functools

import numpy as np
import jax
import jax.numpy as jnp
from jax import lax
from jax.experimental import pallas as pl
from jax.experimental.pallas import tpu as pltpu

F32 = jnp.float32
BF16 = jnp.bfloat16

D_MODEL = 1024
HEAD_DIM = 64
N_Q_HEADS = 8
N_KV_HEADS = 2
GROUP = N_Q_HEADS // N_KV_HEADS
ATTN_WIDTH = N_Q_HEADS * HEAD_DIM
KV_WIDTH = N_KV_HEADS * HEAD_DIM
RG_WIDTH = D_MODEL - ATTN_WIDTH
RG_BLOCK = 64
CONV_W = 4
C_LRU = 8.0
N_EXPERTS = 16
CAP_FACTOR = 2
D_EXPERT = 1024
GRID_W = 64
ROPE_THETA = 10000.0
NORM_EPS = 1e-6
DEPTH = 1
ALPHA = (2.0 * DEPTH) ** 0.25
IN_WIDTH = ATTN_WIDTH + 2 * KV_WIDTH + 2 * RG_WIDTH
Q_SCALE = HEAD_DIM ** -0.5 * float(np.log2(np.e))

LANES = 128
SUBLANES = 8
MXU_DIM = 256
VMEM_LIMIT_BYTES = 56 * 1024 * 1024

MOD_ROWS = SUBLANES
MOD_TILE = 128
ATTN_Q_TILE = 256
COMBINE_TILE = 256


def _params(n_axes=1):
    return pltpu.CompilerParams(dimension_semantics=("arbitrary",) * n_axes,
                                vmem_limit_bytes=VMEM_LIMIT_BYTES)


def _full(shape):
    zeros = (0,) * len(shape)
    return pl.BlockSpec(shape, lambda *_: zeros)


def _ln_plain(x):
    mu = jnp.mean(x, -1, keepdims=True)
    xc = x - mu
    var = jnp.mean(xc * xc, -1, keepdims=True)
    return xc * lax.rsqrt(var + NORM_EPS)


def _dot(a, b):
    return jnp.dot(a, b, preferred_element_type=F32)


def _dot_nt(a, b):
    return lax.dot_general(a, b, (((1,), (1,)), ((), ())), preferred_element_type=F32)


def _dot_tn(a, b):
    return lax.dot_general(a, b, (((0,), (0,)), ((), ())), preferred_element_type=F32)


def _split_bf16(x):
    hi = x.astype(BF16)
    lo = (x - hi.astype(F32)).astype(BF16)
    return hi, lo


def _cast_kernel(a_ref, b_ref, ao_ref, bo_ref):
    ao_ref[...] = a_ref[...].astype(BF16)
    bo_ref[...] = b_ref[...].astype(BF16)


def _cast_weights(w_in, w_out):
    return pl.pallas_call(
        _cast_kernel,
        grid=(1,),
        in_specs=[_full(w_in.shape), _full(w_out.shape)],
        out_specs=[_full(w_in.shape), _full(w_out.shape)],
        out_shape=[jax.ShapeDtypeStruct(w_in.shape, BF16), jax.ShapeDtypeStruct(w_out.shape, BF16)],
        compiler_params=_params(),
        name="cast_weights",
    )(w_in, w_out)


def _mod_kernel(cs_ref, w_ref, b_ref, o_ref):
    @pl.when(pl.program_id(0) == 0)
    def _():
        o_ref[...] = jnp.broadcast_to(b_ref[...], o_ref.shape)

    cs = cs_ref[...]
    s = cs * jax.nn.sigmoid(cs)
    o_ref[...] += _dot(s.astype(BF16), w_ref[...].astype(BF16))


def _mod_vectors(cs, w_mod, b_mod):
    width = w_mod.shape[1]
    return pl.pallas_call(
        _mod_kernel,
        grid=(D_MODEL // MOD_TILE,),
        in_specs=[pl.BlockSpec((MOD_ROWS, MOD_TILE), lambda j: (0, j)),
                  pl.BlockSpec((MOD_TILE, width), lambda j: (j, 0)),
                  _full(b_mod.shape)],
        out_specs=_full((MOD_ROWS, width)),
        out_shape=jax.ShapeDtypeStruct((MOD_ROWS, width), F32),
        compiler_params=_params(),
        name="mod_vectors",
    )(cs, w_mod, b_mod)


def _head_mean_sq(x, bd):
    hi, lo = _split_bf16(x * x)
    return _dot(hi, bd) + _dot(lo, bd)


def _rope_lanes(x, cos, sin_signed, hi_half):
    partner = jnp.where(hi_half, pltpu.roll(x, 16, axis=1), pltpu.roll(x, LANES - 16, axis=1))
    return x * cos + partner * sin_signed


def _front_kernel(*refs, n, mod_row0, mod_row_step, sample):
    if sample:
        (x_ref, mod_ref, win_ref, bd_ref, qg_ref, kg_ref, cw_ref, cb_ref, wg_ref, bg_ref, lam_ref, h0_ref,
         cos_ref, sin_ref, q_ref, k_ref, v_ref, rg_ref, af_ref, bf_ref, ab_ref, bb_ref) = refs
    else:
        (x_ref, mod_ref, win_ref, bd_ref, qg_ref, kg_ref, cw_ref, cb_ref, wg_ref, bg_ref, lam_ref, h0_ref,
         q_ref, k_ref, v_ref, rg_ref, nk_ref, nv_ref, nh_ref, af_ref, bf_ref, ab_ref, bb_ref) = refs

    row = mod_row0 + mod_row_step * pl.program_id(0)
    m = mod_ref[pl.ds(row, 1), :]
    shift1 = m[:, 0:D_MODEL]
    scale1 = m[:, D_MODEL:2 * D_MODEL]
    h = (_ln_plain(x_ref[...]) * (1.0 + scale1) + shift1).astype(BF16)

    def proj(lo, hi):
        return _dot(h, win_ref[:, lo:hi])

    bd = bd_ref[...]
    q = proj(0, ATTN_WIDTH)
    q = q * lax.rsqrt(_head_mean_sq(q, bd) + NORM_EPS) * qg_ref[...]
    k = proj(ATTN_WIDTH, ATTN_WIDTH + KV_WIDTH)
    k = k * lax.rsqrt(_head_mean_sq(k, bd[:KV_WIDTH, :KV_WIDTH]) + NORM_EPS) * kg_ref[...]
    v = proj(ATTN_WIDTH + KV_WIDTH, ATTN_WIDTH + 2 * KV_WIDTH)

    if sample:
        cos = cos_ref[...]
        sin_signed = sin_ref[...]
        lane = lax.broadcasted_iota(jnp.int32, (n, LANES), 1)
        hi_half = (lane & 16) != 0
        q = jnp.concatenate(
            [_rope_lanes(q[:, j * LANES:(j + 1) * LANES], cos, sin_signed, hi_half)
             for j in range(ATTN_WIDTH // LANES)], axis=1)
        k = _rope_lanes(k, cos, sin_signed, hi_half)
    else:
        nk_ref[...] = k
        nv_ref[...] = v
    q_ref[...] = (q * Q_SCALE).astype(BF16)
    k_ref[...] = k.astype(BF16)
    v_ref[...] = v.astype(BF16)

    xr = proj(ATTN_WIDTH + 2 * KV_WIDTH, ATTN_WIDTH + 2 * KV_WIDTH + RG_WIDTH)
    t_idx = lax.broadcasted_iota(jnp.int32, (n, 1), 0)
    cw = cw_ref[...]
    xc = jnp.where(t_idx >= 2, pltpu.roll(xr, 2, axis=0), 0.0) * cw[0:1, :]
    xc = xc + jnp.where(t_idx >= 1, pltpu.roll(xr, 1, axis=0), 0.0) * cw[1:2, :]
    xc = xc + xr * cw[2:3, :]
    xc = xc + jnp.where(t_idx < n - 1, pltpu.roll(xr, n - 1, axis=0), 0.0) * cw[3:4, :]
    xc = xc + cb_ref[...]

    xcb = xc.astype(BF16)
    halves = [_dot(xcb[:, c * MXU_DIM:(c + 1) * MXU_DIM], wg_ref[c].astype(BF16))
              for c in range(RG_WIDTH // MXU_DIM)]

    def gate_pre(idx):
        return jnp.concatenate([hv[:, idx * MXU_DIM:(idx + 1) * MXU_DIM] for hv in halves], axis=1) \
            + bg_ref[idx:idx + 1, :]

    lam = lam_ref[...]
    neg = -lam
    softplus = jnp.maximum(neg, 0.0) + jnp.log1p(jnp.exp(-jnp.abs(neg)))

    decay = (-0.5 * C_LRU) * softplus
    half_xc = 0.5 * xc

    def coeffs(d):
        r2 = jnp.tanh(gate_pre(2 * d)) + 1.0
        i2 = jnp.tanh(gate_pre(2 * d + 1)) + 1.0
        log_a = r2 * decay[d:d + 1, :]
        a = jnp.exp(log_a)
        bx = jnp.sqrt(jnp.tanh(-log_a) * (a * a + 1.0)) * (i2 * half_xc)
        return a, bx

    groups = n // SUBLANES
    rmod = lax.broadcasted_iota(jnp.int32, (1, SUBLANES, 1), 1)

    def tile_scan(a, bx, forward):
        a = a.reshape(groups, SUBLANES, RG_WIDTH)
        bx = bx.reshape(groups, SUBLANES, RG_WIDTH)
        for s in (1, 2, 4):
            ok = (rmod >= s) if forward else (rmod < SUBLANES - s)
            shift = s if forward else SUBLANES - s
            a_sh = jnp.where(ok, pltpu.roll(a, shift, axis=1), 1.0)
            b_sh = jnp.where(ok, pltpu.roll(bx, shift, axis=1), 0.0)
            bx = a * b_sh + bx
            a = a * a_sh
        return a.reshape(n, RG_WIDTH), bx.reshape(n, RG_WIDTH)

    af_ref[...], bf_ref[...] = tile_scan(*coeffs(0), True)
    ab_ref[...], bb_ref[...] = tile_scan(*coeffs(1), False)

    h0 = h0_ref[0]

    def step(g, carry):
        hf, hb = carry
        rf = pl.multiple_of(g * SUBLANES, SUBLANES)
        rb = pl.multiple_of((groups - 1 - g) * SUBLANES, SUBLANES)
        new_f = af_ref[pl.ds(rf, SUBLANES), :] * hf + bf_ref[pl.ds(rf, SUBLANES), :]
        new_b = ab_ref[pl.ds(rb, SUBLANES), :] * hb + bb_ref[pl.ds(rb, SUBLANES), :]
        bf_ref[pl.ds(rf, SUBLANES), :] = new_f
        bb_ref[pl.ds(rb, SUBLANES), :] = new_b
        return (jnp.broadcast_to(new_f[SUBLANES - 1:SUBLANES, :], (SUBLANES, RG_WIDTH)),
                jnp.broadcast_to(new_b[0:1, :], (SUBLANES, RG_WIDTH)))

    hf, hb = lax.fori_loop(
        0, groups, step,
        (jnp.broadcast_to(h0[0:1, :], (SUBLANES, RG_WIDTH)), jnp.broadcast_to(h0[1:2, :], (SUBLANES, RG_WIDTH))))
    if not sample:
        nh_ref[0] = jnp.concatenate([hf[0:1, :], hb[0:1, :]], axis=0)

    gr = proj(ATTN_WIDTH + 2 * KV_WIDTH + RG_WIDTH, IN_WIDTH)
    rg_ref[...] = ((bf_ref[...] + bb_ref[...]) * jax.nn.gelu(gr)).astype(BF16)


def _front(x2d, mod, win_bf, consts, h0, rope, *, n, nb, sample):
    bd, qg, kg, cw, cb, wg, bg, lam = consts
    tokens = nb * n
    seq = lambda w: pl.BlockSpec((n, w), lambda b: (b, 0))
    in_specs = [seq(D_MODEL), _full(mod.shape), _full(win_bf.shape), _full(bd.shape), _full(qg.shape),
                _full(kg.shape), _full(cw.shape), _full(cb.shape), _full(wg.shape), _full(bg.shape),
                _full(lam.shape), pl.BlockSpec((1, 2, RG_WIDTH), lambda b: (b, 0, 0))]
    args = [x2d, mod, win_bf, bd, qg, kg, cw, cb, wg, bg, lam, h0]
    out_specs = [seq(ATTN_WIDTH), seq(KV_WIDTH), seq(KV_WIDTH), seq(RG_WIDTH)]
    out_shape = [jax.ShapeDtypeStruct((tokens, ATTN_WIDTH), BF16), jax.ShapeDtypeStruct((tokens, KV_WIDTH), BF16),
                 jax.ShapeDtypeStruct((tokens, KV_WIDTH), BF16), jax.ShapeDtypeStruct((tokens, RG_WIDTH), BF16)]
    if sample:
        in_specs += [_full(rope[0].shape), _full(rope[1].shape)]
        args += list(rope)
    else:
        out_specs += [seq(KV_WIDTH), seq(KV_WIDTH), pl.BlockSpec((1, 2, RG_WIDTH), lambda b: (b, 0, 0))]
        out_shape += [jax.ShapeDtypeStruct((tokens, KV_WIDTH), F32), jax.ShapeDtypeStruct((tokens, KV_WIDTH), F32),
                      jax.ShapeDtypeStruct((nb, 2, RG_WIDTH), F32)]
    return pl.pallas_call(
        functools.partial(_front_kernel, n=n, mod_row0=1 if sample else 0, mod_row_step=1 if sample else 0,
                          sample=sample),
        grid=(nb,),
        in_specs=in_specs,
        out_specs=out_specs,
        out_shape=out_shape,
        scratch_shapes=[pltpu.VMEM((n, RG_WIDTH), F32)] * 4,
        compiler_params=_params(),
        name="front_sample" if sample else "front_prompt",
    )(*args)


def _attn_kernel(*refs, has_cache):
    if has_cache:
        q_ref, k_ref, v_ref, kc_ref, vc_ref, o_ref = refs
        k = jnp.concatenate([k_ref[...], kc_ref[...].astype(BF16)], axis=0)
        v = jnp.concatenate([v_ref[...], vc_ref[...].astype(BF16)], axis=0)
    else:
        q_ref, k_ref, v_ref, o_ref = refs
        k = k_ref[...]
        v = v_ref[...]
    q = q_ref[...]
    outs = []
    for hq in range(N_Q_HEADS):
        kv = hq // GROUP
        qh = q[:, hq * HEAD_DIM:(hq + 1) * HEAD_DIM]
        kh = k[:, kv * HEAD_DIM:(kv + 1) * HEAD_DIM]
        vh = v[:, kv * HEAD_DIM:(kv + 1) * HEAD_DIM]
        s = _dot_nt(qh, kh)
        e = jnp.exp2(s - jnp.max(s, axis=-1, keepdims=True))
        denom = jnp.sum(e, axis=-1, keepdims=True)
        outs.append(_dot(e.astype(BF16), vh) / denom)
    o_ref[...] = jnp.concatenate(outs, axis=1).astype(BF16)


def _attention(q, k, v, cache, *, n, nb, past):
    tq = min(n, ATTN_Q_TILE)
    in_specs = [pl.BlockSpec((tq, ATTN_WIDTH), lambda b, t: (b * (n // tq) + t, 0)),
                pl.BlockSpec((n, KV_WIDTH), lambda b, t: (b, 0)),
                pl.BlockSpec((n, KV_WIDTH), lambda b, t: (b, 0))]
    args = [q, k, v]
    if cache is not None:
        in_specs += [pl.BlockSpec((past, KV_WIDTH), lambda b, t: (b, 0))] * 2
        args += list(cache)
    return pl.pallas_call(
        functools.partial(_attn_kernel, has_cache=cache is not None),
        grid=(nb, n // tq),
        in_specs=in_specs,
        out_specs=pl.BlockSpec((tq, ATTN_WIDTH), lambda b, t: (b * (n // tq) + t, 0)),
        out_shape=jax.ShapeDtypeStruct((nb * n, ATTN_WIDTH), BF16),
        compiler_params=_params(2),
        name="attn_sample" if cache is not None else "attn_prompt",
    )(*args)


def _post_kernel(attn_ref, rg_ref, x_ref, mod_ref, wout_ref, g1_ref, b1_ref, wrt_ref,
                 x1_ref, h2_ref, afft_ref, *, mod_row0, mod_row_step):
    row = mod_row0 + mod_row_step * pl.program_id(0)
    m = mod_ref[pl.ds(row, 1), :]
    gate1 = m[:, 2 * D_MODEL:3 * D_MODEL]
    shift2 = m[:, 3 * D_MODEL:4 * D_MODEL]
    scale2 = m[:, 4 * D_MODEL:5 * D_MODEL]
    u = jnp.concatenate([attn_ref[...], rg_ref[...]], axis=1)
    mix = _dot(u, wout_ref[...])
    x1 = _ln_plain(ALPHA * x_ref[...] + gate1 * mix) * g1_ref[...] + b1_ref[...]
    x1_ref[...] = x1
    h2 = (_ln_plain(x1) * (1.0 + scale2) + shift2).astype(BF16)
    h2_ref[...] = h2
    logits = _dot_nt(wrt_ref[...].astype(BF16), h2)
    e = jnp.exp(logits - jnp.max(logits, axis=0, keepdims=True))
    afft_ref[...] = e / jnp.sum(e, axis=0, keepdims=True)


def _post(attn, rg, x2d, mod, wout_bf, g1, b1, wrt, *, n, nb, sample):
    tokens = nb * n
    seq = lambda w: pl.BlockSpec((n, w), lambda b: (b, 0))
    return pl.pallas_call(
        functools.partial(_post_kernel, mod_row0=1 if sample else 0, mod_row_step=1 if sample else 0),
        grid=(nb,),
        in_specs=[seq(ATTN_WIDTH), seq(RG_WIDTH), seq(D_MODEL), _full(mod.shape), _full(wout_bf.shape),
                  _full(g1.shape), _full(b1.shape), _full(wrt.shape)],
        out_specs=[seq(D_MODEL), seq(D_MODEL), pl.BlockSpec((N_EXPERTS, n), lambda b: (b, 0))],
        out_shape=[jax.ShapeDtypeStruct((tokens, D_MODEL), F32), jax.ShapeDtypeStruct((tokens, D_MODEL), BF16),
                   jax.ShapeDtypeStruct((nb * N_EXPERTS, n), F32)],
        compiler_params=_params(),
        name="post_sample" if sample else "post_prompt",
    )(attn, rg, x2d, mod, wout_bf, g1, b1, wrt)


def _slot_onehot(mask, pos, cap):
    n = mask.shape[1]
    slot = lax.broadcasted_iota(jnp.int32, (cap, n), 0).astype(F32)
    return [(slot == pos[e:e + 1, :]) & (mask[e:e + 1, :] > 0.5) for e in range(N_EXPERTS)]


def _route_kernel(afft_ref, h2_ref, xs_ref, gs_ref, mask_ref, pos_ref, gate_ref, *, n, cap):
    b = pl.program_id(0)

    @pl.when(b == 0)
    def _():
        aff = afft_ref[...]
        thr = jnp.zeros((aff.shape[0], 1), jnp.int32)
        for bit in range(30, -1, -1):
            cand = thr | (1 << bit)
            cnt = jnp.sum((aff >= lax.bitcast_convert_type(cand, F32)).astype(F32), axis=1, keepdims=True)
            thr = jnp.where(cnt >= cap, cand, thr)
        above = aff >= lax.bitcast_convert_type(thr + 1, F32)
        tied = (aff >= lax.bitcast_convert_type(thr, F32)) & jnp.logical_not(above)
        need = cap - jnp.sum(above.astype(F32), axis=1, keepdims=True)
        before = (lax.broadcasted_iota(jnp.int32, (n, n), 0)
                  < lax.broadcasted_iota(jnp.int32, (n, n), 1)).astype(BF16)
        tie_rank = _dot(tied.astype(BF16), before)
        mask = (above | (tied & (tie_rank < need))).astype(F32)
        mask_ref[...] = mask
        pos_ref[...] = _dot(mask.astype(BF16), before)
        gate_ref[...] = aff * mask

    r0 = pl.multiple_of(b * N_EXPERTS, N_EXPERTS)
    mask = mask_ref[pl.ds(r0, N_EXPERTS), :]
    pos = pos_ref[pl.ds(r0, N_EXPERTS), :]
    gate = gate_ref[pl.ds(r0, N_EXPERTS), :]
    onehots = _slot_onehot(mask, pos, cap)
    sel = jnp.concatenate(onehots, axis=0).astype(BF16)
    xs = _dot(sel, h2_ref[...]).astype(BF16)
    xs_ref[...] = xs.reshape(N_EXPERTS, cap, D_MODEL)
    for e in range(N_EXPERTS):
        g = jnp.sum(jnp.where(onehots[e], gate[e:e + 1, :], 0.0), axis=1, keepdims=True)
        gs_ref[e] = jnp.broadcast_to(g, (cap, LANES))


def _route(afft, h2, *, n, nb, cap):
    rows = nb * N_EXPERTS
    return pl.pallas_call(
        functools.partial(_route_kernel, n=n, cap=cap),
        grid=(nb,),
        in_specs=[_full(afft.shape), pl.BlockSpec((n, D_MODEL), lambda b: (b, 0))],
        out_specs=[pl.BlockSpec((N_EXPERTS, cap, D_MODEL), lambda b: (0, b, 0)),
                   pl.BlockSpec((N_EXPERTS, cap, LANES), lambda b: (0, b, 0)),
                   _full((rows, n)), _full((rows, n))],
        out_shape=[jax.ShapeDtypeStruct((N_EXPERTS, nb * cap, D_MODEL), BF16),
                   jax.ShapeDtypeStruct((N_EXPERTS, nb * cap, LANES), F32),
                   jax.ShapeDtypeStruct((rows, n), F32), jax.ShapeDtypeStruct((rows, n), F32)],
        scratch_shapes=[pltpu.VMEM((rows, n), F32)],
        compiler_params=_params(),
        name="route_n%d" % n,
    )(afft, h2)


def _ffn_kernel(xp_ref, xs_ref, gp_ref, gs_ref, wgu_ref, wd_ref, yp_ref, ys_ref):
    rows_p = xp_ref.shape[1]
    xs = jnp.concatenate([xp_ref[0], xs_ref[0]], axis=0)
    gu = _dot(xs, wgu_ref[0].astype(BF16))
    gate = gu[:, :D_EXPERT]
    up = gu[:, D_EXPERT:]
    act = (gate * jax.nn.sigmoid(gate) * up).astype(BF16)
    y = _dot(act, wd_ref[0].astype(BF16))
    g = jnp.concatenate([gp_ref[0], gs_ref[0]], axis=0)
    y = y * jnp.concatenate([g] * (D_MODEL // LANES), axis=1)
    yp_ref[0] = y[:rows_p].astype(BF16)
    ys_ref[0] = y[rows_p:].astype(BF16)


def _ffn(xs_p, xs_s, gs_p, gs_s, w_gate_up, w_down):
    per_e = lambda a: pl.BlockSpec((1,) + a.shape[1:], lambda e: (e, 0, 0))
    return pl.pallas_call(
        _ffn_kernel,
        grid=(N_EXPERTS,),
        in_specs=[per_e(xs_p), per_e(xs_s), per_e(gs_p), per_e(gs_s), per_e(w_gate_up), per_e(w_down)],
        out_specs=[per_e(xs_p), per_e(xs_s)],
        out_shape=[jax.ShapeDtypeStruct(xs_p.shape, BF16), jax.ShapeDtypeStruct(xs_s.shape, BF16)],
        compiler_params=_params(),
        name="expert_ffn",
    )(xs_p, xs_s, gs_p, gs_s, w_gate_up, w_down)


def _combine_kernel(y_ref, mask_ref, pos_ref, x1_ref, mod_ref, g2_ref, b2_ref, o_ref, *, cap, mod_row0, mod_row_step):
    row = mod_row0 + mod_row_step * pl.program_id(0)
    gate2 = mod_ref[pl.ds(row, 1), 5 * D_MODEL:6 * D_MODEL]
    sel = jnp.concatenate(_slot_onehot(mask_ref[...], pos_ref[...], cap), axis=0).astype(BF16)
    ff = _dot_tn(sel, y_ref[...].reshape(N_EXPERTS * cap, D_MODEL))
    o_ref[...] = _ln_plain(ALPHA * x1_ref[...] + gate2 * ff) * g2_ref[...] + b2_ref[...]


def _combine(y, mask, pos, x1, mod, g2, b2, *, n, nb, cap, sample):
    tn = min(n, COMBINE_TILE)
    tiles = n // tn
    return pl.pallas_call(
        functools.partial(_combine_kernel, cap=cap, mod_row0=1 if sample else 0, mod_row_step=1 if sample else 0),
        grid=(nb, tiles),
        in_specs=[pl.BlockSpec((N_EXPERTS, cap, D_MODEL), lambda b, t: (0, b, 0)),
                  pl.BlockSpec((N_EXPERTS, tn), lambda b, t: (b, t)),
                  pl.BlockSpec((N_EXPERTS, tn), lambda b, t: (b, t)),
                  pl.BlockSpec((tn, D_MODEL), lambda b, t: (b * tiles + t, 0)),
                  _full(mod.shape), _full(g2.shape), _full(b2.shape)],
        out_specs=pl.BlockSpec((tn, D_MODEL), lambda b, t: (b * tiles + t, 0)),
        out_shape=jax.ShapeDtypeStruct((nb * n, D_MODEL), F32),
        compiler_params=_params(2),
        name="combine_sample" if sample else "combine_prompt",
    )(y, mask, pos, x1, mod, g2, b2)


def _block_diag_256(w):
    per_tile = MXU_DIM // RG_BLOCK
    tiles = []
    for c in range(w.shape[0] // per_tile):
        tiles.append(jax.scipy.linalg.block_diag(*[w[c * per_tile + i] for i in range(per_tile)]))
    return jnp.stack(tiles)


def _rope_tables(n):
    lane = np.arange(LANES)
    within = lane % HEAD_DIM
    freq = (within % 16).astype(np.float32)
    inv = jnp.asarray(ROPE_THETA, F32) ** (-jnp.asarray(freq) / 16.0)
    tok = jnp.arange(n)
    pos = jnp.where(jnp.asarray(within < HEAD_DIM // 2)[None, :], (tok // GRID_W)[:, None], (tok % GRID_W)[:, None])
    ang = pos.astype(F32) * inv[None, :]
    sign = jnp.asarray(np.where(within % 32 < 16, -1.0, 1.0).astype(np.float32))
    return jnp.cos(ang), jnp.sin(ang) * sign[None, :]


def kernel(x_prompt, x_sample, cache_k, cache_v, state_h, c, c_ctx, w_mod, b_mod, w_in, q_norm_g, k_norm_g,
           conv_w, conv_b, w_rg_a, b_rg_a, w_rg_x, b_rg_x, rg_lambda, w_out, ln1_g, ln1_b, w_router,
           w_gate_up, w_down, ln2_g, ln2_b):
    assert w_mod.shape[0] == DEPTH == 1
    nb_p, n_p, _ = x_prompt.shape
    nb_s, n_s, _ = x_sample.shape
    past = cache_k.shape[2]
    cap_p = CAP_FACTOR * n_p // N_EXPERTS
    cap_s = CAP_FACTOR * n_s // N_EXPERTS

    cs = jnp.concatenate([c_ctx[None, :], c, jnp.zeros((MOD_ROWS - 1 - nb_s, D_MODEL), F32)], axis=0)
    row = lambda v: v.reshape(1, -1)
    head_avg = np.kron(np.eye(ATTN_WIDTH // HEAD_DIM, dtype=np.float32),
                       np.full((HEAD_DIM, HEAD_DIM), 1.0 / HEAD_DIM, np.float32))
    bd = jnp.asarray(head_avg, BF16)
    qg = row(jnp.tile(q_norm_g[0], N_Q_HEADS))
    kg = row(jnp.tile(k_norm_g[0], N_KV_HEADS))
    wg = 0.5 * jnp.concatenate([_block_diag_256(w_rg_a[0, 0]), _block_diag_256(w_rg_x[0, 0]),
                                _block_diag_256(w_rg_a[0, 1]), _block_diag_256(w_rg_x[0, 1])], axis=2)
    bg = 0.5 * jnp.stack([b_rg_a[0, 0], b_rg_x[0, 0], b_rg_a[0, 1], b_rg_x[0, 1]])
    consts = (bd, qg, kg, conv_w[0], row(conv_b[0]), wg, bg, rg_lambda[0])
    wrt = w_router[0].T

    win_bf, wout_bf = _cast_weights(w_in[0], w_out[0])
    mod = _mod_vectors(cs, w_mod[0], row(b_mod[0]))

    xp = x_prompt.reshape(nb_p * n_p, D_MODEL)
    xs = x_sample.reshape(nb_s * n_s, D_MODEL)

    q_p, k_p, v_p, rg_p, new_k, new_v, new_h = _front(
        xp, mod, win_bf, consts, jnp.zeros((nb_p, 2, RG_WIDTH), F32), None, n=n_p, nb=nb_p, sample=False)
    q_s, k_s, v_s, rg_s = _front(
        xs, mod, win_bf, consts, state_h[:, 0], _rope_tables(n_s), n=n_s, nb=nb_s, sample=True)

    attn_p = _attention(q_p, k_p, v_p, None, n=n_p, nb=nb_p, past=0)
    cache = (cache_k[:, 0].reshape(nb_s * past, KV_WIDTH), cache_v[:, 0].reshape(nb_s * past, KV_WIDTH))
    attn_s = _attention(q_s, k_s, v_s, cache, n=n_s, nb=nb_s, past=past)

    g1, b1, g2, b2 = row(ln1_g[0]), row(ln1_b[0]), row(ln2_g[0]), row(ln2_b[0])
    x1_p, h2_p, aff_p = _post(attn_p, rg_p, xp, mod, wout_bf, g1, b1, wrt, n=n_p, nb=nb_p, sample=False)
    x1_s, h2_s, aff_s = _post(attn_s, rg_s, xs, mod, wout_bf, g1, b1, wrt, n=n_s, nb=nb_s, sample=True)

    xs_p, gs_p, mask_p, pos_p = _route(aff_p, h2_p, n=n_p, nb=nb_p, cap=cap_p)
    xs_s, gs_s, mask_s, pos_s = _route(aff_s, h2_s, n=n_s, nb=nb_s, cap=cap_s)

    y_p, y_s = _ffn(xs_p, xs_s, gs_p, gs_s, w_gate_up[0], w_down[0])

    out_p = _combine(y_p, mask_p, pos_p, x1_p, mod, g2, b2, n=n_p, nb=nb_p, cap=cap_p, sample=False)
    out_s = _combine(y_s, mask_s, pos_s, x1_s, mod, g2, b2, n=n_s, nb=nb_s, cap=cap_s, sample=True)

    return (out_p.reshape(nb_p, n_p, D_MODEL), out_s.reshape(nb_s, n_s, D_MODEL),
            new_k.reshape(nb_p, DEPTH, n_p, N_KV_HEADS, HEAD_DIM),
            new_v.reshape(nb_p, DEPTH, n_p, N_KV_HEADS, HEAD_DIM),
            new_h.reshape(nb_p, DEPTH, 2, RG_WIDTH))
```

```python
import functools

import numpy as np
import jax
import jax.numpy as jnp
from jax import lax
from jax.experimental import pallas as pl
from jax.experimental.pallas import tpu as pltpu

F32 = jnp.float32
BF16 = jnp.bfloat16

D_MODEL = 1024
HEAD_DIM = 64
N_Q_HEADS = 8
N_KV_HEADS = 2
GROUP = N_Q_HEADS // N_KV_HEADS
ATTN_WIDTH = N_Q_HEADS * HEAD_DIM
KV_WIDTH = N_KV_HEADS * HEAD_DIM
RG_WIDTH = D_MODEL - ATTN_WIDTH
RG_BLOCK = 64
CONV_W = 4
C_LRU = 8.0
N_EXPERTS = 16
CAP_FACTOR = 2
D_EXPERT = 1024
GRID_W = 64
ROPE_THETA = 10000.0
NORM_EPS = 1e-6
DEPTH = 1
ALPHA = (2.0 * DEPTH) ** 0.25
IN_WIDTH = ATTN_WIDTH + 2 * KV_WIDTH + 2 * RG_WIDTH
Q_SCALE = HEAD_DIM ** -0.5 * float(np.log2(np.e))

LANES = 128
SUBLANES = 8
MXU_DIM = 256
VMEM_LIMIT_BYTES = 56 * 1024 * 1024

MOD_ROWS = SUBLANES
MOD_TILE = 128
ATTN_Q_TILE = 256
COMBINE_TILE = 256


def _params(n_axes=1):
    return pltpu.CompilerParams(dimension_semantics=("arbitrary",) * n_axes,
                                vmem_limit_bytes=VMEM_LIMIT_BYTES)


def _full(shape):
    zeros = (0,) * len(shape)
    return pl.BlockSpec(shape, lambda *_: zeros)


def _ln_plain(x):
    mu = jnp.mean(x, -1, keepdims=True)
    xc = x - mu
    var = jnp.mean(xc * xc, -1, keepdims=True)
    return xc * lax.rsqrt(var + NORM_EPS)


def _dot(a, b):
    return jnp.dot(a, b, preferred_element_type=F32)


def _dot_nt(a, b):
    return lax.dot_general(a, b, (((1,), (1,)), ((), ())), preferred_element_type=F32)


def _dot_tn(a, b):
    return lax.dot_general(a, b, (((0,), (0,)), ((), ())), preferred_element_type=F32)


def _split_bf16(x):
    hi = x.astype(BF16)
    lo = (x - hi.astype(F32)).astype(BF16)
    return hi, lo


def _cast_kernel(a_ref, b_ref, ao_ref, bo_ref):
    ao_ref[...] = a_ref[...].astype(BF16)
    bo_ref[...] = b_ref[...].astype(BF16)


def _cast_weights(w_in, w_out):
    return pl.pallas_call(
        _cast_kernel,
        grid=(1,),
        in_specs=[_full(w_in.shape), _full(w_out.shape)],
        out_specs=[_full(w_in.shape), _full(w_out.shape)],
        out_shape=[jax.ShapeDtypeStruct(w_in.shape, BF16), jax.ShapeDtypeStruct(w_out.shape, BF16)],
        compiler_params=_params(),
        name="cast_weights",
    )(w_in, w_out)


def _mod_kernel(cs_ref, w_ref, b_ref, o_ref):
    @pl.when(pl.program_id(0) == 0)
    def _():
        o_ref[...] = jnp.broadcast_to(b_ref[...], o_ref.shape)

    cs = cs_ref[...]
    s = cs * jax.nn.sigmoid(cs)
    o_ref[...] += _dot(s.astype(BF16), w_ref[...].astype(BF16))


def _mod_vectors(cs, w_mod, b_mod):
    width = w_mod.shape[1]
    return pl.pallas_call(
        _mod_kernel,
        grid=(D_MODEL // MOD_TILE,),
        in_specs=[pl.BlockSpec((MOD_ROWS, MOD_TILE), lambda j: (0, j)),
                  pl.BlockSpec((MOD_TILE, width), lambda j: (j, 0)),
                  _full(b_mod.shape)],
        out_specs=_full((MOD_ROWS, width)),
        out_shape=jax.ShapeDtypeStruct((MOD_ROWS, width), F32),
        compiler_params=_params(),
        name="mod_vectors",
    )(cs, w_mod, b_mod)


def _head_mean_sq(x, bd):
    hi, lo = _split_bf16(x * x)
    return _dot(hi, bd) + _dot(lo, bd)


def _rope_lanes(x, cos, sin_signed, hi_half):
    partner = jnp.where(hi_half, pltpu.roll(x, 16, axis=1), pltpu.roll(x, LANES - 16, axis=1))
    return x * cos + partner * sin_signed


def _modulated(x, m):
    return (_ln_plain(x) * (1.0 + m[:, D_MODEL:2 * D_MODEL]) + m[:, 0:D_MODEL]).astype(BF16)


def _qkv(h, win_ref, bd_ref, qg_ref, kg_ref, rope):
    n = h.shape[0]
    bd = bd_ref[...]
    q = _dot(h, win_ref[:, 0:ATTN_WIDTH])
    q = q * lax.rsqrt(_head_mean_sq(q, bd) + NORM_EPS) * qg_ref[...]
    k = _dot(h, win_ref[:, ATTN_WIDTH:ATTN_WIDTH + KV_WIDTH])
    k = k * lax.rsqrt(_head_mean_sq(k, bd[:KV_WIDTH, :KV_WIDTH]) + NORM_EPS) * kg_ref[...]
    v = _dot(h, win_ref[:, ATTN_WIDTH + KV_WIDTH:ATTN_WIDTH + 2 * KV_WIDTH])
    if rope is not None:
        cos, sin_signed = rope
        lane = lax.broadcasted_iota(jnp.int32, (n, LANES), 1)
        hi_half = (lane & 16) != 0
        q = jnp.concatenate(
            [_rope_lanes(q[:, j * LANES:(j + 1) * LANES], cos, sin_signed, hi_half)
             for j in range(ATTN_WIDTH // LANES)], axis=1)
        k = _rope_lanes(k, cos, sin_signed, hi_half)
    return (q * Q_SCALE).astype(BF16), k, v


def _attend(q, k, v):
    outs = []
    for hq in range(N_Q_HEADS):
        kv = hq // GROUP
        qh = q[:, hq * HEAD_DIM:(hq + 1) * HEAD_DIM]
        kh = k[:, kv * HEAD_DIM:(kv + 1) * HEAD_DIM]
        vh = v[:, kv * HEAD_DIM:(kv + 1) * HEAD_DIM]
        s = _dot_nt(qh, kh)
        e = jnp.exp2(s - jnp.max(s, axis=-1, keepdims=True))
        denom = jnp.sum(e, axis=-1, keepdims=True)
        outs.append(_dot(e.astype(BF16), vh) / denom)
    return jnp.concatenate(outs, axis=1).astype(BF16)


def _rglru(h, win_ref, cw_ref, cb_ref, wg_ref, bg_ref, lam_ref, h0, scan_refs):
    af_ref, bf_ref, ab_ref, bb_ref = scan_refs
    n = h.shape[0]
    rg_lo = ATTN_WIDTH + 2 * KV_WIDTH

    xr = _dot(h, win_ref[:, rg_lo:rg_lo + RG_WIDTH])
    t_idx = lax.broadcasted_iota(jnp.int32, (n, 1), 0)
    cw = cw_ref[...]
    xc = jnp.where(t_idx >= 2, pltpu.roll(xr, 2, axis=0), 0.0) * cw[0:1, :]
    xc = xc + jnp.where(t_idx >= 1, pltpu.roll(xr, 1, axis=0), 0.0) * cw[1:2, :]
    xc = xc + xr * cw[2:3, :]
    xc = xc + jnp.where(t_idx < n - 1, pltpu.roll(xr, n - 1, axis=0), 0.0) * cw[3:4, :]
    xc = xc + cb_ref[...]

    xcb = xc.astype(BF16)
    halves = [_dot(xcb[:, c * MXU_DIM:(c + 1) * MXU_DIM], wg_ref[c].astype(BF16))
              for c in range(RG_WIDTH // MXU_DIM)]

    def gate_pre(idx):
        return jnp.concatenate([hv[:, idx * MXU_DIM:(idx + 1) * MXU_DIM] for hv in halves], axis=1) \
            + bg_ref[idx:idx + 1, :]

    neg = -lam_ref[...]
    softplus = jnp.maximum(neg, 0.0) + jnp.log1p(jnp.exp(-jnp.abs(neg)))
    decay = (-0.5 * C_LRU) * softplus
    half_xc = 0.5 * xc

    def coeffs(d):
        r2 = jnp.tanh(gate_pre(2 * d)) + 1.0
        i2 = jnp.tanh(gate_pre(2 * d + 1)) + 1.0
        log_a = r2 * decay[d:d + 1, :]
        a = jnp.exp(log_a)
        bx = jnp.sqrt(jnp.tanh(-log_a) * (a * a + 1.0)) * (i2 * half_xc)
        return a, bx

    groups = n // SUBLANES
    rmod = lax.broadcasted_iota(jnp.int32, (1, SUBLANES, 1), 1)

    def tile_scan(a, bx, forward):
        a = a.reshape(groups, SUBLANES, RG_WIDTH)
        bx = bx.reshape(groups, SUBLANES, RG_WIDTH)
        for s in (1, 2, 4):
            ok = (rmod >= s) if forward else (rmod < SUBLANES - s)
            shift = s if forward else SUBLANES - s
            a_sh = jnp.where(ok, pltpu.roll(a, shift, axis=1), 1.0)
            b_sh = jnp.where(ok, pltpu.roll(bx, shift, axis=1), 0.0)
            bx = a * b_sh + bx
            a = a * a_sh
        return a.reshape(n, RG_WIDTH), bx.reshape(n, RG_WIDTH)

    af_ref[...], bf_ref[...] = tile_scan(*coeffs(0), True)
    ab_ref[...], bb_ref[...] = tile_scan(*coeffs(1), False)

    def step(g, carry):
        hf, hb = carry
        rf = pl.multiple_of(g * SUBLANES, SUBLANES)
        rb = pl.multiple_of((groups - 1 - g) * SUBLANES, SUBLANES)
        new_f = af_ref[pl.ds(rf, SUBLANES), :] * hf + bf_ref[pl.ds(rf, SUBLANES), :]
        new_b = ab_ref[pl.ds(rb, SUBLANES), :] * hb + bb_ref[pl.ds(rb, SUBLANES), :]
        bf_ref[pl.ds(rf, SUBLANES), :] = new_f
        bb_ref[pl.ds(rb, SUBLANES), :] = new_b
        return (jnp.broadcast_to(new_f[SUBLANES - 1:SUBLANES, :], (SUBLANES, RG_WIDTH)),
                jnp.broadcast_to(new_b[0:1, :], (SUBLANES, RG_WIDTH)))

    hf, hb = lax.fori_loop(
        0, groups, step,
        (jnp.broadcast_to(h0[0:1, :], (SUBLANES, RG_WIDTH)), jnp.broadcast_to(h0[1:2, :], (SUBLANES, RG_WIDTH))))

    gr = _dot(h, win_ref[:, rg_lo + RG_WIDTH:IN_WIDTH])
    rg = ((bf_ref[...] + bb_ref[...]) * jax.nn.gelu(gr)).astype(BF16)
    return rg, hf[0:1, :], hb[0:1, :]


def _out_and_router(attn, rg, x, m, wout_ref, g1_ref, b1_ref, wrt_ref):
    gate1 = m[:, 2 * D_MODEL:3 * D_MODEL]
    shift2 = m[:, 3 * D_MODEL:4 * D_MODEL]
    scale2 = m[:, 4 * D_MODEL:5 * D_MODEL]
    mix = _dot(jnp.concatenate([attn, rg], axis=1), wout_ref[...])
    x1 = _ln_plain(ALPHA * x + gate1 * mix) * g1_ref[...] + b1_ref[...]
    h2 = (_ln_plain(x1) * (1.0 + scale2) + shift2).astype(BF16)
    logits = _dot_nt(wrt_ref[...].astype(BF16), h2)
    e = jnp.exp(logits - jnp.max(logits, axis=0, keepdims=True))
    return x1, h2, e / jnp.sum(e, axis=0, keepdims=True)


def _mixer_kernel(x_ref, mod_ref, win_ref, wout_ref, bd_ref, qg_ref, kg_ref, cw_ref, cb_ref, wg_ref, bg_ref,
                  lam_ref, g1_ref, b1_ref, wrt_ref,
                  nk_ref, nv_ref, nh_ref, x1_ref, h2_ref, afft_ref, *scan_refs):
    m = mod_ref[0:1, :]
    x = x_ref[...]
    h = _modulated(x, m)
    q, k, v = _qkv(h, win_ref, bd_ref, qg_ref, kg_ref, None)
    nk_ref[...] = k
    nv_ref[...] = v
    attn = _attend(q, k.astype(BF16), v.astype(BF16))
    h0 = jnp.zeros((2, RG_WIDTH), F32)
    rg, hf, hb = _rglru(h, win_ref, cw_ref, cb_ref, wg_ref, bg_ref, lam_ref, h0, scan_refs)
    nh_ref[0] = jnp.concatenate([hf, hb], axis=0)
    x1_ref[...], h2_ref[...], afft_ref[...] = _out_and_router(attn, rg, x, m, wout_ref, g1_ref, b1_ref, wrt_ref)


def _mixer_prompt(x2d, mod, win_bf, wout_bf, consts, g1, b1, wrt, *, n, nb):
    bd, qg, kg, cw, cb, wg, bg, lam = consts
    tokens = nb * n
    seq = lambda w: pl.BlockSpec((n, w), lambda b: (b, 0))
    small = [bd, qg, kg, cw, cb, wg, bg, lam, g1, b1, wrt]
    return pl.pallas_call(
        _mixer_kernel,
        grid=(nb,),
        in_specs=[seq(D_MODEL), _full(mod.shape), _full(win_bf.shape), _full(wout_bf.shape)]
                 + [_full(a.shape) for a in small],
        out_specs=[seq(KV_WIDTH), seq(KV_WIDTH), pl.BlockSpec((1, 2, RG_WIDTH), lambda b: (b, 0, 0)),
                   seq(D_MODEL), seq(D_MODEL), pl.BlockSpec((N_EXPERTS, n), lambda b: (b, 0))],
        out_shape=[jax.ShapeDtypeStruct((tokens, KV_WIDTH), F32), jax.ShapeDtypeStruct((tokens, KV_WIDTH), F32),
                   jax.ShapeDtypeStruct((nb, 2, RG_WIDTH), F32),
                   jax.ShapeDtypeStruct((tokens, D_MODEL), F32), jax.ShapeDtypeStruct((tokens, D_MODEL), BF16),
                   jax.ShapeDtypeStruct((nb * N_EXPERTS, n), F32)],
        scratch_shapes=[pltpu.VMEM((n, RG_WIDTH), F32)] * 4,
        compiler_params=_params(),
        name="mixer_prompt",
    )(x2d, mod, win_bf, wout_bf, *small)


def _front_kernel(x_ref, mod_ref, win_ref, bd_ref, qg_ref, kg_ref, cw_ref, cb_ref, wg_ref, bg_ref, lam_ref, h0_ref,
                  cos_ref, sin_ref, q_ref, k_ref, v_ref, rg_ref, *scan_refs):
    m = mod_ref[pl.ds(1 + pl.program_id(0), 1), :]
    h = _modulated(x_ref[...], m)
    q, k, v = _qkv(h, win_ref, bd_ref, qg_ref, kg_ref, (cos_ref[...], sin_ref[...]))
    q_ref[...] = q
    k_ref[...] = k.astype(BF16)
    v_ref[...] = v.astype(BF16)
    rg_ref[...], _, _ = _rglru(h, win_ref, cw_ref, cb_ref, wg_ref, bg_ref, lam_ref, h0_ref[0], scan_refs)


def _front_sample(x2d, mod, win_bf, consts, h0, rope, *, n, nb):
    tokens = nb * n
    seq = lambda w: pl.BlockSpec((n, w), lambda b: (b, 0))
    small = list(consts)
    return pl.pallas_call(
        _front_kernel,
        grid=(nb,),
        in_specs=[seq(D_MODEL), _full(mod.shape), _full(win_bf.shape)] + [_full(a.shape) for a in small]
                 + [pl.BlockSpec((1, 2, RG_WIDTH), lambda b: (b, 0, 0)), _full(rope[0].shape), _full(rope[1].shape)],
        out_specs=[seq(ATTN_WIDTH), seq(KV_WIDTH), seq(KV_WIDTH), seq(RG_WIDTH)],
        out_shape=[jax.ShapeDtypeStruct((tokens, ATTN_WIDTH), BF16), jax.ShapeDtypeStruct((tokens, KV_WIDTH), BF16),
                   jax.ShapeDtypeStruct((tokens, KV_WIDTH), BF16), jax.ShapeDtypeStruct((tokens, RG_WIDTH), BF16)],
        scratch_shapes=[pltpu.VMEM((n, RG_WIDTH), F32)] * 4,
        compiler_params=_params(),
        name="front_sample",
    )(x2d, mod, win_bf, *small, h0, *rope)


def _attn_kernel(q_ref, k_ref, v_ref, kc_ref, vc_ref, o_ref):
    k = jnp.concatenate([k_ref[...], kc_ref[...].astype(BF16)], axis=0)
    v = jnp.concatenate([v_ref[...], vc_ref[...].astype(BF16)], axis=0)
    o_ref[...] = _attend(q_ref[...], k, v)


def _attn_sample(q, k, v, cache_k, cache_v, *, n, nb, past):
    tq = min(n, ATTN_Q_TILE)
    tiles = n // tq
    return pl.pallas_call(
        _attn_kernel,
        grid=(nb, tiles),
        in_specs=[pl.BlockSpec((tq, ATTN_WIDTH), lambda b, t: (b * tiles + t, 0)),
                  pl.BlockSpec((n, KV_WIDTH), lambda b, t: (b, 0)),
                  pl.BlockSpec((n, KV_WIDTH), lambda b, t: (b, 0)),
                  pl.BlockSpec((past, KV_WIDTH), lambda b, t: (b, 0)),
                  pl.BlockSpec((past, KV_WIDTH), lambda b, t: (b, 0))],
        out_specs=pl.BlockSpec((tq, ATTN_WIDTH), lambda b, t: (b * tiles + t, 0)),
        out_shape=jax.ShapeDtypeStruct((nb * n, ATTN_WIDTH), BF16),
        compiler_params=_params(2),
        name="attn_sample",
    )(q, k, v, cache_k, cache_v)


def _post_kernel(attn_ref, rg_ref, x_ref, mod_ref, wout_ref, g1_ref, b1_ref, wrt_ref, x1_ref, h2_ref, afft_ref):
    m = mod_ref[pl.ds(1 + pl.program_id(0), 1), :]
    x1_ref[...], h2_ref[...], afft_ref[...] = _out_and_router(
        attn_ref[...], rg_ref[...], x_ref[...], m, wout_ref, g1_ref, b1_ref, wrt_ref)


def _post_sample(attn, rg, x2d, mod, wout_bf, g1, b1, wrt, *, n, nb):
    tokens = nb * n
    seq = lambda w: pl.BlockSpec((n, w), lambda b: (b, 0))
    return pl.pallas_call(
        _post_kernel,
        grid=(nb,),
        in_specs=[seq(ATTN_WIDTH), seq(RG_WIDTH), seq(D_MODEL), _full(mod.shape), _full(wout_bf.shape),
                  _full(g1.shape), _full(b1.shape), _full(wrt.shape)],
        out_specs=[seq(D_MODEL), seq(D_MODEL), pl.BlockSpec((N_EXPERTS, n), lambda b: (b, 0))],
        out_shape=[jax.ShapeDtypeStruct((tokens, D_MODEL), F32), jax.ShapeDtypeStruct((tokens, D_MODEL), BF16),
                   jax.ShapeDtypeStruct((nb * N_EXPERTS, n), F32)],
        compiler_params=_params(),
        name="post_sample",
    )(attn, rg, x2d, mod, wout_bf, g1, b1, wrt)


def _slot_onehot(mask, pos, cap):
    n = mask.shape[1]
    slot = lax.broadcasted_iota(jnp.int32, (cap, n), 0).astype(F32)
    return [(slot == pos[e:e + 1, :]) & (mask[e:e + 1, :] > 0.5) for e in range(mask.shape[0])]


def _route_kernel(afft_ref, h2_ref, xs_ref, mask_ref, pos_ref, *, n, cap):
    b = pl.program_id(0)

    @pl.when(b == 0)
    def _():
        aff = afft_ref[...]
        thr = jnp.zeros((aff.shape[0], 1), jnp.int32)
        for bit in range(30, -1, -1):
            cand = thr | (1 << bit)
            cnt = jnp.sum((aff >= lax.bitcast_convert_type(cand, F32)).astype(F32), axis=1, keepdims=True)
            thr = jnp.where(cnt >= cap, cand, thr)
        above = aff >= lax.bitcast_convert_type(thr + 1, F32)
        tied = (aff >= lax.bitcast_convert_type(thr, F32)) & jnp.logical_not(above)
        need = cap - jnp.sum(above.astype(F32), axis=1, keepdims=True)
        before = (lax.broadcasted_iota(jnp.int32, (n, n), 0)
                  < lax.broadcasted_iota(jnp.int32, (n, n), 1)).astype(BF16)
        tie_rank = _dot(tied.astype(BF16), before)
        mask = (above | (tied & (tie_rank < need))).astype(F32)
        mask_ref[...] = mask
        pos_ref[...] = _dot(mask.astype(BF16), before)

    r0 = pl.multiple_of(b * N_EXPERTS, N_EXPERTS)
    onehots = _slot_onehot(mask_ref[pl.ds(r0, N_EXPERTS), :], pos_ref[pl.ds(r0, N_EXPERTS), :], cap)
    sel = jnp.concatenate(onehots, axis=0).astype(BF16)
    xs = _dot(sel, h2_ref[...]).astype(BF16)
    xs_ref[...] = xs.reshape(N_EXPERTS, cap, D_MODEL)


def _route(afft, h2, *, n, nb, cap):
    rows = nb * N_EXPERTS
    return pl.pallas_call(
        functools.partial(_route_kernel, n=n, cap=cap),
        grid=(nb,),
        in_specs=[_full(afft.shape), pl.BlockSpec((n, D_MODEL), lambda b: (b, 0))],
        out_specs=[pl.BlockSpec((N_EXPERTS, cap, D_MODEL), lambda b: (0, b, 0)),
                   _full((rows, n)), _full((rows, n))],
        out_shape=[jax.ShapeDtypeStruct((N_EXPERTS, nb * cap, D_MODEL), BF16),
                   jax.ShapeDtypeStruct((rows, n), F32), jax.ShapeDtypeStruct((rows, n), F32)],
        compiler_params=_params(),
        name="route_n%d" % n,
    )(afft, h2)


def _slot_gates(aff_ref, mask_ref, pos_ref, e, cap):
    cols = []
    for b in range(aff_ref.shape[0] // N_EXPERTS):
        r = b * N_EXPERTS + e
        onehot, = _slot_onehot(mask_ref[pl.ds(r, 1), :], pos_ref[pl.ds(r, 1), :], cap)
        cols.append(jnp.sum(jnp.where(onehot, aff_ref[pl.ds(r, 1), :], 0.0), axis=1, keepdims=True))
    return jnp.concatenate(cols, axis=0)


def _ffn_kernel(xp_ref, xs_ref, ap_ref, mp_ref, pp_ref, as_ref, ms_ref, ps_ref, wgu_ref, wd_ref, yp_ref, ys_ref):
    e = pl.program_id(0)
    rows_p = xp_ref.shape[1]
    xs = jnp.concatenate([xp_ref[0], xs_ref[0]], axis=0)
    gu = _dot(xs, wgu_ref[0].astype(BF16))
    gate = gu[:, :D_EXPERT]
    up = gu[:, D_EXPERT:]
    act = (gate * jax.nn.sigmoid(gate) * up).astype(BF16)
    y = _dot(act, wd_ref[0].astype(BF16))
    g = jnp.concatenate([_slot_gates(ap_ref, mp_ref, pp_ref, e, rows_p * N_EXPERTS // ap_ref.shape[0]),
                         _slot_gates(as_ref, ms_ref, ps_ref, e, xs_ref.shape[1] * N_EXPERTS // as_ref.shape[0])],
                        axis=0)
    y = (y * g).astype(BF16)
    yp_ref[0] = y[:rows_p]
    ys_ref[0] = y[rows_p:]


def _ffn(xs_p, xs_s, route_p, route_s, w_gate_up, w_down):
    per_e = lambda a: pl.BlockSpec((1,) + a.shape[1:], lambda e: (e, 0, 0))
    tables = list(route_p) + list(route_s)
    return pl.pallas_call(
        _ffn_kernel,
        grid=(N_EXPERTS,),
        in_specs=[per_e(xs_p), per_e(xs_s)] + [_full(t.shape) for t in tables] + [per_e(w_gate_up), per_e(w_down)],
        out_specs=[per_e(xs_p), per_e(xs_s)],
        out_shape=[jax.ShapeDtypeStruct(xs_p.shape, BF16), jax.ShapeDtypeStruct(xs_s.shape, BF16)],
        compiler_params=_params(),
        name="expert_ffn",
    )(xs_p, xs_s, *tables, w_gate_up, w_down)


def _combine_kernel(y_ref, mask_ref, pos_ref, x1_ref, mod_ref, g2_ref, b2_ref, o_ref, *, cap, mod_row0, mod_row_step):
    row = mod_row0 + mod_row_step * pl.program_id(0)
    gate2 = mod_ref[pl.ds(row, 1), 5 * D_MODEL:6 * D_MODEL]
    sel = jnp.concatenate(_slot_onehot(mask_ref[...], pos_ref[...], cap), axis=0).astype(BF16)
    ff = _dot_tn(sel, y_ref[...].reshape(N_EXPERTS * cap, D_MODEL))
    o_ref[...] = _ln_plain(ALPHA * x1_ref[...] + gate2 * ff) * g2_ref[...] + b2_ref[...]


def _combine(y, mask, pos, x1, mod, g2, b2, *, n, nb, cap, sample):
    tn = min(n, COMBINE_TILE)
    tiles = n // tn
    return pl.pallas_call(
        functools.partial(_combine_kernel, cap=cap, mod_row0=1 if sample else 0, mod_row_step=1 if sample else 0),
        grid=(nb, tiles),
        in_specs=[pl.BlockSpec((N_EXPERTS, cap, D_MODEL), lambda b, t: (0, b, 0)),
                  pl.BlockSpec((N_EXPERTS, tn), lambda b, t: (b, t)),
                  pl.BlockSpec((N_EXPERTS, tn), lambda b, t: (b, t)),
                  pl.BlockSpec((tn, D_MODEL), lambda b, t: (b * tiles + t, 0)),
                  _full(mod.shape), _full(g2.shape), _full(b2.shape)],
        out_specs=pl.BlockSpec((tn, D_MODEL), lambda b, t: (b * tiles + t, 0)),
        out_shape=jax.ShapeDtypeStruct((nb * n, D_MODEL), F32),
        compiler_params=_params(2),
        name="combine_sample" if sample else "combine_prompt",
    )(y, mask, pos, x1, mod, g2, b2)


def _block_diag_256(w):
    per_tile = MXU_DIM // RG_BLOCK
    tiles = []
    for c in range(w.shape[0] // per_tile):
        tiles.append(jax.scipy.linalg.block_diag(*[w[c * per_tile + i] for i in range(per_tile)]))
    return jnp.stack(tiles)


def _rope_tables(n):
    lane = np.arange(LANES)
    within = lane % HEAD_DIM
    freq = (within % 16).astype(np.float32)
    inv = jnp.asarray(ROPE_THETA, F32) ** (-jnp.asarray(freq) / 16.0)
    tok = jnp.arange(n)
    pos = jnp.where(jnp.asarray(within < HEAD_DIM // 2)[None, :], (tok // GRID_W)[:, None], (tok % GRID_W)[:, None])
    ang = pos.astype(F32) * inv[None, :]
    sign = jnp.asarray(np.where(within % 32 < 16, -1.0, 1.0).astype(np.float32))
    return jnp.cos(ang), jnp.sin(ang) * sign[None, :]


def kernel(x_prompt, x_sample, cache_k, cache_v, state_h, c, c_ctx, w_mod, b_mod, w_in, q_norm_g, k_norm_g,
           conv_w, conv_b, w_rg_a, b_rg_a, w_rg_x, b_rg_x, rg_lambda, w_out, ln1_g, ln1_b, w_router,
           w_gate_up, w_down, ln2_g, ln2_b):
    assert w_mod.shape[0] == DEPTH == 1
    nb_p, n_p, _ = x_prompt.shape
    nb_s, n_s, _ = x_sample.shape
    past = cache_k.shape[2]
    cap_p = CAP_FACTOR * n_p // N_EXPERTS
    cap_s = CAP_FACTOR * n_s // N_EXPERTS

    cs = jnp.concatenate([c_ctx[None, :], c, jnp.zeros((MOD_ROWS - 1 - nb_s, D_MODEL), F32)], axis=0)
    row = lambda v: v.reshape(1, -1)
    head_avg = np.kron(np.eye(ATTN_WIDTH // HEAD_DIM, dtype=np.float32),
                       np.full((HEAD_DIM, HEAD_DIM), 1.0 / HEAD_DIM, np.float32))
    bd = jnp.asarray(head_avg, BF16)
    qg = row(jnp.tile(q_norm_g[0], N_Q_HEADS))
    kg = row(jnp.tile(k_norm_g[0], N_KV_HEADS))
    wg = 0.5 * jnp.concatenate([_block_diag_256(w_rg_a[0, 0]), _block_diag_256(w_rg_x[0, 0]),
                                _block_diag_256(w_rg_a[0, 1]), _block_diag_256(w_rg_x[0, 1])], axis=2)
    bg = 0.5 * jnp.stack([b_rg_a[0, 0], b_rg_x[0, 0], b_rg_a[0, 1], b_rg_x[0, 1]])
    consts = (bd, qg, kg, conv_w[0], row(conv_b[0]), wg, bg, rg_lambda[0])
    wrt = w_router[0].T
    g1, b1, g2, b2 = row(ln1_g[0]), row(ln1_b[0]), row(ln2_g[0]), row(ln2_b[0])

    win_bf, wout_bf = _cast_weights(w_in[0], w_out[0])
    mod = _mod_vectors(cs, w_mod[0], row(b_mod[0]))

    xp = x_prompt.reshape(nb_p * n_p, D_MODEL)
    xs = x_sample.reshape(nb_s * n_s, D_MODEL)

    new_k, new_v, new_h, x1_p, h2_p, aff_p = _mixer_prompt(
        xp, mod, win_bf, wout_bf, consts, g1, b1, wrt, n=n_p, nb=nb_p)

    q_s, k_s, v_s, rg_s = _front_sample(xs, mod, win_bf, consts, state_h[:, 0], _rope_tables(n_s), n=n_s, nb=nb_s)
    attn_s = _attn_sample(q_s, k_s, v_s, cache_k[:, 0].reshape(nb_s * past, KV_WIDTH),
                          cache_v[:, 0].reshape(nb_s * past, KV_WIDTH), n=n_s, nb=nb_s, past=past)
    x1_s, h2_s, aff_s = _post_sample(attn_s, rg_s, xs, mod, wout_bf, g1, b1, wrt, n=n_s, nb=nb_s)

    xs_p, mask_p, pos_p = _route(aff_p, h2_p, n=n_p, nb=nb_p, cap=cap_p)
    xs_s, mask_s, pos_s = _route(aff_s, h2_s, n=n_s, nb=nb_s, cap=cap_s)

    y_p, y_s = _ffn(xs_p, xs_s, (aff_p, mask_p, pos_p), (aff_s, mask_s, pos_s), w_gate_up[0], w_down[0])

    out_p = _combine(y_p, mask_p, pos_p, x1_p, mod, g2, b2, n=n_p, nb=nb_p, cap=cap_p, sample=False)
    out_s = _combine(y_s, mask_s, pos_s, x1_s, mod, g2, b2, n=n_s, nb=nb_s, cap=cap_s, sample=True)

    return (out_p.reshape(nb_p, n_p, D_MODEL), out_s.reshape(nb_s, n_s, D_MODEL),
            new_k.reshape(nb_p, DEPTH, n_p, N_KV_HEADS, HEAD_DIM),
            new_v.reshape(nb_p, DEPTH, n_p, N_KV_HEADS, HEAD_DIM),
            new_h.reshape(nb_p, DEPTH, 2, RG_WIDTH))
```

```python
import functools

import numpy as np
import jax
import jax.numpy as jnp
from jax import lax
from jax.experimental import pallas as pl
from jax.experimental.pallas import tpu as pltpu

F32 = jnp.float32
BF16 = jnp.bfloat16

D_MODEL = 1024
HEAD_DIM = 64
N_Q_HEADS = 8
N_KV_HEADS = 2
GROUP = N_Q_HEADS // N_KV_HEADS
ATTN_WIDTH = N_Q_HEADS * HEAD_DIM
KV_WIDTH = N_KV_HEADS * HEAD_DIM
RG_WIDTH = D_MODEL - ATTN_WIDTH
RG_BLOCK = 64
CONV_W = 4
C_LRU = 8.0
N_EXPERTS = 16
CAP_FACTOR = 2
D_EXPERT = 1024
GRID_W = 64
ROPE_THETA = 10000.0
NORM_EPS = 1e-6
DEPTH = 1
ALPHA = (2.0 * DEPTH) ** 0.25
IN_WIDTH = ATTN_WIDTH + 2 * KV_WIDTH + 2 * RG_WIDTH
Q_SCALE = HEAD_DIM ** -0.5 * float(np.log2(np.e))

LANES = 128
SUBLANES = 8
MXU_DIM = 256
VMEM_LIMIT_BYTES = 56 * 1024 * 1024

MOD_ROWS = SUBLANES
MOD_TILE = 128
ATTN_Q_TILE = 256
COMBINE_TILE = 256


def _params(n_axes=1):
    return pltpu.CompilerParams(dimension_semantics=("arbitrary",) * n_axes,
                                vmem_limit_bytes=VMEM_LIMIT_BYTES)


def _full(shape):
    zeros = (0,) * len(shape)
    return pl.BlockSpec(shape, lambda *_: zeros)


def _ln_plain(x):
    mu = jnp.mean(x, -1, keepdims=True)
    xc = x - mu
    var = jnp.mean(xc * xc, -1, keepdims=True)
    return xc * lax.rsqrt(var + NORM_EPS)


def _dot(a, b):
    return jnp.dot(a, b, preferred_element_type=F32)


def _dot_nt(a, b):
    return lax.dot_general(a, b, (((1,), (1,)), ((), ())), preferred_element_type=F32)


def _dot_tn(a, b):
    return lax.dot_general(a, b, (((0,), (0,)), ((), ())), preferred_element_type=F32)


def _split_bf16(x):
    hi = x.astype(BF16)
    lo = (x - hi.astype(F32)).astype(BF16)
    return hi, lo


def _cast_kernel(a_ref, b_ref, ao_ref, bo_ref):
    ao_ref[...] = a_ref[...].astype(BF16)
    bo_ref[...] = b_ref[...].astype(BF16)


def _cast_weights(w_in, w_out):
    return pl.pallas_call(
        _cast_kernel,
        grid=(1,),
        in_specs=[_full(w_in.shape), _full(w_out.shape)],
        out_specs=[_full(w_in.shape), _full(w_out.shape)],
        out_shape=[jax.ShapeDtypeStruct(w_in.shape, BF16), jax.ShapeDtypeStruct(w_out.shape, BF16)],
        compiler_params=_params(),
        name="cast_weights",
    )(w_in, w_out)


def _mod_kernel(cs_ref, w_ref, b_ref, o_ref):
    @pl.when(pl.program_id(0) == 0)
    def _():
        o_ref[...] = jnp.broadcast_to(b_ref[...], o_ref.shape)

    cs = cs_ref[...]
    s = cs * jax.nn.sigmoid(cs)
    o_ref[...] += _dot(s.astype(BF16), w_ref[...].astype(BF16))


def _mod_vectors(cs, w_mod, b_mod):
    width = w_mod.shape[1]
    return pl.pallas_call(
        _mod_kernel,
        grid=(D_MODEL // MOD_TILE,),
        in_specs=[pl.BlockSpec((MOD_ROWS, MOD_TILE), lambda j: (0, j)),
                  pl.BlockSpec((MOD_TILE, width), lambda j: (j, 0)),
                  _full(b_mod.shape)],
        out_specs=_full((MOD_ROWS, width)),
        out_shape=jax.ShapeDtypeStruct((MOD_ROWS, width), F32),
        compiler_params=_params(),
        name="mod_vectors",
    )(cs, w_mod, b_mod)


def _head_mean_sq(x, bd):
    hi, lo = _split_bf16(x * x)
    return _dot(hi, bd) + _dot(lo, bd)


def _rope_lanes(x, cos, sin_signed, hi_half):
    partner = jnp.where(hi_half, pltpu.roll(x, 16, axis=1), pltpu.roll(x, LANES - 16, axis=1))
    return x * cos + partner * sin_signed


def _modulated(x, m):
    return (_ln_plain(x) * (1.0 + m[:, D_MODEL:2 * D_MODEL]) + m[:, 0:D_MODEL]).astype(BF16)


def _qkv(h, win_ref, bd_ref, qg_ref, kg_ref, rope):
    n = h.shape[0]
    bd = bd_ref[...]
    q = _dot(h, win_ref[:, 0:ATTN_WIDTH])
    q = q * lax.rsqrt(_head_mean_sq(q, bd) + NORM_EPS) * qg_ref[...]
    k = _dot(h, win_ref[:, ATTN_WIDTH:ATTN_WIDTH + KV_WIDTH])
    k = k * lax.rsqrt(_head_mean_sq(k, bd[:KV_WIDTH, :KV_WIDTH]) + NORM_EPS) * kg_ref[...]
    v = _dot(h, win_ref[:, ATTN_WIDTH + KV_WIDTH:ATTN_WIDTH + 2 * KV_WIDTH])
    if rope is not None:
        cos, sin_signed = rope
        lane = lax.broadcasted_iota(jnp.int32, (n, LANES), 1)
        hi_half = (lane & 16) != 0
        q = jnp.concatenate(
            [_rope_lanes(q[:, j * LANES:(j + 1) * LANES], cos, sin_signed, hi_half)
             for j in range(ATTN_WIDTH // LANES)], axis=1)
        k = _rope_lanes(k, cos, sin_signed, hi_half)
    return (q * Q_SCALE).astype(BF16), k, v


def _attend(q, k, v):
    outs = []
    for hq in range(N_Q_HEADS):
        kv = hq // GROUP
        qh = q[:, hq * HEAD_DIM:(hq + 1) * HEAD_DIM]
        kh = k[:, kv * HEAD_DIM:(kv + 1) * HEAD_DIM]
        vh = v[:, kv * HEAD_DIM:(kv + 1) * HEAD_DIM]
        s = _dot_nt(qh, kh)
        e = jnp.exp2(s - jnp.max(s, axis=-1, keepdims=True))
        denom = jnp.sum(e, axis=-1, keepdims=True)
        outs.append(_dot(e.astype(BF16), vh) / denom)
    return jnp.concatenate(outs, axis=1).astype(BF16)


def _rglru(h, win_ref, cw_ref, cb_ref, wg_ref, bg_ref, lam_ref, h0, scan_refs):
    af_ref, bf_ref, ab_ref, bb_ref = scan_refs
    n = h.shape[0]
    rg_lo = ATTN_WIDTH + 2 * KV_WIDTH

    xr = _dot(h, win_ref[:, rg_lo:rg_lo + RG_WIDTH])
    t_idx = lax.broadcasted_iota(jnp.int32, (n, 1), 0)
    cw = cw_ref[...]
    xc = jnp.where(t_idx >= 2, pltpu.roll(xr, 2, axis=0), 0.0) * cw[0:1, :]
    xc = xc + jnp.where(t_idx >= 1, pltpu.roll(xr, 1, axis=0), 0.0) * cw[1:2, :]
    xc = xc + xr * cw[2:3, :]
    xc = xc + jnp.where(t_idx < n - 1, pltpu.roll(xr, n - 1, axis=0), 0.0) * cw[3:4, :]
    xc = xc + cb_ref[...]

    xcb = xc.astype(BF16)
    halves = [_dot(xcb[:, c * MXU_DIM:(c + 1) * MXU_DIM], wg_ref[c].astype(BF16))
              for c in range(RG_WIDTH // MXU_DIM)]

    def gate_pre(idx):
        return jnp.concatenate([hv[:, idx * MXU_DIM:(idx + 1) * MXU_DIM] for hv in halves], axis=1) \
            + bg_ref[idx:idx + 1, :]

    neg = -lam_ref[...]
    softplus = jnp.maximum(neg, 0.0) + jnp.log1p(jnp.exp(-jnp.abs(neg)))
    decay = (-0.5 * C_LRU) * softplus
    half_xc = 0.5 * xc

    def coeffs(d):
        r2 = jnp.tanh(gate_pre(2 * d)) + 1.0
        i2 = jnp.tanh(gate_pre(2 * d + 1)) + 1.0
        log_a = r2 * decay[d:d + 1, :]
        a = jnp.exp(log_a)
        bx = jnp.sqrt(jnp.tanh(-log_a) * (a * a + 1.0)) * (i2 * half_xc)
        return a, bx

    groups = n // SUBLANES
    rmod = lax.broadcasted_iota(jnp.int32, (1, SUBLANES, 1), 1)

    def tile_scan(a, bx, forward):
        a = a.reshape(groups, SUBLANES, RG_WIDTH)
        bx = bx.reshape(groups, SUBLANES, RG_WIDTH)
        for s in (1, 2, 4):
            ok = (rmod >= s) if forward else (rmod < SUBLANES - s)
            shift = s if forward else SUBLANES - s
            a_sh = jnp.where(ok, pltpu.roll(a, shift, axis=1), 1.0)
            b_sh = jnp.where(ok, pltpu.roll(bx, shift, axis=1), 0.0)
            bx = a * b_sh + bx
            a = a * a_sh
        return a.reshape(n, RG_WIDTH), bx.reshape(n, RG_WIDTH)

    af_ref[...], bf_ref[...] = tile_scan(*coeffs(0), True)
    ab_ref[...], bb_ref[...] = tile_scan(*coeffs(1), False)

    def step(g, carry):
        hf, hb = carry
        rf = pl.multiple_of(g * SUBLANES, SUBLANES)
        rb = pl.multiple_of((groups - 1 - g) * SUBLANES, SUBLANES)
        new_f = af_ref[pl.ds(rf, SUBLANES), :] * hf + bf_ref[pl.ds(rf, SUBLANES), :]
        new_b = ab_ref[pl.ds(rb, SUBLANES), :] * hb + bb_ref[pl.ds(rb, SUBLANES), :]
        bf_ref[pl.ds(rf, SUBLANES), :] = new_f
        bb_ref[pl.ds(rb, SUBLANES), :] = new_b
        return (jnp.broadcast_to(new_f[SUBLANES - 1:SUBLANES, :], (SUBLANES, RG_WIDTH)),
                jnp.broadcast_to(new_b[0:1, :], (SUBLANES, RG_WIDTH)))

    hf, hb = lax.fori_loop(
        0, groups, step,
        (jnp.broadcast_to(h0[0:1, :], (SUBLANES, RG_WIDTH)), jnp.broadcast_to(h0[1:2, :], (SUBLANES, RG_WIDTH))))

    gr = _dot(h, win_ref[:, rg_lo + RG_WIDTH:IN_WIDTH])
    rg = ((bf_ref[...] + bb_ref[...]) * jax.nn.gelu(gr)).astype(BF16)
    return rg, hf[0:1, :], hb[0:1, :]


def _out_and_router(attn, rg, x, m, wout_ref, g1_ref, b1_ref, wrt_ref):
    gate1 = m[:, 2 * D_MODEL:3 * D_MODEL]
    shift2 = m[:, 3 * D_MODEL:4 * D_MODEL]
    scale2 = m[:, 4 * D_MODEL:5 * D_MODEL]
    mix = _dot(jnp.concatenate([attn, rg], axis=1), wout_ref[...])
    x1 = _ln_plain(ALPHA * x + gate1 * mix) * g1_ref[...] + b1_ref[...]
    h2 = (_ln_plain(x1) * (1.0 + scale2) + shift2).astype(BF16)
    logits = _dot_nt(wrt_ref[...].astype(BF16), h2)
    e = jnp.exp(logits - jnp.max(logits, axis=0, keepdims=True))
    return x1, h2, e / jnp.sum(e, axis=0, keepdims=True)


def _mixer_kernel(x_ref, mod_ref, win_ref, wout_ref, bd_ref, qg_ref, kg_ref, cw_ref, cb_ref, wg_ref, bg_ref,
                  lam_ref, g1_ref, b1_ref, wrt_ref,
                  nk_ref, nv_ref, nh_ref, x1_ref, h2_ref, afft_ref, *scan_refs):
    m = mod_ref[0:1, :]
    x = x_ref[...]
    h = _modulated(x, m)
    q, k, v = _qkv(h, win_ref, bd_ref, qg_ref, kg_ref, None)
    nk_ref[0] = k.T
    nv_ref[0] = v.T
    attn = _attend(q, k.astype(BF16), v.astype(BF16))
    h0 = jnp.zeros((2, RG_WIDTH), F32)
    rg, hf, hb = _rglru(h, win_ref, cw_ref, cb_ref, wg_ref, bg_ref, lam_ref, h0, scan_refs)
    nh_ref[0] = jnp.concatenate([hf, hb], axis=0)
    x1_ref[...], h2_ref[...], afft_ref[...] = _out_and_router(attn, rg, x, m, wout_ref, g1_ref, b1_ref, wrt_ref)


def _mixer_prompt(x2d, mod, win_bf, wout_bf, consts, g1, b1, wrt, *, n, nb):
    bd, qg, kg, cw, cb, wg, bg, lam = consts
    tokens = nb * n
    seq = lambda w: pl.BlockSpec((n, w), lambda b: (b, 0))
    small = [bd, qg, kg, cw, cb, wg, bg, lam, g1, b1, wrt]
    return pl.pallas_call(
        _mixer_kernel,
        grid=(nb,),
        in_specs=[seq(D_MODEL), _full(mod.shape), _full(win_bf.shape), _full(wout_bf.shape)]
                 + [_full(a.shape) for a in small],
        out_specs=[pl.BlockSpec((1, KV_WIDTH, n), lambda b: (b, 0, 0)),
                   pl.BlockSpec((1, KV_WIDTH, n), lambda b: (b, 0, 0)),
                   pl.BlockSpec((1, 2, RG_WIDTH), lambda b: (b, 0, 0)),
                   seq(D_MODEL), seq(D_MODEL), pl.BlockSpec((N_EXPERTS, n), lambda b: (b, 0))],
        out_shape=[jax.ShapeDtypeStruct((nb, KV_WIDTH, n), F32), jax.ShapeDtypeStruct((nb, KV_WIDTH, n), F32),
                   jax.ShapeDtypeStruct((nb, 2, RG_WIDTH), F32),
                   jax.ShapeDtypeStruct((tokens, D_MODEL), F32), jax.ShapeDtypeStruct((tokens, D_MODEL), BF16),
                   jax.ShapeDtypeStruct((nb * N_EXPERTS, n), F32)],
        scratch_shapes=[pltpu.VMEM((n, RG_WIDTH), F32)] * 4,
        compiler_params=_params(),
        name="mixer_prompt",
    )(x2d, mod, win_bf, wout_bf, *small)


def _front_kernel(x_ref, mod_ref, win_ref, bd_ref, qg_ref, kg_ref, cw_ref, cb_ref, wg_ref, bg_ref, lam_ref, h0_ref,
                  cos_ref, sin_ref, q_ref, k_ref, v_ref, rg_ref, *scan_refs):
    m = mod_ref[pl.ds(1 + pl.program_id(0), 1), :]
    h = _modulated(x_ref[...], m)
    q, k, v = _qkv(h, win_ref, bd_ref, qg_ref, kg_ref, (cos_ref[...], sin_ref[...]))
    q_ref[...] = q
    k_ref[...] = k.astype(BF16)
    v_ref[...] = v.astype(BF16)
    rg_ref[...], _, _ = _rglru(h, win_ref, cw_ref, cb_ref, wg_ref, bg_ref, lam_ref, h0_ref[0], scan_refs)


def _front_sample(x2d, mod, win_bf, consts, h0, rope, *, n, nb):
    tokens = nb * n
    seq = lambda w: pl.BlockSpec((n, w), lambda b: (b, 0))
    small = list(consts)
    return pl.pallas_call(
        _front_kernel,
        grid=(nb,),
        in_specs=[seq(D_MODEL), _full(mod.shape), _full(win_bf.shape)] + [_full(a.shape) for a in small]
                 + [pl.BlockSpec((1, 2, RG_WIDTH), lambda b: (b, 0, 0)), _full(rope[0].shape), _full(rope[1].shape)],
        out_specs=[seq(ATTN_WIDTH), seq(KV_WIDTH), seq(KV_WIDTH), seq(RG_WIDTH)],
        out_shape=[jax.ShapeDtypeStruct((tokens, ATTN_WIDTH), BF16), jax.ShapeDtypeStruct((tokens, KV_WIDTH), BF16),
                   jax.ShapeDtypeStruct((tokens, KV_WIDTH), BF16), jax.ShapeDtypeStruct((tokens, RG_WIDTH), BF16)],
        scratch_shapes=[pltpu.VMEM((n, RG_WIDTH), F32)] * 4,
        compiler_params=_params(),
        name="front_sample",
    )(x2d, mod, win_bf, *small, h0, *rope)


def _attn_kernel(q_ref, k_ref, v_ref, kc_ref, vc_ref, o_ref):
    k = jnp.concatenate([k_ref[...], kc_ref[...].astype(BF16)], axis=0)
    v = jnp.concatenate([v_ref[...], vc_ref[...].astype(BF16)], axis=0)
    o_ref[...] = _attend(q_ref[...], k, v)


def _attn_sample(q, k, v, cache_k, cache_v, *, n, nb, past):
    tq = min(n, ATTN_Q_TILE)
    tiles = n // tq
    return pl.pallas_call(
        _attn_kernel,
        grid=(nb, tiles),
        in_specs=[pl.BlockSpec((tq, ATTN_WIDTH), lambda b, t: (b * tiles + t, 0)),
                  pl.BlockSpec((n, KV_WIDTH), lambda b, t: (b, 0)),
                  pl.BlockSpec((n, KV_WIDTH), lambda b, t: (b, 0)),
                  pl.BlockSpec((past, KV_WIDTH), lambda b, t: (b, 0)),
                  pl.BlockSpec((past, KV_WIDTH), lambda b, t: (b, 0))],
        out_specs=pl.BlockSpec((tq, ATTN_WIDTH), lambda b, t: (b * tiles + t, 0)),
        out_shape=jax.ShapeDtypeStruct((nb * n, ATTN_WIDTH), BF16),
        compiler_params=_params(2),
        name="attn_sample",
    )(q, k, v, cache_k, cache_v)


def _post_kernel(attn_ref, rg_ref, x_ref, mod_ref, wout_ref, g1_ref, b1_ref, wrt_ref, x1_ref, h2_ref, afft_ref):
    m = mod_ref[pl.ds(1 + pl.program_id(0), 1), :]
    x1_ref[...], h2_ref[...], afft_ref[...] = _out_and_router(
        attn_ref[...], rg_ref[...], x_ref[...], m, wout_ref, g1_ref, b1_ref, wrt_ref)


def _post_sample(attn, rg, x2d, mod, wout_bf, g1, b1, wrt, *, n, nb):
    tokens = nb * n
    seq = lambda w: pl.BlockSpec((n, w), lambda b: (b, 0))
    return pl.pallas_call(
        _post_kernel,
        grid=(nb,),
        in_specs=[seq(ATTN_WIDTH), seq(RG_WIDTH), seq(D_MODEL), _full(mod.shape), _full(wout_bf.shape),
                  _full(g1.shape), _full(b1.shape), _full(wrt.shape)],
        out_specs=[seq(D_MODEL), seq(D_MODEL), pl.BlockSpec((N_EXPERTS, n), lambda b: (b, 0))],
        out_shape=[jax.ShapeDtypeStruct((tokens, D_MODEL), F32), jax.ShapeDtypeStruct((tokens, D_MODEL), BF16),
                   jax.ShapeDtypeStruct((nb * N_EXPERTS, n), F32)],
        compiler_params=_params(),
        name="post_sample",
    )(attn, rg, x2d, mod, wout_bf, g1, b1, wrt)


def _slot_onehot(mask, pos, cap):
    n = mask.shape[1]
    slot = lax.broadcasted_iota(jnp.int32, (cap, n), 0).astype(F32)
    return [(slot == pos[e:e + 1, :]) & (mask[e:e + 1, :] > 0.5) for e in range(mask.shape[0])]


def _route_kernel(afft_ref, h2_ref, xs_ref, mask_ref, pos_ref, *, n, cap):
    b = pl.program_id(0)

    @pl.when(b == 0)
    def _():
        aff = afft_ref[...]
        thr = jnp.zeros((aff.shape[0], 1), jnp.int32)
        for bit in range(30, -1, -1):
            cand = thr | (1 << bit)
            cnt = jnp.sum((aff >= lax.bitcast_convert_type(cand, F32)).astype(F32), axis=1, keepdims=True)
            thr = jnp.where(cnt >= cap, cand, thr)
        above = aff >= lax.bitcast_convert_type(thr + 1, F32)
        tied = (aff >= lax.bitcast_convert_type(thr, F32)) & jnp.logical_not(above)
        need = cap - jnp.sum(above.astype(F32), axis=1, keepdims=True)
        before = (lax.broadcasted_iota(jnp.int32, (n, n), 0)
                  < lax.broadcasted_iota(jnp.int32, (n, n), 1)).astype(BF16)
        tie_rank = _dot(tied.astype(BF16), before)
        mask = (above | (tied & (tie_rank < need))).astype(F32)
        mask_ref[...] = mask
        pos_ref[...] = _dot(mask.astype(BF16), before)

    r0 = pl.multiple_of(b * N_EXPERTS, N_EXPERTS)
    onehots = _slot_onehot(mask_ref[pl.ds(r0, N_EXPERTS), :], pos_ref[pl.ds(r0, N_EXPERTS), :], cap)
    sel = jnp.concatenate(onehots, axis=0).astype(BF16)
    xs = _dot(sel, h2_ref[...]).astype(BF16)
    xs_ref[...] = xs.reshape(N_EXPERTS, cap, D_MODEL)


def _route(afft, h2, *, n, nb, cap):
    rows = nb * N_EXPERTS
    return pl.pallas_call(
        functools.partial(_route_kernel, n=n, cap=cap),
        grid=(nb,),
        in_specs=[_full(afft.shape), pl.BlockSpec((n, D_MODEL), lambda b: (b, 0))],
        out_specs=[pl.BlockSpec((N_EXPERTS, cap, D_MODEL), lambda b: (0, b, 0)),
                   _full((rows, n)), _full((rows, n))],
        out_shape=[jax.ShapeDtypeStruct((N_EXPERTS, nb * cap, D_MODEL), BF16),
                   jax.ShapeDtypeStruct((rows, n), F32), jax.ShapeDtypeStruct((rows, n), F32)],
        compiler_params=_params(),
        name="route_n%d" % n,
    )(afft, h2)


def _slot_gates(aff_ref, mask_ref, pos_ref, e, cap):
    cols = []
    for b in range(aff_ref.shape[0] // N_EXPERTS):
        r = b * N_EXPERTS + e
        onehot, = _slot_onehot(mask_ref[pl.ds(r, 1), :], pos_ref[pl.ds(r, 1), :], cap)
        cols.append(jnp.sum(jnp.where(onehot, aff_ref[pl.ds(r, 1), :], 0.0), axis=1, keepdims=True))
    return jnp.concatenate(cols, axis=0)


def _ffn_kernel(xp_ref, xs_ref, ap_ref, mp_ref, pp_ref, as_ref, ms_ref, ps_ref, wgu_ref, wd_ref, yp_ref, ys_ref):
    e = pl.program_id(0)
    rows_p = xp_ref.shape[1]
    xs = jnp.concatenate([xp_ref[0], xs_ref[0]], axis=0)
    gu = _dot(xs, wgu_ref[0].astype(BF16))
    gate = gu[:, :D_EXPERT]
    up = gu[:, D_EXPERT:]
    act = (gate * jax.nn.sigmoid(gate) * up).astype(BF16)
    y = _dot(act, wd_ref[0].astype(BF16))
    g = jnp.concatenate([_slot_gates(ap_ref, mp_ref, pp_ref, e, rows_p * N_EXPERTS // ap_ref.shape[0]),
                         _slot_gates(as_ref, ms_ref, ps_ref, e, xs_ref.shape[1] * N_EXPERTS // as_ref.shape[0])],
                        axis=0)
    y = (y * g).astype(BF16)
    yp_ref[0] = y[:rows_p]
    ys_ref[0] = y[rows_p:]


def _ffn(xs_p, xs_s, route_p, route_s, w_gate_up, w_down):
    per_e = lambda a: pl.BlockSpec((1,) + a.shape[1:], lambda e: (e, 0, 0))
    tables = list(route_p) + list(route_s)
    return pl.pallas_call(
        _ffn_kernel,
        grid=(N_EXPERTS,),
        in_specs=[per_e(xs_p), per_e(xs_s)] + [_full(t.shape) for t in tables] + [per_e(w_gate_up), per_e(w_down)],
        out_specs=[per_e(xs_p), per_e(xs_s)],
        out_shape=[jax.ShapeDtypeStruct(xs_p.shape, BF16), jax.ShapeDtypeStruct(xs_s.shape, BF16)],
        compiler_params=_params(),
        name="expert_ffn",
    )(xs_p, xs_s, *tables, w_gate_up, w_down)


def _combine_kernel(y_ref, mask_ref, pos_ref, x1_ref, mod_ref, g2_ref, b2_ref, o_ref, *, cap, mod_row0, mod_row_step):
    row = mod_row0 + mod_row_step * pl.program_id(0)
    gate2 = mod_ref[pl.ds(row, 1), 5 * D_MODEL:6 * D_MODEL]
    sel = jnp.concatenate(_slot_onehot(mask_ref[...], pos_ref[...], cap), axis=0).astype(BF16)
    ff = _dot_tn(sel, y_ref[...].reshape(N_EXPERTS * cap, D_MODEL))
    o_ref[...] = _ln_plain(ALPHA * x1_ref[...] + gate2 * ff) * g2_ref[...] + b2_ref[...]


def _combine(y, mask, pos, x1, mod, g2, b2, *, n, nb, cap, sample):
    tn = min(n, COMBINE_TILE)
    tiles = n // tn
    return pl.pallas_call(
        functools.partial(_combine_kernel, cap=cap, mod_row0=1 if sample else 0, mod_row_step=1 if sample else 0),
        grid=(nb, tiles),
        in_specs=[pl.BlockSpec((N_EXPERTS, cap, D_MODEL), lambda b, t: (0, b, 0)),
                  pl.BlockSpec((N_EXPERTS, tn), lambda b, t: (b, t)),
                  pl.BlockSpec((N_EXPERTS, tn), lambda b, t: (b, t)),
                  pl.BlockSpec((tn, D_MODEL), lambda b, t: (b * tiles + t, 0)),
                  _full(mod.shape), _full(g2.shape), _full(b2.shape)],
        out_specs=pl.BlockSpec((tn, D_MODEL), lambda b, t: (b * tiles + t, 0)),
        out_shape=jax.ShapeDtypeStruct((nb * n, D_MODEL), F32),
        compiler_params=_params(2),
        name="combine_sample" if sample else "combine_prompt",
    )(y, mask, pos, x1, mod, g2, b2)


def _block_diag_256(w):
    per_tile = MXU_DIM // RG_BLOCK
    tiles = []
    for c in range(w.shape[0] // per_tile):
        tiles.append(jax.scipy.linalg.block_diag(*[w[c * per_tile + i] for i in range(per_tile)]))
    return jnp.stack(tiles)


def _rope_tables(n):
    lane = np.arange(LANES)
    within = lane % HEAD_DIM
    freq = (within % 16).astype(np.float32)
    inv = np.float32(ROPE_THETA) ** (-freq / np.float32(16.0))
    tok = np.arange(n)
    pos = np.where((within < HEAD_DIM // 2)[None, :], (tok // GRID_W)[:, None], (tok % GRID_W)[:, None])
    ang = pos.astype(np.float32) * inv[None, :]
    sign = np.where(within % 32 < 16, -1.0, 1.0).astype(np.float32)
    return jnp.asarray(np.cos(ang)), jnp.asarray(np.sin(ang) * sign[None, :])


def kernel(x_prompt, x_sample, cache_k, cache_v, state_h, c, c_ctx, w_mod, b_mod, w_in, q_norm_g, k_norm_g,
           conv_w, conv_b, w_rg_a, b_rg_a, w_rg_x, b_rg_x, rg_lambda, w_out, ln1_g, ln1_b, w_router,
           w_gate_up, w_down, ln2_g, ln2_b):
    assert w_mod.shape[0] == DEPTH == 1
    nb_p, n_p, _ = x_prompt.shape
    nb_s, n_s, _ = x_sample.shape
    past = cache_k.shape[2]
    cap_p = CAP_FACTOR * n_p // N_EXPERTS
    cap_s = CAP_FACTOR * n_s // N_EXPERTS

    cs = jnp.concatenate([c_ctx[None, :], c, jnp.zeros((MOD_ROWS - 1 - nb_s, D_MODEL), F32)], axis=0)
    row = lambda v: v.reshape(1, -1)
    head_avg = np.kron(np.eye(ATTN_WIDTH // HEAD_DIM, dtype=np.float32),
                       np.full((HEAD_DIM, HEAD_DIM), 1.0 / HEAD_DIM, np.float32))
    bd = jnp.asarray(head_avg, BF16)
    qg = row(jnp.tile(q_norm_g[0], N_Q_HEADS))
    kg = row(jnp.tile(k_norm_g[0], N_KV_HEADS))
    wg = 0.5 * jnp.concatenate([_block_diag_256(w_rg_a[0, 0]), _block_diag_256(w_rg_x[0, 0]),
                                _block_diag_256(w_rg_a[0, 1]), _block_diag_256(w_rg_x[0, 1])], axis=2)
    bg = 0.5 * jnp.stack([b_rg_a[0, 0], b_rg_x[0, 0], b_rg_a[0, 1], b_rg_x[0, 1]])
    consts = (bd, qg, kg, conv_w[0], row(conv_b[0]), wg, bg, rg_lambda[0])
    wrt = w_router[0].T
    g1, b1, g2, b2 = row(ln1_g[0]), row(ln1_b[0]), row(ln2_g[0]), row(ln2_b[0])

    win_bf, wout_bf = _cast_weights(w_in[0], w_out[0])
    mod = _mod_vectors(cs, w_mod[0], row(b_mod[0]))

    xp = x_prompt.reshape(nb_p * n_p, D_MODEL)
    xs = x_sample.reshape(nb_s * n_s, D_MODEL)

    new_k, new_v, new_h, x1_p, h2_p, aff_p = _mixer_prompt(
        xp, mod, win_bf, wout_bf, consts, g1, b1, wrt, n=n_p, nb=nb_p)

    q_s, k_s, v_s, rg_s = _front_sample(xs, mod, win_bf, consts, state_h[:, 0], _rope_tables(n_s), n=n_s, nb=nb_s)
    attn_s = _attn_sample(q_s, k_s, v_s, cache_k[:, 0].reshape(nb_s * past, KV_WIDTH),
                          cache_v[:, 0].reshape(nb_s * past, KV_WIDTH), n=n_s, nb=nb_s, past=past)
    x1_s, h2_s, aff_s = _post_sample(attn_s, rg_s, xs, mod, wout_bf, g1, b1, wrt, n=n_s, nb=nb_s)

    xs_p, mask_p, pos_p = _route(aff_p, h2_p, n=n_p, nb=nb_p, cap=cap_p)
    xs_s, mask_s, pos_s = _route(aff_s, h2_s, n=n_s, nb=nb_s, cap=cap_s)

    y_p, y_s = _ffn(xs_p, xs_s, (aff_p, mask_p, pos_p), (aff_s, mask_s, pos_s), w_gate_up[0], w_down[0])

    out_p = _combine(y_p, mask_p, pos_p, x1_p, mod, g2, b2, n=n_p, nb=nb_p, cap=cap_p, sample=False)
    out_s = _combine(y_s, mask_s, pos_s, x1_s, mod, g2, b2, n=n_s, nb=nb_s, cap=cap_s, sample=True)

    def cache_layout(t):
        t = t.reshape(nb_p, DEPTH, N_KV_HEADS, HEAD_DIM, n_p)
        return jnp.transpose(t, (0, 1, 4, 2, 3))

    return (out_p.reshape(nb_p, n_p, D_MODEL), out_s.reshape(nb_s, n_s, D_MODEL),
            cache_layout(new_k), cache_layout(new_v), new_h.reshape(nb_p, DEPTH, 2, RG_WIDTH))
```

```python
import functools

import numpy as np
import jax
import jax.numpy as jnp
from jax import lax
from jax.experimental import pallas as pl
from jax.experimental.pallas import tpu as pltpu

F32 = jnp.float32
BF16 = jnp.bfloat16

D_MODEL = 1024
HEAD_DIM = 64
N_Q_HEADS = 8
N_KV_HEADS = 2
GROUP = N_Q_HEADS // N_KV_HEADS
ATTN_WIDTH = N_Q_HEADS * HEAD_DIM
KV_WIDTH = N_KV_HEADS * HEAD_DIM
RG_WIDTH = D_MODEL - ATTN_WIDTH
RG_BLOCK = 64
CONV_W = 4
C_LRU = 8.0
N_EXPERTS = 16
CAP_FACTOR = 2
D_EXPERT = 1024
GRID_W = 64
ROPE_THETA = 10000.0
NORM_EPS = 1e-6
DEPTH = 1
ALPHA = (2.0 * DEPTH) ** 0.25
IN_WIDTH = ATTN_WIDTH + 2 * KV_WIDTH + 2 * RG_WIDTH
Q_SCALE = HEAD_DIM ** -0.5 * float(np.log2(np.e))

LANES = 128
SUBLANES = 8
MXU_DIM = 256
VMEM_LIMIT_BYTES = 56 * 1024 * 1024

MOD_ROWS = SUBLANES
MOD_TILE = 128
ATTN_Q_TILE = 256
COMBINE_TILE = 256
MIXER_ORDER = (1, 0, 0, 1, 1, 0, 1, 1, 0, 1, 0, 1, 0, 1, 1, 0, 1, 1, 0, 0, 1, 0, 1, 0, 0, 0, 1, 0, 0, 1, 0)


def _params(n_axes=1, flags=None):
    return pltpu.CompilerParams(dimension_semantics=("arbitrary",) * n_axes,
                                vmem_limit_bytes=VMEM_LIMIT_BYTES, flags=flags)


def _full(shape):
    zeros = (0,) * len(shape)
    return pl.BlockSpec(shape, lambda *_: zeros)


def _ln_plain(x):
    mu = jnp.mean(x, -1, keepdims=True)
    xc = x - mu
    var = jnp.mean(xc * xc, -1, keepdims=True)
    return xc * lax.rsqrt(var + NORM_EPS)


def _dot(a, b):
    return jnp.dot(a, b, preferred_element_type=F32)


def _dot_nt(a, b):
    return lax.dot_general(a, b, (((1,), (1,)), ((), ())), preferred_element_type=F32)


def _dot_tn(a, b):
    return lax.dot_general(a, b, (((0,), (0,)), ((), ())), preferred_element_type=F32)


def _split_bf16(x):
    hi = x.astype(BF16)
    lo = (x - hi.astype(F32)).astype(BF16)
    return hi, lo


def _cast_kernel(a_ref, b_ref, ao_ref, bo_ref):
    ao_ref[...] = a_ref[...].astype(BF16)
    bo_ref[...] = b_ref[...].astype(BF16)


def _cast_weights(w_in, w_out):
    return pl.pallas_call(
        _cast_kernel,
        grid=(1,),
        in_specs=[_full(w_in.shape), _full(w_out.shape)],
        out_specs=[_full(w_in.shape), _full(w_out.shape)],
        out_shape=[jax.ShapeDtypeStruct(w_in.shape, BF16), jax.ShapeDtypeStruct(w_out.shape, BF16)],
        compiler_params=_params(),
        name="cast_weights",
    )(w_in, w_out)


def _mod_kernel(cs_ref, w_ref, b_ref, o_ref):
    @pl.when(pl.program_id(0) == 0)
    def _():
        o_ref[...] = jnp.broadcast_to(b_ref[...], o_ref.shape)

    cs = cs_ref[...]
    s = cs * jax.nn.sigmoid(cs)
    o_ref[...] += _dot(s.astype(BF16), w_ref[...].astype(BF16))


def _mod_vectors(cs, w_mod, b_mod):
    width = w_mod.shape[1]
    return pl.pallas_call(
        _mod_kernel,
        grid=(D_MODEL // MOD_TILE,),
        in_specs=[pl.BlockSpec((MOD_ROWS, MOD_TILE), lambda j: (0, j)),
                  pl.BlockSpec((MOD_TILE, width), lambda j: (j, 0)),
                  _full(b_mod.shape)],
        out_specs=_full((MOD_ROWS, width)),
        out_shape=jax.ShapeDtypeStruct((MOD_ROWS, width), F32),
        compiler_params=_params(),
        name="mod_vectors",
    )(cs, w_mod, b_mod)


def _head_mean_sq(x, bd):
    hi, lo = _split_bf16(x * x)
    return _dot(hi, bd) + _dot(lo, bd)


def _rope_lanes(x, cos, sin_signed, hi_half):
    partner = jnp.where(hi_half, pltpu.roll(x, 16, axis=1), pltpu.roll(x, LANES - 16, axis=1))
    return x * cos + partner * sin_signed


def _modulated(x, m):
    return (_ln_plain(x) * (1.0 + m[:, D_MODEL:2 * D_MODEL]) + m[:, 0:D_MODEL]).astype(BF16)


def _qkv(h, win_ref, bd_ref, qg_ref, kg_ref, rope):
    n = h.shape[0]
    bd = bd_ref[...]
    q = _dot(h, win_ref[:, 0:ATTN_WIDTH])
    yield
    q = q * lax.rsqrt(_head_mean_sq(q, bd) + NORM_EPS) * qg_ref[...]
    yield
    k = _dot(h, win_ref[:, ATTN_WIDTH:ATTN_WIDTH + KV_WIDTH])
    k = k * lax.rsqrt(_head_mean_sq(k, bd[:KV_WIDTH, :KV_WIDTH]) + NORM_EPS) * kg_ref[...]
    v = _dot(h, win_ref[:, ATTN_WIDTH + KV_WIDTH:ATTN_WIDTH + 2 * KV_WIDTH])
    yield
    if rope is not None:
        cos, sin_signed = rope
        lane = lax.broadcasted_iota(jnp.int32, (n, LANES), 1)
        hi_half = (lane & 16) != 0
        q = jnp.concatenate(
            [_rope_lanes(q[:, j * LANES:(j + 1) * LANES], cos, sin_signed, hi_half)
             for j in range(ATTN_WIDTH // LANES)], axis=1)
        k = _rope_lanes(k, cos, sin_signed, hi_half)
    return (q * Q_SCALE).astype(BF16), k, v


def _drive(*streams, order=()):
    results = [None] * len(streams)
    live = list(range(len(streams)))
    plan = [j for j in order]
    while live:
        idx = plan.pop(0) if plan else live[0]
        if idx not in live:
            continue
        if not plan:
            live.append(live.pop(0))
        try:
            next(streams[idx])
        except StopIteration as stop:
            results[idx] = stop.value
            live.remove(idx)
    return results


def _attend(q, k, v):
    outs = []
    for hq in range(N_Q_HEADS):
        kv = hq // GROUP
        qh = q[:, hq * HEAD_DIM:(hq + 1) * HEAD_DIM]
        kh = k[:, kv * HEAD_DIM:(kv + 1) * HEAD_DIM]
        vh = v[:, kv * HEAD_DIM:(kv + 1) * HEAD_DIM]
        s = _dot_nt(qh, kh)
        e = jnp.exp2(s - jnp.max(s, axis=-1, keepdims=True))
        denom = jnp.sum(e, axis=-1, keepdims=True)
        outs.append(_dot(e.astype(BF16), vh) / denom)
        yield
    return jnp.concatenate(outs, axis=1).astype(BF16)


def _rg_inputs(h, win_ref):
    rg_lo = ATTN_WIDTH + 2 * KV_WIDTH
    xr = _dot(h, win_ref[:, rg_lo:rg_lo + RG_WIDTH])
    yield
    gr = _dot(h, win_ref[:, rg_lo + RG_WIDTH:IN_WIDTH])
    yield
    return xr, gr


def _rglru(xr, gr, cw_ref, cb_ref, wg_ref, bg_ref, lam_ref, h0, scan_refs, unroll):
    af_ref, bf_ref, ab_ref, bb_ref = scan_refs
    n = xr.shape[0]

    t_idx = lax.broadcasted_iota(jnp.int32, (n, 1), 0)
    cw = cw_ref[...]
    xc = jnp.where(t_idx >= 2, pltpu.roll(xr, 2, axis=0), 0.0) * cw[0:1, :]
    xc = xc + jnp.where(t_idx >= 1, pltpu.roll(xr, 1, axis=0), 0.0) * cw[1:2, :]
    xc = xc + xr * cw[2:3, :]
    xc = xc + jnp.where(t_idx < n - 1, pltpu.roll(xr, n - 1, axis=0), 0.0) * cw[3:4, :]
    xc = xc + cb_ref[...]
    yield

    xcb = xc.astype(BF16)
    halves = [_dot(xcb[:, c * MXU_DIM:(c + 1) * MXU_DIM], wg_ref[c].astype(BF16))
              for c in range(RG_WIDTH // MXU_DIM)]

    def gate_pre(idx):
        return jnp.concatenate([hv[:, idx * MXU_DIM:(idx + 1) * MXU_DIM] for hv in halves], axis=1) \
            + bg_ref[idx:idx + 1, :]

    yield
    neg = -lam_ref[...]
    softplus =jnp.maximum(neg, 0.0) + jnp.log1p(jnp.exp(-jnp.abs(neg)))
    decay = (-0.5 * C_LRU) * softplus
    half_xc = 0.5 * xc

    def coeffs(d):
        r2 = jnp.tanh(gate_pre(2 * d)) + 1.0
        i2 = jnp.tanh(gate_pre(2 * d + 1)) + 1.0
        log_a = r2 * decay[d:d + 1, :]
        a = jnp.exp(log_a)
        bx = jnp.sqrt(jnp.tanh(-log_a) * (a * a + 1.0)) * (i2 * half_xc)
        return a, bx

    groups = n // SUBLANES
    rmod = lax.broadcasted_iota(jnp.int32, (1, SUBLANES, 1), 1)

    def tile_scan(d, forward):
        a, bx = coeffs(d)
        yield
        a = a.reshape(groups, SUBLANES, RG_WIDTH)
        bx = bx.reshape(groups, SUBLANES, RG_WIDTH)
        for s in (1, 2, 4):
            ok = (rmod >= s) if forward else (rmod < SUBLANES - s)
            shift = s if forward else SUBLANES - s
            a_sh = jnp.where(ok, pltpu.roll(a, shift, axis=1), 1.0)
            b_sh = jnp.where(ok, pltpu.roll(bx, shift, axis=1), 0.0)
            bx = a * b_sh + bx
            a = a * a_sh
            yield
        return a.reshape(n, RG_WIDTH), bx.reshape(n, RG_WIDTH)

    af_ref[...], bf_ref[...] = yield from tile_scan(0, True)
    ab_ref[...], bb_ref[...] = yield from tile_scan(1, False)

    def step(g, carry):
        hf, hb = carry
        rf = pl.multiple_of(g * SUBLANES, SUBLANES)
        rb = pl.multiple_of((groups - 1 - g) * SUBLANES, SUBLANES)
        new_f = af_ref[pl.ds(rf, SUBLANES), :] * hf + bf_ref[pl.ds(rf, SUBLANES), :]
        new_b = ab_ref[pl.ds(rb, SUBLANES), :] * hb + bb_ref[pl.ds(rb, SUBLANES), :]
        bf_ref[pl.ds(rf, SUBLANES), :] = new_f
        bb_ref[pl.ds(rb, SUBLANES), :] = new_b
        return (jnp.broadcast_to(new_f[SUBLANES - 1:SUBLANES, :], (SUBLANES, RG_WIDTH)),
                jnp.broadcast_to(new_b[0:1, :], (SUBLANES, RG_WIDTH)))

    hf, hb = lax.fori_loop(
        0, groups, step,
        (jnp.broadcast_to(h0[0:1, :], (SUBLANES, RG_WIDTH)), jnp.broadcast_to(h0[1:2, :], (SUBLANES, RG_WIDTH))),
        unroll=unroll)
    yield

    rg =((bf_ref[...] + bb_ref[...]) * jax.nn.gelu(gr)).astype(BF16)
    return rg, hf[0:1, :], hb[0:1, :]


def _out_and_router(attn, rg, x, m, wout_ref, g1_ref, b1_ref, wrt_ref):
    gate1 = m[:, 2 * D_MODEL:3 * D_MODEL]
    shift2 = m[:, 3 * D_MODEL:4 * D_MODEL]
    scale2 = m[:, 4 * D_MODEL:5 * D_MODEL]
    mix = _dot(jnp.concatenate([attn, rg], axis=1), wout_ref[...])
    yield
    x1 = _ln_plain(ALPHA * x + gate1 * mix) * g1_ref[...] + b1_ref[...]
    yield
    h2 = (_ln_plain(x1) * (1.0 + scale2) + shift2).astype(BF16)
    yield
    logits = _dot_nt(wrt_ref[...].astype(BF16), h2)
    e = jnp.exp(logits - jnp.max(logits, axis=0, keepdims=True))
    return x1, h2, e / jnp.sum(e, axis=0, keepdims=True)


def _mixer_kernel(x_ref, mod_ref, win_ref, wout_ref, bd_ref, qg_ref, kg_ref, cw_ref, cb_ref, wg_ref, bg_ref,
                  lam_ref, g1_ref, b1_ref, wrt_ref,
                  nk_ref, nv_ref, nh_ref, x1_ref, h2_ref, afft_ref,
                  xprev_ref, xr_ref, gr_ref, attn_ref, *scan_refs):
    @pl.when(pl.program_id(0) == 0)
    def _():
        xprev_ref[...] = jnp.zeros_like(xprev_ref)
        xr_ref[...] = jnp.zeros_like(xr_ref)
        gr_ref[...] = jnp.zeros_like(gr_ref)
        attn_ref[...] = jnp.zeros_like(attn_ref)

    m = mod_ref[0:1, :]

    def second_half():
        h0 = jnp.zeros((2, RG_WIDTH), F32)
        rg, hf, hb = yield from _rglru(xr_ref[...], gr_ref[...], cw_ref, cb_ref, wg_ref, bg_ref, lam_ref, h0,
                                       scan_refs, True)
        nh_ref[0] = jnp.concatenate([hf, hb], axis=0)
        x1_ref[...], h2_ref[...], afft_ref[...] = yield from _out_and_router(
            attn_ref[...], rg, xprev_ref[...], m, wout_ref, g1_ref, b1_ref, wrt_ref)

    def first_half():
        x = x_ref[...]
        h = _modulated(x, m)
        yield
        q, k, v = yield from _qkv(h, win_ref, bd_ref, qg_ref, kg_ref, None)
        nk_ref[0] = k.T
        nv_ref[0] = v.T
        yield
        xr, gr = yield from _rg_inputs(h, win_ref)
        attn = yield from _attend(q, k.astype(BF16), v.astype(BF16))
        return x, attn, xr, gr

    (x, attn, xr, gr), _ = _drive(first_half(), second_half(), order=MIXER_ORDER)
    xprev_ref[...] = x
    attn_ref[...] = attn
    xr_ref[...] = xr
    gr_ref[...] = gr


def _mixer_prompt(x2d, mod, win_bf, wout_bf, consts, g1, b1, wrt, *, n, nb):
    bd, qg, kg, cw, cb, wg, bg, lam = consts
    tokens = nb * n
    first = lambda w: pl.BlockSpec((n, w), lambda i: (jnp.minimum(i, nb - 1), 0))
    second = lambda w: pl.BlockSpec((n, w), lambda i: (jnp.maximum(i - 1, 0), 0))
    small = [bd, qg, kg, cw, cb, wg, bg, lam, g1, b1, wrt]
    return pl.pallas_call(
        _mixer_kernel,
        grid=(nb + 1,),
        in_specs=[first(D_MODEL), _full(mod.shape), _full(win_bf.shape), _full(wout_bf.shape)]
                 + [_full(a.shape) for a in small],
        out_specs=[pl.BlockSpec((1, KV_WIDTH, n), lambda i: (jnp.minimum(i, nb - 1), 0, 0)),
                   pl.BlockSpec((1, KV_WIDTH, n), lambda i: (jnp.minimum(i, nb - 1), 0, 0)),
                   pl.BlockSpec((1, 2, RG_WIDTH), lambda i: (jnp.maximum(i - 1, 0), 0, 0)),
                   second(D_MODEL), second(D_MODEL),
                   pl.BlockSpec((N_EXPERTS, n), lambda i: (jnp.maximum(i - 1, 0), 0))],
        out_shape=[jax.ShapeDtypeStruct((nb, KV_WIDTH, n), F32), jax.ShapeDtypeStruct((nb, KV_WIDTH, n), F32),
                   jax.ShapeDtypeStruct((nb, 2, RG_WIDTH), F32),
                   jax.ShapeDtypeStruct((tokens, D_MODEL), F32), jax.ShapeDtypeStruct((tokens, D_MODEL), BF16),
                   jax.ShapeDtypeStruct((nb * N_EXPERTS, n), F32)],
        scratch_shapes=[pltpu.VMEM((n, D_MODEL), F32), pltpu.VMEM((n, RG_WIDTH), F32),
                        pltpu.VMEM((n, RG_WIDTH), F32), pltpu.VMEM((n, ATTN_WIDTH), BF16)]
                       + [pltpu.VMEM((n, RG_WIDTH), F32)] * 4,
        compiler_params=_params(),
        name="mixer_prompt",
    )(x2d, mod, win_bf, wout_bf, *small)


def _front_kernel(x_ref, mod_ref, win_ref, bd_ref, qg_ref, kg_ref, cw_ref, cb_ref, wg_ref, bg_ref, lam_ref, h0_ref,
                  cos_ref, sin_ref, q_ref, k_ref, v_ref, rg_ref, *scan_refs):
    m = mod_ref[pl.ds(1 + pl.program_id(0), 1), :]
    h = _modulated(x_ref[...], m)
    (q, k, v), = _drive(_qkv(h, win_ref, bd_ref, qg_ref, kg_ref, (cos_ref[...], sin_ref[...])))
    q_ref[...] = q
    k_ref[...] = k.astype(BF16)
    v_ref[...] = v.astype(BF16)
    (xr, gr), = _drive(_rg_inputs(h, win_ref))
    (rg_ref[...], _, _), = _drive(_rglru(xr, gr, cw_ref, cb_ref, wg_ref, bg_ref, lam_ref, h0_ref[0], scan_refs, 1))


def _front_sample(x2d, mod, win_bf, consts, h0, rope, *, n, nb):
    tokens = nb * n
    seq = lambda w: pl.BlockSpec((n, w), lambda b: (b, 0))
    small = list(consts)
    return pl.pallas_call(
        _front_kernel,
        grid=(nb,),
        in_specs=[seq(D_MODEL), _full(mod.shape), _full(win_bf.shape)] + [_full(a.shape) for a in small]
                 + [pl.BlockSpec((1, 2, RG_WIDTH), lambda b: (b, 0, 0)), _full(rope[0].shape), _full(rope[1].shape)],
        out_specs=[seq(ATTN_WIDTH), seq(KV_WIDTH), seq(KV_WIDTH), seq(RG_WIDTH)],
        out_shape=[jax.ShapeDtypeStruct((tokens, ATTN_WIDTH), BF16), jax.ShapeDtypeStruct((tokens, KV_WIDTH), BF16),
                   jax.ShapeDtypeStruct((tokens, KV_WIDTH), BF16), jax.ShapeDtypeStruct((tokens, RG_WIDTH), BF16)],
        scratch_shapes=[pltpu.VMEM((n, RG_WIDTH), F32)] * 4,
        compiler_params=_params(),
        name="front_sample",
    )(x2d, mod, win_bf, *small, h0, *rope)


def _attn_kernel(q_ref, k_ref, v_ref, kc_ref, vc_ref, o_ref):
    k = jnp.concatenate([k_ref[...], kc_ref[...].astype(BF16)], axis=0)
    v = jnp.concatenate([v_ref[...], vc_ref[...].astype(BF16)], axis=0)
    o_ref[...], = _drive(_attend(q_ref[...], k, v))


def _attn_sample(q, k, v, cache_k, cache_v, *, n, nb, past):
    tq = min(n, ATTN_Q_TILE)
    tiles = n // tq
    return pl.pallas_call(
        _attn_kernel,
        grid=(nb, tiles),
        in_specs=[pl.BlockSpec((tq, ATTN_WIDTH), lambda b, t: (b * tiles + t, 0)),
                  pl.BlockSpec((n, KV_WIDTH), lambda b, t: (b, 0)),
                  pl.BlockSpec((n, KV_WIDTH), lambda b, t: (b, 0)),
                  pl.BlockSpec((past, KV_WIDTH), lambda b, t: (b, 0)),
                  pl.BlockSpec((past, KV_WIDTH), lambda b, t: (b, 0))],
        out_specs=pl.BlockSpec((tq, ATTN_WIDTH), lambda b, t: (b * tiles + t, 0)),
        out_shape=jax.ShapeDtypeStruct((nb * n, ATTN_WIDTH), BF16),
        compiler_params=_params(2),
        name="attn_sample",
    )(q, k, v, cache_k, cache_v)


def _post_kernel(attn_ref, rg_ref, x_ref, mod_ref, wout_ref, g1_ref, b1_ref, wrt_ref, x1_ref, h2_ref, afft_ref):
    m = mod_ref[pl.ds(1 + pl.program_id(0), 1), :]
    (x1_ref[...], h2_ref[...], afft_ref[...]), = _drive(_out_and_router(
        attn_ref[...], rg_ref[...], x_ref[...], m, wout_ref, g1_ref, b1_ref, wrt_ref))


def _post_sample(attn, rg, x2d, mod, wout_bf, g1, b1, wrt, *, n, nb):
    tokens = nb * n
    seq = lambda w: pl.BlockSpec((n, w), lambda b: (b, 0))
    return pl.pallas_call(
        _post_kernel,
        grid=(nb,),
        in_specs=[seq(ATTN_WIDTH), seq(RG_WIDTH), seq(D_MODEL), _full(mod.shape), _full(wout_bf.shape),
                  _full(g1.shape), _full(b1.shape), _full(wrt.shape)],
        out_specs=[seq(D_MODEL), seq(D_MODEL), pl.BlockSpec((N_EXPERTS, n), lambda b: (b, 0))],
        out_shape=[jax.ShapeDtypeStruct((tokens, D_MODEL), F32), jax.ShapeDtypeStruct((tokens, D_MODEL), BF16),
                   jax.ShapeDtypeStruct((nb * N_EXPERTS, n), F32)],
        compiler_params=_params(),
        name="post_sample",
    )(attn, rg, x2d, mod, wout_bf, g1, b1, wrt)


def _slot_onehot(mask, pos, cap):
    n = mask.shape[1]
    slot = lax.broadcasted_iota(jnp.int32, (cap, n), 0).astype(F32)
    return [(slot == pos[e:e + 1, :]) & (mask[e:e + 1, :] > 0.5) for e in range(mask.shape[0])]


def _route_kernel(afft_ref, h2_ref, xs_ref, mask_ref, pos_ref, *, n, cap):
    b = pl.program_id(0)

    @pl.when(b == 0)
    def _():
        aff = afft_ref[...]
        thr = jnp.zeros((aff.shape[0], 1), jnp.int32)
        for bit in range(30, -1, -1):
            cand = thr | (1 << bit)
            cnt = jnp.sum((aff >= lax.bitcast_convert_type(cand, F32)).astype(F32), axis=1, keepdims=True)
            thr = jnp.where(cnt >= cap, cand, thr)
        above = aff >= lax.bitcast_convert_type(thr + 1, F32)
        tied = (aff >= lax.bitcast_convert_type(thr, F32)) & jnp.logical_not(above)
        need = cap - jnp.sum(above.astype(F32), axis=1, keepdims=True)
        before = (lax.broadcasted_iota(jnp.int32, (n, n), 0)
                  < lax.broadcasted_iota(jnp.int32, (n, n), 1)).astype(BF16)
        tie_rank = _dot(tied.astype(BF16), before)
        mask = (above | (tied & (tie_rank < need))).astype(F32)
        mask_ref[...] = mask
        pos_ref[...] = _dot(mask.astype(BF16), before)

    r0 = pl.multiple_of(b * N_EXPERTS, N_EXPERTS)
    onehots = _slot_onehot(mask_ref[pl.ds(r0, N_EXPERTS), :], pos_ref[pl.ds(r0, N_EXPERTS), :], cap)
    sel = jnp.concatenate(onehots, axis=0).astype(BF16)
    xs = _dot(sel, h2_ref[...]).astype(BF16)
    xs_ref[...] = xs.reshape(N_EXPERTS, cap, D_MODEL)


def _route(afft, h2, *, n, nb, cap):
    rows = nb * N_EXPERTS
    return pl.pallas_call(
        functools.partial(_route_kernel, n=n, cap=cap),
        grid=(nb,),
        in_specs=[_full(afft.shape), pl.BlockSpec((n, D_MODEL), lambda b: (b, 0))],
        out_specs=[pl.BlockSpec((N_EXPERTS, cap, D_MODEL), lambda b: (0, b, 0)),
                   _full((rows, n)), _full((rows, n))],
        out_shape=[jax.ShapeDtypeStruct((N_EXPERTS, nb * cap, D_MODEL), BF16),
                   jax.ShapeDtypeStruct((rows, n), F32), jax.ShapeDtypeStruct((rows, n), F32)],
        compiler_params=_params(),
        name="route_n%d" % n,
    )(afft, h2)


def _slot_gates(aff_ref, mask_ref, pos_ref, e, cap):
    cols = []
    for b in range(aff_ref.shape[0] // N_EXPERTS):
        r = b * N_EXPERTS + e
        onehot, = _slot_onehot(mask_ref[pl.ds(r, 1), :], pos_ref[pl.ds(r, 1), :], cap)
        cols.append(jnp.sum(jnp.where(onehot, aff_ref[pl.ds(r, 1), :], 0.0), axis=1, keepdims=True))
    return jnp.concatenate(cols, axis=0)


def _ffn_kernel(xp_ref, xs_ref, ap_ref, mp_ref, pp_ref, as_ref, ms_ref, ps_ref, wgu_ref, wd_ref, yp_ref, ys_ref):
    e = pl.program_id(0)
    rows_p = xp_ref.shape[1]
    xs = jnp.concatenate([xp_ref[0], xs_ref[0]], axis=0)
    gu = _dot(xs, wgu_ref[0].astype(BF16))
    gate = gu[:, :D_EXPERT]
    up = gu[:, D_EXPERT:]
    act = (gate * jax.nn.sigmoid(gate) * up).astype(BF16)
    y = _dot(act, wd_ref[0].astype(BF16))
    g = jnp.concatenate([_slot_gates(ap_ref, mp_ref, pp_ref, e, rows_p * N_EXPERTS // ap_ref.shape[0]),
                         _slot_gates(as_ref, ms_ref, ps_ref, e, xs_ref.shape[1] * N_EXPERTS // as_ref.shape[0])],
                        axis=0)
    y = (y * g).astype(BF16)
    yp_ref[0] = y[:rows_p]
    ys_ref[0] = y[rows_p:]


def _ffn(xs_p, xs_s, route_p, route_s, w_gate_up, w_down):
    per_e = lambda a: pl.BlockSpec((1,) + a.shape[1:], lambda e: (e, 0, 0))
    tables = list(route_p) + list(route_s)
    return pl.pallas_call(
        _ffn_kernel,
        grid=(N_EXPERTS,),
        in_specs=[per_e(xs_p), per_e(xs_s)] + [_full(t.shape) for t in tables] + [per_e(w_gate_up), per_e(w_down)],
        out_specs=[per_e(xs_p), per_e(xs_s)],
        out_shape=[jax.ShapeDtypeStruct(xs_p.shape, BF16), jax.ShapeDtypeStruct(xs_s.shape, BF16)],
        compiler_params=_params(),
        name="expert_ffn",
    )(xs_p, xs_s, *tables, w_gate_up, w_down)


def _combine_kernel(y_ref, mask_ref, pos_ref, x1_ref, mod_ref, g2_ref, b2_ref, o_ref, *, cap, mod_row0, mod_row_step):
    row = mod_row0 + mod_row_step * pl.program_id(0)
    gate2 = mod_ref[pl.ds(row, 1), 5 * D_MODEL:6 * D_MODEL]
    sel = jnp.concatenate(_slot_onehot(mask_ref[...], pos_ref[...], cap), axis=0).astype(BF16)
    ff = _dot_tn(sel, y_ref[...].reshape(N_EXPERTS * cap, D_MODEL))
    o_ref[...] = _ln_plain(ALPHA * x1_ref[...] + gate2 * ff) * g2_ref[...] + b2_ref[...]


def _combine(y, mask, pos, x1, mod, g2, b2, *, n, nb, cap, sample):
    tn = min(n, COMBINE_TILE)
    tiles = n // tn
    return pl.pallas_call(
        functools.partial(_combine_kernel, cap=cap, mod_row0=1 if sample else 0, mod_row_step=1 if sample else 0),
        grid=(nb, tiles),
        in_specs=[pl.BlockSpec((N_EXPERTS, cap, D_MODEL), lambda b, t: (0, b, 0)),
                  pl.BlockSpec((N_EXPERTS, tn), lambda b, t: (b, t)),
                  pl.BlockSpec((N_EXPERTS, tn), lambda b, t: (b, t)),
                  pl.BlockSpec((tn, D_MODEL), lambda b, t: (b * tiles + t, 0)),
                  _full(mod.shape), _full(g2.shape), _full(b2.shape)],
        out_specs=pl.BlockSpec((tn, D_MODEL), lambda b, t: (b * tiles + t, 0)),
        out_shape=jax.ShapeDtypeStruct((nb * n, D_MODEL), F32),
        compiler_params=_params(2),
        name="combine_sample" if sample else "combine_prompt",
    )(y, mask, pos, x1, mod, g2, b2)


def _block_diag_256(w):
    per_tile = MXU_DIM // RG_BLOCK
    tiles = []
    for c in range(w.shape[0] // per_tile):
        tiles.append(jax.scipy.linalg.block_diag(*[w[c * per_tile + i] for i in range(per_tile)]))
    return jnp.stack(tiles)


def _rope_tables(n):
    lane = np.arange(LANES)
    within = lane % HEAD_DIM
    freq = (within % 16).astype(np.float32)
    inv = np.float32(ROPE_THETA) ** (-freq / np.float32(16.0))
    tok = np.arange(n)
    pos = np.where((within < HEAD_DIM // 2)[None, :], (tok // GRID_W)[:, None], (tok % GRID_W)[:, None])
    ang = pos.astype(np.float32) * inv[None, :]
    sign = np.where(within % 32 < 16, -1.0, 1.0).astype(np.float32)
    return jnp.asarray(np.cos(ang)), jnp.asarray(np.sin(ang) * sign[None, :])


def kernel(x_prompt, x_sample, cache_k, cache_v, state_h, c, c_ctx, w_mod, b_mod, w_in, q_norm_g, k_norm_g,
           conv_w, conv_b, w_rg_a, b_rg_a, w_rg_x, b_rg_x, rg_lambda, w_out, ln1_g, ln1_b, w_router,
           w_gate_up, w_down, ln2_g, ln2_b):
    assert w_mod.shape[0] == DEPTH == 1
    nb_p, n_p, _ = x_prompt.shape
    nb_s, n_s, _ = x_sample.shape
    past = cache_k.shape[2]
    cap_p = CAP_FACTOR * n_p // N_EXPERTS
    cap_s = CAP_FACTOR * n_s // N_EXPERTS

    cs = jnp.concatenate([c_ctx[None, :], c, jnp.zeros((MOD_ROWS - 1 - nb_s, D_MODEL), F32)], axis=0)
    row = lambda v: v.reshape(1, -1)
    head_avg = np.kron(np.eye(ATTN_WIDTH // HEAD_DIM, dtype=np.float32),
                       np.full((HEAD_DIM, HEAD_DIM), 1.0 / HEAD_DIM, np.float32))
    bd = jnp.asarray(head_avg, BF16)
    qg = row(jnp.tile(q_norm_g[0], N_Q_HEADS))
    kg = row(jnp.tile(k_norm_g[0], N_KV_HEADS))
    wg = 0.5 * jnp.concatenate([_block_diag_256(w_rg_a[0, 0]), _block_diag_256(w_rg_x[0, 0]),
                                _block_diag_256(w_rg_a[0, 1]), _block_diag_256(w_rg_x[0, 1])], axis=2)
    bg = 0.5 * jnp.stack([b_rg_a[0, 0], b_rg_x[0, 0], b_rg_a[0, 1], b_rg_x[0, 1]])
    consts = (bd, qg, kg, conv_w[0], row(conv_b[0]), wg, bg, rg_lambda[0])
    wrt = w_router[0].T
    g1, b1, g2, b2 = row(ln1_g[0]), row(ln1_b[0]), row(ln2_g[0]), row(ln2_b[0])

    win_bf, wout_bf = _cast_weights(w_in[0], w_out[0])
    mod = _mod_vectors(cs, w_mod[0], row(b_mod[0]))

    xp = x_prompt.reshape(nb_p * n_p, D_MODEL)
    xs = x_sample.reshape(nb_s * n_s, D_MODEL)

    new_k, new_v, new_h, x1_p, h2_p, aff_p = _mixer_prompt(
        xp, mod, win_bf, wout_bf, consts, g1, b1, wrt, n=n_p, nb=nb_p)

    q_s, k_s, v_s, rg_s = _front_sample(xs, mod, win_bf, consts, state_h[:, 0], _rope_tables(n_s), n=n_s, nb=nb_s)
    attn_s = _attn_sample(q_s, k_s, v_s, cache_k[:, 0].reshape(nb_s * past, KV_WIDTH),
                          cache_v[:, 0].reshape(nb_s * past, KV_WIDTH), n=n_s, nb=nb_s, past=past)
    x1_s, h2_s, aff_s = _post_sample(attn_s, rg_s, xs, mod, wout_bf, g1, b1, wrt, n=n_s, nb=nb_s)

    xs_p, mask_p, pos_p = _route(aff_p, h2_p, n=n_p, nb=nb_p, cap=cap_p)
    xs_s, mask_s, pos_s = _route(aff_s, h2_s, n=n_s, nb=nb_s, cap=cap_s)

    y_p, y_s = _ffn(xs_p, xs_s, (aff_p, mask_p, pos_p), (aff_s, mask_s, pos_s), w_gate_up[0], w_down[0])

    out_p = _combine(y_p, mask_p, pos_p, x1_p, mod, g2, b2, n=n_p, nb=nb_p, cap=cap_p, sample=False)
    out_s = _combine(y_s, mask_s, pos_s, x1_s, mod, g2, b2, n=n_s, nb=nb_s, cap=cap_s, sample=True)

    def cache_layout(t):
        t = t.reshape(nb_p, DEPTH, N_KV_HEADS, HEAD_DIM, n_p)
        return jnp.transpose(t, (0, 1, 4, 2, 3))

    return (out_p.reshape(nb_p, n_p, D_MODEL), out_s.reshape(nb_s, n_s, D_MODEL),
            cache_layout(new_k), cache_layout(new_v), new_h.reshape(nb_p, DEPTH, 2, RG_WIDTH))
```

```python
import functools

import numpy as np
import jax
import jax.numpy as jnp
from jax import lax
from jax.experimental import pallas as pl
from jax.experimental.pallas import tpu as pltpu

F32 = jnp.float32
BF16 = jnp.bfloat16

D_MODEL = 1024
HEAD_DIM = 64
N_Q_HEADS = 8
N_KV_HEADS = 2
GROUP = N_Q_HEADS // N_KV_HEADS
ATTN_WIDTH = N_Q_HEADS * HEAD_DIM
KV_WIDTH = N_KV_HEADS * HEAD_DIM
RG_WIDTH = D_MODEL - ATTN_WIDTH
RG_BLOCK = 64
CONV_W = 4
C_LRU = 8.0
N_EXPERTS = 16
CAP_FACTOR = 2
D_EXPERT = 1024
GRID_W = 64
ROPE_THETA = 10000.0
NORM_EPS = 1e-6
DEPTH = 1
ALPHA = (2.0 * DEPTH) ** 0.25
IN_WIDTH = ATTN_WIDTH + 2 * KV_WIDTH + 2 * RG_WIDTH
Q_SCALE = HEAD_DIM ** -0.5 * float(np.log2(np.e))

LANES = 128
SUBLANES = 8
MXU_DIM = 256
VMEM_LIMIT_BYTES = 56 * 1024 * 1024

MOD_ROWS = SUBLANES
MOD_TILE = 128
ATTN_Q_TILE = 256
COMBINE_TILE = 256
MIXER_ORDER = (1, 0, 0, 1, 1, 0, 1, 1, 0, 1, 0, 1, 0, 1, 1, 0, 1, 1, 0, 0, 1, 0, 1, 0, 0, 0, 1, 0, 0, 1, 0)
FRONT_ORDER = (1, 1, 1, 1)
POST_ROW_STREAMS = 4
PROMPT_SEQS_PER_STEP = 2


def _params(n_axes=1, flags=None):
    return pltpu.CompilerParams(dimension_semantics=("arbitrary",) * n_axes,
                                vmem_limit_bytes=VMEM_LIMIT_BYTES, flags=flags)


def _full(shape):
    zeros = (0,) * len(shape)
    return pl.BlockSpec(shape, lambda *_: zeros)


def _ln_plain(x):
    mu = jnp.mean(x, -1, keepdims=True)
    xc = x - mu
    var = jnp.mean(xc * xc, -1, keepdims=True)
    return xc * lax.rsqrt(var + NORM_EPS)


def _dot(a, b):
    return jnp.dot(a, b, preferred_element_type=F32)


def _dot_nt(a, b):
    return lax.dot_general(a, b, (((1,), (1,)), ((), ())), preferred_element_type=F32)


def _dot_tn(a, b):
    return lax.dot_general(a, b, (((0,), (0,)), ((), ())), preferred_element_type=F32)


def _split_bf16(x):
    hi = x.astype(BF16)
    lo = (x - hi.astype(F32)).astype(BF16)
    return hi, lo


def _cast_kernel(a_ref, b_ref, ao_ref, bo_ref):
    ao_ref[...] = a_ref[...].astype(BF16)
    bo_ref[...] = b_ref[...].astype(BF16)


def _cast_weights(w_in, w_out):
    return pl.pallas_call(
        _cast_kernel,
        grid=(1,),
        in_specs=[_full(w_in.shape), _full(w_out.shape)],
        out_specs=[_full(w_in.shape), _full(w_out.shape)],
        out_shape=[jax.ShapeDtypeStruct(w_in.shape, BF16), jax.ShapeDtypeStruct(w_out.shape, BF16)],
        compiler_params=_params(),
        name="cast_weights",
    )(w_in, w_out)


def _mod_kernel(cs_ref, w_ref, b_ref, o_ref):
    @pl.when(pl.program_id(0) == 0)
    def _():
        o_ref[...] = jnp.broadcast_to(b_ref[...], o_ref.shape)

    cs = cs_ref[...]
    s = cs * jax.nn.sigmoid(cs)
    o_ref[...] += _dot(s.astype(BF16), w_ref[...].astype(BF16))


def _mod_vectors(cs, w_mod, b_mod):
    width = w_mod.shape[1]
    return pl.pallas_call(
        _mod_kernel,
        grid=(D_MODEL // MOD_TILE,),
        in_specs=[pl.BlockSpec((MOD_ROWS, MOD_TILE), lambda j: (0, j)),
                  pl.BlockSpec((MOD_TILE, width), lambda j: (j, 0)),
                  _full(b_mod.shape)],
        out_specs=_full((MOD_ROWS, width)),
        out_shape=jax.ShapeDtypeStruct((MOD_ROWS, width), F32),
        compiler_params=_params(),
        name="mod_vectors",
    )(cs, w_mod, b_mod)


def _head_mean_sq(x, bd):
    hi, lo = _split_bf16(x * x)
    return _dot(hi, bd) + _dot(lo, bd)


def _rope_lanes(x, cos, sin_signed, hi_half):
    partner = jnp.where(hi_half, pltpu.roll(x, 16, axis=1), pltpu.roll(x, LANES - 16, axis=1))
    return x * cos + partner * sin_signed


def _modulated(x, m):
    return (_ln_plain(x) * (1.0 + m[:, D_MODEL:2 * D_MODEL]) + m[:, 0:D_MODEL]).astype(BF16)


def _qkv(h, win_ref, bd_ref, qg_ref, kg_ref, rope):
    n = h.shape[0]
    bd = bd_ref[...]
    q = _dot(h, win_ref[:, 0:ATTN_WIDTH])
    yield
    q = q * lax.rsqrt(_head_mean_sq(q, bd) + NORM_EPS) * qg_ref[...]
    yield
    k = _dot(h, win_ref[:, ATTN_WIDTH:ATTN_WIDTH + KV_WIDTH])
    k = k * lax.rsqrt(_head_mean_sq(k, bd[:KV_WIDTH, :KV_WIDTH]) + NORM_EPS) * kg_ref[...]
    v = _dot(h, win_ref[:, ATTN_WIDTH + KV_WIDTH:ATTN_WIDTH + 2 * KV_WIDTH])
    yield
    if rope is not None:
        cos, sin_signed = rope
        lane = lax.broadcasted_iota(jnp.int32, (n, LANES), 1)
        hi_half = (lane & 16) != 0
        q = jnp.concatenate(
            [_rope_lanes(q[:, j * LANES:(j + 1) * LANES], cos, sin_signed, hi_half)
             for j in range(ATTN_WIDTH // LANES)], axis=1)
        k = _rope_lanes(k, cos, sin_signed, hi_half)
    return (q * Q_SCALE).astype(BF16), k, v


def _drive(*streams, order=()):
    results = [None] * len(streams)
    live = list(range(len(streams)))
    plan = [j for j in order]
    while live:
        idx = plan.pop(0) if plan else live[0]
        if idx not in live:
            continue
        if not plan:
            live.append(live.pop(0))
        try:
            next(streams[idx])
        except StopIteration as stop:
            results[idx] = stop.value
            live.remove(idx)
    return results


def _attend(q, k, v, lookahead):
    def scores(hq):
        kv = hq // GROUP
        return _dot_nt(q[:, hq * HEAD_DIM:(hq + 1) * HEAD_DIM], k[:, kv * HEAD_DIM:(kv + 1) * HEAD_DIM])

    outs = []
    s_next = scores(0) if lookahead else None
    for hq in range(N_Q_HEADS):
        if lookahead:
            s, s_next = s_next, (scores(hq + 1) if hq + 1 < N_Q_HEADS else None)
        else:
            s = scores(hq)
        kv = hq // GROUP
        e = jnp.exp2(s - jnp.max(s, axis=-1, keepdims=True))
        denom = jnp.sum(e, axis=-1, keepdims=True)
        pv = _dot(e.astype(BF16), v[:, kv * HEAD_DIM:(kv + 1) * HEAD_DIM])
        outs.append(pv / denom)
        yield
    return jnp.concatenate(outs, axis=1).astype(BF16)


def _rg_inputs(h, win_ref):
    rg_lo = ATTN_WIDTH + 2 * KV_WIDTH
    xr = _dot(h, win_ref[:, rg_lo:rg_lo + RG_WIDTH])
    yield
    gr = _dot(h, win_ref[:, rg_lo + RG_WIDTH:IN_WIDTH])
    yield
    return xr, gr


def _rglru(xr, gr, cw_ref, cb_ref, wg_ref, bg_ref, lam_ref, h0, scan_refs, unroll):
    af_ref, bf_ref, ab_ref, bb_ref = scan_refs
    n = xr.shape[0]

    t_idx = lax.broadcasted_iota(jnp.int32, (n, 1), 0)
    cw = cw_ref[...]
    xc = jnp.where(t_idx >= 2, pltpu.roll(xr, 2, axis=0), 0.0) * cw[0:1, :]
    xc = xc + jnp.where(t_idx >= 1, pltpu.roll(xr, 1, axis=0), 0.0) * cw[1:2, :]
    xc = xc + xr * cw[2:3, :]
    xc = xc + jnp.where(t_idx < n - 1, pltpu.roll(xr, n - 1, axis=0), 0.0) * cw[3:4, :]
    xc = xc + cb_ref[...]
    yield

    xcb = xc.astype(BF16)
    halves = [_dot(xcb[:, c * MXU_DIM:(c + 1) * MXU_DIM], wg_ref[c].astype(BF16))
              for c in range(RG_WIDTH // MXU_DIM)]

    def gate_pre(idx):
        return jnp.concatenate([hv[:, idx * MXU_DIM:(idx + 1) * MXU_DIM] for hv in halves], axis=1) \
            + bg_ref[idx:idx + 1, :]

    yield
    neg = -lam_ref[...]
    softplus =jnp.maximum(neg, 0.0) + jnp.log1p(jnp.exp(-jnp.abs(neg)))
    decay = (-0.5 * C_LRU) * softplus
    half_xc = 0.5 * xc

    def coeffs(d):
        r2 = jnp.tanh(gate_pre(2 * d)) + 1.0
        i2 = jnp.tanh(gate_pre(2 * d + 1)) + 1.0
        log_a = r2 * decay[d:d + 1, :]
        a = jnp.exp(log_a)
        bx = jnp.sqrt(jnp.tanh(-log_a) * (a * a + 1.0)) * (i2 * half_xc)
        return a, bx

    groups = n // SUBLANES
    rmod = lax.broadcasted_iota(jnp.int32, (1, SUBLANES, 1), 1)

    def tile_scan(d, forward):
        a, bx = coeffs(d)
        yield
        a = a.reshape(groups, SUBLANES, RG_WIDTH)
        bx = bx.reshape(groups, SUBLANES, RG_WIDTH)
        for s in (1, 2, 4):
            ok = (rmod >= s) if forward else (rmod < SUBLANES - s)
            shift = s if forward else SUBLANES - s
            a_sh = jnp.where(ok, pltpu.roll(a, shift, axis=1), 1.0)
            b_sh = jnp.where(ok, pltpu.roll(bx, shift, axis=1), 0.0)
            bx = a * b_sh + bx
            a = a * a_sh
            yield
        return a.reshape(n, RG_WIDTH), bx.reshape(n, RG_WIDTH)

    af_ref[...], bf_ref[...] = yield from tile_scan(0, True)
    ab_ref[...], bb_ref[...] = yield from tile_scan(1, False)

    def step(g, carry):
        hf, hb = carry
        rf = pl.multiple_of(g * SUBLANES, SUBLANES)
        rb = pl.multiple_of((groups - 1 - g) * SUBLANES, SUBLANES)
        new_f = af_ref[pl.ds(rf, SUBLANES), :] * hf + bf_ref[pl.ds(rf, SUBLANES), :]
        new_b = ab_ref[pl.ds(rb, SUBLANES), :] * hb + bb_ref[pl.ds(rb, SUBLANES), :]
        bf_ref[pl.ds(rf, SUBLANES), :] = new_f
        bb_ref[pl.ds(rb, SUBLANES), :] = new_b
        return (jnp.broadcast_to(new_f[SUBLANES - 1:SUBLANES, :], (SUBLANES, RG_WIDTH)),
                jnp.broadcast_to(new_b[0:1, :], (SUBLANES, RG_WIDTH)))

    hf, hb = lax.fori_loop(
        0, groups, step,
        (jnp.broadcast_to(h0[0:1, :], (SUBLANES, RG_WIDTH)), jnp.broadcast_to(h0[1:2, :], (SUBLANES, RG_WIDTH))),
        unroll=unroll)
    yield

    rg =((bf_ref[...] + bb_ref[...]) * jax.nn.gelu(gr)).astype(BF16)
    return rg, hf[0:1, :], hb[0:1, :]


def _out_and_router(attn, rg, x, m, wout_ref, g1_ref, b1_ref, wrt_ref):
    gate1 = m[:, 2 * D_MODEL:3 * D_MODEL]
    shift2 = m[:, 3 * D_MODEL:4 * D_MODEL]
    scale2 = m[:, 4 * D_MODEL:5 * D_MODEL]
    mix = _dot(jnp.concatenate([attn, rg], axis=1), wout_ref[...])
    yield
    x1 = _ln_plain(ALPHA * x + gate1 * mix) * g1_ref[...] + b1_ref[...]
    yield
    h2 = (_ln_plain(x1) * (1.0 + scale2) + shift2).astype(BF16)
    yield
    logits = _dot_nt(wrt_ref[...].astype(BF16), h2)
    e = jnp.exp(logits - jnp.max(logits, axis=0, keepdims=True))
    return x1, h2, e / jnp.sum(e, axis=0, keepdims=True)


def _mixer_kernel(x_ref, mod_ref, win_ref, wout_ref, bd_ref, qg_ref, kg_ref, cw_ref, cb_ref, wg_ref, bg_ref,
                  lam_ref, g1_ref, b1_ref, wrt_ref,
                  nk_ref, nv_ref, nh_ref, x1_ref, h2_ref, afft_ref,
                  xprev_ref, xr_ref, gr_ref, attn_ref, *scan_refs):
    @pl.when(pl.program_id(0) == 0)
    def _():
        xprev_ref[...] = jnp.zeros_like(xprev_ref)
        xr_ref[...] = jnp.zeros_like(xr_ref)
        gr_ref[...] = jnp.zeros_like(gr_ref)
        attn_ref[...] = jnp.zeros_like(attn_ref)

    m = mod_ref[0:1, :]

    def second_half():
        h0 = jnp.zeros((2, RG_WIDTH), F32)
        rg, hf, hb = yield from _rglru(xr_ref[...], gr_ref[...], cw_ref, cb_ref, wg_ref, bg_ref, lam_ref, h0,
                                       scan_refs, True)
        nh_ref[0] = jnp.concatenate([hf, hb], axis=0)
        x1_ref[...], h2_ref[...], afft_ref[...] = yield from _out_and_router(
            attn_ref[...], rg, xprev_ref[...], m, wout_ref, g1_ref, b1_ref, wrt_ref)

    def first_half():
        x = x_ref[...]
        h = _modulated(x, m)
        yield
        q, k, v = yield from _qkv(h, win_ref, bd_ref, qg_ref, kg_ref, None)
        nk_ref[0] = k.T
        nv_ref[0] = v.T
        yield
        xr, gr = yield from _rg_inputs(h, win_ref)
        attn = yield from _attend(q, k.astype(BF16), v.astype(BF16), False)
        return x, attn, xr, gr

    (x, attn, xr, gr), _ = _drive(first_half(), second_half(), order=MIXER_ORDER)
    xprev_ref[...] = x
    attn_ref[...] = attn
    xr_ref[...] = xr
    gr_ref[...] = gr


def _mixer_prompt(x2d, mod, win_bf, wout_bf, consts, g1, b1, wrt, *, n, nb):
    bd, qg, kg, cw, cb, wg, bg, lam = consts
    tokens = nb * n
    first = lambda w: pl.BlockSpec((n, w), lambda i: (jnp.minimum(i, nb - 1), 0))
    second = lambda w: pl.BlockSpec((n, w), lambda i: (jnp.maximum(i - 1, 0), 0))
    small = [bd, qg, kg, cw, cb, wg, bg, lam, g1, b1, wrt]
    return pl.pallas_call(
        _mixer_kernel,
        grid=(nb + 1,),
        in_specs=[first(D_MODEL), _full(mod.shape), _full(win_bf.shape), _full(wout_bf.shape)]
                 + [_full(a.shape) for a in small],
        out_specs=[pl.BlockSpec((1, KV_WIDTH, n), lambda i: (jnp.minimum(i, nb - 1), 0, 0)),
                   pl.BlockSpec((1, KV_WIDTH, n), lambda i: (jnp.minimum(i, nb - 1), 0, 0)),
                   pl.BlockSpec((1, 2, RG_WIDTH), lambda i: (jnp.maximum(i - 1, 0), 0, 0)),
                   second(D_MODEL), second(D_MODEL),
                   pl.BlockSpec((N_EXPERTS, n), lambda i: (jnp.maximum(i - 1, 0), 0))],
        out_shape=[jax.ShapeDtypeStruct((nb, KV_WIDTH, n), F32), jax.ShapeDtypeStruct((nb, KV_WIDTH, n), F32),
                   jax.ShapeDtypeStruct((nb, 2, RG_WIDTH), F32),
                   jax.ShapeDtypeStruct((tokens, D_MODEL), F32), jax.ShapeDtypeStruct((tokens, D_MODEL), BF16),
                   jax.ShapeDtypeStruct((nb * N_EXPERTS, n), F32)],
        scratch_shapes=[pltpu.VMEM((n, D_MODEL), F32), pltpu.VMEM((n, RG_WIDTH), F32),
                        pltpu.VMEM((n, RG_WIDTH), F32), pltpu.VMEM((n, ATTN_WIDTH), BF16)]
                       + [pltpu.VMEM((n, RG_WIDTH), F32)] * 4,
        compiler_params=_params(),
        name="mixer_prompt",
    )(x2d, mod, win_bf, wout_bf, *small)


def _front_kernel(x_ref, mod_ref, win_ref, bd_ref, qg_ref, kg_ref, cw_ref, cb_ref, wg_ref, bg_ref, lam_ref, h0_ref,
                  cos_ref, sin_ref, q_ref, k_ref, v_ref, rg_ref, *scan_refs):
    m = mod_ref[pl.ds(1 + pl.program_id(0), 1), :]
    h = _modulated(x_ref[...], m)

    def attention_inputs():
        q, k, v = yield from _qkv(h, win_ref, bd_ref, qg_ref, kg_ref, (cos_ref[...], sin_ref[...]))
        q_ref[...] = q
        k_ref[...] = k.astype(BF16)
        v_ref[...] = v.astype(BF16)

    def recurrent_branch():
        xr, gr = yield from _rg_inputs(h, win_ref)
        rg_ref[...], _, _ = yield from _rglru(xr, gr, cw_ref, cb_ref, wg_ref, bg_ref, lam_ref, h0_ref[0],
                                              scan_refs, 1)

    _drive(recurrent_branch(), attention_inputs(), order=FRONT_ORDER)


def _front_sample(x2d, mod, win_bf, consts, h0, rope, *, n, nb):
    tokens = nb * n
    seq = lambda w: pl.BlockSpec((n, w), lambda b: (b, 0))
    small = list(consts)
    return pl.pallas_call(
        _front_kernel,
        grid=(nb,),
        in_specs=[seq(D_MODEL), _full(mod.shape), _full(win_bf.shape)] + [_full(a.shape) for a in small]
                 + [pl.BlockSpec((1, 2, RG_WIDTH), lambda b: (b, 0, 0)), _full(rope[0].shape), _full(rope[1].shape)],
        out_specs=[seq(ATTN_WIDTH), seq(KV_WIDTH), seq(KV_WIDTH), seq(RG_WIDTH)],
        out_shape=[jax.ShapeDtypeStruct((tokens, ATTN_WIDTH), BF16), jax.ShapeDtypeStruct((tokens, KV_WIDTH), BF16),
                   jax.ShapeDtypeStruct((tokens, KV_WIDTH), BF16), jax.ShapeDtypeStruct((tokens, RG_WIDTH), BF16)],
        scratch_shapes=[pltpu.VMEM((n, RG_WIDTH), F32)] * 4,
        compiler_params=_params(),
        name="front_sample",
    )(x2d, mod, win_bf, *small, h0, *rope)


def _attn_kernel(q_ref, k_ref, v_ref, kc_ref, vc_ref, o_ref):
    k = jnp.concatenate([k_ref[...], kc_ref[...].astype(BF16)], axis=0)
    v = jnp.concatenate([v_ref[...], vc_ref[...].astype(BF16)], axis=0)
    o_ref[...], = _drive(_attend(q_ref[...], k, v, True))


def _attn_sample(q, k, v, cache_k, cache_v, *, n, nb, past):
    tq = min(n, ATTN_Q_TILE)
    tiles = n // tq
    return pl.pallas_call(
        _attn_kernel,
        grid=(nb, tiles),
        in_specs=[pl.BlockSpec((tq, ATTN_WIDTH), lambda b, t: (b * tiles + t, 0)),
                  pl.BlockSpec((n, KV_WIDTH), lambda b, t: (b, 0)),
                  pl.BlockSpec((n, KV_WIDTH), lambda b, t: (b, 0)),
                  pl.BlockSpec((past, KV_WIDTH), lambda b, t: (b, 0)),
                  pl.BlockSpec((past, KV_WIDTH), lambda b, t: (b, 0))],
        out_specs=pl.BlockSpec((tq, ATTN_WIDTH), lambda b, t: (b * tiles + t, 0)),
        out_shape=jax.ShapeDtypeStruct((nb * n, ATTN_WIDTH), BF16),
        compiler_params=_params(2),
        name="attn_sample",
    )(q, k, v, cache_k, cache_v)


def _post_kernel(attn_ref, rg_ref, x_ref, mod_ref, wout_ref, g1_ref, b1_ref, wrt_ref, x1_ref, h2_ref, afft_ref):
    m = mod_ref[pl.ds(1 + pl.program_id(0), 1), :]
    rows = x_ref.shape[0] // POST_ROW_STREAMS

    def row_block(j):
        r = pl.ds(j * rows, rows)
        x1_ref[r, :], h2_ref[r, :], afft_ref[:, r] = yield from _out_and_router(
            attn_ref[r, :], rg_ref[r, :], x_ref[r, :], m, wout_ref, g1_ref, b1_ref, wrt_ref)

    stagger = [j for i in range(POST_ROW_STREAMS) for j in range(i + 1)]
    _drive(*[row_block(j) for j in range(POST_ROW_STREAMS)], order=stagger)


def _post_sample(attn, rg, x2d, mod, wout_bf, g1, b1, wrt, *, n, nb):
    tokens = nb * n
    seq = lambda w: pl.BlockSpec((n, w), lambda b: (b, 0))
    return pl.pallas_call(
        _post_kernel,
        grid=(nb,),
        in_specs=[seq(ATTN_WIDTH), seq(RG_WIDTH), seq(D_MODEL), _full(mod.shape), _full(wout_bf.shape),
                  _full(g1.shape), _full(b1.shape), _full(wrt.shape)],
        out_specs=[seq(D_MODEL), seq(D_MODEL), pl.BlockSpec((N_EXPERTS, n), lambda b: (b, 0))],
        out_shape=[jax.ShapeDtypeStruct((tokens, D_MODEL), F32), jax.ShapeDtypeStruct((tokens, D_MODEL), BF16),
                   jax.ShapeDtypeStruct((nb * N_EXPERTS, n), F32)],
        compiler_params=_params(),
        name="post_sample",
    )(attn, rg, x2d, mod, wout_bf, g1, b1, wrt)


def _slot_onehot(mask, pos, cap):
    n = mask.shape[1]
    slot = lax.broadcasted_iota(jnp.int32, (cap, n), 0).astype(F32)
    return [(slot == pos[e:e + 1, :]) & (mask[e:e + 1, :] > 0.5) for e in range(mask.shape[0])]


def _route_kernel(afft_ref, h2_ref, xs_ref, mask_ref, pos_ref, *, n, cap, seqs):
    b = pl.program_id(0)

    @pl.when(b == 0)
    def _():
        aff = afft_ref[...]
        thr = jnp.zeros((aff.shape[0], 1), jnp.int32)
        for bit in range(30, -1, -1):
            cand = thr | (1 << bit)
            cnt = jnp.sum((aff >= lax.bitcast_convert_type(cand, F32)).astype(F32), axis=1, keepdims=True)
            thr = jnp.where(cnt >= cap, cand, thr)
        above = aff >= lax.bitcast_convert_type(thr + 1, F32)
        tied = (aff >= lax.bitcast_convert_type(thr, F32)) & jnp.logical_not(above)
        need = cap - jnp.sum(above.astype(F32), axis=1, keepdims=True)
        before = (lax.broadcasted_iota(jnp.int32, (n, n), 0)
                  < lax.broadcasted_iota(jnp.int32, (n, n), 1)).astype(BF16)
        tie_rank = _dot(tied.astype(BF16), before)
        mask = (above | (tied & (tie_rank < need))).astype(F32)
        mask_ref[...] = mask
        pos_ref[...] = _dot(mask.astype(BF16), before)

    for j in range(seqs):
        r0 = pl.multiple_of((b * seqs + j) * N_EXPERTS, N_EXPERTS)
        onehots = _slot_onehot(mask_ref[pl.ds(r0, N_EXPERTS), :], pos_ref[pl.ds(r0, N_EXPERTS), :], cap)
        sel = jnp.concatenate(onehots, axis=0).astype(BF16)
        xs = _dot(sel, h2_ref[j * n:(j + 1) * n, :]).astype(BF16)
        xs_ref[:, j * cap:(j + 1) * cap, :] = xs.reshape(N_EXPERTS, cap, D_MODEL)


def _route(afft, h2, *, n, nb, cap, seqs):
    rows = nb * N_EXPERTS
    return pl.pallas_call(
        functools.partial(_route_kernel, n=n, cap=cap, seqs=seqs),
        grid=(nb // seqs,),
        in_specs=[_full(afft.shape), pl.BlockSpec((seqs * n, D_MODEL), lambda b: (b, 0))],
        out_specs=[pl.BlockSpec((N_EXPERTS, seqs * cap, D_MODEL), lambda b: (0, b, 0)),
                   _full((rows, n)), _full((rows, n))],
        out_shape=[jax.ShapeDtypeStruct((N_EXPERTS, nb * cap, D_MODEL), BF16),
                   jax.ShapeDtypeStruct((rows, n), F32), jax.ShapeDtypeStruct((rows, n), F32)],
        compiler_params=_params(),
        name="route_n%d" % n,
    )(afft, h2)


def _slot_gates(aff_ref, mask_ref, pos_ref, e, cap):
    cols = []
    for b in range(aff_ref.shape[0] // N_EXPERTS):
        r = b * N_EXPERTS + e
        onehot, = _slot_onehot(mask_ref[pl.ds(r, 1), :], pos_ref[pl.ds(r, 1), :], cap)
        cols.append(jnp.sum(jnp.where(onehot, aff_ref[pl.ds(r, 1), :], 0.0), axis=1, keepdims=True))
    return jnp.concatenate(cols, axis=0)


def _ffn_kernel(xp_ref, xs_ref, ap_ref, mp_ref, pp_ref, as_ref, ms_ref, ps_ref, wgu_ref, wd_ref, yp_ref, ys_ref):
    e = pl.program_id(0)
    rows_p = xp_ref.shape[1]
    xs = jnp.concatenate([xp_ref[0], xs_ref[0]], axis=0)
    gu = _dot(xs, wgu_ref[0].astype(BF16))
    gate = gu[:, :D_EXPERT]
    up = gu[:, D_EXPERT:]
    act = (gate * jax.nn.sigmoid(gate) * up).astype(BF16)
    y = _dot(act, wd_ref[0].astype(BF16))
    g = jnp.concatenate([_slot_gates(ap_ref, mp_ref, pp_ref, e, rows_p * N_EXPERTS // ap_ref.shape[0]),
                         _slot_gates(as_ref, ms_ref, ps_ref, e, xs_ref.shape[1] * N_EXPERTS // as_ref.shape[0])],
                        axis=0)
    y = (y * g).astype(BF16)
    yp_ref[0] = y[:rows_p]
    ys_ref[0] = y[rows_p:]


def _ffn(xs_p, xs_s, route_p, route_s, w_gate_up, w_down):
    per_e = lambda a: pl.BlockSpec((1,) + a.shape[1:], lambda e: (e, 0, 0))
    tables = list(route_p) + list(route_s)
    return pl.pallas_call(
        _ffn_kernel,
        grid=(N_EXPERTS,),
        in_specs=[per_e(xs_p), per_e(xs_s)] + [_full(t.shape) for t in tables] + [per_e(w_gate_up), per_e(w_down)],
        out_specs=[per_e(xs_p), per_e(xs_s)],
        out_shape=[jax.ShapeDtypeStruct(xs_p.shape, BF16), jax.ShapeDtypeStruct(xs_s.shape, BF16)],
        compiler_params=_params(),
        name="expert_ffn",
    )(xs_p, xs_s, *tables, w_gate_up, w_down)


def _combine_kernel(y_ref, mask_ref, pos_ref, x1_ref, mod_ref, g2_ref, b2_ref, o_ref, *, cap, seqs, mod_row0,
                    mod_row_step):
    row = mod_row0 + mod_row_step * pl.program_id(0)
    gate2 = mod_ref[pl.ds(row, 1), 5 * D_MODEL:6 * D_MODEL]
    tn = x1_ref.shape[0] // seqs
    for j in range(seqs):
        e0 = j * N_EXPERTS
        sel = jnp.concatenate(
            _slot_onehot(mask_ref[e0:e0 + N_EXPERTS, :], pos_ref[e0:e0 + N_EXPERTS, :], cap), axis=0).astype(BF16)
        y = y_ref[:, j * cap:(j + 1) * cap, :].reshape(N_EXPERTS * cap, D_MODEL)
        ff = _dot_tn(sel, y)
        r = pl.ds(j * tn, tn)
        o_ref[r, :] = _ln_plain(ALPHA * x1_ref[r, :] + gate2 * ff) * g2_ref[...] + b2_ref[...]


def _combine(y, mask, pos, x1, mod, g2, b2, *, n, nb, cap, seqs, sample):
    tn = min(n, COMBINE_TILE)
    tiles = n // tn
    assert seqs == 1 or tiles == 1
    return pl.pallas_call(
        functools.partial(_combine_kernel, cap=cap, seqs=seqs, mod_row0=1 if sample else 0,
                          mod_row_step=1 if sample else 0),
        grid=(nb // seqs, tiles),
        in_specs=[pl.BlockSpec((N_EXPERTS, seqs * cap, D_MODEL), lambda b, t: (0, b, 0)),
                  pl.BlockSpec((seqs * N_EXPERTS, tn), lambda b, t: (b, t)),
                  pl.BlockSpec((seqs * N_EXPERTS, tn), lambda b, t: (b, t)),
                  pl.BlockSpec((seqs * tn, D_MODEL), lambda b, t: (b * tiles + t, 0)),
                  _full(mod.shape), _full(g2.shape), _full(b2.shape)],
        out_specs=pl.BlockSpec((seqs * tn, D_MODEL), lambda b, t: (b * tiles + t, 0)),
        out_shape=jax.ShapeDtypeStruct((nb * n, D_MODEL), F32),
        compiler_params=_params(2),
        name="combine_sample" if sample else "combine_prompt",
    )(y, mask, pos, x1, mod, g2, b2)


def _block_diag_256(w):
    per_tile = MXU_DIM // RG_BLOCK
    tiles = []
    for c in range(w.shape[0] // per_tile):
        tiles.append(jax.scipy.linalg.block_diag(*[w[c * per_tile + i] for i in range(per_tile)]))
    return jnp.stack(tiles)


def _rope_tables(n):
    lane = np.arange(LANES)
    within = lane % HEAD_DIM
    freq = (within % 16).astype(np.float32)
    inv = np.float32(ROPE_THETA) ** (-freq / np.float32(16.0))
    tok = np.arange(n)
    pos = np.where((within < HEAD_DIM // 2)[None, :], (tok // GRID_W)[:, None], (tok % GRID_W)[:, None])
    ang = pos.astype(np.float32) * inv[None, :]
    sign = np.where(within % 32 < 16, -1.0, 1.0).astype(np.float32)
    return jnp.asarray(np.cos(ang)), jnp.asarray(np.sin(ang) * sign[None, :])


def kernel(x_prompt, x_sample, cache_k, cache_v, state_h, c, c_ctx, w_mod, b_mod, w_in, q_norm_g, k_norm_g,
           conv_w, conv_b, w_rg_a, b_rg_a, w_rg_x, b_rg_x, rg_lambda, w_out, ln1_g, ln1_b, w_router,
           w_gate_up, w_down, ln2_g, ln2_b):
    assert w_mod.shape[0] == DEPTH == 1
    nb_p, n_p, _ = x_prompt.shape
    nb_s, n_s, _ = x_sample.shape
    past = cache_k.shape[2]
    cap_p = CAP_FACTOR * n_p // N_EXPERTS
    cap_s = CAP_FACTOR * n_s // N_EXPERTS

    cs = jnp.concatenate([c_ctx[None, :], c, jnp.zeros((MOD_ROWS - 1 - nb_s, D_MODEL), F32)], axis=0)
    row = lambda v: v.reshape(1, -1)
    head_avg = np.kron(np.eye(ATTN_WIDTH // HEAD_DIM, dtype=np.float32),
                       np.full((HEAD_DIM, HEAD_DIM), 1.0 / HEAD_DIM, np.float32))
    bd = jnp.asarray(head_avg, BF16)
    qg = row(jnp.tile(q_norm_g[0], N_Q_HEADS))
    kg = row(jnp.tile(k_norm_g[0], N_KV_HEADS))
    wg = 0.5 * jnp.concatenate([_block_diag_256(w_rg_a[0, 0]), _block_diag_256(w_rg_x[0, 0]),
                                _block_diag_256(w_rg_a[0, 1]), _block_diag_256(w_rg_x[0, 1])], axis=2)
    bg = 0.5 * jnp.stack([b_rg_a[0, 0], b_rg_x[0, 0], b_rg_a[0, 1], b_rg_x[0, 1]])
    consts = (bd, qg, kg, conv_w[0], row(conv_b[0]), wg, bg, rg_lambda[0])
    wrt = w_router[0].T
    g1, b1, g2, b2 = row(ln1_g[0]), row(ln1_b[0]), row(ln2_g[0]), row(ln2_b[0])

    win_bf, wout_bf = _cast_weights(w_in[0], w_out[0])
    mod = _mod_vectors(cs, w_mod[0], row(b_mod[0]))

    xp = x_prompt.reshape(nb_p * n_p, D_MODEL)
    xs = x_sample.reshape(nb_s * n_s, D_MODEL)

    new_k, new_v, new_h, x1_p, h2_p, aff_p = _mixer_prompt(
        xp, mod, win_bf, wout_bf, consts, g1, b1, wrt, n=n_p, nb=nb_p)

    q_s, k_s, v_s, rg_s = _front_sample(xs, mod, win_bf, consts, state_h[:, 0], _rope_tables(n_s), n=n_s, nb=nb_s)
    attn_s = _attn_sample(q_s, k_s, v_s, cache_k[:, 0].reshape(nb_s * past, KV_WIDTH),
                          cache_v[:, 0].reshape(nb_s * past, KV_WIDTH), n=n_s, nb=nb_s, past=past)
    x1_s, h2_s, aff_s = _post_sample(attn_s, rg_s, xs, mod, wout_bf, g1, b1, wrt, n=n_s, nb=nb_s)

    seqs_p = PROMPT_SEQS_PER_STEP if nb_p % PROMPT_SEQS_PER_STEP == 0 else 1
    xs_p, mask_p, pos_p = _route(aff_p, h2_p, n=n_p, nb=nb_p, cap=cap_p, seqs=seqs_p)
    xs_s, mask_s, pos_s = _route(aff_s, h2_s, n=n_s, nb=nb_s, cap=cap_s, seqs=1)

    y_p, y_s = _ffn(xs_p, xs_s, (aff_p, mask_p, pos_p), (aff_s, mask_s, pos_s), w_gate_up[0], w_down[0])

    out_p = _combine(y_p, mask_p, pos_p, x1_p, mod, g2, b2, n=n_p, nb=nb_p, cap=cap_p, seqs=seqs_p, sample=False)
    out_s = _combine(y_s, mask_s, pos_s, x1_s, mod, g2, b2, n=n_s, nb=nb_s, cap=cap_s, seqs=1, sample=True)

    def cache_layout(t):
        t = t.reshape(nb_p, DEPTH, N_KV_HEADS, HEAD_DIM, n_p)
        return jnp.transpose(t, (0, 1, 4, 2, 3))

    return (out_p.reshape(nb_p, n_p, D_MODEL), out_s.reshape(nb_s, n_s, D_MODEL),
            cache_layout(new_k), cache_layout(new_v), new_h.reshape(nb_p, DEPTH, 2, RG_WIDTH))
```

```python
import functools

import numpy as np
import jax
import jax.numpy as jnp
from jax import lax
from jax.experimental import pallas as pl
from jax.experimental.pallas import tpu as pltpu

F32 = jnp.float32
BF16 = jnp.bfloat16

D_MODEL = 1024
HEAD_DIM = 64
N_Q_HEADS = 8
N_KV_HEADS = 2
GROUP = N_Q_HEADS // N_KV_HEADS
ATTN_WIDTH = N_Q_HEADS * HEAD_DIM
KV_WIDTH = N_KV_HEADS * HEAD_DIM
RG_WIDTH = D_MODEL - ATTN_WIDTH
RG_BLOCK = 64
CONV_W = 4
C_LRU = 8.0
N_EXPERTS = 16
CAP_FACTOR = 2
D_EXPERT = 1024
GRID_W = 64
ROPE_THETA = 10000.0
NORM_EPS = 1e-6
DEPTH = 1
ALPHA = (2.0 * DEPTH) ** 0.25
IN_WIDTH = ATTN_WIDTH + 2 * KV_WIDTH + 2 * RG_WIDTH
Q_SCALE = HEAD_DIM ** -0.5 * float(np.log2(np.e))

LANES = 128
SUBLANES = 8
MXU_DIM = 256
VMEM_LIMIT_BYTES = 56 * 1024 * 1024

MOD_ROWS = SUBLANES
MOD_TILE = 256
ATTN_Q_TILE = 256
COMBINE_TILE = 256
MIXER_ORDER = (1, 0, 0, 1, 1, 0, 1, 1, 0, 1, 0, 1, 0, 1, 1, 0, 1, 1, 0, 0, 1, 0, 1, 0, 0, 0, 1, 0, 0, 1, 0)
FRONT_ROW_STREAMS = 4
POST_ROW_STREAMS = 4
PROMPT_SEQS_PER_STEP = 4


def _params(n_axes=1, flags=None):
    return pltpu.CompilerParams(dimension_semantics=("arbitrary",) * n_axes,
                                vmem_limit_bytes=VMEM_LIMIT_BYTES, flags=flags)


def _full(shape):
    zeros = (0,) * len(shape)
    return pl.BlockSpec(shape, lambda *_: zeros)


def _ln_plain(x):
    mu = jnp.mean(x, -1, keepdims=True)
    xc = x - mu
    var = jnp.mean(xc * xc, -1, keepdims=True)
    return xc * lax.rsqrt(var + NORM_EPS)


def _dot(a, b):
    return jnp.dot(a, b, preferred_element_type=F32)


def _dot_nt(a, b):
    return lax.dot_general(a, b, (((1,), (1,)), ((), ())), preferred_element_type=F32)


def _dot_tn(a, b):
    return lax.dot_general(a, b, (((0,), (0,)), ((), ())), preferred_element_type=F32)


def _split_bf16(x):
    hi = x.astype(BF16)
    lo = (x - hi.astype(F32)).astype(BF16)
    return hi, lo


def _cast_kernel(a_ref, b_ref, ao_ref, bo_ref):
    ao_ref[...] = a_ref[...].astype(BF16)
    bo_ref[...] = b_ref[...].astype(BF16)


def _cast_weights(w_in, w_out):
    return pl.pallas_call(
        _cast_kernel,
        grid=(1,),
        in_specs=[_full(w_in.shape), _full(w_out.shape)],
        out_specs=[_full(w_in.shape), _full(w_out.shape)],
        out_shape=[jax.ShapeDtypeStruct(w_in.shape, BF16), jax.ShapeDtypeStruct(w_out.shape, BF16)],
        compiler_params=_params(),
        name="cast_weights",
    )(w_in, w_out)


def _mod_kernel(cs_ref, w_ref, b_ref, o_ref):
    @pl.when(pl.program_id(0) == 0)
    def _():
        o_ref[...] = jnp.broadcast_to(b_ref[...], o_ref.shape)

    cs = cs_ref[...]
    s = cs * jax.nn.sigmoid(cs)
    o_ref[...] += _dot(s.astype(BF16), w_ref[...].astype(BF16))


def _mod_vectors(cs, w_mod, b_mod):
    width = w_mod.shape[1]
    return pl.pallas_call(
        _mod_kernel,
        grid=(D_MODEL // MOD_TILE,),
        in_specs=[pl.BlockSpec((MOD_ROWS, MOD_TILE), lambda j: (0, j)),
                  pl.BlockSpec((MOD_TILE, width), lambda j: (j, 0)),
                  _full(b_mod.shape)],
        out_specs=_full((MOD_ROWS, width)),
        out_shape=jax.ShapeDtypeStruct((MOD_ROWS, width), F32),
        compiler_params=_params(),
        name="mod_vectors",
    )(cs, w_mod, b_mod)


def _head_mean_sq(x, bd):
    hi, lo = _split_bf16(x * x)
    return _dot(hi, bd) + _dot(lo, bd)


def _rope_lanes(x, cos, sin_signed, hi_half):
    partner = jnp.where(hi_half, pltpu.roll(x, 16, axis=1), pltpu.roll(x, LANES - 16, axis=1))
    return x * cos + partner * sin_signed


def _modulated(x, m):
    return (_ln_plain(x) * (1.0 + m[:, D_MODEL:2 * D_MODEL]) + m[:, 0:D_MODEL]).astype(BF16)


def _qkv(h, win_ref, bd_ref, qg_ref, kg_ref, rope):
    n = h.shape[0]
    bd = bd_ref[...]
    q = _dot(h, win_ref[:, 0:ATTN_WIDTH])
    yield
    q = q * lax.rsqrt(_head_mean_sq(q, bd) + NORM_EPS) * qg_ref[...]
    yield
    k = _dot(h, win_ref[:, ATTN_WIDTH:ATTN_WIDTH + KV_WIDTH])
    k = k * lax.rsqrt(_head_mean_sq(k, bd[:KV_WIDTH, :KV_WIDTH]) + NORM_EPS) * kg_ref[...]
    v = _dot(h, win_ref[:, ATTN_WIDTH + KV_WIDTH:ATTN_WIDTH + 2 * KV_WIDTH])
    yield
    if rope is not None:
        cos, sin_signed = rope
        lane = lax.broadcasted_iota(jnp.int32, (n, LANES), 1)
        hi_half = (lane & 16) != 0
        q = jnp.concatenate(
            [_rope_lanes(q[:, j * LANES:(j + 1) * LANES], cos, sin_signed, hi_half)
             for j in range(ATTN_WIDTH // LANES)], axis=1)
        k = _rope_lanes(k, cos, sin_signed, hi_half)
    return (q * Q_SCALE).astype(BF16), k, v


def _drive(*streams, order=()):
    results = [None] * len(streams)
    live = list(range(len(streams)))
    plan = [j for j in order]
    while live:
        idx = plan.pop(0) if plan else live[0]
        if idx not in live:
            continue
        if not plan:
            live.append(live.pop(0))
        try:
            next(streams[idx])
        except StopIteration as stop:
            results[idx] = stop.value
            live.remove(idx)
    return results


def _attend(q, k, v, lookahead):
    def scores(hq):
        kv = hq // GROUP
        return _dot_nt(q[:, hq * HEAD_DIM:(hq + 1) * HEAD_DIM], k[:, kv * HEAD_DIM:(kv + 1) * HEAD_DIM])

    outs = []
    s_next = scores(0) if lookahead else None
    for hq in range(N_Q_HEADS):
        if lookahead:
            s, s_next = s_next, (scores(hq + 1) if hq + 1 < N_Q_HEADS else None)
        else:
            s = scores(hq)
        kv = hq // GROUP
        e = jnp.exp2(s - jnp.max(s, axis=-1, keepdims=True))
        denom = jnp.sum(e, axis=-1, keepdims=True)
        pv = _dot(e.astype(BF16), v[:, kv * HEAD_DIM:(kv + 1) * HEAD_DIM])
        outs.append(pv / denom)
        yield
    return jnp.concatenate(outs, axis=1).astype(BF16)


def _rg_inputs(h, win_ref):
    rg_lo = ATTN_WIDTH + 2 * KV_WIDTH
    xr = _dot(h, win_ref[:, rg_lo:rg_lo + RG_WIDTH])
    yield
    gr = _dot(h, win_ref[:, rg_lo + RG_WIDTH:IN_WIDTH])
    yield
    return xr, gr


def _rglru(xr, gr, cw_ref, cb_ref, wg_ref, bg_ref, lam_ref, h0, scan_refs, unroll):
    af_ref, bf_ref, ab_ref, bb_ref = scan_refs
    n = xr.shape[0]

    t_idx = lax.broadcasted_iota(jnp.int32, (n, 1), 0)
    cw = cw_ref[...]
    xc = jnp.where(t_idx >= 2, pltpu.roll(xr, 2, axis=0), 0.0) * cw[0:1, :]
    xc = xc + jnp.where(t_idx >= 1, pltpu.roll(xr, 1, axis=0), 0.0) * cw[1:2, :]
    xc = xc + xr * cw[2:3, :]
    xc = xc + jnp.where(t_idx < n - 1, pltpu.roll(xr, n - 1, axis=0), 0.0) * cw[3:4, :]
    xc = xc + cb_ref[...]
    yield

    xcb = xc.astype(BF16)
    halves = [_dot(xcb[:, c * MXU_DIM:(c + 1) * MXU_DIM], wg_ref[c].astype(BF16))
              for c in range(RG_WIDTH // MXU_DIM)]

    def gate_pre(idx):
        return jnp.concatenate([hv[:, idx * MXU_DIM:(idx + 1) * MXU_DIM] for hv in halves], axis=1) \
            + bg_ref[idx:idx + 1, :]

    yield
    neg = -lam_ref[...]
    softplus =jnp.maximum(neg, 0.0) + jnp.log1p(jnp.exp(-jnp.abs(neg)))
    decay = (-0.5 * C_LRU) * softplus
    half_xc = 0.5 * xc

    def coeffs(d):
        r2 = jnp.tanh(gate_pre(2 * d)) + 1.0
        i2 = jnp.tanh(gate_pre(2 * d + 1)) + 1.0
        log_a = r2 * decay[d:d + 1, :]
        a = jnp.exp(log_a)
        bx = jnp.sqrt(jnp.tanh(-log_a) * (a * a + 1.0)) * (i2 * half_xc)
        return a, bx

    groups = n // SUBLANES
    rmod = lax.broadcasted_iota(jnp.int32, (1, SUBLANES, 1), 1)

    def tile_scan(d, forward):
        a, bx = coeffs(d)
        yield
        a = a.reshape(groups, SUBLANES, RG_WIDTH)
        bx = bx.reshape(groups, SUBLANES, RG_WIDTH)
        for s in (1, 2, 4):
            ok = (rmod >= s) if forward else (rmod < SUBLANES - s)
            shift = s if forward else SUBLANES - s
            a_sh = jnp.where(ok, pltpu.roll(a, shift, axis=1), 1.0)
            b_sh = jnp.where(ok, pltpu.roll(bx, shift, axis=1), 0.0)
            bx = a * b_sh + bx
            a = a * a_sh
            yield
        return a.reshape(n, RG_WIDTH), bx.reshape(n, RG_WIDTH)

    af_ref[...], bf_ref[...] = yield from tile_scan(0, True)
    ab_ref[...], bb_ref[...] = yield from tile_scan(1, False)

    def step(g, carry):
        hf, hb = carry
        rf = pl.multiple_of(g * SUBLANES, SUBLANES)
        rb = pl.multiple_of((groups - 1 - g) * SUBLANES, SUBLANES)
        new_f = af_ref[pl.ds(rf, SUBLANES), :] * hf + bf_ref[pl.ds(rf, SUBLANES), :]
        new_b = ab_ref[pl.ds(rb, SUBLANES), :] * hb + bb_ref[pl.ds(rb, SUBLANES), :]
        bf_ref[pl.ds(rf, SUBLANES), :] = new_f
        bb_ref[pl.ds(rb, SUBLANES), :] = new_b
        return (jnp.broadcast_to(new_f[SUBLANES - 1:SUBLANES, :], (SUBLANES, RG_WIDTH)),
                jnp.broadcast_to(new_b[0:1, :], (SUBLANES, RG_WIDTH)))

    hf, hb = lax.fori_loop(
        0, groups, step,
        (jnp.broadcast_to(h0[0:1, :], (SUBLANES, RG_WIDTH)), jnp.broadcast_to(h0[1:2, :], (SUBLANES, RG_WIDTH))),
        unroll=unroll)
    yield

    rg =((bf_ref[...] + bb_ref[...]) * jax.nn.gelu(gr)).astype(BF16)
    return rg, hf[0:1, :], hb[0:1, :]


def _out_and_router(attn, rg, x, m, wout_ref, g1_ref, b1_ref, wrt_ref):
    gate1 = m[:, 2 * D_MODEL:3 * D_MODEL]
    shift2 = m[:, 3 * D_MODEL:4 * D_MODEL]
    scale2 = m[:, 4 * D_MODEL:5 * D_MODEL]
    mix = _dot(jnp.concatenate([attn, rg], axis=1), wout_ref[...])
    yield
    x1 = _ln_plain(ALPHA * x + gate1 * mix) * g1_ref[...] + b1_ref[...]
    yield
    h2 = (_ln_plain(x1) * (1.0 + scale2) + shift2).astype(BF16)
    yield
    logits = _dot_nt(wrt_ref[...].astype(BF16), h2)
    e = jnp.exp(logits - jnp.max(logits, axis=0, keepdims=True))
    return x1, h2, e / jnp.sum(e, axis=0, keepdims=True)


def _mixer_kernel(x_ref, mod_ref, win_ref, wout_ref, bd_ref, qg_ref, kg_ref, cw_ref, cb_ref, wg_ref, bg_ref,
                  lam_ref, g1_ref, b1_ref, wrt_ref,
                  nk_ref, nv_ref, nh_ref, x1_ref, h2_ref, afft_ref,
                  xprev_ref, xr_ref, gr_ref, attn_ref, *scan_refs):
    @pl.when(pl.program_id(0) == 0)
    def _():
        xprev_ref[...] = jnp.zeros_like(xprev_ref)
        xr_ref[...] = jnp.zeros_like(xr_ref)
        gr_ref[...] = jnp.zeros_like(gr_ref)
        attn_ref[...] = jnp.zeros_like(attn_ref)

    m = mod_ref[0:1, :]

    def second_half():
        h0 = jnp.zeros((2, RG_WIDTH), F32)
        rg, hf, hb = yield from _rglru(xr_ref[...], gr_ref[...], cw_ref, cb_ref, wg_ref, bg_ref, lam_ref, h0,
                                       scan_refs, True)
        nh_ref[0] = jnp.concatenate([hf, hb], axis=0)
        x1_ref[...], h2_ref[...], afft_ref[...] = yield from _out_and_router(
            attn_ref[...], rg, xprev_ref[...], m, wout_ref, g1_ref, b1_ref, wrt_ref)

    def first_half():
        x = x_ref[...]
        h = _modulated(x, m)
        yield
        q, k, v = yield from _qkv(h, win_ref, bd_ref, qg_ref, kg_ref, None)
        nk_ref[0] = k.T
        nv_ref[0] = v.T
        yield
        xr, gr = yield from _rg_inputs(h, win_ref)
        attn = yield from _attend(q, k.astype(BF16), v.astype(BF16), False)
        return x, attn, xr, gr

    (x, attn, xr, gr), _ = _drive(first_half(), second_half(), order=MIXER_ORDER)
    xprev_ref[...] = x
    attn_ref[...] = attn
    xr_ref[...] = xr
    gr_ref[...] = gr


def _mixer_prompt(x2d, mod, win_bf, wout_bf, consts, g1, b1, wrt, *, n, nb):
    bd, qg, kg, cw, cb, wg, bg, lam = consts
    tokens = nb * n
    first = lambda w: pl.BlockSpec((n, w), lambda i: (jnp.minimum(i, nb - 1), 0))
    second = lambda w: pl.BlockSpec((n, w), lambda i: (jnp.maximum(i - 1, 0), 0))
    small = [bd, qg, kg, cw, cb, wg, bg, lam, g1, b1, wrt]
    return pl.pallas_call(
        _mixer_kernel,
        grid=(nb + 1,),
        in_specs=[first(D_MODEL), _full(mod.shape), _full(win_bf.shape), _full(wout_bf.shape)]
                 + [_full(a.shape) for a in small],
        out_specs=[pl.BlockSpec((1, KV_WIDTH, n), lambda i: (jnp.minimum(i, nb - 1), 0, 0)),
                   pl.BlockSpec((1, KV_WIDTH, n), lambda i: (jnp.minimum(i, nb - 1), 0, 0)),
                   pl.BlockSpec((1, 2, RG_WIDTH), lambda i: (jnp.maximum(i - 1, 0), 0, 0)),
                   second(D_MODEL), second(D_MODEL),
                   pl.BlockSpec((N_EXPERTS, n), lambda i: (jnp.maximum(i - 1, 0), 0))],
        out_shape=[jax.ShapeDtypeStruct((nb, KV_WIDTH, n), F32), jax.ShapeDtypeStruct((nb, KV_WIDTH, n), F32),
                   jax.ShapeDtypeStruct((nb, 2, RG_WIDTH), F32),
                   jax.ShapeDtypeStruct((tokens, D_MODEL), F32), jax.ShapeDtypeStruct((tokens, D_MODEL), BF16),
                   jax.ShapeDtypeStruct((nb * N_EXPERTS, n), F32)],
        scratch_shapes=[pltpu.VMEM((n, D_MODEL), F32), pltpu.VMEM((n, RG_WIDTH), F32),
                        pltpu.VMEM((n, RG_WIDTH), F32), pltpu.VMEM((n, ATTN_WIDTH), BF16)]
                       + [pltpu.VMEM((n, RG_WIDTH), F32)] * 4,
        compiler_params=_params(),
        name="mixer_prompt",
    )(x2d, mod, win_bf, wout_bf, *small)


def _front_kernel(x_ref, mod_ref, win_ref, bd_ref, qg_ref, kg_ref, cw_ref, cb_ref, wg_ref, bg_ref, lam_ref, h0_ref,
                  cos_ref, sin_ref, q_ref, k_ref, v_ref, rg_ref, xr_ref, gr_ref, *scan_refs):
    m = mod_ref[pl.ds(1 + pl.program_id(0), 1), :]
    rows = x_ref.shape[0] // FRONT_ROW_STREAMS

    def row_block(j):
        r = pl.ds(j * rows, rows)
        h = _modulated(x_ref[r, :], m)
        yield
        q, k, v = yield from _qkv(h, win_ref, bd_ref, qg_ref, kg_ref, (cos_ref[r, :], sin_ref[r, :]))
        q_ref[r, :] = q
        k_ref[r, :] = k.astype(BF16)
        v_ref[r, :] = v.astype(BF16)
        yield
        xr_ref[r, :], gr_ref[r, :] = yield from _rg_inputs(h, win_ref)

    chunks, lag = 8, 2
    stagger = [j for t in range(chunks + lag * (FRONT_ROW_STREAMS - 1)) for j in range(FRONT_ROW_STREAMS)
               if 0 <= t - lag * j < chunks]
    _drive(*[row_block(j) for j in range(FRONT_ROW_STREAMS)], order=stagger)
    (rg_ref[...], _, _), = _drive(_rglru(xr_ref[...], gr_ref[...], cw_ref, cb_ref, wg_ref, bg_ref, lam_ref,
                                         h0_ref[0], scan_refs, 1))


def _front_sample(x2d, mod, win_bf, consts, h0, rope, *, n, nb):
    tokens = nb * n
    seq = lambda w: pl.BlockSpec((n, w), lambda b: (b, 0))
    small = list(consts)
    return pl.pallas_call(
        _front_kernel,
        grid=(nb,),
        in_specs=[seq(D_MODEL), _full(mod.shape), _full(win_bf.shape)] + [_full(a.shape) for a in small]
                 + [pl.BlockSpec((1, 2, RG_WIDTH), lambda b: (b, 0, 0)), _full(rope[0].shape), _full(rope[1].shape)],
        out_specs=[seq(ATTN_WIDTH), seq(KV_WIDTH), seq(KV_WIDTH), seq(RG_WIDTH)],
        out_shape=[jax.ShapeDtypeStruct((tokens, ATTN_WIDTH), BF16), jax.ShapeDtypeStruct((tokens, KV_WIDTH), BF16),
                   jax.ShapeDtypeStruct((tokens, KV_WIDTH), BF16), jax.ShapeDtypeStruct((tokens, RG_WIDTH), BF16)],
        scratch_shapes=[pltpu.VMEM((n, RG_WIDTH), F32)] * 6,
        compiler_params=_params(),
        name="front_sample",
    )(x2d, mod, win_bf, *small, h0, *rope)


def _attn_kernel(q_ref, k_ref, v_ref, kc_ref, vc_ref, o_ref):
    k = jnp.concatenate([k_ref[...], kc_ref[...].astype(BF16)], axis=0)
    v = jnp.concatenate([v_ref[...], vc_ref[...].astype(BF16)], axis=0)
    o_ref[...], = _drive(_attend(q_ref[...], k, v, True))


def _attn_sample(q, k, v, cache_k, cache_v, *, n, nb, past):
    tq = min(n, ATTN_Q_TILE)
    tiles = n // tq
    return pl.pallas_call(
        _attn_kernel,
        grid=(nb, tiles),
        in_specs=[pl.BlockSpec((tq, ATTN_WIDTH), lambda b, t: (b * tiles + t, 0)),
                  pl.BlockSpec((n, KV_WIDTH), lambda b, t: (b, 0)),
                  pl.BlockSpec((n, KV_WIDTH), lambda b, t: (b, 0)),
                  pl.BlockSpec((past, KV_WIDTH), lambda b, t: (b, 0)),
                  pl.BlockSpec((past, KV_WIDTH), lambda b, t: (b, 0))],
        out_specs=pl.BlockSpec((tq, ATTN_WIDTH), lambda b, t: (b * tiles + t, 0)),
        out_shape=jax.ShapeDtypeStruct((nb * n, ATTN_WIDTH), BF16),
        compiler_params=_params(2),
        name="attn_sample",
    )(q, k, v, cache_k, cache_v)


def _post_kernel(attn_ref, rg_ref, x_ref, mod_ref, wout_ref, g1_ref, b1_ref, wrt_ref, x1_ref, h2_ref, afft_ref):
    m = mod_ref[pl.ds(1 + pl.program_id(0), 1), :]
    rows = x_ref.shape[0] // POST_ROW_STREAMS

    def row_block(j):
        r = pl.ds(j * rows, rows)
        x1_ref[r, :], h2_ref[r, :], afft_ref[:, r] = yield from _out_and_router(
            attn_ref[r, :], rg_ref[r, :], x_ref[r, :], m, wout_ref, g1_ref, b1_ref, wrt_ref)

    stagger = [j for i in range(POST_ROW_STREAMS) for j in range(i + 1)]
    _drive(*[row_block(j) for j in range(POST_ROW_STREAMS)], order=stagger)


def _post_sample(attn, rg, x2d, mod, wout_bf, g1, b1, wrt, *, n, nb):
    tokens = nb * n
    seq = lambda w: pl.BlockSpec((n, w), lambda b: (b, 0))
    return pl.pallas_call(
        _post_kernel,
        grid=(nb,),
        in_specs=[seq(ATTN_WIDTH), seq(RG_WIDTH), seq(D_MODEL), _full(mod.shape), _full(wout_bf.shape),
                  _full(g1.shape), _full(b1.shape), _full(wrt.shape)],
        out_specs=[seq(D_MODEL), seq(D_MODEL), pl.BlockSpec((N_EXPERTS, n), lambda b: (b, 0))],
        out_shape=[jax.ShapeDtypeStruct((tokens, D_MODEL), F32), jax.ShapeDtypeStruct((tokens, D_MODEL), BF16),
                   jax.ShapeDtypeStruct((nb * N_EXPERTS, n), F32)],
        compiler_params=_params(),
        name="post_sample",
    )(attn, rg, x2d, mod, wout_bf, g1, b1, wrt)


def _slot_onehot(mask, pos, cap):
    n = mask.shape[1]
    slot = lax.broadcasted_iota(jnp.int32, (cap, n), 0).astype(F32)
    return [(slot == pos[e:e + 1, :]) & (mask[e:e + 1, :] > 0.5) for e in range(mask.shape[0])]


def _route_kernel(afft_ref, h2_ref, xs_ref, mask_ref, pos_ref, *, n, cap, seqs):
    b = pl.program_id(0)

    @pl.when(b == 0)
    def _():
        aff = afft_ref[...]
        thr = jnp.zeros((aff.shape[0], 1), jnp.int32)
        for bit in range(30, -1, -1):
            cand = thr | (1 << bit)
            cnt = jnp.sum((aff >= lax.bitcast_convert_type(cand, F32)).astype(F32), axis=1, keepdims=True)
            thr = jnp.where(cnt >= cap, cand, thr)
        above = aff >= lax.bitcast_convert_type(thr + 1, F32)
        tied = (aff >= lax.bitcast_convert_type(thr, F32)) & jnp.logical_not(above)
        need = cap - jnp.sum(above.astype(F32), axis=1, keepdims=True)
        before = (lax.broadcasted_iota(jnp.int32, (n, n), 0)
                  < lax.broadcasted_iota(jnp.int32, (n, n), 1)).astype(BF16)
        tie_rank = _dot(tied.astype(BF16), before)
        mask = (above | (tied & (tie_rank < need))).astype(F32)
        mask_ref[...] = mask
        pos_ref[...] = _dot(mask.astype(BF16), before)

    for j in range(seqs):
        r0 = pl.multiple_of((b * seqs + j) * N_EXPERTS, N_EXPERTS)
        onehots = _slot_onehot(mask_ref[pl.ds(r0, N_EXPERTS), :], pos_ref[pl.ds(r0, N_EXPERTS), :], cap)
        sel = jnp.concatenate(onehots, axis=0).astype(BF16)
        xs = _dot(sel, h2_ref[j * n:(j + 1) * n, :]).astype(BF16)
        xs_ref[:, j * cap:(j + 1) * cap, :] = xs.reshape(N_EXPERTS, cap, D_MODEL)


def _route(afft, h2, *, n, nb, cap, seqs):
    rows = nb * N_EXPERTS
    return pl.pallas_call(
        functools.partial(_route_kernel, n=n, cap=cap, seqs=seqs),
        grid=(nb // seqs,),
        in_specs=[_full(afft.shape), pl.BlockSpec((seqs * n, D_MODEL), lambda b: (b, 0))],
        out_specs=[pl.BlockSpec((N_EXPERTS, seqs * cap, D_MODEL), lambda b: (0, b, 0)),
                   _full((rows, n)), _full((rows, n))],
        out_shape=[jax.ShapeDtypeStruct((N_EXPERTS, nb * cap, D_MODEL), BF16),
                   jax.ShapeDtypeStruct((rows, n), F32), jax.ShapeDtypeStruct((rows, n), F32)],
        compiler_params=_params(),
        name="route_n%d" % n,
    )(afft, h2)


def _slot_gates(aff_ref, mask_ref, pos_ref, e, cap):
    cols = []
    for b in range(aff_ref.shape[0] // N_EXPERTS):
        r = b * N_EXPERTS + e
        onehot, = _slot_onehot(mask_ref[pl.ds(r, 1), :], pos_ref[pl.ds(r, 1), :], cap)
        cols.append(jnp.sum(jnp.where(onehot, aff_ref[pl.ds(r, 1), :], 0.0), axis=1, keepdims=True))
    return jnp.concatenate(cols, axis=0)


def _ffn_kernel(xp_ref, xs_ref, ap_ref, mp_ref, pp_ref, as_ref, ms_ref, ps_ref, wgu_ref, wd_ref, yp_ref, ys_ref):
    e = pl.program_id(0)
    rows_p = xp_ref.shape[1]
    xs = jnp.concatenate([xp_ref[0], xs_ref[0]], axis=0)
    gu = _dot(xs, wgu_ref[0].astype(BF16))
    gate = gu[:, :D_EXPERT]
    up = gu[:, D_EXPERT:]
    act = (gate * jax.nn.sigmoid(gate) * up).astype(BF16)
    y = _dot(act, wd_ref[0].astype(BF16))
    g = jnp.concatenate([_slot_gates(ap_ref, mp_ref, pp_ref, e, rows_p * N_EXPERTS // ap_ref.shape[0]),
                         _slot_gates(as_ref, ms_ref, ps_ref, e, xs_ref.shape[1] * N_EXPERTS // as_ref.shape[0])],
                        axis=0)
    y = (y * g).astype(BF16)
    yp_ref[0] = y[:rows_p]
    ys_ref[0] = y[rows_p:]


def _ffn(xs_p, xs_s, route_p, route_s, w_gate_up, w_down):
    per_e = lambda a: pl.BlockSpec((1,) + a.shape[1:], lambda e: (e, 0, 0))
    tables = list(route_p) + list(route_s)
    return pl.pallas_call(
        _ffn_kernel,
        grid=(N_EXPERTS,),
        in_specs=[per_e(xs_p), per_e(xs_s)] + [_full(t.shape) for t in tables] + [per_e(w_gate_up), per_e(w_down)],
        out_specs=[per_e(xs_p), per_e(xs_s)],
        out_shape=[jax.ShapeDtypeStruct(xs_p.shape, BF16), jax.ShapeDtypeStruct(xs_s.shape, BF16)],
        compiler_params=_params(),
        name="expert_ffn",
    )(xs_p, xs_s, *tables, w_gate_up, w_down)


def _combine_kernel(y_ref, mask_ref, pos_ref, x1_ref, mod_ref, g2_ref, b2_ref, o_ref, *, cap, seqs, mod_row0,
                    mod_row_step):
    row = mod_row0 + mod_row_step * pl.program_id(0)
    gate2 = mod_ref[pl.ds(row, 1), 5 * D_MODEL:6 * D_MODEL]
    tn = x1_ref.shape[0] // seqs
    for j in range(seqs):
        e0 = j * N_EXPERTS
        sel = jnp.concatenate(
            _slot_onehot(mask_ref[e0:e0 + N_EXPERTS, :], pos_ref[e0:e0 + N_EXPERTS, :], cap), axis=0).astype(BF16)
        y = y_ref[:, j * cap:(j + 1) * cap, :].reshape(N_EXPERTS * cap, D_MODEL)
        ff = _dot_tn(sel, y)
        r = pl.ds(j * tn, tn)
        o_ref[r, :] = _ln_plain(ALPHA * x1_ref[r, :] + gate2 * ff) * g2_ref[...] + b2_ref[...]


def _combine(y, mask, pos, x1, mod, g2, b2, *, n, nb, cap, seqs, sample):
    tn = min(n, COMBINE_TILE)
    tiles = n // tn
    assert seqs == 1 or tiles == 1
    return pl.pallas_call(
        functools.partial(_combine_kernel, cap=cap, seqs=seqs, mod_row0=1 if sample else 0,
                          mod_row_step=1 if sample else 0),
        grid=(nb // seqs, tiles),
        in_specs=[pl.BlockSpec((N_EXPERTS, seqs * cap, D_MODEL), lambda b, t: (0, b, 0)),
                  pl.BlockSpec((seqs * N_EXPERTS, tn), lambda b, t: (b, t)),
                  pl.BlockSpec((seqs * N_EXPERTS, tn), lambda b, t: (b, t)),
                  pl.BlockSpec((seqs * tn, D_MODEL), lambda b, t: (b * tiles + t, 0)),
                  _full(mod.shape), _full(g2.shape), _full(b2.shape)],
        out_specs=pl.BlockSpec((seqs * tn, D_MODEL), lambda b, t: (b * tiles + t, 0)),
        out_shape=jax.ShapeDtypeStruct((nb * n, D_MODEL), F32),
        compiler_params=_params(2),
        name="combine_sample" if sample else "combine_prompt",
    )(y, mask, pos, x1, mod, g2, b2)


def _block_diag_256(w):
    per_tile = MXU_DIM // RG_BLOCK
    tiles = []
    for c in range(w.shape[0] // per_tile):
        tiles.append(jax.scipy.linalg.block_diag(*[w[c * per_tile + i] for i in range(per_tile)]))
    return jnp.stack(tiles)


def _rope_tables(n):
    lane = np.arange(LANES)
    within = lane % HEAD_DIM
    freq = (within % 16).astype(np.float32)
    inv = np.float32(ROPE_THETA) ** (-freq / np.float32(16.0))
    tok = np.arange(n)
    pos = np.where((within < HEAD_DIM // 2)[None, :], (tok // GRID_W)[:, None], (tok % GRID_W)[:, None])
    ang = pos.astype(np.float32) * inv[None, :]
    sign = np.where(within % 32 < 16, -1.0, 1.0).astype(np.float32)
    return jnp.asarray(np.cos(ang)), jnp.asarray(np.sin(ang) * sign[None, :])


def kernel(x_prompt, x_sample, cache_k, cache_v, state_h, c, c_ctx, w_mod, b_mod, w_in, q_norm_g, k_norm_g,
           conv_w, conv_b, w_rg_a, b_rg_a, w_rg_x, b_rg_x, rg_lambda, w_out, ln1_g, ln1_b, w_router,
           w_gate_up, w_down, ln2_g, ln2_b):
    assert w_mod.shape[0] == DEPTH == 1
    nb_p, n_p, _ = x_prompt.shape
    nb_s, n_s, _ = x_sample.shape
    past = cache_k.shape[2]
    cap_p = CAP_FACTOR * n_p // N_EXPERTS
    cap_s = CAP_FACTOR * n_s // N_EXPERTS

    cs = jnp.concatenate([c_ctx[None, :], c, jnp.zeros((MOD_ROWS - 1 - nb_s, D_MODEL), F32)], axis=0)
    row = lambda v: v.reshape(1, -1)
    head_avg = np.kron(np.eye(ATTN_WIDTH // HEAD_DIM, dtype=np.float32),
                       np.full((HEAD_DIM, HEAD_DIM), 1.0 / HEAD_DIM, np.float32))
    bd = jnp.asarray(head_avg, BF16)
    qg = row(jnp.tile(q_norm_g[0], N_Q_HEADS))
    kg = row(jnp.tile(k_norm_g[0], N_KV_HEADS))
    wg = 0.5 * jnp.concatenate([_block_diag_256(w_rg_a[0, 0]), _block_diag_256(w_rg_x[0, 0]),
                                _block_diag_256(w_rg_a[0, 1]), _block_diag_256(w_rg_x[0, 1])], axis=2)
    bg = 0.5 * jnp.stack([b_rg_a[0, 0], b_rg_x[0, 0], b_rg_a[0, 1], b_rg_x[0, 1]])
    consts = (bd, qg, kg, conv_w[0], row(conv_b[0]), wg, bg, rg_lambda[0])
    wrt = w_router[0].T
    g1, b1, g2, b2 = row(ln1_g[0]), row(ln1_b[0]), row(ln2_g[0]), row(ln2_b[0])

    win_bf, wout_bf = _cast_weights(w_in[0], w_out[0])
    mod = _mod_vectors(cs, w_mod[0], row(b_mod[0]))

    xp = x_prompt.reshape(nb_p * n_p, D_MODEL)
    xs = x_sample.reshape(nb_s * n_s, D_MODEL)

    new_k, new_v, new_h, x1_p, h2_p, aff_p = _mixer_prompt(
        xp, mod, win_bf, wout_bf, consts, g1, b1, wrt, n=n_p, nb=nb_p)

    q_s, k_s, v_s, rg_s = _front_sample(xs, mod, win_bf, consts, state_h[:, 0], _rope_tables(n_s), n=n_s, nb=nb_s)
    attn_s = _attn_sample(q_s, k_s, v_s, cache_k[:, 0].reshape(nb_s * past, KV_WIDTH),
                          cache_v[:, 0].reshape(nb_s * past, KV_WIDTH), n=n_s, nb=nb_s, past=past)
    x1_s, h2_s, aff_s = _post_sample(attn_s, rg_s, xs, mod, wout_bf, g1, b1, wrt, n=n_s, nb=nb_s)

    seqs_p = PROMPT_SEQS_PER_STEP if nb_p % PROMPT_SEQS_PER_STEP == 0 else 1
    xs_p, mask_p, pos_p = _route(aff_p, h2_p, n=n_p, nb=nb_p, cap=cap_p, seqs=seqs_p)
    xs_s, mask_s, pos_s = _route(aff_s, h2_s, n=n_s, nb=nb_s, cap=cap_s, seqs=1)

    y_p, y_s = _ffn(xs_p, xs_s, (aff_p, mask_p, pos_p), (aff_s, mask_s, pos_s), w_gate_up[0], w_down[0])

    out_p = _combine(y_p, mask_p, pos_p, x1_p, mod, g2, b2, n=n_p, nb=nb_p, cap=cap_p, seqs=seqs_p, sample=False)
    out_s = _combine(y_s, mask_s, pos_s, x1_s, mod, g2, b2, n=n_s, nb=nb_s, cap=cap_s, seqs=1, sample=True)

    def cache_layout(t):
        t = t.reshape(nb_p, DEPTH, N_KV_HEADS, HEAD_DIM, n_p)
        return jnp.transpose(t, (0, 1, 4, 2, 3))

    return (out_p.reshape(nb_p, n_p, D_MODEL), out_s.reshape(nb_s, n_s, D_MODEL),
            cache_layout(new_k), cache_layout(new_v), new_h.reshape(nb_p, DEPTH, 2, RG_WIDTH))
```

```python
import functools

import numpy as np
import jax
import jax.numpy as jnp
from jax import lax
from jax.experimental import pallas as pl
from jax.experimental.pallas import tpu as pltpu

F32 = jnp.float32
BF16 = jnp.bfloat16

D_MODEL = 1024
HEAD_DIM = 64
N_Q_HEADS = 8
N_KV_HEADS = 2
GROUP = N_Q_HEADS // N_KV_HEADS
ATTN_WIDTH = N_Q_HEADS * HEAD_DIM
KV_WIDTH = N_KV_HEADS * HEAD_DIM
RG_WIDTH = D_MODEL - ATTN_WIDTH
RG_BLOCK = 64
CONV_W = 4
C_LRU = 8.0
N_EXPERTS = 16
CAP_FACTOR = 2
D_EXPERT = 1024
GRID_W = 64
ROPE_THETA = 10000.0
NORM_EPS = 1e-6
DEPTH = 1
ALPHA = (2.0 * DEPTH) ** 0.25
IN_WIDTH = ATTN_WIDTH + 2 * KV_WIDTH + 2 * RG_WIDTH
Q_SCALE = HEAD_DIM ** -0.5 * float(np.log2(np.e))

LANES = 128
SUBLANES = 8
MXU_DIM = 256
VMEM_LIMIT_BYTES = 56 * 1024 * 1024

MOD_ROWS = SUBLANES
MOD_TILE = 256
ATTN_Q_TILE = 256
COMBINE_TILE = 256
MIXER_ORDER = (1, 0, 0, 1, 1, 0, 1, 0, 1, 0, 1, 0, 1, 1, 0, 1, 0, 0, 1, 1, 0, 1, 0, 0, 0, 1, 0, 0, 1, 0)
PERM_BLOCK = MXU_DIM
SEG_LEN = PERM_BLOCK // SUBLANES
POST_ROW_STREAMS = 4
MIXER_ROW_STREAMS = 1
PROMPT_SEQS_PER_STEP = 4


def _params(n_axes=1, flags=None):
    return pltpu.CompilerParams(dimension_semantics=("arbitrary",) * n_axes,
                                vmem_limit_bytes=VMEM_LIMIT_BYTES, flags=flags)


def _full(shape):
    zeros = (0,) * len(shape)
    return pl.BlockSpec(shape, lambda *_: zeros)


def _ln_plain(x):
    mu = jnp.mean(x, -1, keepdims=True)
    xc = x - mu
    var = jnp.mean(xc * xc, -1, keepdims=True)
    return xc * lax.rsqrt(var + NORM_EPS)


def _dot(a, b):
    return jnp.dot(a, b, preferred_element_type=F32)


def _dot_nt(a, b):
    return lax.dot_general(a, b, (((1,), (1,)), ((), ())), preferred_element_type=F32)


def _dot_tn(a, b):
    return lax.dot_general(a, b, (((0,), (0,)), ((), ())), preferred_element_type=F32)


def _split_bf16(x):
    hi = x.astype(BF16)
    lo = (x - hi.astype(F32)).astype(BF16)
    return hi, lo


def _cast_kernel(a_ref, b_ref, ao_ref, bo_ref):
    ao_ref[...] = a_ref[...].astype(BF16)
    bo_ref[...] = b_ref[...].astype(BF16)


def _cast_weights(w_in, w_out):
    return pl.pallas_call(
        _cast_kernel,
        grid=(1,),
        in_specs=[_full(w_in.shape), _full(w_out.shape)],
        out_specs=[_full(w_in.shape), _full(w_out.shape)],
        out_shape=[jax.ShapeDtypeStruct(w_in.shape, BF16), jax.ShapeDtypeStruct(w_out.shape, BF16)],
        compiler_params=_params(),
        name="cast_weights",
    )(w_in, w_out)


def _mod_kernel(cs_ref, w_ref, b_ref, o_ref):
    @pl.when(pl.program_id(0) == 0)
    def _():
        o_ref[...] = jnp.broadcast_to(b_ref[...], o_ref.shape)

    cs = cs_ref[...]
    s = cs * jax.nn.sigmoid(cs)
    o_ref[...] += _dot(s.astype(BF16), w_ref[...].astype(BF16))


def _mod_vectors(cs, w_mod, b_mod):
    width = w_mod.shape[1]
    return pl.pallas_call(
        _mod_kernel,
        grid=(D_MODEL // MOD_TILE,),
        in_specs=[pl.BlockSpec((MOD_ROWS, MOD_TILE), lambda j: (0, j)),
                  pl.BlockSpec((MOD_TILE, width), lambda j: (j, 0)),
                  _full(b_mod.shape)],
        out_specs=_full((MOD_ROWS, width)),
        out_shape=jax.ShapeDtypeStruct((MOD_ROWS, width), F32),
        compiler_params=_params(),
        name="mod_vectors",
    )(cs, w_mod, b_mod)


def _head_mean_sq(x, bd):
    hi, lo = _split_bf16(x * x)
    return _dot(hi, bd) + _dot(lo, bd)


def _rope_lanes(x, cos, sin_signed, hi_half):
    partner = jnp.where(hi_half, pltpu.roll(x, 16, axis=1), pltpu.roll(x, LANES - 16, axis=1))
    return x * cos + partner * sin_signed


def _modulated(x, m):
    return (_ln_plain(x) * (1.0 + m[:, D_MODEL:2 * D_MODEL]) + m[:, 0:D_MODEL]).astype(BF16)


def _qkv(h, win_ref, bd_ref, qg_ref, kg_ref, rope):
    n = h.shape[0]
    bd = bd_ref[...]
    q = _dot(h, win_ref[:, 0:ATTN_WIDTH])
    yield
    q = q * lax.rsqrt(_head_mean_sq(q, bd) + NORM_EPS) * qg_ref[...]
    yield
    k = _dot(h, win_ref[:, ATTN_WIDTH:ATTN_WIDTH + KV_WIDTH])
    k = k * lax.rsqrt(_head_mean_sq(k, bd[:KV_WIDTH, :KV_WIDTH]) + NORM_EPS) * kg_ref[...]
    v = _dot(h, win_ref[:, ATTN_WIDTH + KV_WIDTH:ATTN_WIDTH + 2 * KV_WIDTH])
    yield
    if rope is not None:
        cos, sin_signed = rope
        lane = lax.broadcasted_iota(jnp.int32, (n, LANES), 1)
        hi_half = (lane & 16) != 0
        q = jnp.concatenate(
            [_rope_lanes(q[:, j * LANES:(j + 1) * LANES], cos, sin_signed, hi_half)
             for j in range(ATTN_WIDTH // LANES)], axis=1)
        k = _rope_lanes(k, cos, sin_signed, hi_half)
    return (q * Q_SCALE).astype(BF16), k, v


def _weave(streams, order=()):
    results = [None] * len(streams)
    live = list(range(len(streams)))
    plan = [j for j in order]
    while live:
        idx = plan.pop(0) if plan else live[0]
        if idx not in live:
            continue
        if not plan:
            live.append(live.pop(0))
        try:
            next(streams[idx])
            yield
        except StopIteration as stop:
            results[idx] = stop.value
            live.remove(idx)
    return results


def _stagger(streams, chunks, lag):
    return [j for t in range(chunks + lag * (streams - 1)) for j in range(streams) if 0 <= t - lag * j < chunks]


def _drive(*streams, order=()):
    weave = _weave(streams, order)
    while True:
        try:
            next(weave)
        except StopIteration as stop:
            return stop.value


def _attend(q, k, v, lookahead):
    def scores(hq):
        kv = hq // GROUP
        return _dot_nt(q[:, hq * HEAD_DIM:(hq + 1) * HEAD_DIM], k[:, kv * HEAD_DIM:(kv + 1) * HEAD_DIM])

    outs = []
    s_next = scores(0) if lookahead else None
    for hq in range(N_Q_HEADS):
        if lookahead:
            s, s_next = s_next, (scores(hq + 1) if hq + 1 < N_Q_HEADS else None)
        else:
            s = scores(hq)
        kv = hq // GROUP
        e = jnp.exp2(s - jnp.max(s, axis=-1, keepdims=True))
        denom = jnp.sum(e, axis=-1, keepdims=True)
        pv = _dot(e.astype(BF16), v[:, kv * HEAD_DIM:(kv + 1) * HEAD_DIM])
        outs.append(pv / denom)
        yield
    return jnp.concatenate(outs, axis=1).astype(BF16)


def _rg_inputs(h, win_ref):
    rg_lo = ATTN_WIDTH + 2 * KV_WIDTH
    xr = _dot(h, win_ref[:, rg_lo:rg_lo + RG_WIDTH])
    yield
    gr = _dot(h, win_ref[:, rg_lo + RG_WIDTH:IN_WIDTH])
    yield
    return xr, gr


def _segment_permutation():
    p = np.zeros((PERM_BLOCK, PERM_BLOCK), np.float32)
    for t in range(SEG_LEN):
        for j in range(SUBLANES):
            p[t * SUBLANES + j, j * SEG_LEN + t] = 1.0
    return p


def _rglru(xr_ref, gr_ref, perm_t_ref, cw_ref, cb_ref, wg_ref, bg_ref, lam_ref, h0, scan_refs):
    af_ref, bf_ref, ab_ref, bb_ref = scan_refs
    n = xr_ref.shape[0]
    blocks = n // PERM_BLOCK
    sub = lax.broadcasted_iota(jnp.int32, (SUBLANES, 1), 0)
    zero_row = jnp.zeros((1, RG_WIDTH), F32)
    cw = cw_ref[...]
    neg = -lam_ref[...]
    softplus = jnp.maximum(neg, 0.0) + jnp.log1p(jnp.exp(-jnp.abs(neg)))
    decay = (-0.5 * C_LRU) * softplus

    def group(b, t):
        return xr_ref[b * PERM_BLOCK + t * SUBLANES:b * PERM_BLOCK + (t + 1) * SUBLANES, :]

    def before(b, t):
        wrap = group(b - 1, t)[SUBLANES - 1:SUBLANES, :] if b > 0 else zero_row
        return jnp.where(sub == 0, wrap, pltpu.roll(group(b, t), 1, axis=0))

    def after(b, t):
        wrap = group(b + 1, t)[0:1, :] if b + 1 < blocks else zero_row
        return jnp.where(sub == SUBLANES - 1, wrap, pltpu.roll(group(b, t), SUBLANES - 1, axis=0))

    ends = []
    for b in range(blocks):
        rows = slice(b * PERM_BLOCK, (b + 1) * PERM_BLOCK)
        ext = jnp.concatenate([before(b, SEG_LEN - 2), before(b, SEG_LEN - 1), xr_ref[rows, :], after(b, 0)], axis=0)
        xc = ext[0:PERM_BLOCK] * cw[0:1, :]
        for tap in range(1, CONV_W):
            xc = xc + ext[tap * SUBLANES:tap * SUBLANES + PERM_BLOCK] * cw[tap:tap + 1, :]
        xc = xc + cb_ref[...]
        yield

        xcb = xc.astype(BF16)
        halves = [_dot(xcb[:, c * MXU_DIM:(c + 1) * MXU_DIM], wg_ref[c].astype(BF16))
                  for c in range(RG_WIDTH // MXU_DIM)]
        yield
        half_xc = 0.5 * xc
        for d, (a_ref, b_ref) in enumerate(((af_ref, bf_ref), (ab_ref, bb_ref))):
            pre = [jnp.concatenate([hv[:, i * MXU_DIM:(i + 1) * MXU_DIM] for hv in halves], axis=1)
                   + bg_ref[i:i + 1, :] for i in (2 * d, 2 * d + 1)]
            r2 = jnp.tanh(pre[0]) + 1.0
            i2 = jnp.tanh(pre[1]) + 1.0
            log_a = r2 * decay[d:d + 1, :]
            a = jnp.exp(log_a)
            a_ref[rows, :] = a
            b_ref[rows, :] = jnp.sqrt(jnp.tanh(-log_a) * (a * a + 1.0)) * (i2 * half_xc)
            yield

        for a_ref, b_ref, steps in ((af_ref, bf_ref, range(SEG_LEN)), (ab_ref, bb_ref, range(SEG_LEN - 1, -1, -1))):
            hend = jnp.zeros((SUBLANES, RG_WIDTH), F32)
            pend = jnp.ones((SUBLANES, RG_WIDTH), F32)
            for t in steps:
                r = slice(b * PERM_BLOCK + t * SUBLANES, b * PERM_BLOCK + (t + 1) * SUBLANES)
                a = a_ref[r, :]
                hend = a * hend + b_ref[r, :]
                pend = a * pend
            ends.append((hend, pend))
            yield

    def carries(order, state, which):
        into = {}
        for b, j in order:
            into[b, j] = state
            hend, pend = ends[2 * b + which]
            state = hend[j:j + 1, :] + pend[j:j + 1, :] * state
        return into, state

    segs = [(b, j) for b in range(blocks) for j in range(SUBLANES)]
    into_f, hf = carries(segs, h0[0:1, :], 0)
    into_b, hb = carries(segs[::-1], h0[1:2, :], 1)
    yield

    for b in range(blocks):
        for a_ref, b_ref, into, steps in ((af_ref, bf_ref, into_f, range(SEG_LEN)),
                                          (ab_ref, bb_ref, into_b, range(SEG_LEN - 1, -1, -1))):
            h = jnp.concatenate([into[b, j] for j in range(SUBLANES)], axis=0)
            for t in steps:
                r = slice(b * PERM_BLOCK + t * SUBLANES, b * PERM_BLOCK + (t + 1) * SUBLANES)
                h = a_ref[r, :] * h + b_ref[r, :]
                b_ref[r, :] = h
            yield

    out = []
    for b in range(blocks):
        rows = slice(b * PERM_BLOCK, (b + 1) * PERM_BLOCK)
        rg = ((bf_ref[rows, :] + bb_ref[rows, :]) * jax.nn.gelu(gr_ref[rows, :])).astype(BF16)
        out.append(_dot(perm_t_ref[...], rg).astype(BF16))
        yield
    return jnp.concatenate(out, axis=0), hf, hb


def _out_and_router(attn, rg, x, m, wout_ref, g1_ref, b1_ref, wrt_ref):
    gate1 = m[:, 2 * D_MODEL:3 * D_MODEL]
    shift2 = m[:, 3 * D_MODEL:4 * D_MODEL]
    scale2 = m[:, 4 * D_MODEL:5 * D_MODEL]
    mix = _dot(jnp.concatenate([attn, rg], axis=1), wout_ref[...])
    yield
    x1 = _ln_plain(ALPHA * x + gate1 * mix) * g1_ref[...] + b1_ref[...]
    yield
    h2 = (_ln_plain(x1) * (1.0 + scale2) + shift2).astype(BF16)
    yield
    logits = _dot_nt(wrt_ref[...].astype(BF16), h2)
    e = jnp.exp(logits - jnp.max(logits, axis=0, keepdims=True))
    return x1, h2, e / jnp.sum(e, axis=0, keepdims=True)


def _mixer_kernel(x_ref, mod_ref, win_ref, wout_ref, bd_ref, qg_ref, kg_ref, cw_ref, cb_ref, wg_ref, bg_ref,
                  lam_ref, perm_ref, perm_t_ref, g1_ref, b1_ref, wrt_ref,
                  nk_ref, nv_ref, nh_ref, x1_ref, h2_ref, afft_ref,
                  xprev_ref, xr_ref, gr_ref, attn_ref, *scan_refs):
    @pl.when(pl.program_id(0) == 0)
    def _():
        xprev_ref[...] = jnp.zeros_like(xprev_ref)
        xr_ref[...] = jnp.zeros_like(xr_ref)
        gr_ref[...] = jnp.zeros_like(gr_ref)
        attn_ref[...] = jnp.zeros_like(attn_ref)

    m = mod_ref[0:1, :]

    def second_half():
        h0 = jnp.zeros((2, RG_WIDTH), F32)
        rg, hf, hb = yield from _rglru(xr_ref, gr_ref, perm_t_ref, cw_ref, cb_ref, wg_ref, bg_ref, lam_ref, h0,
                                       scan_refs)
        nh_ref[0] = jnp.concatenate([hf, hb], axis=0)
        rows = rg.shape[0] // MIXER_ROW_STREAMS
        blocks = [slice(j * rows, (j + 1) * rows) for j in range(MIXER_ROW_STREAMS)]
        outs = yield from _weave(
            [_out_and_router(attn_ref[r, :], rg[r, :], xprev_ref[r, :], m, wout_ref, g1_ref, b1_ref, wrt_ref)
             for r in blocks], _stagger(MIXER_ROW_STREAMS, 4, 1))
        for r, (x1, h2, aff) in zip(blocks, outs):
            x1_ref[r, :] = x1
            h2_ref[r, :] = h2
            afft_ref[:, r] = aff

    def first_half():
        x = x_ref[...]
        h = _modulated(x, m)
        yield
        q, k, v = yield from _qkv(h, win_ref, bd_ref, qg_ref, kg_ref, None)
        nk_ref[0] = k.T
        nv_ref[0] = v.T
        yield
        xr, gr = yield from _rg_inputs(_dot(perm_ref[...], h).astype(BF16), win_ref)
        attn = yield from _attend(q, k.astype(BF16), v.astype(BF16), False)
        return x, attn, xr, gr

    (x, attn, xr, gr), _ = _drive(first_half(), second_half(), order=MIXER_ORDER)
    xprev_ref[...] = x
    attn_ref[...] = attn
    xr_ref[...] = xr
    gr_ref[...] = gr


def _mixer_prompt(x2d, mod, win_bf, wout_bf, consts, g1, b1, wrt, *, n, nb):
    assert n == PERM_BLOCK
    tokens = nb * n
    first =lambda w: pl.BlockSpec((n, w), lambda i: (jnp.minimum(i, nb - 1), 0))
    second = lambda w: pl.BlockSpec((n, w), lambda i: (jnp.maximum(i - 1, 0), 0))
    small = list(consts) + [g1, b1, wrt]
    return pl.pallas_call(
        _mixer_kernel,
        grid=(nb + 1,),
        in_specs=[first(D_MODEL), _full(mod.shape), _full(win_bf.shape), _full(wout_bf.shape)]
                 + [_full(a.shape) for a in small],
        out_specs=[pl.BlockSpec((1, KV_WIDTH, n), lambda i: (jnp.minimum(i, nb - 1), 0, 0)),
                   pl.BlockSpec((1, KV_WIDTH, n), lambda i: (jnp.minimum(i, nb - 1), 0, 0)),
                   pl.BlockSpec((1, 2, RG_WIDTH), lambda i: (jnp.maximum(i - 1, 0), 0, 0)),
                   second(D_MODEL), second(D_MODEL),
                   pl.BlockSpec((N_EXPERTS, n), lambda i: (jnp.maximum(i - 1, 0), 0))],
        out_shape=[jax.ShapeDtypeStruct((nb, KV_WIDTH, n), F32), jax.ShapeDtypeStruct((nb, KV_WIDTH, n), F32),
                   jax.ShapeDtypeStruct((nb, 2, RG_WIDTH), F32),
                   jax.ShapeDtypeStruct((tokens, D_MODEL), F32), jax.ShapeDtypeStruct((tokens, D_MODEL), BF16),
                   jax.ShapeDtypeStruct((nb * N_EXPERTS, n), F32)],
        scratch_shapes=[pltpu.VMEM((n, D_MODEL), F32), pltpu.VMEM((n, RG_WIDTH), F32),
                        pltpu.VMEM((n, RG_WIDTH), F32), pltpu.VMEM((n, ATTN_WIDTH), BF16)]
                       + [pltpu.VMEM((n, RG_WIDTH), F32)] * 4,
        compiler_params=_params(),
        name="mixer_prompt",
    )(x2d, mod, win_bf, wout_bf, *small)


def _front_kernel(x_ref, mod_ref, win_ref, bd_ref, qg_ref, kg_ref, cw_ref, cb_ref, wg_ref, bg_ref, lam_ref,
                  perm_ref, perm_t_ref, h0_ref, cos_ref, sin_ref,
                  q_ref, k_ref, v_ref, rg_ref, xr_ref, gr_ref, *scan_refs):
    m = mod_ref[pl.ds(1 + pl.program_id(0), 1), :]
    rows = PERM_BLOCK
    streams = x_ref.shape[0] // rows

    def row_block(j):
        r = pl.ds(j * rows, rows)
        h = _modulated(x_ref[r, :], m)
        yield
        q, k, v = yield from _qkv(h, win_ref, bd_ref, qg_ref, kg_ref, (cos_ref[r, :], sin_ref[r, :]))
        q_ref[r, :] = q
        k_ref[r, :] = k.astype(BF16)
        v_ref[r, :] = v.astype(BF16)
        yield
        hp = _dot(perm_ref[...], h).astype(BF16)
        xr_ref[r, :], gr_ref[r, :] = yield from _rg_inputs(hp, win_ref)

    chunks, lag = 8, 2
    stagger = [j for t in range(chunks + lag * (streams - 1)) for j in range(streams) if 0 <= t - lag * j < chunks]
    _drive(*[row_block(j) for j in range(streams)], order=stagger)
    (rg_ref[...], _, _), = _drive(_rglru(xr_ref, gr_ref, perm_t_ref, cw_ref, cb_ref, wg_ref, bg_ref, lam_ref,
                                         h0_ref[0], scan_refs))


def _front_sample(x2d, mod, win_bf, consts, h0, rope, *, n, nb):
    tokens = nb * n
    seq = lambda w: pl.BlockSpec((n, w), lambda b: (b, 0))
    small = list(consts)
    return pl.pallas_call(
        _front_kernel,
        grid=(nb,),
        in_specs=[seq(D_MODEL), _full(mod.shape), _full(win_bf.shape)] + [_full(a.shape) for a in small]
                 + [pl.BlockSpec((1, 2, RG_WIDTH), lambda b: (b, 0, 0)), _full(rope[0].shape), _full(rope[1].shape)],
        out_specs=[seq(ATTN_WIDTH), seq(KV_WIDTH), seq(KV_WIDTH), seq(RG_WIDTH)],
        out_shape=[jax.ShapeDtypeStruct((tokens, ATTN_WIDTH), BF16), jax.ShapeDtypeStruct((tokens, KV_WIDTH), BF16),
                   jax.ShapeDtypeStruct((tokens, KV_WIDTH), BF16), jax.ShapeDtypeStruct((tokens, RG_WIDTH), BF16)],
        scratch_shapes=[pltpu.VMEM((n, RG_WIDTH), F32)] * 6,
        compiler_params=_params(),
        name="front_sample",
    )(x2d, mod, win_bf, *small, h0, *rope)


def _attn_kernel(q_ref, k_ref, v_ref, kc_ref, vc_ref, o_ref):
    k = jnp.concatenate([k_ref[...], kc_ref[...].astype(BF16)], axis=0)
    v = jnp.concatenate([v_ref[...], vc_ref[...].astype(BF16)], axis=0)
    o_ref[...], = _drive(_attend(q_ref[...], k, v, True))


def _attn_sample(q, k, v, cache_k, cache_v, *, n, nb, past):
    tq = min(n, ATTN_Q_TILE)
    tiles = n // tq
    return pl.pallas_call(
        _attn_kernel,
        grid=(nb, tiles),
        in_specs=[pl.BlockSpec((tq, ATTN_WIDTH), lambda b, t: (b * tiles + t, 0)),
                  pl.BlockSpec((n, KV_WIDTH), lambda b, t: (b, 0)),
                  pl.BlockSpec((n, KV_WIDTH), lambda b, t: (b, 0)),
                  pl.BlockSpec((past, KV_WIDTH), lambda b, t: (b, 0)),
                  pl.BlockSpec((past, KV_WIDTH), lambda b, t: (b, 0))],
        out_specs=pl.BlockSpec((tq, ATTN_WIDTH), lambda b, t: (b * tiles + t, 0)),
        out_shape=jax.ShapeDtypeStruct((nb * n, ATTN_WIDTH), BF16),
        compiler_params=_params(2),
        name="attn_sample",
    )(q, k, v, cache_k, cache_v)


def _post_kernel(attn_ref, rg_ref, x_ref, mod_ref, wout_ref, g1_ref, b1_ref, wrt_ref, x1_ref, h2_ref, afft_ref):
    m = mod_ref[pl.ds(1 + pl.program_id(0), 1), :]
    rows = x_ref.shape[0] // POST_ROW_STREAMS

    def row_block(j):
        r = pl.ds(j * rows, rows)
        x1_ref[r, :], h2_ref[r, :], afft_ref[:, r] = yield from _out_and_router(
            attn_ref[r, :], rg_ref[r, :], x_ref[r, :], m, wout_ref, g1_ref, b1_ref, wrt_ref)

    stagger = [j for i in range(POST_ROW_STREAMS) for j in range(i + 1)]
    _drive(*[row_block(j) for j in range(POST_ROW_STREAMS)], order=stagger)


def _post_sample(attn, rg, x2d, mod, wout_bf, g1, b1, wrt, *, n, nb):
    tokens = nb * n
    seq = lambda w: pl.BlockSpec((n, w), lambda b: (b, 0))
    return pl.pallas_call(
        _post_kernel,
        grid=(nb,),
        in_specs=[seq(ATTN_WIDTH), seq(RG_WIDTH), seq(D_MODEL), _full(mod.shape), _full(wout_bf.shape),
                  _full(g1.shape), _full(b1.shape), _full(wrt.shape)],
        out_specs=[seq(D_MODEL), seq(D_MODEL), pl.BlockSpec((N_EXPERTS, n), lambda b: (b, 0))],
        out_shape=[jax.ShapeDtypeStruct((tokens, D_MODEL), F32), jax.ShapeDtypeStruct((tokens, D_MODEL), BF16),
                   jax.ShapeDtypeStruct((nb * N_EXPERTS, n), F32)],
        compiler_params=_params(),
        name="post_sample",
    )(attn, rg, x2d, mod, wout_bf, g1, b1, wrt)


def _slot_onehot(mask, pos, cap):
    n = mask.shape[1]
    slot = lax.broadcasted_iota(jnp.int32, (cap, n), 0).astype(F32)
    return [(slot == pos[e:e + 1, :]) & (mask[e:e + 1, :] > 0.5) for e in range(mask.shape[0])]


def _route_kernel(afft_ref, h2_ref, xs_ref, mask_ref, pos_ref, *, n, cap, seqs):
    b = pl.program_id(0)

    @pl.when(b == 0)
    def _():
        aff = afft_ref[...]
        thr = jnp.zeros((aff.shape[0], 1), jnp.int32)
        for bit in range(30, -1, -1):
            cand = thr | (1 << bit)
            cnt = jnp.sum((aff >= lax.bitcast_convert_type(cand, F32)).astype(F32), axis=1, keepdims=True)
            thr = jnp.where(cnt >= cap, cand, thr)
        above = aff >= lax.bitcast_convert_type(thr + 1, F32)
        tied = (aff >= lax.bitcast_convert_type(thr, F32)) & jnp.logical_not(above)
        need = cap - jnp.sum(above.astype(F32), axis=1, keepdims=True)
        before = (lax.broadcasted_iota(jnp.int32, (n, n), 0)
                  < lax.broadcasted_iota(jnp.int32, (n, n), 1)).astype(BF16)
        tie_rank = _dot(tied.astype(BF16), before)
        mask = (above | (tied & (tie_rank < need))).astype(F32)
        mask_ref[...] = mask
        pos_ref[...] = _dot(mask.astype(BF16), before)

    for j in range(seqs):
        r0 = pl.multiple_of((b * seqs + j) * N_EXPERTS, N_EXPERTS)
        onehots = _slot_onehot(mask_ref[pl.ds(r0, N_EXPERTS), :], pos_ref[pl.ds(r0, N_EXPERTS), :], cap)
        sel = jnp.concatenate(onehots, axis=0).astype(BF16)
        xs = _dot(sel, h2_ref[j * n:(j + 1) * n, :]).astype(BF16)
        xs_ref[:, j * cap:(j + 1) * cap, :] = xs.reshape(N_EXPERTS, cap, D_MODEL)


def _route(afft, h2, *, n, nb, cap, seqs):
    rows = nb * N_EXPERTS
    return pl.pallas_call(
        functools.partial(_route_kernel, n=n, cap=cap, seqs=seqs),
        grid=(nb // seqs,),
        in_specs=[_full(afft.shape), pl.BlockSpec((seqs * n, D_MODEL), lambda b: (b, 0))],
        out_specs=[pl.BlockSpec((N_EXPERTS, seqs * cap, D_MODEL), lambda b: (0, b, 0)),
                   _full((rows, n)), _full((rows, n))],
        out_shape=[jax.ShapeDtypeStruct((N_EXPERTS, nb * cap, D_MODEL), BF16),
                   jax.ShapeDtypeStruct((rows, n), F32), jax.ShapeDtypeStruct((rows, n), F32)],
        compiler_params=_params(),
        name="route_n%d" % n,
    )(afft, h2)


def _slot_gates(aff_ref, mask_ref, pos_ref, e, cap):
    cols = []
    for b in range(aff_ref.shape[0] // N_EXPERTS):
        r = b * N_EXPERTS + e
        onehot, = _slot_onehot(mask_ref[pl.ds(r, 1), :], pos_ref[pl.ds(r, 1), :], cap)
        cols.append(jnp.sum(jnp.where(onehot, aff_ref[pl.ds(r, 1), :], 0.0), axis=1, keepdims=True))
    return jnp.concatenate(cols, axis=0)


def _ffn_kernel(xp_ref, xs_ref, ap_ref, mp_ref, pp_ref, as_ref, ms_ref, ps_ref, wgu_ref, wd_ref, yp_ref, ys_ref):
    e = pl.program_id(0)
    rows_p = xp_ref.shape[1]
    xs = jnp.concatenate([xp_ref[0], xs_ref[0]], axis=0)
    gu = _dot(xs, wgu_ref[0].astype(BF16))
    gate = gu[:, :D_EXPERT]
    up = gu[:, D_EXPERT:]
    act = (gate * jax.nn.sigmoid(gate) * up).astype(BF16)
    y = _dot(act, wd_ref[0].astype(BF16))
    g = jnp.concatenate([_slot_gates(ap_ref, mp_ref, pp_ref, e, rows_p * N_EXPERTS // ap_ref.shape[0]),
                         _slot_gates(as_ref, ms_ref, ps_ref, e, xs_ref.shape[1] * N_EXPERTS // as_ref.shape[0])],
                        axis=0)
    y = (y * g).astype(BF16)
    yp_ref[0] = y[:rows_p]
    ys_ref[0] = y[rows_p:]


def _ffn(xs_p, xs_s, route_p, route_s, w_gate_up, w_down):
    per_e = lambda a: pl.BlockSpec((1,) + a.shape[1:], lambda e: (e, 0, 0))
    tables = list(route_p) + list(route_s)
    return pl.pallas_call(
        _ffn_kernel,
        grid=(N_EXPERTS,),
        in_specs=[per_e(xs_p), per_e(xs_s)] + [_full(t.shape) for t in tables] + [per_e(w_gate_up), per_e(w_down)],
        out_specs=[per_e(xs_p), per_e(xs_s)],
        out_shape=[jax.ShapeDtypeStruct(xs_p.shape, BF16), jax.ShapeDtypeStruct(xs_s.shape, BF16)],
        compiler_params=_params(),
        name="expert_ffn",
    )(xs_p, xs_s, *tables, w_gate_up, w_down)


def _combine_kernel(y_ref, mask_ref, pos_ref, x1_ref, mod_ref, g2_ref, b2_ref, o_ref, *, cap, seqs, mod_row0,
                    mod_row_step):
    row = mod_row0 + mod_row_step * pl.program_id(0)
    gate2 = mod_ref[pl.ds(row, 1), 5 * D_MODEL:6 * D_MODEL]
    tn = x1_ref.shape[0] // seqs
    for j in range(seqs):
        e0 = j * N_EXPERTS
        sel = jnp.concatenate(
            _slot_onehot(mask_ref[e0:e0 + N_EXPERTS, :], pos_ref[e0:e0 + N_EXPERTS, :], cap), axis=0).astype(BF16)
        y = y_ref[:, j * cap:(j + 1) * cap, :].reshape(N_EXPERTS * cap, D_MODEL)
        ff = _dot_tn(sel, y)
        r = pl.ds(j * tn, tn)
        o_ref[r, :] = _ln_plain(ALPHA * x1_ref[r, :] + gate2 * ff) * g2_ref[...] + b2_ref[...]


def _combine(y, mask, pos, x1, mod, g2, b2, *, n, nb, cap, seqs, sample):
    tn = min(n, COMBINE_TILE)
    tiles = n // tn
    assert seqs == 1 or tiles == 1
    return pl.pallas_call(
        functools.partial(_combine_kernel, cap=cap, seqs=seqs, mod_row0=1 if sample else 0,
                          mod_row_step=1 if sample else 0),
        grid=(nb // seqs, tiles),
        in_specs=[pl.BlockSpec((N_EXPERTS, seqs * cap, D_MODEL), lambda b, t: (0, b, 0)),
                  pl.BlockSpec((seqs * N_EXPERTS, tn), lambda b, t: (b, t)),
                  pl.BlockSpec((seqs * N_EXPERTS, tn), lambda b, t: (b, t)),
                  pl.BlockSpec((seqs * tn, D_MODEL), lambda b, t: (b * tiles + t, 0)),
                  _full(mod.shape), _full(g2.shape), _full(b2.shape)],
        out_specs=pl.BlockSpec((seqs * tn, D_MODEL), lambda b, t: (b * tiles + t, 0)),
        out_shape=jax.ShapeDtypeStruct((nb * n, D_MODEL), F32),
        compiler_params=_params(2),
        name="combine_sample" if sample else "combine_prompt",
    )(y, mask, pos, x1, mod, g2, b2)


def _block_diag_256(w):
    per_tile = MXU_DIM // RG_BLOCK
    tiles = []
    for c in range(w.shape[0] // per_tile):
        tiles.append(jax.scipy.linalg.block_diag(*[w[c * per_tile + i] for i in range(per_tile)]))
    return jnp.stack(tiles)


def _rope_tables(n):
    lane = np.arange(LANES)
    within = lane % HEAD_DIM
    freq = (within % 16).astype(np.float32)
    inv = np.float32(ROPE_THETA) ** (-freq / np.float32(16.0))
    tok = np.arange(n)
    pos = np.where((within < HEAD_DIM // 2)[None, :], (tok // GRID_W)[:, None], (tok % GRID_W)[:, None])
    ang = pos.astype(np.float32) * inv[None, :]
    sign = np.where(within % 32 < 16, -1.0, 1.0).astype(np.float32)
    return jnp.asarray(np.cos(ang)), jnp.asarray(np.sin(ang) * sign[None, :])


def kernel(x_prompt, x_sample, cache_k, cache_v, state_h, c, c_ctx, w_mod, b_mod, w_in, q_norm_g, k_norm_g,
           conv_w, conv_b, w_rg_a, b_rg_a, w_rg_x, b_rg_x, rg_lambda, w_out, ln1_g, ln1_b, w_router,
           w_gate_up, w_down, ln2_g, ln2_b):
    assert w_mod.shape[0] == DEPTH == 1
    nb_p, n_p, _ = x_prompt.shape
    nb_s, n_s, _ = x_sample.shape
    past = cache_k.shape[2]
    cap_p = CAP_FACTOR * n_p // N_EXPERTS
    cap_s = CAP_FACTOR * n_s // N_EXPERTS

    cs = jnp.concatenate([c_ctx[None, :], c, jnp.zeros((MOD_ROWS - 1 - nb_s, D_MODEL), F32)], axis=0)
    row = lambda v: v.reshape(1, -1)
    head_avg = np.kron(np.eye(ATTN_WIDTH // HEAD_DIM, dtype=np.float32),
                       np.full((HEAD_DIM, HEAD_DIM), 1.0 / HEAD_DIM, np.float32))
    bd = jnp.asarray(head_avg, BF16)
    qg = row(jnp.tile(q_norm_g[0], N_Q_HEADS))
    kg = row(jnp.tile(k_norm_g[0], N_KV_HEADS))
    wg = 0.5 * jnp.concatenate([_block_diag_256(w_rg_a[0, 0]), _block_diag_256(w_rg_x[0, 0]),
                                _block_diag_256(w_rg_a[0, 1]), _block_diag_256(w_rg_x[0, 1])], axis=2)
    bg = 0.5 * jnp.stack([b_rg_a[0, 0], b_rg_x[0, 0], b_rg_a[0, 1], b_rg_x[0, 1]])
    perm = _segment_permutation()
    consts = (bd, qg, kg, conv_w[0], row(conv_b[0]), wg, bg, rg_lambda[0],
              jnp.asarray(perm, BF16), jnp.asarray(perm.T, BF16))
    wrt = w_router[0].T
    g1, b1, g2, b2 = row(ln1_g[0]), row(ln1_b[0]), row(ln2_g[0]), row(ln2_b[0])

    win_bf, wout_bf = _cast_weights(w_in[0], w_out[0])
    mod = _mod_vectors(cs, w_mod[0], row(b_mod[0]))

    xp = x_prompt.reshape(nb_p * n_p, D_MODEL)
    xs = x_sample.reshape(nb_s * n_s, D_MODEL)

    new_k, new_v, new_h, x1_p, h2_p, aff_p = _mixer_prompt(
        xp, mod, win_bf, wout_bf, consts, g1, b1, wrt, n=n_p, nb=nb_p)

    q_s, k_s, v_s, rg_s = _front_sample(xs, mod, win_bf, consts, state_h[:, 0], _rope_tables(n_s), n=n_s, nb=nb_s)
    attn_s = _attn_sample(q_s, k_s, v_s, cache_k[:, 0].reshape(nb_s * past, KV_WIDTH),
                          cache_v[:, 0].reshape(nb_s * past, KV_WIDTH), n=n_s, nb=nb_s, past=past)
    x1_s, h2_s, aff_s = _post_sample(attn_s, rg_s, xs, mod, wout_bf, g1, b1, wrt, n=n_s, nb=nb_s)

    seqs_p = PROMPT_SEQS_PER_STEP if nb_p % PROMPT_SEQS_PER_STEP == 0 else 1
    xs_p, mask_p, pos_p = _route(aff_p, h2_p, n=n_p, nb=nb_p, cap=cap_p, seqs=seqs_p)
    xs_s, mask_s, pos_s = _route(aff_s, h2_s, n=n_s, nb=nb_s, cap=cap_s, seqs=1)

    y_p, y_s = _ffn(xs_p, xs_s, (aff_p, mask_p, pos_p), (aff_s, mask_s, pos_s), w_gate_up[0], w_down[0])

    out_p = _combine(y_p, mask_p, pos_p, x1_p, mod, g2, b2, n=n_p, nb=nb_p, cap=cap_p, seqs=seqs_p, sample=False)
    out_s = _combine(y_s, mask_s, pos_s, x1_s, mod, g2, b2, n=n_s, nb=nb_s, cap=cap_s, seqs=1, sample=True)

    def cache_layout(t):
        t = t.reshape(nb_p, DEPTH, N_KV_HEADS, HEAD_DIM, n_p)
        return jnp.transpose(t, (0, 1, 4, 2, 3))

    return (out_p.reshape(nb_p, n_p, D_MODEL), out_s.reshape(nb_s, n_s, D_MODEL),
            cache_layout(new_k), cache_layout(new_v), new_h.reshape(nb_p, DEPTH, 2, RG_WIDTH))
```

```python
import functools

import numpy as np
import jax
import jax.numpy as jnp
from jax import lax
from jax.experimental import pallas as pl
from jax.experimental.pallas import tpu as pltpu

F32 = jnp.float32
BF16 = jnp.bfloat16

D_MODEL = 1024
HEAD_DIM = 64
N_Q_HEADS = 8
N_KV_HEADS = 2
GROUP = N_Q_HEADS // N_KV_HEADS
ATTN_WIDTH = N_Q_HEADS * HEAD_DIM
KV_WIDTH = N_KV_HEADS * HEAD_DIM
RG_WIDTH = D_MODEL - ATTN_WIDTH
RG_BLOCK = 64
CONV_W = 4
C_LRU = 8.0
N_EXPERTS = 16
CAP_FACTOR = 2
D_EXPERT = 1024
GRID_W = 64
ROPE_THETA = 10000.0
NORM_EPS = 1e-6
DEPTH = 1
ALPHA = (2.0 * DEPTH) ** 0.25
IN_WIDTH = ATTN_WIDTH + 2 * KV_WIDTH + 2 * RG_WIDTH
Q_SCALE = HEAD_DIM ** -0.5 * float(np.log2(np.e))

LANES = 128
SUBLANES = 8
MXU_DIM = 256
VMEM_LIMIT_BYTES = 56 * 1024 * 1024

MOD_ROWS = SUBLANES
MOD_TILE = 256
ATTN_Q_TILE = 256
COMBINE_TILE = 256
MIXER_ORDER = (1, 0, 0, 1, 1, 0, 1, 0, 1, 0, 1, 0, 1, 1, 0, 1, 0, 0, 1, 1, 0, 1, 0, 0, 0, 1, 0, 0, 1, 0)
PERM_BLOCK = MXU_DIM
SEG_LEN = PERM_BLOCK // SUBLANES
POST_ROW_STREAMS = 4
MIXER_ROW_STREAMS = 1
PROMPT_SEQS_PER_STEP = 4


def _params(n_axes=1, flags=None):
    return pltpu.CompilerParams(dimension_semantics=("arbitrary",) * n_axes,
                                vmem_limit_bytes=VMEM_LIMIT_BYTES, flags=flags)


def _full(shape):
    zeros = (0,) * len(shape)
    return pl.BlockSpec(shape, lambda *_: zeros)


def _ln_plain(x):
    mu = jnp.mean(x, -1, keepdims=True)
    xc = x - mu
    var = jnp.mean(xc * xc, -1, keepdims=True)
    return xc * lax.rsqrt(var + NORM_EPS)


def _dot(a, b):
    return jnp.dot(a, b, preferred_element_type=F32)


def _dot_nt(a, b):
    return lax.dot_general(a, b, (((1,), (1,)), ((), ())), preferred_element_type=F32)


def _dot_tn(a, b):
    return lax.dot_general(a, b, (((0,), (0,)), ((), ())), preferred_element_type=F32)


def _split_bf16(x):
    hi = x.astype(BF16)
    lo = (x - hi.astype(F32)).astype(BF16)
    return hi, lo


def _cast_kernel(a_ref, b_ref, ao_ref, bo_ref):
    ao_ref[...] = a_ref[...].astype(BF16)
    bo_ref[...] = b_ref[...].astype(BF16)


def _cast_weights(w_in, w_out):
    rows = lambda w: pl.BlockSpec((MOD_TILE, w.shape[1]), lambda j: (j, 0))
    return pl.pallas_call(
        _cast_kernel,
        grid=(D_MODEL // MOD_TILE,),
        in_specs=[rows(w_in), rows(w_out)],
        out_specs=[rows(w_in), rows(w_out)],
        out_shape=[jax.ShapeDtypeStruct(w_in.shape, BF16), jax.ShapeDtypeStruct(w_out.shape, BF16)],
        compiler_params=_params(),
        name="cast_weights",
    )(w_in, w_out)


def _mod_kernel(cs_ref, w_ref, b_ref, o_ref):
    @pl.when(pl.program_id(0) == 0)
    def _():
        o_ref[...] = jnp.broadcast_to(b_ref[...], o_ref.shape)

    cs = cs_ref[...]
    s = cs * jax.nn.sigmoid(cs)
    o_ref[...] += _dot(s.astype(BF16), w_ref[...].astype(BF16))


def _mod_vectors(cs, w_mod, b_mod):
    width = w_mod.shape[1]
    return pl.pallas_call(
        _mod_kernel,
        grid=(D_MODEL // MOD_TILE,),
        in_specs=[pl.BlockSpec((MOD_ROWS, MOD_TILE), lambda j: (0, j)),
                  pl.BlockSpec((MOD_TILE, width), lambda j: (j, 0)),
                  _full(b_mod.shape)],
        out_specs=_full((MOD_ROWS, width)),
        out_shape=jax.ShapeDtypeStruct((MOD_ROWS, width), F32),
        compiler_params=_params(),
        name="mod_vectors",
    )(cs, w_mod, b_mod)


def _head_mean_sq(x, bd):
    hi, lo = _split_bf16(x * x)
    return _dot(hi, bd) + _dot(lo, bd)


def _rope_lanes(x, cos, sin_signed, hi_half):
    partner = jnp.where(hi_half, pltpu.roll(x, 16, axis=1), pltpu.roll(x, LANES - 16, axis=1))
    return x * cos + partner * sin_signed


def _modulated(x, m):
    return (_ln_plain(x) * (1.0 + m[:, D_MODEL:2 * D_MODEL]) + m[:, 0:D_MODEL]).astype(BF16)


def _qkv(h, win_ref, bd_ref, qg_ref, kg_ref, rope):
    n = h.shape[0]
    bd = bd_ref[...]
    q = _dot(h, win_ref[:, 0:ATTN_WIDTH])
    yield
    q = q * lax.rsqrt(_head_mean_sq(q, bd) + NORM_EPS) * qg_ref[...]
    yield
    k = _dot(h, win_ref[:, ATTN_WIDTH:ATTN_WIDTH + KV_WIDTH])
    k = k * lax.rsqrt(_head_mean_sq(k, bd[:KV_WIDTH, :KV_WIDTH]) + NORM_EPS) * kg_ref[...]
    v = _dot(h, win_ref[:, ATTN_WIDTH + KV_WIDTH:ATTN_WIDTH + 2 * KV_WIDTH])
    yield
    if rope is not None:
        cos, sin_signed = rope
        lane = lax.broadcasted_iota(jnp.int32, (n, LANES), 1)
        hi_half = (lane & 16) != 0
        q = jnp.concatenate(
            [_rope_lanes(q[:, j * LANES:(j + 1) * LANES], cos, sin_signed, hi_half)
             for j in range(ATTN_WIDTH // LANES)], axis=1)
        k = _rope_lanes(k, cos, sin_signed, hi_half)
    return (q * Q_SCALE).astype(BF16), k, v


def _weave(streams, order=()):
    results = [None] * len(streams)
    live = list(range(len(streams)))
    plan = [j for j in order]
    while live:
        idx = plan.pop(0) if plan else live[0]
        if idx not in live:
            continue
        if not plan:
            live.append(live.pop(0))
        try:
            next(streams[idx])
            yield
        except StopIteration as stop:
            results[idx] = stop.value
            live.remove(idx)
    return results


def _stagger(streams, chunks, lag):
    return [j for t in range(chunks + lag * (streams - 1)) for j in range(streams) if 0 <= t - lag * j < chunks]


def _drive(*streams, order=()):
    weave = _weave(streams, order)
    while True:
        try:
            next(weave)
        except StopIteration as stop:
            return stop.value


def _attend(q, k, v, lookahead):
    def scores(hq):
        kv = hq // GROUP
        return _dot_nt(q[:, hq * HEAD_DIM:(hq + 1) * HEAD_DIM], k[:, kv * HEAD_DIM:(kv + 1) * HEAD_DIM])

    outs = []
    s_next = scores(0) if lookahead else None
    for hq in range(N_Q_HEADS):
        if lookahead:
            s, s_next = s_next, (scores(hq + 1) if hq + 1 < N_Q_HEADS else None)
        else:
            s = scores(hq)
        kv = hq // GROUP
        e = jnp.exp2(s - jnp.max(s, axis=-1, keepdims=True))
        denom = jnp.sum(e, axis=-1, keepdims=True)
        pv = _dot(e.astype(BF16), v[:, kv * HEAD_DIM:(kv + 1) * HEAD_DIM])
        outs.append(pv / denom)
        yield
    return jnp.concatenate(outs, axis=1).astype(BF16)


def _rg_inputs(h, win_ref):
    rg_lo = ATTN_WIDTH + 2 * KV_WIDTH
    xr = _dot(h, win_ref[:, rg_lo:rg_lo + RG_WIDTH])
    yield
    gr = _dot(h, win_ref[:, rg_lo + RG_WIDTH:IN_WIDTH])
    yield
    return xr, gr


def _segment_permutation():
    p = np.zeros((PERM_BLOCK, PERM_BLOCK), np.float32)
    for t in range(SEG_LEN):
        for j in range(SUBLANES):
            p[t * SUBLANES + j, j * SEG_LEN + t] = 1.0
    return p


def _rglru(xr_ref, gr_ref, perm_t_ref, cw_ref, cb_ref, wg_ref, bg_ref, lam_ref, h0, scan_refs):
    af_ref, bf_ref, ab_ref, bb_ref = scan_refs
    n = xr_ref.shape[0]
    blocks = n // PERM_BLOCK
    sub = lax.broadcasted_iota(jnp.int32, (SUBLANES, 1), 0)
    zero_row = jnp.zeros((1, RG_WIDTH), F32)
    cw = cw_ref[...]
    neg = -lam_ref[...]
    softplus = jnp.maximum(neg, 0.0) + jnp.log1p(jnp.exp(-jnp.abs(neg)))
    decay = (-0.5 * C_LRU) * softplus

    def group(b, t):
        return xr_ref[b * PERM_BLOCK + t * SUBLANES:b * PERM_BLOCK + (t + 1) * SUBLANES, :]

    def before(b, t):
        wrap = group(b - 1, t)[SUBLANES - 1:SUBLANES, :] if b > 0 else zero_row
        return jnp.where(sub == 0, wrap, pltpu.roll(group(b, t), 1, axis=0))

    def after(b, t):
        wrap = group(b + 1, t)[0:1, :] if b + 1 < blocks else zero_row
        return jnp.where(sub == SUBLANES - 1, wrap, pltpu.roll(group(b, t), SUBLANES - 1, axis=0))

    ends = []
    for b in range(blocks):
        rows = slice(b * PERM_BLOCK, (b + 1) * PERM_BLOCK)
        ext = jnp.concatenate([before(b, SEG_LEN - 2), before(b, SEG_LEN - 1), xr_ref[rows, :], after(b, 0)], axis=0)
        xc = ext[0:PERM_BLOCK] * cw[0:1, :]
        for tap in range(1, CONV_W):
            xc = xc + ext[tap * SUBLANES:tap * SUBLANES + PERM_BLOCK] * cw[tap:tap + 1, :]
        xc = xc + cb_ref[...]
        yield

        xcb = xc.astype(BF16)
        halves = [_dot(xcb[:, c * MXU_DIM:(c + 1) * MXU_DIM], wg_ref[c].astype(BF16))
                  for c in range(RG_WIDTH // MXU_DIM)]
        yield
        half_xc = 0.5 * xc
        for d, (a_ref, b_ref) in enumerate(((af_ref, bf_ref), (ab_ref, bb_ref))):
            pre = [jnp.concatenate([hv[:, i * MXU_DIM:(i + 1) * MXU_DIM] for hv in halves], axis=1)
                   + bg_ref[i:i + 1, :] for i in (2 * d, 2 * d + 1)]
            r2 = jnp.tanh(pre[0]) + 1.0
            i2 = jnp.tanh(pre[1]) + 1.0
            log_a = r2 * decay[d:d + 1, :]
            a = jnp.exp(log_a)
            a_ref[rows, :] = a
            b_ref[rows, :] = jnp.sqrt(jnp.tanh(-log_a) * (a * a + 1.0)) * (i2 * half_xc)
            yield

        for a_ref, b_ref, steps in ((af_ref, bf_ref, range(SEG_LEN)), (ab_ref, bb_ref, range(SEG_LEN - 1, -1, -1))):
            hend = jnp.zeros((SUBLANES, RG_WIDTH), F32)
            pend = jnp.ones((SUBLANES, RG_WIDTH), F32)
            for t in steps:
                r = slice(b * PERM_BLOCK + t * SUBLANES, b * PERM_BLOCK + (t + 1) * SUBLANES)
                a = a_ref[r, :]
                hend = a * hend + b_ref[r, :]
                pend = a * pend
            ends.append((hend, pend))
            yield

    def carries(order, state, which):
        into = {}
        for b, j in order:
            into[b, j] = state
            hend, pend = ends[2 * b + which]
            state = hend[j:j + 1, :] + pend[j:j + 1, :] * state
        return into, state

    segs = [(b, j) for b in range(blocks) for j in range(SUBLANES)]
    into_f, hf = carries(segs, h0[0:1, :], 0)
    into_b, hb = carries(segs[::-1], h0[1:2, :], 1)
    yield

    for b in range(blocks):
        for a_ref, b_ref, into, steps in ((af_ref, bf_ref, into_f, range(SEG_LEN)),
                                          (ab_ref, bb_ref, into_b, range(SEG_LEN - 1, -1, -1))):
            h = jnp.concatenate([into[b, j] for j in range(SUBLANES)], axis=0)
            for t in steps:
                r = slice(b * PERM_BLOCK + t * SUBLANES, b * PERM_BLOCK + (t + 1) * SUBLANES)
                h = a_ref[r, :] * h + b_ref[r, :]
                b_ref[r, :] = h
            yield

    out = []
    for b in range(blocks):
        rows = slice(b * PERM_BLOCK, (b + 1) * PERM_BLOCK)
        rg = ((bf_ref[rows, :] + bb_ref[rows, :]) * jax.nn.gelu(gr_ref[rows, :])).astype(BF16)
        out.append(_dot(perm_t_ref[...], rg).astype(BF16))
        yield
    return jnp.concatenate(out, axis=0), hf, hb


def _out_and_router(attn, rg, x, m, wout_ref, g1_ref, b1_ref, wrt_ref):
    gate1 = m[:, 2 * D_MODEL:3 * D_MODEL]
    shift2 = m[:, 3 * D_MODEL:4 * D_MODEL]
    scale2 = m[:, 4 * D_MODEL:5 * D_MODEL]
    mix = _dot(jnp.concatenate([attn, rg], axis=1), wout_ref[...])
    yield
    x1 = _ln_plain(ALPHA * x + gate1 * mix) * g1_ref[...] + b1_ref[...]
    yield
    h2 = (_ln_plain(x1) * (1.0 + scale2) + shift2).astype(BF16)
    yield
    logits = _dot_nt(wrt_ref[...].astype(BF16), h2)
    e = jnp.exp(logits - jnp.max(logits, axis=0, keepdims=True))
    return x1, h2, e / jnp.sum(e, axis=0, keepdims=True)


def _mixer_kernel(x_ref, mod_ref, win_ref, wout_ref, bd_ref, qg_ref, kg_ref, cw_ref, cb_ref, wg_ref, bg_ref,
                  lam_ref, perm_ref, perm_t_ref, g1_ref, b1_ref, wrt_ref,
                  nk_ref, nv_ref, nh_ref, x1_ref, h2_ref, afft_ref,
                  xprev_ref, xr_ref, gr_ref, attn_ref, *scan_refs):
    step = pl.program_id(0)
    last = pl.num_programs(0) - 1
    m = mod_ref[0:1, :]

    def second_half():
        h0 = jnp.zeros((2, RG_WIDTH), F32)
        rg, hf, hb = yield from _rglru(xr_ref, gr_ref, perm_t_ref, cw_ref, cb_ref, wg_ref, bg_ref, lam_ref, h0,
                                       scan_refs)
        nh_ref[0] = jnp.concatenate([hf, hb], axis=0)
        rows = rg.shape[0] // MIXER_ROW_STREAMS
        blocks = [slice(j * rows, (j + 1) * rows) for j in range(MIXER_ROW_STREAMS)]
        outs = yield from _weave(
            [_out_and_router(attn_ref[r, :], rg[r, :], xprev_ref[r, :], m, wout_ref, g1_ref, b1_ref, wrt_ref)
             for r in blocks], _stagger(MIXER_ROW_STREAMS, 4, 1))
        for r, (x1, h2, aff) in zip(blocks, outs):
            x1_ref[r, :] = x1
            h2_ref[r, :] = h2
            afft_ref[:, r] = aff

    def first_half():
        x = x_ref[...]
        h = _modulated(x, m)
        yield
        q, k, v = yield from _qkv(h, win_ref, bd_ref, qg_ref, kg_ref, None)
        nk_ref[0] = k.T
        nv_ref[0] = v.T
        yield
        xr, gr = yield from _rg_inputs(_dot(perm_ref[...], h).astype(BF16), win_ref)
        attn = yield from _attend(q, k.astype(BF16), v.astype(BF16), False)
        return x, attn, xr, gr

    def hand_over(x, attn, xr, gr):
        xprev_ref[...] = x
        attn_ref[...] = attn
        xr_ref[...] = xr
        gr_ref[...] = gr

    @pl.when(step == 0)
    def _():
        hand_over(*_drive(first_half())[0])

    @pl.when((step > 0) & (step < last))
    def _():
        hand_over(*_drive(first_half(), second_half(), order=MIXER_ORDER)[0])

    @pl.when(step == last)
    def _():
        _drive(second_half())


def _mixer_prompt(x2d, mod, win_bf, wout_bf, consts, g1, b1, wrt, *, n, nb):
    assert n == PERM_BLOCK
    tokens = nb * n
    first =lambda w: pl.BlockSpec((n, w), lambda i: (jnp.minimum(i, nb - 1), 0))
    second = lambda w: pl.BlockSpec((n, w), lambda i: (jnp.maximum(i - 1, 0), 0))
    small = list(consts) + [g1, b1, wrt]
    return pl.pallas_call(
        _mixer_kernel,
        grid=(nb + 1,),
        in_specs=[first(D_MODEL), _full(mod.shape), _full(win_bf.shape), _full(wout_bf.shape)]
                 + [_full(a.shape) for a in small],
        out_specs=[pl.BlockSpec((1, KV_WIDTH, n), lambda i: (jnp.minimum(i, nb - 1), 0, 0)),
                   pl.BlockSpec((1, KV_WIDTH, n), lambda i: (jnp.minimum(i, nb - 1), 0, 0)),
                   pl.BlockSpec((1, 2, RG_WIDTH), lambda i: (jnp.maximum(i - 1, 0), 0, 0)),
                   second(D_MODEL), second(D_MODEL),
                   pl.BlockSpec((N_EXPERTS, n), lambda i: (jnp.maximum(i - 1, 0), 0))],
        out_shape=[jax.ShapeDtypeStruct((nb, KV_WIDTH, n), F32), jax.ShapeDtypeStruct((nb, KV_WIDTH, n), F32),
                   jax.ShapeDtypeStruct((nb, 2, RG_WIDTH), F32),
                   jax.ShapeDtypeStruct((tokens, D_MODEL), F32), jax.ShapeDtypeStruct((tokens, D_MODEL), BF16),
                   jax.ShapeDtypeStruct((nb * N_EXPERTS, n), F32)],
        scratch_shapes=[pltpu.VMEM((n, D_MODEL), F32), pltpu.VMEM((n, RG_WIDTH), F32),
                        pltpu.VMEM((n, RG_WIDTH), F32), pltpu.VMEM((n, ATTN_WIDTH), BF16)]
                       + [pltpu.VMEM((n, RG_WIDTH), F32)] * 4,
        compiler_params=_params(),
        name="mixer_prompt",
    )(x2d, mod, win_bf, wout_bf, *small)


def _front_kernel(x_ref, mod_ref, win_ref, bd_ref, qg_ref, kg_ref, cw_ref, cb_ref, wg_ref, bg_ref, lam_ref,
                  perm_ref, perm_t_ref, h0_ref, cos_ref, sin_ref,
                  q_ref, k_ref, v_ref, rg_ref, xr_ref, gr_ref, *scan_refs):
    m = mod_ref[pl.ds(1 + pl.program_id(0), 1), :]
    rows = PERM_BLOCK
    streams = x_ref.shape[0] // rows

    def row_block(j):
        r = pl.ds(j * rows, rows)
        h = _modulated(x_ref[r, :], m)
        yield
        q, k, v = yield from _qkv(h, win_ref, bd_ref, qg_ref, kg_ref, (cos_ref[r, :], sin_ref[r, :]))
        q_ref[r, :] = q
        k_ref[r, :] = k.astype(BF16)
        v_ref[r, :] = v.astype(BF16)
        yield
        hp = _dot(perm_ref[...], h).astype(BF16)
        xr_ref[r, :], gr_ref[r, :] = yield from _rg_inputs(hp, win_ref)

    chunks, lag = 8, 2
    stagger = [j for t in range(chunks + lag * (streams - 1)) for j in range(streams) if 0 <= t - lag * j < chunks]
    _drive(*[row_block(j) for j in range(streams)], order=stagger)
    (rg_ref[...], _, _), = _drive(_rglru(xr_ref, gr_ref, perm_t_ref, cw_ref, cb_ref, wg_ref, bg_ref, lam_ref,
                                         h0_ref[0], scan_refs))


def _front_sample(x2d, mod, win_bf, consts, h0, rope, *, n, nb):
    tokens = nb * n
    seq = lambda w: pl.BlockSpec((n, w), lambda b: (b, 0))
    small = list(consts)
    return pl.pallas_call(
        _front_kernel,
        grid=(nb,),
        in_specs=[seq(D_MODEL), _full(mod.shape), _full(win_bf.shape)] + [_full(a.shape) for a in small]
                 + [pl.BlockSpec((1, 2, RG_WIDTH), lambda b: (b, 0, 0)), _full(rope[0].shape), _full(rope[1].shape)],
        out_specs=[seq(ATTN_WIDTH), seq(KV_WIDTH), seq(KV_WIDTH), seq(RG_WIDTH)],
        out_shape=[jax.ShapeDtypeStruct((tokens, ATTN_WIDTH), BF16), jax.ShapeDtypeStruct((tokens, KV_WIDTH), BF16),
                   jax.ShapeDtypeStruct((tokens, KV_WIDTH), BF16), jax.ShapeDtypeStruct((tokens, RG_WIDTH), BF16)],
        scratch_shapes=[pltpu.VMEM((n, RG_WIDTH), F32)] * 6,
        compiler_params=_params(),
        name="front_sample",
    )(x2d, mod, win_bf, *small, h0, *rope)


def _attn_kernel(q_ref, k_ref, v_ref, kc_ref, vc_ref, o_ref):
    k = jnp.concatenate([k_ref[...], kc_ref[...].astype(BF16)], axis=0)
    v = jnp.concatenate([v_ref[...], vc_ref[...].astype(BF16)], axis=0)
    o_ref[...], = _drive(_attend(q_ref[...], k, v, True))


def _attn_sample(q, k, v, cache_k, cache_v, *, n, nb, past):
    tq = min(n, ATTN_Q_TILE)
    tiles = n // tq
    return pl.pallas_call(
        _attn_kernel,
        grid=(nb, tiles),
        in_specs=[pl.BlockSpec((tq, ATTN_WIDTH), lambda b, t: (b * tiles + t, 0)),
                  pl.BlockSpec((n, KV_WIDTH), lambda b, t: (b, 0)),
                  pl.BlockSpec((n, KV_WIDTH), lambda b, t: (b, 0)),
                  pl.BlockSpec((past, KV_WIDTH), lambda b, t: (b, 0)),
                  pl.BlockSpec((past, KV_WIDTH), lambda b, t: (b, 0))],
        out_specs=pl.BlockSpec((tq, ATTN_WIDTH), lambda b, t: (b * tiles + t, 0)),
        out_shape=jax.ShapeDtypeStruct((nb * n, ATTN_WIDTH), BF16),
        compiler_params=_params(2),
        name="attn_sample",
    )(q, k, v, cache_k, cache_v)


def _post_kernel(attn_ref, rg_ref, x_ref, mod_ref, wout_ref, g1_ref, b1_ref, wrt_ref, x1_ref, h2_ref, afft_ref):
    m = mod_ref[pl.ds(1 + pl.program_id(0), 1), :]
    rows = x_ref.shape[0] // POST_ROW_STREAMS

    def row_block(j):
        r = pl.ds(j * rows, rows)
        x1_ref[r, :], h2_ref[r, :], afft_ref[:, r] = yield from _out_and_router(
            attn_ref[r, :], rg_ref[r, :], x_ref[r, :], m, wout_ref, g1_ref, b1_ref, wrt_ref)

    stagger = [j for i in range(POST_ROW_STREAMS) for j in range(i + 1)]
    _drive(*[row_block(j) for j in range(POST_ROW_STREAMS)], order=stagger)


def _post_sample(attn, rg, x2d, mod, wout_bf, g1, b1, wrt, *, n, nb):
    tokens = nb * n
    seq = lambda w: pl.BlockSpec((n, w), lambda b: (b, 0))
    return pl.pallas_call(
        _post_kernel,
        grid=(nb,),
        in_specs=[seq(ATTN_WIDTH), seq(RG_WIDTH), seq(D_MODEL), _full(mod.shape), _full(wout_bf.shape),
                  _full(g1.shape), _full(b1.shape), _full(wrt.shape)],
        out_specs=[seq(D_MODEL), seq(D_MODEL), pl.BlockSpec((N_EXPERTS, n), lambda b: (b, 0))],
        out_shape=[jax.ShapeDtypeStruct((tokens, D_MODEL), F32), jax.ShapeDtypeStruct((tokens, D_MODEL), BF16),
                   jax.ShapeDtypeStruct((nb * N_EXPERTS, n), F32)],
        compiler_params=_params(),
        name="post_sample",
    )(attn, rg, x2d, mod, wout_bf, g1, b1, wrt)


def _slot_onehot(mask, pos, cap):
    n = mask.shape[1]
    slot = lax.broadcasted_iota(jnp.int32, (cap, n), 0).astype(F32)
    return [(slot == pos[e:e + 1, :]) & (mask[e:e + 1, :] > 0.5) for e in range(mask.shape[0])]


def _route_kernel(afft_ref, h2_ref, xs_ref, mask_ref, pos_ref, *, n, cap, seqs):
    b = pl.program_id(0)

    @pl.when(b == 0)
    def _():
        aff = afft_ref[...]
        thr = jnp.zeros((aff.shape[0], 1), jnp.int32)
        for bit in range(30, -1, -1):
            cand = thr | (1 << bit)
            cnt = jnp.sum((aff >= lax.bitcast_convert_type(cand, F32)).astype(F32), axis=1, keepdims=True)
            thr = jnp.where(cnt >= cap, cand, thr)
        above = aff >= lax.bitcast_convert_type(thr + 1, F32)
        tied = (aff >= lax.bitcast_convert_type(thr, F32)) & jnp.logical_not(above)
        need = cap - jnp.sum(above.astype(F32), axis=1, keepdims=True)
        before = (lax.broadcasted_iota(jnp.int32, (n, n), 0)
                  < lax.broadcasted_iota(jnp.int32, (n, n), 1)).astype(BF16)
        tie_rank = _dot(tied.astype(BF16), before)
        mask = (above | (tied & (tie_rank < need))).astype(F32)
        mask_ref[...] = mask
        pos_ref[...] = _dot(mask.astype(BF16), before)

    for j in range(seqs):
        r0 = pl.multiple_of((b * seqs + j) * N_EXPERTS, N_EXPERTS)
        onehots = _slot_onehot(mask_ref[pl.ds(r0, N_EXPERTS), :], pos_ref[pl.ds(r0, N_EXPERTS), :], cap)
        sel = jnp.concatenate(onehots, axis=0).astype(BF16)
        xs = _dot(sel, h2_ref[j * n:(j + 1) * n, :]).astype(BF16)
        xs_ref[:, j * cap:(j + 1) * cap, :] = xs.reshape(N_EXPERTS, cap, D_MODEL)


def _route(afft, h2, *, n, nb, cap, seqs):
    rows = nb * N_EXPERTS
    return pl.pallas_call(
        functools.partial(_route_kernel, n=n, cap=cap, seqs=seqs),
        grid=(nb // seqs,),
        in_specs=[_full(afft.shape), pl.BlockSpec((seqs * n, D_MODEL), lambda b: (b, 0))],
        out_specs=[pl.BlockSpec((N_EXPERTS, seqs * cap, D_MODEL), lambda b: (0, b, 0)),
                   _full((rows, n)), _full((rows, n))],
        out_shape=[jax.ShapeDtypeStruct((N_EXPERTS, nb * cap, D_MODEL), BF16),
                   jax.ShapeDtypeStruct((rows, n), F32), jax.ShapeDtypeStruct((rows, n), F32)],
        compiler_params=_params(),
        name="route_n%d" % n,
    )(afft, h2)


def _slot_gates(aff_ref, mask_ref, pos_ref, e, cap):
    cols = []
    for b in range(aff_ref.shape[0] // N_EXPERTS):
        r = b * N_EXPERTS + e
        onehot, = _slot_onehot(mask_ref[pl.ds(r, 1), :], pos_ref[pl.ds(r, 1), :], cap)
        cols.append(jnp.sum(jnp.where(onehot, aff_ref[pl.ds(r, 1), :], 0.0), axis=1, keepdims=True))
    return jnp.concatenate(cols, axis=0)


def _ffn_kernel(xp_ref, xs_ref, ap_ref, mp_ref, pp_ref, as_ref, ms_ref, ps_ref, wgu_ref, wd_ref, yp_ref, ys_ref):
    e = pl.program_id(0)
    rows_p = xp_ref.shape[1]
    xs = jnp.concatenate([xp_ref[0], xs_ref[0]], axis=0)
    gu = _dot(xs, wgu_ref[0].astype(BF16))
    gate = gu[:, :D_EXPERT]
    up = gu[:, D_EXPERT:]
    act = (gate * jax.nn.sigmoid(gate) * up).astype(BF16)
    y = _dot(act, wd_ref[0].astype(BF16))
    g = jnp.concatenate([_slot_gates(ap_ref, mp_ref, pp_ref, e, rows_p * N_EXPERTS // ap_ref.shape[0]),
                         _slot_gates(as_ref, ms_ref, ps_ref, e, xs_ref.shape[1] * N_EXPERTS // as_ref.shape[0])],
                        axis=0)
    y = (y * g).astype(BF16)
    yp_ref[0] = y[:rows_p]
    ys_ref[0] = y[rows_p:]


def _ffn(xs_p, xs_s, route_p, route_s, w_gate_up, w_down):
    per_e = lambda a: pl.BlockSpec((1,) + a.shape[1:], lambda e: (e, 0, 0))
    tables = list(route_p) + list(route_s)
    return pl.pallas_call(
        _ffn_kernel,
        grid=(N_EXPERTS,),
        in_specs=[per_e(xs_p), per_e(xs_s)] + [_full(t.shape) for t in tables] + [per_e(w_gate_up), per_e(w_down)],
        out_specs=[per_e(xs_p), per_e(xs_s)],
        out_shape=[jax.ShapeDtypeStruct(xs_p.shape, BF16), jax.ShapeDtypeStruct(xs_s.shape, BF16)],
        compiler_params=_params(),
        name="expert_ffn",
    )(xs_p, xs_s, *tables, w_gate_up, w_down)


def _combine_kernel(y_ref, mask_ref, pos_ref, x1_ref, mod_ref, g2_ref, b2_ref, o_ref, *, cap, seqs, mod_row0,
                    mod_row_step):
    row = mod_row0 + mod_row_step * pl.program_id(0)
    gate2 = mod_ref[pl.ds(row, 1), 5 * D_MODEL:6 * D_MODEL]
    tn = x1_ref.shape[0] // seqs
    for j in range(seqs):
        e0 = j * N_EXPERTS
        sel = jnp.concatenate(
            _slot_onehot(mask_ref[e0:e0 + N_EXPERTS, :], pos_ref[e0:e0 + N_EXPERTS, :], cap), axis=0).astype(BF16)
        y = y_ref[:, j * cap:(j + 1) * cap, :].reshape(N_EXPERTS * cap, D_MODEL)
        ff = _dot_tn(sel, y)
        r = pl.ds(j * tn, tn)
        o_ref[r, :] = _ln_plain(ALPHA * x1_ref[r, :] + gate2 * ff) * g2_ref[...] + b2_ref[...]


def _combine(y, mask, pos, x1, mod, g2, b2, *, n, nb, cap, seqs, sample):
    tn = min(n, COMBINE_TILE)
    tiles = n // tn
    assert seqs == 1 or tiles == 1
    return pl.pallas_call(
        functools.partial(_combine_kernel, cap=cap, seqs=seqs, mod_row0=1 if sample else 0,
                          mod_row_step=1 if sample else 0),
        grid=(nb // seqs, tiles),
        in_specs=[pl.BlockSpec((N_EXPERTS, seqs * cap, D_MODEL), lambda b, t: (0, b, 0)),
                  pl.BlockSpec((seqs * N_EXPERTS, tn), lambda b, t: (b, t)),
                  pl.BlockSpec((seqs * N_EXPERTS, tn), lambda b, t: (b, t)),
                  pl.BlockSpec((seqs * tn, D_MODEL), lambda b, t: (b * tiles + t, 0)),
                  _full(mod.shape), _full(g2.shape), _full(b2.shape)],
        out_specs=pl.BlockSpec((seqs * tn, D_MODEL), lambda b, t: (b * tiles + t, 0)),
        out_shape=jax.ShapeDtypeStruct((nb * n, D_MODEL), F32),
        compiler_params=_params(2),
        name="combine_sample" if sample else "combine_prompt",
    )(y, mask, pos, x1, mod, g2, b2)


def _block_diag_256(w):
    per_tile = MXU_DIM // RG_BLOCK
    tiles = []
    for c in range(w.shape[0] // per_tile):
        tiles.append(jax.scipy.linalg.block_diag(*[w[c * per_tile + i] for i in range(per_tile)]))
    return jnp.stack(tiles)


def _rope_tables(n):
    lane = np.arange(LANES)
    within = lane % HEAD_DIM
    freq = (within % 16).astype(np.float32)
    inv = np.float32(ROPE_THETA) ** (-freq / np.float32(16.0))
    tok = np.arange(n)
    pos = np.where((within < HEAD_DIM // 2)[None, :], (tok // GRID_W)[:, None], (tok % GRID_W)[:, None])
    ang = pos.astype(np.float32) * inv[None, :]
    sign = np.where(within % 32 < 16, -1.0, 1.0).astype(np.float32)
    return jnp.asarray(np.cos(ang)), jnp.asarray(np.sin(ang) * sign[None, :])


def kernel(x_prompt, x_sample, cache_k, cache_v, state_h, c, c_ctx, w_mod, b_mod, w_in, q_norm_g, k_norm_g,
           conv_w, conv_b, w_rg_a, b_rg_a, w_rg_x, b_rg_x, rg_lambda, w_out, ln1_g, ln1_b, w_router,
           w_gate_up, w_down, ln2_g, ln2_b):
    assert w_mod.shape[0] == DEPTH == 1
    nb_p, n_p, _ = x_prompt.shape
    nb_s, n_s, _ = x_sample.shape
    past = cache_k.shape[2]
    cap_p = CAP_FACTOR * n_p // N_EXPERTS
    cap_s = CAP_FACTOR * n_s // N_EXPERTS

    cs = jnp.concatenate([c_ctx[None, :], c, jnp.zeros((MOD_ROWS - 1 - nb_s, D_MODEL), F32)], axis=0)
    row = lambda v: v.reshape(1, -1)
    head_avg = np.kron(np.eye(ATTN_WIDTH // HEAD_DIM, dtype=np.float32),
                       np.full((HEAD_DIM, HEAD_DIM), 1.0 / HEAD_DIM, np.float32))
    bd = jnp.asarray(head_avg, BF16)
    qg = row(jnp.tile(q_norm_g[0], N_Q_HEADS))
    kg = row(jnp.tile(k_norm_g[0], N_KV_HEADS))
    wg = 0.5 * jnp.concatenate([_block_diag_256(w_rg_a[0, 0]), _block_diag_256(w_rg_x[0, 0]),
                                _block_diag_256(w_rg_a[0, 1]), _block_diag_256(w_rg_x[0, 1])], axis=2)
    bg = 0.5 * jnp.stack([b_rg_a[0, 0], b_rg_x[0, 0], b_rg_a[0, 1], b_rg_x[0, 1]])
    perm = _segment_permutation()
    consts = (bd, qg, kg, conv_w[0], row(conv_b[0]), wg, bg, rg_lambda[0],
              jnp.asarray(perm, BF16), jnp.asarray(perm.T, BF16))
    wrt = w_router[0].T
    g1, b1, g2, b2 = row(ln1_g[0]), row(ln1_b[0]), row(ln2_g[0]), row(ln2_b[0])

    win_bf, wout_bf = _cast_weights(w_in[0], w_out[0])
    mod = _mod_vectors(cs, w_mod[0], row(b_mod[0]))

    xp = x_prompt.reshape(nb_p * n_p, D_MODEL)
    xs = x_sample.reshape(nb_s * n_s, D_MODEL)

    new_k, new_v, new_h, x1_p, h2_p, aff_p = _mixer_prompt(
        xp, mod, win_bf, wout_bf, consts, g1, b1, wrt, n=n_p, nb=nb_p)

    q_s, k_s, v_s, rg_s = _front_sample(xs, mod, win_bf, consts, state_h[:, 0], _rope_tables(n_s), n=n_s, nb=nb_s)
    attn_s = _attn_sample(q_s, k_s, v_s, cache_k[:, 0].reshape(nb_s * past, KV_WIDTH),
                          cache_v[:, 0].reshape(nb_s * past, KV_WIDTH), n=n_s, nb=nb_s, past=past)
    x1_s, h2_s, aff_s = _post_sample(attn_s, rg_s, xs, mod, wout_bf, g1, b1, wrt, n=n_s, nb=nb_s)

    seqs_p = PROMPT_SEQS_PER_STEP if nb_p % PROMPT_SEQS_PER_STEP == 0 else 1
    xs_p, mask_p, pos_p = _route(aff_p, h2_p, n=n_p, nb=nb_p, cap=cap_p, seqs=seqs_p)
    xs_s, mask_s, pos_s = _route(aff_s, h2_s, n=n_s, nb=nb_s, cap=cap_s, seqs=1)

    y_p, y_s = _ffn(xs_p, xs_s, (aff_p, mask_p, pos_p), (aff_s, mask_s, pos_s), w_gate_up[0], w_down[0])

    out_p = _combine(y_p, mask_p, pos_p, x1_p, mod, g2, b2, n=n_p, nb=nb_p, cap=cap_p, seqs=seqs_p, sample=False)
    out_s = _combine(y_s, mask_s, pos_s, x1_s, mod, g2, b2, n=n_s, nb=nb_s, cap=cap_s, seqs=1, sample=True)

    def cache_layout(t):
        t = t.reshape(nb_p, DEPTH, N_KV_HEADS, HEAD_DIM, n_p)
        return jnp.transpose(t, (0, 1, 4, 2, 3))

    return (out_p.reshape(nb_p, n_p, D_MODEL), out_s.reshape(nb_s, n_s, D_MODEL),
            cache_layout(new_k), cache_layout(new_v), new_h.reshape(nb_p, DEPTH, 2, RG_WIDTH))
```

```python
import functools

import numpy as np
import jax
import jax.numpy as jnp
from jax import lax
from jax.experimental import pallas as pl
from jax.experimental.pallas import tpu as pltpu

F32 = jnp.float32
BF16 = jnp.bfloat16

D_MODEL = 1024
HEAD_DIM = 64
N_Q_HEADS = 8
N_KV_HEADS = 2
GROUP = N_Q_HEADS // N_KV_HEADS
ATTN_WIDTH = N_Q_HEADS * HEAD_DIM
KV_WIDTH = N_KV_HEADS * HEAD_DIM
RG_WIDTH = D_MODEL - ATTN_WIDTH
RG_BLOCK = 64
CONV_W = 4
C_LRU = 8.0
N_EXPERTS = 16
CAP_FACTOR = 2
D_EXPERT = 1024
GRID_W = 64
ROPE_THETA = 10000.0
NORM_EPS = 1e-6
DEPTH = 1
ALPHA = (2.0 * DEPTH) ** 0.25
IN_WIDTH = ATTN_WIDTH + 2 * KV_WIDTH + 2 * RG_WIDTH
Q_SCALE = HEAD_DIM ** -0.5 * float(np.log2(np.e))

LANES = 128
SUBLANES = 8
MXU_DIM = 256
VMEM_LIMIT_BYTES = 56 * 1024 * 1024

MOD_ROWS = SUBLANES
MOD_TILE = 256
ATTN_Q_TILE = 256
COMBINE_TILE = 256
MIXER_ORDER = (1, 0, 0, 1, 1, 0, 1, 1, 0, 1, 0, 1, 0, 1, 1, 0, 1, 1, 0, 0, 1, 0, 1, 0, 0, 0, 1, 0, 0, 1, 0)
PERM_BLOCK = MXU_DIM
SEG_LEN = PERM_BLOCK // SUBLANES
POST_ROW_STREAMS = 4
PROMPT_SEQS_PER_STEP = 4


def _params(n_axes=1):
    return pltpu.CompilerParams(dimension_semantics=("arbitrary",) * n_axes,
                                vmem_limit_bytes=VMEM_LIMIT_BYTES)


def _full(shape):
    zeros = (0,) * len(shape)
    return pl.BlockSpec(shape, lambda *_: zeros)


def _ln_plain(x):
    mu = jnp.mean(x, -1, keepdims=True)
    xc = x - mu
    var = jnp.mean(xc * xc, -1, keepdims=True)
    return xc * lax.rsqrt(var + NORM_EPS)


def _dot(a, b):
    return jnp.dot(a, b, preferred_element_type=F32)


def _dot_nt(a, b):
    return lax.dot_general(a, b, (((1,), (1,)), ((), ())), preferred_element_type=F32)


def _dot_tn(a, b):
    return lax.dot_general(a, b, (((0,), (0,)), ((), ())), preferred_element_type=F32)


def _split_bf16(x):
    hi = x.astype(BF16)
    lo = (x - hi.astype(F32)).astype(BF16)
    return hi, lo


def _cast_kernel(a_ref, b_ref, ao_ref, bo_ref):
    ao_ref[...] = a_ref[...].astype(BF16)
    bo_ref[...] = b_ref[...].astype(BF16)


def _cast_weights(w_in, w_out):
    return pl.pallas_call(
        _cast_kernel,
        grid=(1,),
        in_specs=[_full(w_in.shape), _full(w_out.shape)],
        out_specs=[_full(w_in.shape), _full(w_out.shape)],
        out_shape=[jax.ShapeDtypeStruct(w_in.shape, BF16), jax.ShapeDtypeStruct(w_out.shape, BF16)],
        compiler_params=_params(),
        name="cast_weights",
    )(w_in, w_out)


def _mod_kernel(cs_ref, w_ref, b_ref, o_ref):
    @pl.when(pl.program_id(0) == 0)
    def _():
        o_ref[...] = jnp.broadcast_to(b_ref[...], o_ref.shape)

    cs = cs_ref[...]
    s = cs * jax.nn.sigmoid(cs)
    o_ref[...] += _dot(s.astype(BF16), w_ref[...].astype(BF16))


def _mod_vectors(cs, w_mod, b_mod):
    width = w_mod.shape[1]
    return pl.pallas_call(
        _mod_kernel,
        grid=(D_MODEL // MOD_TILE,),
        in_specs=[pl.BlockSpec((MOD_ROWS, MOD_TILE), lambda j: (0, j)),
                  pl.BlockSpec((MOD_TILE, width), lambda j: (j, 0)),
                  _full(b_mod.shape)],
        out_specs=_full((MOD_ROWS, width)),
        out_shape=jax.ShapeDtypeStruct((MOD_ROWS, width), F32),
        compiler_params=_params(),
        name="mod_vectors",
    )(cs, w_mod, b_mod)


def _head_mean_sq(x, bd):
    hi, lo = _split_bf16(x * x)
    return _dot(hi, bd) + _dot(lo, bd)


def _rope_lanes(x, cos, sin_signed, hi_half):
    partner = jnp.where(hi_half, pltpu.roll(x, 16, axis=1), pltpu.roll(x, LANES - 16, axis=1))
    return x * cos + partner * sin_signed


def _modulated(x, m):
    return (_ln_plain(x) * (1.0 + m[:, D_MODEL:2 * D_MODEL]) + m[:, 0:D_MODEL]).astype(BF16)


def _qkv(h, win_ref, bd_ref, qg_ref, kg_ref, rope):
    n = h.shape[0]
    bd = bd_ref[...]
    q = _dot(h, win_ref[:, 0:ATTN_WIDTH])
    yield
    q = q * lax.rsqrt(_head_mean_sq(q, bd) + NORM_EPS) * qg_ref[...]
    yield
    k = _dot(h, win_ref[:, ATTN_WIDTH:ATTN_WIDTH + KV_WIDTH])
    k = k * lax.rsqrt(_head_mean_sq(k, bd[:KV_WIDTH, :KV_WIDTH]) + NORM_EPS) * kg_ref[...]
    v = _dot(h, win_ref[:, ATTN_WIDTH + KV_WIDTH:ATTN_WIDTH + 2 * KV_WIDTH])
    yield
    if rope is not None:
        cos, sin_signed = rope
        lane = lax.broadcasted_iota(jnp.int32, (n, LANES), 1)
        hi_half = (lane & 16) != 0
        q = jnp.concatenate(
            [_rope_lanes(q[:, j * LANES:(j + 1) * LANES], cos, sin_signed, hi_half)
             for j in range(ATTN_WIDTH // LANES)], axis=1)
        k = _rope_lanes(k, cos, sin_signed, hi_half)
    return (q * Q_SCALE).astype(BF16), k, v


def _drive(*streams, order=()):
    results = [None] * len(streams)
    live = list(range(len(streams)))
    plan = [j for j in order]
    while live:
        idx = plan.pop(0) if plan else live[0]
        if idx not in live:
            continue
        if not plan:
            live.append(live.pop(0))
        try:
            next(streams[idx])
        except StopIteration as stop:
            results[idx] = stop.value
            live.remove(idx)
    return results


def _attend(q, k, v, lookahead):
    def scores(hq):
        kv = hq // GROUP
        return _dot_nt(q[:, hq * HEAD_DIM:(hq + 1) * HEAD_DIM], k[:, kv * HEAD_DIM:(kv + 1) * HEAD_DIM])

    outs = []
    s_next = scores(0) if lookahead else None
    for hq in range(N_Q_HEADS):
        if lookahead:
            s, s_next = s_next, (scores(hq + 1) if hq + 1 < N_Q_HEADS else None)
        else:
            s = scores(hq)
        kv = hq // GROUP
        e = jnp.exp2(s - jnp.max(s, axis=-1, keepdims=True))
        denom = jnp.sum(e, axis=-1, keepdims=True)
        pv = _dot(e.astype(BF16), v[:, kv * HEAD_DIM:(kv + 1) * HEAD_DIM])
        outs.append(pv / denom)
        yield
    return jnp.concatenate(outs, axis=1).astype(BF16)


def _rg_inputs(h, win_ref):
    rg_lo = ATTN_WIDTH + 2 * KV_WIDTH
    xr = _dot(h, win_ref[:, rg_lo:rg_lo + RG_WIDTH])
    yield
    gr = _dot(h, win_ref[:, rg_lo + RG_WIDTH:IN_WIDTH])
    yield
    return xr, gr


def _rglru_rows(xr, gr, cw_ref, cb_ref, wg_ref, bg_ref, lam_ref, h0, scan_refs):
    af_ref, bf_ref, ab_ref, bb_ref = scan_refs
    n = xr.shape[0]

    t_idx = lax.broadcasted_iota(jnp.int32, (n, 1), 0)
    cw = cw_ref[...]
    xc = jnp.where(t_idx >= 2, pltpu.roll(xr, 2, axis=0), 0.0) * cw[0:1, :]
    xc = xc + jnp.where(t_idx >= 1, pltpu.roll(xr, 1, axis=0), 0.0) * cw[1:2, :]
    xc = xc + xr * cw[2:3, :]
    xc = xc + jnp.where(t_idx < n - 1, pltpu.roll(xr, n - 1, axis=0), 0.0) * cw[3:4, :]
    xc = xc + cb_ref[...]
    yield

    xcb = xc.astype(BF16)
    halves = [_dot(xcb[:, c * MXU_DIM:(c + 1) * MXU_DIM], wg_ref[c].astype(BF16))
              for c in range(RG_WIDTH // MXU_DIM)]

    def gate_pre(idx):
        return jnp.concatenate([hv[:, idx * MXU_DIM:(idx + 1) * MXU_DIM] for hv in halves], axis=1) \
            + bg_ref[idx:idx + 1, :]

    yield
    neg = -lam_ref[...]
    softplus = jnp.maximum(neg, 0.0) + jnp.log1p(jnp.exp(-jnp.abs(neg)))
    decay = (-0.5 * C_LRU) * softplus
    half_xc = 0.5 * xc

    def coeffs(d):
        r2 = jnp.tanh(gate_pre(2 * d)) + 1.0
        i2 = jnp.tanh(gate_pre(2 * d + 1)) + 1.0
        log_a = r2 * decay[d:d + 1, :]
        a = jnp.exp(log_a)
        bx = jnp.sqrt(jnp.tanh(-log_a) * (a * a + 1.0)) * (i2 * half_xc)
        return a, bx

    groups = n // SUBLANES
    rmod = lax.broadcasted_iota(jnp.int32, (1, SUBLANES, 1), 1)

    def tile_scan(d, forward):
        a, bx = coeffs(d)
        yield
        a = a.reshape(groups, SUBLANES, RG_WIDTH)
        bx = bx.reshape(groups, SUBLANES, RG_WIDTH)
        for s in (1, 2, 4):
            ok = (rmod >= s) if forward else (rmod < SUBLANES - s)
            shift = s if forward else SUBLANES - s
            a_sh = jnp.where(ok, pltpu.roll(a, shift, axis=1), 1.0)
            b_sh = jnp.where(ok, pltpu.roll(bx, shift, axis=1), 0.0)
            bx = a * b_sh + bx
            a = a * a_sh
            yield
        return a.reshape(n, RG_WIDTH), bx.reshape(n, RG_WIDTH)

    af_ref[...], bf_ref[...] = yield from tile_scan(0, True)
    ab_ref[...], bb_ref[...] = yield from tile_scan(1, False)

    def step(g, carry):
        hf, hb = carry
        rf = pl.multiple_of(g * SUBLANES, SUBLANES)
        rb = pl.multiple_of((groups - 1 - g) * SUBLANES, SUBLANES)
        new_f = af_ref[pl.ds(rf, SUBLANES), :] * hf + bf_ref[pl.ds(rf, SUBLANES), :]
        new_b = ab_ref[pl.ds(rb, SUBLANES), :] * hb + bb_ref[pl.ds(rb, SUBLANES), :]
        bf_ref[pl.ds(rf, SUBLANES), :] = new_f
        bb_ref[pl.ds(rb, SUBLANES), :] = new_b
        return (jnp.broadcast_to(new_f[SUBLANES - 1:SUBLANES, :], (SUBLANES, RG_WIDTH)),
                jnp.broadcast_to(new_b[0:1, :], (SUBLANES, RG_WIDTH)))

    hf, hb = lax.fori_loop(
        0, groups, step,
        (jnp.broadcast_to(h0[0:1, :], (SUBLANES, RG_WIDTH)), jnp.broadcast_to(h0[1:2, :], (SUBLANES, RG_WIDTH))),
        unroll=True)
    yield

    rg = ((bf_ref[...] + bb_ref[...]) * jax.nn.gelu(gr)).astype(BF16)
    return rg, hf[0:1, :], hb[0:1, :]


def _segment_permutation():
    p = np.zeros((PERM_BLOCK, PERM_BLOCK), np.float32)
    for t in range(SEG_LEN):
        for j in range(SUBLANES):
            p[t * SUBLANES + j, j * SEG_LEN + t] = 1.0
    return p


def _rglru_segments(xr_ref, gr_ref, perm_t_ref, cw_ref, cb_ref, wg_ref, bg_ref, lam_ref, h0, scan_refs, ready):
    af_ref, bf_ref, ab_ref, bb_ref = scan_refs
    n = xr_ref.shape[0]
    blocks = n // PERM_BLOCK
    sub = lax.broadcasted_iota(jnp.int32, (SUBLANES, 1), 0)
    zero_row = jnp.zeros((1, RG_WIDTH), F32)
    cw = cw_ref[...]
    neg = -lam_ref[...]
    softplus = jnp.maximum(neg, 0.0) + jnp.log1p(jnp.exp(-jnp.abs(neg)))
    decay = (-0.5 * C_LRU) * softplus

    def group(b, t):
        return xr_ref[b * PERM_BLOCK + t * SUBLANES:b * PERM_BLOCK + (t + 1) * SUBLANES, :]

    def before(b, t):
        wrap = group(b - 1, t)[SUBLANES - 1:SUBLANES, :] if b > 0 else zero_row
        return jnp.where(sub == 0, wrap, pltpu.roll(group(b, t), 1, axis=0))

    def after(b, t):
        wrap = group(b + 1, t)[0:1, :] if b + 1 < blocks else zero_row
        return jnp.where(sub == SUBLANES - 1, wrap, pltpu.roll(group(b, t), SUBLANES - 1, axis=0))

    ends = []
    for b in range(blocks):
        while not ready(min(b + 1, blocks - 1)):
            yield
        rows = slice(b * PERM_BLOCK, (b + 1) * PERM_BLOCK)
        ext = jnp.concatenate([before(b, SEG_LEN - 2), before(b, SEG_LEN - 1), xr_ref[rows, :], after(b, 0)], axis=0)
        xc = ext[0:PERM_BLOCK] * cw[0:1, :]
        for tap in range(1, CONV_W):
            xc = xc + ext[tap * SUBLANES:tap * SUBLANES + PERM_BLOCK] * cw[tap:tap + 1, :]
        xc = xc + cb_ref[...]
        yield

        xcb = xc.astype(BF16)
        halves = [_dot(xcb[:, c * MXU_DIM:(c + 1) * MXU_DIM], wg_ref[c].astype(BF16))
                  for c in range(RG_WIDTH // MXU_DIM)]
        yield
        half_xc = 0.5 * xc
        for d, (a_ref, b_ref) in enumerate(((af_ref, bf_ref), (ab_ref, bb_ref))):
            pre = [jnp.concatenate([hv[:, i * MXU_DIM:(i + 1) * MXU_DIM] for hv in halves], axis=1)
                   + bg_ref[i:i + 1, :] for i in (2 * d, 2 * d + 1)]
            r2 = jnp.tanh(pre[0]) + 1.0
            i2 = jnp.tanh(pre[1]) + 1.0
            log_a = r2 * decay[d:d + 1, :]
            a = jnp.exp(log_a)
            a_ref[rows, :] = a
            b_ref[rows, :] = jnp.sqrt(jnp.tanh(-log_a) * (a * a + 1.0)) * (i2 * half_xc)
            yield

        for a_ref, b_ref, steps in ((af_ref, bf_ref, range(SEG_LEN)), (ab_ref, bb_ref, range(SEG_LEN - 1, -1, -1))):
            hend = jnp.zeros((SUBLANES, RG_WIDTH), F32)
            pend = jnp.ones((SUBLANES, RG_WIDTH), F32)
            for t in steps:
                r = slice(b * PERM_BLOCK + t * SUBLANES, b * PERM_BLOCK + (t + 1) * SUBLANES)
                a = a_ref[r, :]
                hend = a * hend + b_ref[r, :]
                pend = a * pend
            ends.append((hend, pend))
            yield

    def carries(order, state, which):
        into = {}
        for b, j in order:
            into[b, j] = state
            hend, pend = ends[2 * b + which]
            state = hend[j:j + 1, :] + pend[j:j + 1, :] * state
        return into, state

    segs = [(b, j) for b in range(blocks) for j in range(SUBLANES)]
    into_f, hf = carries(segs, h0[0:1, :], 0)
    into_b, hb = carries(segs[::-1], h0[1:2, :], 1)
    yield

    for b in range(blocks):
        for a_ref, b_ref, into, steps in ((af_ref, bf_ref, into_f, range(SEG_LEN)),
                                          (ab_ref, bb_ref, into_b, range(SEG_LEN - 1, -1, -1))):
            h = jnp.concatenate([into[b, j] for j in range(SUBLANES)], axis=0)
            for t in steps:
                r = slice(b * PERM_BLOCK + t * SUBLANES, b * PERM_BLOCK + (t + 1) * SUBLANES)
                h = a_ref[r, :] * h + b_ref[r, :]
                b_ref[r, :] = h
            yield

    out = []
    for b in range(blocks):
        rows = slice(b * PERM_BLOCK, (b + 1) * PERM_BLOCK)
        rg = ((bf_ref[rows, :] + bb_ref[rows, :]) * jax.nn.gelu(gr_ref[rows, :])).astype(BF16)
        out.append(_dot(perm_t_ref[...], rg).astype(BF16))
        yield
    return jnp.concatenate(out, axis=0), hf, hb


def _out_and_router(attn, rg, x, m, wout_ref, g1_ref, b1_ref, wrt_ref):
    gate1 = m[:, 2 * D_MODEL:3 * D_MODEL]
    shift2 = m[:, 3 * D_MODEL:4 * D_MODEL]
    scale2 = m[:, 4 * D_MODEL:5 * D_MODEL]
    mix = _dot(jnp.concatenate([attn, rg], axis=1), wout_ref[...])
    yield
    x1 = _ln_plain(ALPHA * x + gate1 * mix) * g1_ref[...] + b1_ref[...]
    yield
    h2 = (_ln_plain(x1) * (1.0 + scale2) + shift2).astype(BF16)
    yield
    logits = _dot_nt(wrt_ref[...].astype(BF16), h2)
    e = jnp.exp(logits - jnp.max(logits, axis=0, keepdims=True))
    return x1, h2, e / jnp.sum(e, axis=0, keepdims=True)


def _mixer_kernel(x_ref, mod_ref, win_ref, wout_ref, bd_ref, qg_ref, kg_ref, cw_ref, cb_ref, wg_ref, bg_ref,
                  lam_ref, g1_ref, b1_ref, wrt_ref,
                  nk_ref, nv_ref, nh_ref, x1_ref, h2_ref, afft_ref,
                  xprev_ref, xr_ref, gr_ref, attn_ref, *scan_refs):
    step = pl.program_id(0)
    last = pl.num_programs(0) - 1
    m = mod_ref[0:1, :]

    def second_half():
        h0 = jnp.zeros((2, RG_WIDTH), F32)
        rg, hf, hb = yield from _rglru_rows(xr_ref[...], gr_ref[...], cw_ref, cb_ref, wg_ref, bg_ref, lam_ref, h0,
                                            scan_refs)
        nh_ref[0] = jnp.concatenate([hf, hb], axis=0)
        x1_ref[...], h2_ref[...], afft_ref[...] = yield from _out_and_router(
            attn_ref[...], rg, xprev_ref[...], m, wout_ref, g1_ref, b1_ref, wrt_ref)

    def first_half():
        x = x_ref[...]
        h = _modulated(x, m)
        yield
        q, k, v = yield from _qkv(h, win_ref, bd_ref, qg_ref, kg_ref, None)
        nk_ref[0] = k.T
        nv_ref[0] = v.T
        yield
        xr, gr = yield from _rg_inputs(h, win_ref)
        attn = yield from _attend(q, k.astype(BF16), v.astype(BF16), False)
        return x, attn, xr, gr

    def hand_over(x, attn, xr, gr):
        xprev_ref[...] = x
        attn_ref[...] = attn
        xr_ref[...] = xr
        gr_ref[...] = gr

    @pl.when(step == 0)
    def _():
        hand_over(*_drive(first_half())[0])

    @pl.when((step > 0) & (step < last))
    def _():
        hand_over(*_drive(first_half(), second_half(), order=MIXER_ORDER)[0])

    @pl.when(step == last)
    def _():
        _drive(second_half())


def _mixer_prompt(x2d, mod, win_bf, wout_bf, consts, g1, b1, wrt, *, n, nb):
    tokens = nb * n
    first = lambda w: pl.BlockSpec((n, w), lambda i: (jnp.minimum(i, nb - 1), 0))
    second = lambda w: pl.BlockSpec((n, w), lambda i: (jnp.maximum(i - 1, 0), 0))
    small = list(consts) + [g1, b1, wrt]
    return pl.pallas_call(
        _mixer_kernel,
        grid=(nb + 1,),
        in_specs=[first(D_MODEL), _full(mod.shape), _full(win_bf.shape), _full(wout_bf.shape)]
                 + [_full(a.shape) for a in small],
        out_specs=[pl.BlockSpec((1, KV_WIDTH, n), lambda i: (jnp.minimum(i, nb - 1), 0, 0)),
                   pl.BlockSpec((1, KV_WIDTH, n), lambda i: (jnp.minimum(i, nb - 1), 0, 0)),
                   pl.BlockSpec((1, 2, RG_WIDTH), lambda i: (jnp.maximum(i - 1, 0), 0, 0)),
                   second(D_MODEL), second(D_MODEL),
                   pl.BlockSpec((N_EXPERTS, n), lambda i: (jnp.maximum(i - 1, 0), 0))],
        out_shape=[jax.ShapeDtypeStruct((nb, KV_WIDTH, n), F32), jax.ShapeDtypeStruct((nb, KV_WIDTH, n), F32),
                   jax.ShapeDtypeStruct((nb, 2, RG_WIDTH), F32),
                   jax.ShapeDtypeStruct((tokens, D_MODEL), F32), jax.ShapeDtypeStruct((tokens, D_MODEL), BF16),
                   jax.ShapeDtypeStruct((nb * N_EXPERTS, n), F32)],
        scratch_shapes=[pltpu.VMEM((n, D_MODEL), F32), pltpu.VMEM((n, RG_WIDTH), F32),
                        pltpu.VMEM((n, RG_WIDTH), F32), pltpu.VMEM((n, ATTN_WIDTH), BF16)]
                       + [pltpu.VMEM((n, RG_WIDTH), F32)] * 4,
        compiler_params=_params(),
        name="mixer_prompt",
    )(x2d, mod, win_bf, wout_bf, *small)


def _front_kernel(x_ref, mod_ref, win_ref, bd_ref, qg_ref, kg_ref, cw_ref, cb_ref, wg_ref, bg_ref, lam_ref,
                  perm_ref, perm_t_ref, h0_ref, cos_ref, sin_ref,
                  q_ref, k_ref, v_ref, rg_ref, xr_ref, gr_ref, *scan_refs):
    m = mod_ref[pl.ds(1 + pl.program_id(0), 1), :]
    rows = PERM_BLOCK
    streams = x_ref.shape[0] // rows

    def row_block(j):
        r = pl.ds(j * rows, rows)
        h = _modulated(x_ref[r, :], m)
        yield
        q, k, v = yield from _qkv(h, win_ref, bd_ref, qg_ref, kg_ref, (cos_ref[r, :], sin_ref[r, :]))
        q_ref[r, :] = q
        k_ref[r, :] = k.astype(BF16)
        v_ref[r, :] = v.astype(BF16)
        yield
        hp = _dot(perm_ref[...], h).astype(BF16)
        xr_ref[r, :], gr_ref[r, :] = yield from _rg_inputs(hp, win_ref)
        traced[j] = True

    traced = [False] * streams
    recurrent = _rglru_segments(xr_ref, gr_ref, perm_t_ref, cw_ref, cb_ref, wg_ref, bg_ref, lam_ref, h0_ref[0],
                                scan_refs, lambda block: traced[block])
    chunks = 8
    order = [0] * chunks + [1] * chunks + [s for j in range(2, streams) for _ in range(chunks) for s in (j, streams)]
    results = _drive(*[row_block(j) for j in range(streams)], recurrent, order=order)
    rg_ref[...], _, _ = results[-1]


def _front_sample(x2d, mod, win_bf, consts, h0, rope, *, n, nb):
    tokens = nb * n
    seq = lambda w: pl.BlockSpec((n, w), lambda b: (b, 0))
    small = list(consts)
    return pl.pallas_call(
        _front_kernel,
        grid=(nb,),
        in_specs=[seq(D_MODEL), _full(mod.shape), _full(win_bf.shape)] + [_full(a.shape) for a in small]
                 + [pl.BlockSpec((1, 2, RG_WIDTH), lambda b: (b, 0, 0)), _full(rope[0].shape), _full(rope[1].shape)],
        out_specs=[seq(ATTN_WIDTH), seq(KV_WIDTH), seq(KV_WIDTH), seq(RG_WIDTH)],
        out_shape=[jax.ShapeDtypeStruct((tokens, ATTN_WIDTH), BF16), jax.ShapeDtypeStruct((tokens, KV_WIDTH), BF16),
                   jax.ShapeDtypeStruct((tokens, KV_WIDTH), BF16), jax.ShapeDtypeStruct((tokens, RG_WIDTH), BF16)],
        scratch_shapes=[pltpu.VMEM((n, RG_WIDTH), F32)] * 6,
        compiler_params=_params(),
        name="front_sample",
    )(x2d, mod, win_bf, *small, h0, *rope)


def _attn_kernel(q_ref, k_ref, v_ref, kc_ref, vc_ref, o_ref):
    k = jnp.concatenate([k_ref[...], kc_ref[...].astype(BF16)], axis=0)
    v = jnp.concatenate([v_ref[...], vc_ref[...].astype(BF16)], axis=0)
    o_ref[...], = _drive(_attend(q_ref[...], k, v, True))


def _attn_sample(q, k, v, cache_k, cache_v, *, n, nb, past):
    tq = min(n, ATTN_Q_TILE)
    tiles = n // tq
    return pl.pallas_call(
        _attn_kernel,
        grid=(nb, tiles),
        in_specs=[pl.BlockSpec((tq, ATTN_WIDTH), lambda b, t: (b * tiles + t, 0)),
                  pl.BlockSpec((n, KV_WIDTH), lambda b, t: (b, 0)),
                  pl.BlockSpec((n, KV_WIDTH), lambda b, t: (b, 0)),
                  pl.BlockSpec((past, KV_WIDTH), lambda b, t: (b, 0)),
                  pl.BlockSpec((past, KV_WIDTH), lambda b, t: (b, 0))],
        out_specs=pl.BlockSpec((tq, ATTN_WIDTH), lambda b, t: (b * tiles + t, 0)),
        out_shape=jax.ShapeDtypeStruct((nb * n, ATTN_WIDTH), BF16),
        compiler_params=_params(2),
        name="attn_sample",
    )(q, k, v, cache_k, cache_v)


def _post_kernel(attn_ref, rg_ref, x_ref, mod_ref, wout_ref, g1_ref, b1_ref, wrt_ref, x1_ref, h2_ref, afft_ref):
    m = mod_ref[pl.ds(1 + pl.program_id(0), 1), :]
    rows = x_ref.shape[0] // POST_ROW_STREAMS

    def row_block(j):
        r = pl.ds(j * rows, rows)
        x1_ref[r, :], h2_ref[r, :], afft_ref[:, r] = yield from _out_and_router(
            attn_ref[r, :], rg_ref[r, :], x_ref[r, :], m, wout_ref, g1_ref, b1_ref, wrt_ref)

    stagger = [j for i in range(POST_ROW_STREAMS) for j in range(i + 1)]
    _drive(*[row_block(j) for j in range(POST_ROW_STREAMS)], order=stagger)


def _post_sample(attn, rg, x2d, mod, wout_bf, g1, b1, wrt, *, n, nb):
    tokens = nb * n
    seq = lambda w: pl.BlockSpec((n, w), lambda b: (b, 0))
    return pl.pallas_call(
        _post_kernel,
        grid=(nb,),
        in_specs=[seq(ATTN_WIDTH), seq(RG_WIDTH), seq(D_MODEL), _full(mod.shape), _full(wout_bf.shape),
                  _full(g1.shape), _full(b1.shape), _full(wrt.shape)],
        out_specs=[seq(D_MODEL), seq(D_MODEL), pl.BlockSpec((N_EXPERTS, n), lambda b: (b, 0))],
        out_shape=[jax.ShapeDtypeStruct((tokens, D_MODEL), F32), jax.ShapeDtypeStruct((tokens, D_MODEL), BF16),
                   jax.ShapeDtypeStruct((nb * N_EXPERTS, n), F32)],
        compiler_params=_params(),
        name="post_sample",
    )(attn, rg, x2d, mod, wout_bf, g1, b1, wrt)


def _slot_onehot(mask, pos, cap):
    n = mask.shape[1]
    slot = lax.broadcasted_iota(jnp.int32, (cap, n), 0).astype(F32)
    return [(slot == pos[e:e + 1, :]) & (mask[e:e + 1, :] > 0.5) for e in range(mask.shape[0])]


def _route_kernel(afft_ref, h2_ref, xs_ref, mask_ref, pos_ref, *, n, cap, seqs):
    b = pl.program_id(0)

    @pl.when(b == 0)
    def _():
        aff = afft_ref[...]
        thr = jnp.zeros((aff.shape[0], 1), jnp.int32)
        for bit in range(30, -1, -1):
            cand = thr | (1 << bit)
            cnt = jnp.sum((aff >= lax.bitcast_convert_type(cand, F32)).astype(F32), axis=1, keepdims=True)
            thr = jnp.where(cnt >= cap, cand, thr)
        above = aff >= lax.bitcast_convert_type(thr + 1, F32)
        tied = (aff >= lax.bitcast_convert_type(thr, F32)) & jnp.logical_not(above)
        need = cap - jnp.sum(above.astype(F32), axis=1, keepdims=True)
        before = (lax.broadcasted_iota(jnp.int32, (n, n), 0)
                  < lax.broadcasted_iota(jnp.int32, (n, n), 1)).astype(BF16)
        tie_rank = _dot(tied.astype(BF16), before)
        mask = (above | (tied & (tie_rank < need))).astype(F32)
        mask_ref[...] = mask
        pos_ref[...] = _dot(mask.astype(BF16), before)

    for j in range(seqs):
        r0 = pl.multiple_of((b * seqs + j) * N_EXPERTS, N_EXPERTS)
        onehots = _slot_onehot(mask_ref[pl.ds(r0, N_EXPERTS), :], pos_ref[pl.ds(r0, N_EXPERTS), :], cap)
        sel = jnp.concatenate(onehots, axis=0).astype(BF16)
        xs = _dot(sel, h2_ref[j * n:(j + 1) * n, :]).astype(BF16)
        xs_ref[:, j * cap:(j + 1) * cap, :] = xs.reshape(N_EXPERTS, cap, D_MODEL)


def _route(afft, h2, *, n, nb, cap, seqs):
    rows = nb * N_EXPERTS
    return pl.pallas_call(
        functools.partial(_route_kernel, n=n, cap=cap, seqs=seqs),
        grid=(nb // seqs,),
        in_specs=[_full(afft.shape), pl.BlockSpec((seqs * n, D_MODEL), lambda b: (b, 0))],
        out_specs=[pl.BlockSpec((N_EXPERTS, seqs * cap, D_MODEL), lambda b: (0, b, 0)),
                   _full((rows, n)), _full((rows, n))],
        out_shape=[jax.ShapeDtypeStruct((N_EXPERTS, nb * cap, D_MODEL), BF16),
                   jax.ShapeDtypeStruct((rows, n), F32), jax.ShapeDtypeStruct((rows, n), F32)],
        compiler_params=_params(),
        name="route_n%d" % n,
    )(afft, h2)


def _slot_gates(aff_ref, mask_ref, pos_ref, e, cap):
    cols = []
    for b in range(aff_ref.shape[0] // N_EXPERTS):
        r = b * N_EXPERTS + e
        onehot, = _slot_onehot(mask_ref[pl.ds(r, 1), :], pos_ref[pl.ds(r, 1), :], cap)
        cols.append(jnp.sum(jnp.where(onehot, aff_ref[pl.ds(r, 1), :], 0.0), axis=1, keepdims=True))
    return jnp.concatenate(cols, axis=0)


def _ffn_kernel(xp_ref, xs_ref, ap_ref, mp_ref, pp_ref, as_ref, ms_ref, ps_ref, wgu_ref, wd_ref, yp_ref, ys_ref):
    e = pl.program_id(0)
    rows_p = xp_ref.shape[1]
    xs = jnp.concatenate([xp_ref[0], xs_ref[0]], axis=0)
    gu = _dot(xs, wgu_ref[0].astype(BF16))
    gate = gu[:, :D_EXPERT]
    up = gu[:, D_EXPERT:]
    act = (gate * jax.nn.sigmoid(gate) * up).astype(BF16)
    y = _dot(act, wd_ref[0].astype(BF16))
    g = jnp.concatenate([_slot_gates(ap_ref, mp_ref, pp_ref, e, rows_p * N_EXPERTS // ap_ref.shape[0]),
                         _slot_gates(as_ref, ms_ref, ps_ref, e, xs_ref.shape[1] * N_EXPERTS // as_ref.shape[0])],
                        axis=0)
    y = (y * g).astype(BF16)
    yp_ref[0] = y[:rows_p]
    ys_ref[0] = y[rows_p:]


def _ffn(xs_p, xs_s, route_p, route_s, w_gate_up, w_down):
    per_e = lambda a: pl.BlockSpec((1,) + a.shape[1:], lambda e: (e, 0, 0))
    tables = list(route_p) + list(route_s)
    return pl.pallas_call(
        _ffn_kernel,
        grid=(N_EXPERTS,),
        in_specs=[per_e(xs_p), per_e(xs_s)] + [_full(t.shape) for t in tables] + [per_e(w_gate_up), per_e(w_down)],
        out_specs=[per_e(xs_p), per_e(xs_s)],
        out_shape=[jax.ShapeDtypeStruct(xs_p.shape, BF16), jax.ShapeDtypeStruct(xs_s.shape, BF16)],
        compiler_params=_params(),
        name="expert_ffn",
    )(xs_p, xs_s, *tables, w_gate_up, w_down)


def _combine_kernel(y_ref, mask_ref, pos_ref, x1_ref, mod_ref, g2_ref, b2_ref, o_ref, *, cap, seqs, mod_row0,
                    mod_row_step):
    row = mod_row0 + mod_row_step * pl.program_id(0)
    gate2 = mod_ref[pl.ds(row, 1), 5 * D_MODEL:6 * D_MODEL]
    tn = x1_ref.shape[0] // seqs
    for j in range(seqs):
        e0 = j * N_EXPERTS
        sel = jnp.concatenate(
            _slot_onehot(mask_ref[e0:e0 + N_EXPERTS, :], pos_ref[e0:e0 + N_EXPERTS, :], cap), axis=0).astype(BF16)
        y = y_ref[:, j * cap:(j + 1) * cap, :].reshape(N_EXPERTS * cap, D_MODEL)
        ff = _dot_tn(sel, y)
        r = pl.ds(j * tn, tn)
        o_ref[r, :] = _ln_plain(ALPHA * x1_ref[r, :] + gate2 * ff) * g2_ref[...] + b2_ref[...]


def _combine(y, mask, pos, x1, mod, g2, b2, *, n, nb, cap, seqs, sample):
    tn = min(n, COMBINE_TILE)
    tiles = n // tn
    assert seqs == 1 or tiles == 1
    return pl.pallas_call(
        functools.partial(_combine_kernel, cap=cap, seqs=seqs, mod_row0=1 if sample else 0,
                          mod_row_step=1 if sample else 0),
        grid=(nb // seqs, tiles),
        in_specs=[pl.BlockSpec((N_EXPERTS, seqs * cap, D_MODEL), lambda b, t: (0, b, 0)),
                  pl.BlockSpec((seqs * N_EXPERTS, tn), lambda b, t: (b, t)),
                  pl.BlockSpec((seqs * N_EXPERTS, tn), lambda b, t: (b, t)),
                  pl.BlockSpec((seqs * tn, D_MODEL), lambda b, t: (b * tiles + t, 0)),
                  _full(mod.shape), _full(g2.shape), _full(b2.shape)],
        out_specs=pl.BlockSpec((seqs * tn, D_MODEL), lambda b, t: (b * tiles + t, 0)),
        out_shape=jax.ShapeDtypeStruct((nb * n, D_MODEL), F32),
        compiler_params=_params(2),
        name="combine_sample" if sample else "combine_prompt",
    )(y, mask, pos, x1, mod, g2, b2)


def _block_diag_256(w):
    per_tile = MXU_DIM // RG_BLOCK
    tiles = []
    for c in range(w.shape[0] // per_tile):
        tiles.append(jax.scipy.linalg.block_diag(*[w[c * per_tile + i] for i in range(per_tile)]))
    return jnp.stack(tiles)


def _rope_tables(n):
    lane = np.arange(LANES)
    within = lane % HEAD_DIM
    freq = (within % 16).astype(np.float32)
    inv = np.float32(ROPE_THETA) ** (-freq / np.float32(16.0))
    tok = np.arange(n)
    pos = np.where((within < HEAD_DIM // 2)[None, :], (tok // GRID_W)[:, None], (tok % GRID_W)[:, None])
    ang = pos.astype(np.float32) * inv[None, :]
    sign = np.where(within % 32 < 16, -1.0, 1.0).astype(np.float32)
    return jnp.asarray(np.cos(ang)), jnp.asarray(np.sin(ang) * sign[None, :])


def kernel(x_prompt, x_sample, cache_k, cache_v, state_h, c, c_ctx, w_mod, b_mod, w_in, q_norm_g, k_norm_g,
           conv_w, conv_b, w_rg_a, b_rg_a, w_rg_x, b_rg_x, rg_lambda, w_out, ln1_g, ln1_b, w_router,
           w_gate_up, w_down, ln2_g, ln2_b):
    assert w_mod.shape[0] == DEPTH == 1
    nb_p, n_p, _ = x_prompt.shape
    nb_s, n_s, _ = x_sample.shape
    past = cache_k.shape[2]
    cap_p = CAP_FACTOR * n_p // N_EXPERTS
    cap_s = CAP_FACTOR * n_s // N_EXPERTS

    cs = jnp.concatenate([c_ctx[None, :], c, jnp.zeros((MOD_ROWS - 1 - nb_s, D_MODEL), F32)], axis=0)
    row = lambda v: v.reshape(1, -1)
    head_avg = np.kron(np.eye(ATTN_WIDTH // HEAD_DIM, dtype=np.float32),
                       np.full((HEAD_DIM, HEAD_DIM), 1.0 / HEAD_DIM, np.float32))
    bd = jnp.asarray(head_avg, BF16)
    qg = row(jnp.tile(q_norm_g[0], N_Q_HEADS))
    kg = row(jnp.tile(k_norm_g[0], N_KV_HEADS))
    wg = 0.5 * jnp.concatenate([_block_diag_256(w_rg_a[0, 0]), _block_diag_256(w_rg_x[0, 0]),
                                _block_diag_256(w_rg_a[0, 1]), _block_diag_256(w_rg_x[0, 1])], axis=2)
    bg = 0.5 * jnp.stack([b_rg_a[0, 0], b_rg_x[0, 0], b_rg_a[0, 1], b_rg_x[0, 1]])
    perm = _segment_permutation()
    consts = (bd, qg, kg, conv_w[0], row(conv_b[0]), wg, bg, rg_lambda[0])
    perms = (jnp.asarray(perm, BF16), jnp.asarray(perm.T, BF16))
    wrt = w_router[0].T
    g1, b1, g2, b2 = row(ln1_g[0]), row(ln1_b[0]), row(ln2_g[0]), row(ln2_b[0])

    win_bf, wout_bf = _cast_weights(w_in[0], w_out[0])
    mod = _mod_vectors(cs, w_mod[0], row(b_mod[0]))

    xp = x_prompt.reshape(nb_p * n_p, D_MODEL)
    xs = x_sample.reshape(nb_s * n_s, D_MODEL)

    new_k, new_v, new_h, x1_p, h2_p, aff_p = _mixer_prompt(
        xp, mod, win_bf, wout_bf, consts, g1, b1, wrt, n=n_p, nb=nb_p)

    q_s, k_s, v_s, rg_s = _front_sample(xs, mod, win_bf, consts + perms, state_h[:, 0], _rope_tables(n_s),
                                        n=n_s, nb=nb_s)
    attn_s = _attn_sample(q_s, k_s, v_s, cache_k[:, 0].reshape(nb_s * past, KV_WIDTH),
                          cache_v[:, 0].reshape(nb_s * past, KV_WIDTH), n=n_s, nb=nb_s, past=past)
    x1_s, h2_s, aff_s = _post_sample(attn_s, rg_s, xs, mod, wout_bf, g1, b1, wrt, n=n_s, nb=nb_s)

    seqs_p = PROMPT_SEQS_PER_STEP if nb_p % PROMPT_SEQS_PER_STEP == 0 else 1
    xs_p, mask_p, pos_p = _route(aff_p, h2_p, n=n_p, nb=nb_p, cap=cap_p, seqs=seqs_p)
    xs_s, mask_s, pos_s = _route(aff_s, h2_s, n=n_s, nb=nb_s, cap=cap_s, seqs=1)

    y_p, y_s = _ffn(xs_p, xs_s, (aff_p, mask_p, pos_p), (aff_s, mask_s, pos_s), w_gate_up[0], w_down[0])

    out_p = _combine(y_p, mask_p, pos_p, x1_p, mod, g2, b2, n=n_p, nb=nb_p, cap=cap_p, seqs=seqs_p, sample=False)
    out_s = _combine(y_s, mask_s, pos_s, x1_s, mod, g2, b2, n=n_s, nb=nb_s, cap=cap_s, seqs=1, sample=True)

    def cache_layout(t):
        t = t.reshape(nb_p, DEPTH, N_KV_HEADS, HEAD_DIM, n_p)
        return jnp.transpose(t, (0, 1, 4, 2, 3))

    return (out_p.reshape(nb_p, n_p, D_MODEL), out_s.reshape(nb_s, n_s, D_MODEL),
            cache_layout(new_k), cache_layout(new_v), new_h.reshape(nb_p, DEPTH, 2, RG_WIDTH))
```

```python
import functools

import numpy as np
import jax
import jax.numpy as jnp
from jax import lax
from jax.experimental import pallas as pl
from jax.experimental.pallas import tpu as pltpu

F32 = jnp.float32
BF16 = jnp.bfloat16

D_MODEL = 1024
HEAD_DIM = 64
N_Q_HEADS = 8
N_KV_HEADS = 2
GROUP = N_Q_HEADS // N_KV_HEADS
ATTN_WIDTH = N_Q_HEADS * HEAD_DIM
KV_WIDTH = N_KV_HEADS * HEAD_DIM
RG_WIDTH = D_MODEL - ATTN_WIDTH
RG_BLOCK = 64
CONV_W = 4
C_LRU = 8.0
N_EXPERTS = 16
CAP_FACTOR = 2
D_EXPERT = 1024
GRID_W = 64
ROPE_THETA = 10000.0
NORM_EPS = 1e-6
DEPTH = 1
ALPHA = (2.0 * DEPTH) ** 0.25
IN_WIDTH = ATTN_WIDTH + 2 * KV_WIDTH + 2 * RG_WIDTH
Q_SCALE = HEAD_DIM ** -0.5 * float(np.log2(np.e))

LANES = 128
SUBLANES = 8
MXU_DIM = 256
VMEM_LIMIT_BYTES = 56 * 1024 * 1024

MOD_ROWS = SUBLANES
MOD_TILE = 256
ATTN_Q_TILE = 256
COMBINE_TILE = 256
MIXER_ORDER = (1, 0, 0, 1, 1, 0, 1, 1, 0, 1, 0, 1, 0, 1, 1, 0, 1, 1, 0, 0, 1, 0, 1, 0, 0, 0, 1, 0, 0, 1, 0)
PERM_BLOCK = MXU_DIM
SEG_LEN = PERM_BLOCK // SUBLANES
TAIL_ROW_STREAMS = 2
PROMPT_SEQS_PER_STEP = 4


def _params(n_axes=1):
    return pltpu.CompilerParams(dimension_semantics=("arbitrary",) * n_axes,
                                vmem_limit_bytes=VMEM_LIMIT_BYTES)


def _full(shape):
    zeros = (0,) * len(shape)
    return pl.BlockSpec(shape, lambda *_: zeros)


def _ln_plain(x):
    mu = jnp.mean(x, -1, keepdims=True)
    xc = x - mu
    var = jnp.mean(xc * xc, -1, keepdims=True)
    return xc * lax.rsqrt(var + NORM_EPS)


def _dot(a, b):
    return jnp.dot(a, b, preferred_element_type=F32)


def _dot_nt(a, b):
    return lax.dot_general(a, b, (((1,), (1,)), ((), ())), preferred_element_type=F32)


def _dot_tn(a, b):
    return lax.dot_general(a, b, (((0,), (0,)), ((), ())), preferred_element_type=F32)


def _split_bf16(x):
    hi = x.astype(BF16)
    lo = (x - hi.astype(F32)).astype(BF16)
    return hi, lo


def _cast_kernel(a_ref, b_ref, ao_ref, bo_ref):
    ao_ref[...] = a_ref[...].astype(BF16)
    bo_ref[...] = b_ref[...].astype(BF16)


def _cast_weights(w_in, w_out):
    return pl.pallas_call(
        _cast_kernel,
        grid=(1,),
        in_specs=[_full(w_in.shape), _full(w_out.shape)],
        out_specs=[_full(w_in.shape), _full(w_out.shape)],
        out_shape=[jax.ShapeDtypeStruct(w_in.shape, BF16), jax.ShapeDtypeStruct(w_out.shape, BF16)],
        compiler_params=_params(),
        name="cast_weights",
    )(w_in, w_out)


def _mod_kernel(cs_ref, w_ref, b_ref, o_ref):
    @pl.when(pl.program_id(0) == 0)
    def _():
        o_ref[...] = jnp.broadcast_to(b_ref[...], o_ref.shape)

    cs = cs_ref[...]
    s = cs * jax.nn.sigmoid(cs)
    o_ref[...] += _dot(s.astype(BF16), w_ref[...].astype(BF16))


def _mod_vectors(cs, w_mod, b_mod):
    width = w_mod.shape[1]
    return pl.pallas_call(
        _mod_kernel,
        grid=(D_MODEL // MOD_TILE,),
        in_specs=[pl.BlockSpec((MOD_ROWS, MOD_TILE), lambda j: (0, j)),
                  pl.BlockSpec((MOD_TILE, width), lambda j: (j, 0)),
                  _full(b_mod.shape)],
        out_specs=_full((MOD_ROWS, width)),
        out_shape=jax.ShapeDtypeStruct((MOD_ROWS, width), F32),
        compiler_params=_params(),
        name="mod_vectors",
    )(cs, w_mod, b_mod)


def _head_mean_sq(x, bd):
    hi, lo = _split_bf16(x * x)
    return _dot(hi, bd) + _dot(lo, bd)


def _rope_lanes(x, cos, sin_signed, hi_half):
    partner = jnp.where(hi_half, pltpu.roll(x, 16, axis=1), pltpu.roll(x, LANES - 16, axis=1))
    return x * cos + partner * sin_signed


def _modulated(x, m):
    return (_ln_plain(x) * (1.0 + m[:, D_MODEL:2 * D_MODEL]) + m[:, 0:D_MODEL]).astype(BF16)


def _qkv(h, win_ref, bd_ref, qg_ref, kg_ref, rope):
    n = h.shape[0]
    bd = bd_ref[...]
    q = _dot(h, win_ref[:, 0:ATTN_WIDTH])
    yield
    q = q * lax.rsqrt(_head_mean_sq(q, bd) + NORM_EPS) * qg_ref[...]
    yield
    k = _dot(h, win_ref[:, ATTN_WIDTH:ATTN_WIDTH + KV_WIDTH])
    k = k * lax.rsqrt(_head_mean_sq(k, bd[:KV_WIDTH, :KV_WIDTH]) + NORM_EPS) * kg_ref[...]
    v = _dot(h, win_ref[:, ATTN_WIDTH + KV_WIDTH:ATTN_WIDTH + 2 * KV_WIDTH])
    yield
    if rope is not None:
        cos, sin_signed = rope
        lane = lax.broadcasted_iota(jnp.int32, (n, LANES), 1)
        hi_half = (lane & 16) != 0
        q = jnp.concatenate(
            [_rope_lanes(q[:, j * LANES:(j + 1) * LANES], cos, sin_signed, hi_half)
             for j in range(ATTN_WIDTH // LANES)], axis=1)
        k = _rope_lanes(k, cos, sin_signed, hi_half)
    return (q * Q_SCALE).astype(BF16), k, v


def _drive(*streams, order=()):
    results = [None] * len(streams)
    live = list(range(len(streams)))
    plan = [j for j in order]
    while live:
        idx = plan.pop(0) if plan else live[0]
        if idx not in live:
            continue
        if not plan:
            live.append(live.pop(0))
        try:
            next(streams[idx])
        except StopIteration as stop:
            results[idx] = stop.value
            live.remove(idx)
    return results


def _attend(q, k, v, lookahead):
    def scores(hq):
        kv = hq // GROUP
        return _dot_nt(q[:, hq * HEAD_DIM:(hq + 1) * HEAD_DIM], k[:, kv * HEAD_DIM:(kv + 1) * HEAD_DIM])

    outs = []
    s_next = scores(0) if lookahead else None
    for hq in range(N_Q_HEADS):
        if lookahead:
            s, s_next = s_next, (scores(hq + 1) if hq + 1 < N_Q_HEADS else None)
        else:
            s = scores(hq)
        kv = hq // GROUP
        e = jnp.exp2(s - jnp.max(s, axis=-1, keepdims=True))
        denom = jnp.sum(e, axis=-1, keepdims=True)
        pv = _dot(e.astype(BF16), v[:, kv * HEAD_DIM:(kv + 1) * HEAD_DIM])
        outs.append(pv / denom)
        yield
    return jnp.concatenate(outs, axis=1).astype(BF16)


def _rg_inputs(h, win_ref):
    rg_lo = ATTN_WIDTH + 2 * KV_WIDTH
    xr = _dot(h, win_ref[:, rg_lo:rg_lo + RG_WIDTH])
    yield
    gr = _dot(h, win_ref[:, rg_lo + RG_WIDTH:IN_WIDTH])
    yield
    return xr, gr


def _rglru_rows(xr, gr, cw_ref, cb_ref, wg_ref, bg_ref, lam_ref, h0, scan_refs):
    af_ref, bf_ref, ab_ref, bb_ref = scan_refs
    n = xr.shape[0]

    t_idx = lax.broadcasted_iota(jnp.int32, (n, 1), 0)
    cw = cw_ref[...]
    xc = jnp.where(t_idx >= 2, pltpu.roll(xr, 2, axis=0), 0.0) * cw[0:1, :]
    xc = xc + jnp.where(t_idx >= 1, pltpu.roll(xr, 1, axis=0), 0.0) * cw[1:2, :]
    xc = xc + xr * cw[2:3, :]
    xc = xc + jnp.where(t_idx < n - 1, pltpu.roll(xr, n - 1, axis=0), 0.0) * cw[3:4, :]
    xc = xc + cb_ref[...]
    yield

    xcb = xc.astype(BF16)
    halves = [_dot(xcb[:, c * MXU_DIM:(c + 1) * MXU_DIM], wg_ref[c].astype(BF16))
              for c in range(RG_WIDTH // MXU_DIM)]

    def gate_pre(idx):
        return jnp.concatenate([hv[:, idx * MXU_DIM:(idx + 1) * MXU_DIM] for hv in halves], axis=1) \
            + bg_ref[idx:idx + 1, :]

    yield
    neg = -lam_ref[...]
    softplus = jnp.maximum(neg, 0.0) + jnp.log1p(jnp.exp(-jnp.abs(neg)))
    decay = (-0.5 * C_LRU) * softplus
    half_xc = 0.5 * xc

    def coeffs(d):
        r2 = jnp.tanh(gate_pre(2 * d)) + 1.0
        i2 = jnp.tanh(gate_pre(2 * d + 1)) + 1.0
        log_a = r2 * decay[d:d + 1, :]
        a = jnp.exp(log_a)
        bx = jnp.sqrt(jnp.tanh(-log_a) * (a * a + 1.0)) * (i2 * half_xc)
        return a, bx

    groups = n // SUBLANES
    rmod = lax.broadcasted_iota(jnp.int32, (1, SUBLANES, 1), 1)

    def tile_scan(d, forward):
        a, bx = coeffs(d)
        yield
        a = a.reshape(groups, SUBLANES, RG_WIDTH)
        bx = bx.reshape(groups, SUBLANES, RG_WIDTH)
        for s in (1, 2, 4):
            ok = (rmod >= s) if forward else (rmod < SUBLANES - s)
            shift = s if forward else SUBLANES - s
            a_sh = jnp.where(ok, pltpu.roll(a, shift, axis=1), 1.0)
            b_sh = jnp.where(ok, pltpu.roll(bx, shift, axis=1), 0.0)
            bx = a * b_sh + bx
            a = a * a_sh
            yield
        return a.reshape(n, RG_WIDTH), bx.reshape(n, RG_WIDTH)

    af_ref[...], bf_ref[...] = yield from tile_scan(0, True)
    ab_ref[...], bb_ref[...] = yield from tile_scan(1, False)

    def step(g, carry):
        hf, hb = carry
        rf = pl.multiple_of(g * SUBLANES, SUBLANES)
        rb = pl.multiple_of((groups - 1 - g) * SUBLANES, SUBLANES)
        new_f = af_ref[pl.ds(rf, SUBLANES), :] * hf + bf_ref[pl.ds(rf, SUBLANES), :]
        new_b = ab_ref[pl.ds(rb, SUBLANES), :] * hb + bb_ref[pl.ds(rb, SUBLANES), :]
        bf_ref[pl.ds(rf, SUBLANES), :] = new_f
        bb_ref[pl.ds(rb, SUBLANES), :] = new_b
        return (jnp.broadcast_to(new_f[SUBLANES - 1:SUBLANES, :], (SUBLANES, RG_WIDTH)),
                jnp.broadcast_to(new_b[0:1, :], (SUBLANES, RG_WIDTH)))

    hf, hb = lax.fori_loop(
        0, groups, step,
        (jnp.broadcast_to(h0[0:1, :], (SUBLANES, RG_WIDTH)), jnp.broadcast_to(h0[1:2, :], (SUBLANES, RG_WIDTH))),
        unroll=True)
    yield

    rg = ((bf_ref[...] + bb_ref[...]) * jax.nn.gelu(gr)).astype(BF16)
    return rg, hf[0:1, :], hb[0:1, :]


def _segment_permutation():
    p = np.zeros((PERM_BLOCK, PERM_BLOCK), np.float32)
    for t in range(SEG_LEN):
        for j in range(SUBLANES):
            p[t * SUBLANES + j, j * SEG_LEN + t] = 1.0
    return p


def _rglru_segments(xr_ref, gr_ref, perm_t_ref, cw_ref, cb_ref, wg_ref, bg_ref, lam_ref, h0, scan_refs, ready):
    af_ref, bf_ref, ab_ref, bb_ref = scan_refs
    n = xr_ref.shape[0]
    blocks = n // PERM_BLOCK
    sub = lax.broadcasted_iota(jnp.int32, (SUBLANES, 1), 0)
    zero_row = jnp.zeros((1, RG_WIDTH), F32)
    cw = cw_ref[...]
    neg = -lam_ref[...]
    softplus = jnp.maximum(neg, 0.0) + jnp.log1p(jnp.exp(-jnp.abs(neg)))
    decay = (-0.5 * C_LRU) * softplus

    def group(b, t):
        return xr_ref[b * PERM_BLOCK + t * SUBLANES:b * PERM_BLOCK + (t + 1) * SUBLANES, :]

    def before(b, t):
        wrap = group(b - 1, t)[SUBLANES - 1:SUBLANES, :] if b > 0 else zero_row
        return jnp.where(sub == 0, wrap, pltpu.roll(group(b, t), 1, axis=0))

    def after(b, t):
        wrap = group(b + 1, t)[0:1, :] if b + 1 < blocks else zero_row
        return jnp.where(sub == SUBLANES - 1, wrap, pltpu.roll(group(b, t), SUBLANES - 1, axis=0))

    ends = []
    for b in range(blocks):
        while not ready(min(b + 1, blocks - 1)):
            yield
        rows = slice(b * PERM_BLOCK, (b + 1) * PERM_BLOCK)
        ext = jnp.concatenate([before(b, SEG_LEN - 2), before(b, SEG_LEN - 1), xr_ref[rows, :], after(b, 0)], axis=0)
        xc = ext[0:PERM_BLOCK] * cw[0:1, :]
        for tap in range(1, CONV_W):
            xc = xc + ext[tap * SUBLANES:tap * SUBLANES + PERM_BLOCK] * cw[tap:tap + 1, :]
        xc = xc + cb_ref[...]
        yield

        xcb = xc.astype(BF16)
        halves = [_dot(xcb[:, c * MXU_DIM:(c + 1) * MXU_DIM], wg_ref[c].astype(BF16))
                  for c in range(RG_WIDTH // MXU_DIM)]
        yield
        half_xc = 0.5 * xc
        for d, (a_ref, b_ref) in enumerate(((af_ref, bf_ref), (ab_ref, bb_ref))):
            pre = [jnp.concatenate([hv[:, i * MXU_DIM:(i + 1) * MXU_DIM] for hv in halves], axis=1)
                   + bg_ref[i:i + 1, :] for i in (2 * d, 2 * d + 1)]
            r2 = jnp.tanh(pre[0]) + 1.0
            i2 = jnp.tanh(pre[1]) + 1.0
            log_a = r2 * decay[d:d + 1, :]
            a = jnp.exp(log_a)
            a_ref[rows, :] = a
            b_ref[rows, :] = jnp.sqrt(jnp.tanh(-log_a) * (a * a + 1.0)) * (i2 * half_xc)
            yield

        for a_ref, b_ref, steps in ((af_ref, bf_ref, range(SEG_LEN)), (ab_ref, bb_ref, range(SEG_LEN - 1, -1, -1))):
            hend = jnp.zeros((SUBLANES, RG_WIDTH), F32)
            pend = jnp.ones((SUBLANES, RG_WIDTH), F32)
            for t in steps:
                r = slice(b * PERM_BLOCK + t * SUBLANES, b * PERM_BLOCK + (t + 1) * SUBLANES)
                a = a_ref[r, :]
                hend = a * hend + b_ref[r, :]
                pend = a * pend
            ends.append((hend, pend))
            yield

    def carries(order, state, which):
        into = {}
        for b, j in order:
            into[b, j] = state
            hend, pend = ends[2 * b + which]
            state = hend[j:j + 1, :] + pend[j:j + 1, :] * state
        return into, state

    segs = [(b, j) for b in range(blocks) for j in range(SUBLANES)]
    into_f, hf = carries(segs, h0[0:1, :], 0)
    into_b, hb = carries(segs[::-1], h0[1:2, :], 1)
    yield

    for b in range(blocks):
        for a_ref, b_ref, into, steps in ((af_ref, bf_ref, into_f, range(SEG_LEN)),
                                          (ab_ref, bb_ref, into_b, range(SEG_LEN - 1, -1, -1))):
            h = jnp.concatenate([into[b, j] for j in range(SUBLANES)], axis=0)
            for t in steps:
                r = slice(b * PERM_BLOCK + t * SUBLANES, b * PERM_BLOCK + (t + 1) * SUBLANES)
                h = a_ref[r, :] * h + b_ref[r, :]
                b_ref[r, :] = h
            yield

    out = []
    for b in range(blocks):
        rows = slice(b * PERM_BLOCK, (b + 1) * PERM_BLOCK)
        rg = ((bf_ref[rows, :] + bb_ref[rows, :]) * jax.nn.gelu(gr_ref[rows, :])).astype(BF16)
        out.append(_dot(perm_t_ref[...], rg).astype(BF16))
        yield
    return jnp.concatenate(out, axis=0), hf, hb


def _out_and_router(attn, rg, x, m, wout_ref, g1_ref, b1_ref, wrt_ref):
    gate1 = m[:, 2 * D_MODEL:3 * D_MODEL]
    shift2 = m[:, 3 * D_MODEL:4 * D_MODEL]
    scale2 = m[:, 4 * D_MODEL:5 * D_MODEL]
    mix = _dot(jnp.concatenate([attn, rg], axis=1), wout_ref[...])
    yield
    x1 = _ln_plain(ALPHA * x + gate1 * mix) * g1_ref[...] + b1_ref[...]
    yield
    h2 = (_ln_plain(x1) * (1.0 + scale2) + shift2).astype(BF16)
    yield
    logits = _dot_nt(wrt_ref[...].astype(BF16), h2)
    e = jnp.exp(logits - jnp.max(logits, axis=0, keepdims=True))
    return x1, h2, e / jnp.sum(e, axis=0, keepdims=True)


def _mixer_kernel(x_ref, mod_ref, win_ref, wout_ref, bd_ref, qg_ref, kg_ref, cw_ref, cb_ref, wg_ref, bg_ref,
                  lam_ref, g1_ref, b1_ref, wrt_ref,
                  nk_ref, nv_ref, nh_ref, x1_ref, h2_ref, afft_ref,
                  xprev_ref, xr_ref, gr_ref, attn_ref, *scan_refs):
    @pl.when(pl.program_id(0) == 0)
    def _():
        xprev_ref[...] = jnp.zeros_like(xprev_ref)
        xr_ref[...] = jnp.zeros_like(xr_ref)
        gr_ref[...] = jnp.zeros_like(gr_ref)
        attn_ref[...] = jnp.zeros_like(attn_ref)

    m = mod_ref[0:1, :]

    def second_half():
        h0 = jnp.zeros((2, RG_WIDTH), F32)
        rg, hf, hb = yield from _rglru_rows(xr_ref[...], gr_ref[...], cw_ref, cb_ref, wg_ref, bg_ref, lam_ref, h0,
                                            scan_refs)
        nh_ref[0] = jnp.concatenate([hf, hb], axis=0)
        x1_ref[...], h2_ref[...], afft_ref[...] = yield from _out_and_router(
            attn_ref[...], rg, xprev_ref[...], m, wout_ref, g1_ref, b1_ref, wrt_ref)

    def first_half():
        x = x_ref[...]
        h = _modulated(x, m)
        yield
        q, k, v = yield from _qkv(h, win_ref, bd_ref, qg_ref, kg_ref, None)
        nk_ref[0] = k.T
        nv_ref[0] = v.T
        yield
        xr, gr = yield from _rg_inputs(h, win_ref)
        attn = yield from _attend(q, k.astype(BF16), v.astype(BF16), False)
        return x, attn, xr, gr

    (x, attn, xr, gr), _ = _drive(first_half(), second_half(), order=MIXER_ORDER)
    xprev_ref[...] = x
    attn_ref[...] = attn
    xr_ref[...] = xr
    gr_ref[...] = gr


def _mixer_prompt(x2d, mod, win_bf, wout_bf, consts, g1, b1, wrt, *, n, nb):
    tokens = nb * n
    first = lambda w: pl.BlockSpec((n, w), lambda i: (jnp.minimum(i, nb - 1), 0))
    second = lambda w: pl.BlockSpec((n, w), lambda i: (jnp.maximum(i - 1, 0), 0))
    small = list(consts) + [g1, b1, wrt]
    return pl.pallas_call(
        _mixer_kernel,
        grid=(nb + 1,),
        in_specs=[first(D_MODEL), _full(mod.shape), _full(win_bf.shape), _full(wout_bf.shape)]
                 + [_full(a.shape) for a in small],
        out_specs=[pl.BlockSpec((1, KV_WIDTH, n), lambda i: (jnp.minimum(i, nb - 1), 0, 0)),
                   pl.BlockSpec((1, KV_WIDTH, n), lambda i: (jnp.minimum(i, nb - 1), 0, 0)),
                   pl.BlockSpec((1, 2, RG_WIDTH), lambda i: (jnp.maximum(i - 1, 0), 0, 0)),
                   second(D_MODEL), second(D_MODEL),
                   pl.BlockSpec((N_EXPERTS, n), lambda i: (jnp.maximum(i - 1, 0), 0))],
        out_shape=[jax.ShapeDtypeStruct((nb, KV_WIDTH, n), F32), jax.ShapeDtypeStruct((nb, KV_WIDTH, n), F32),
                   jax.ShapeDtypeStruct((nb, 2, RG_WIDTH), F32),
                   jax.ShapeDtypeStruct((tokens, D_MODEL), F32), jax.ShapeDtypeStruct((tokens, D_MODEL), BF16),
                   jax.ShapeDtypeStruct((nb * N_EXPERTS, n), F32)],
        scratch_shapes=[pltpu.VMEM((n, D_MODEL), F32), pltpu.VMEM((n, RG_WIDTH), F32),
                        pltpu.VMEM((n, RG_WIDTH), F32), pltpu.VMEM((n, ATTN_WIDTH), BF16)]
                       + [pltpu.VMEM((n, RG_WIDTH), F32)] * 4,
        compiler_params=_params(),
        name="mixer_prompt",
    )(x2d, mod, win_bf, wout_bf, *small)


def _front_kernel(x_ref, mod_ref, win_ref, bd_ref, qg_ref, kg_ref, cw_ref, cb_ref, wg_ref, bg_ref, lam_ref,
                  perm_ref, perm_t_ref, h0_ref, cos_ref, sin_ref,
                  q_ref, k_ref, v_ref, rg_ref, xr_ref, gr_ref, *scan_refs):
    m = mod_ref[pl.ds(1 + pl.program_id(0), 1), :]
    rows = PERM_BLOCK
    streams = x_ref.shape[0] // rows

    def row_block(j):
        r = pl.ds(j * rows, rows)
        h = _modulated(x_ref[r, :], m)
        yield
        q, k, v = yield from _qkv(h, win_ref, bd_ref, qg_ref, kg_ref, (cos_ref[r, :], sin_ref[r, :]))
        q_ref[r, :] = q
        k_ref[r, :] = k.astype(BF16)
        v_ref[r, :] = v.astype(BF16)
        yield
        hp = _dot(perm_ref[...], h).astype(BF16)
        xr_ref[r, :], gr_ref[r, :] = yield from _rg_inputs(hp, win_ref)
        traced[j] = True

    traced = [False] * streams
    recurrent = _rglru_segments(xr_ref, gr_ref, perm_t_ref, cw_ref, cb_ref, wg_ref, bg_ref, lam_ref, h0_ref[0],
                                scan_refs, lambda block: traced[block])
    chunks = 8
    order = [0] * chunks + [1] * chunks + [s for j in range(2, streams) for _ in range(chunks) for s in (j, streams)]
    results = _drive(*[row_block(j) for j in range(streams)], recurrent, order=order)
    rg_ref[...], _, _ = results[-1]


def _front_sample(x2d, mod, win_bf, consts, h0, rope, *, n, nb):
    tokens = nb * n
    seq = lambda w: pl.BlockSpec((n, w), lambda b: (b, 0))
    small = list(consts)
    return pl.pallas_call(
        _front_kernel,
        grid=(nb,),
        in_specs=[seq(D_MODEL), _full(mod.shape), _full(win_bf.shape)] + [_full(a.shape) for a in small]
                 + [pl.BlockSpec((1, 2, RG_WIDTH), lambda b: (b, 0, 0)), _full(rope[0].shape), _full(rope[1].shape)],
        out_specs=[seq(ATTN_WIDTH), seq(KV_WIDTH), seq(KV_WIDTH), seq(RG_WIDTH)],
        out_shape=[jax.ShapeDtypeStruct((tokens, ATTN_WIDTH), BF16), jax.ShapeDtypeStruct((tokens, KV_WIDTH), BF16),
                   jax.ShapeDtypeStruct((tokens, KV_WIDTH), BF16), jax.ShapeDtypeStruct((tokens, RG_WIDTH), BF16)],
        scratch_shapes=[pltpu.VMEM((n, RG_WIDTH), F32)] * 6,
        compiler_params=_params(),
        name="front_sample",
    )(x2d, mod, win_bf, *small, h0, *rope)


def _attn_post_kernel(q_ref, k_ref, v_ref, kc_ref, vc_ref, rg_ref, x_ref, mod_ref, wout_ref, g1_ref, b1_ref, wrt_ref,
                      x1_ref, h2_ref, afft_ref):
    m = mod_ref[pl.ds(1 + pl.program_id(0), 1), :]
    k = jnp.concatenate([k_ref[...], kc_ref[...].astype(BF16)], axis=0)
    v = jnp.concatenate([v_ref[...], vc_ref[...].astype(BF16)], axis=0)
    attn, = _drive(_attend(q_ref[...], k, v, True))
    rows = attn.shape[0] // TAIL_ROW_STREAMS

    def row_block(j):
        r = slice(j * rows, (j + 1) * rows)
        x1_ref[r, :], h2_ref[r, :], afft_ref[:, r] = yield from _out_and_router(
            attn[r, :], rg_ref[r, :], x_ref[r, :], m, wout_ref, g1_ref, b1_ref, wrt_ref)

    _drive(*[row_block(j) for j in range(TAIL_ROW_STREAMS)], order=(0,))


def _attn_post_sample(q, k, v, cache_k, cache_v, rg, x2d, mod, wout_bf, g1, b1, wrt, *, n, nb, past):
    tq = min(n, ATTN_Q_TILE)
    tiles = n // tq
    tokens = nb * n
    tile = lambda w: pl.BlockSpec((tq, w), lambda b, t: (b * tiles + t, 0))
    whole = lambda rows: pl.BlockSpec((rows, KV_WIDTH), lambda b, t: (b, 0))
    return pl.pallas_call(
        _attn_post_kernel,
        grid=(nb, tiles),
        in_specs=[tile(ATTN_WIDTH), whole(n), whole(n), whole(past), whole(past), tile(RG_WIDTH), tile(D_MODEL),
                  _full(mod.shape), _full(wout_bf.shape), _full(g1.shape), _full(b1.shape), _full(wrt.shape)],
        out_specs=[tile(D_MODEL), tile(D_MODEL), pl.BlockSpec((N_EXPERTS, tq), lambda b, t: (b, t))],
        out_shape=[jax.ShapeDtypeStruct((tokens, D_MODEL), F32), jax.ShapeDtypeStruct((tokens, D_MODEL), BF16),
                   jax.ShapeDtypeStruct((nb * N_EXPERTS, n), F32)],
        compiler_params=_params(2),
        name="attn_post_sample",
    )(q, k, v, cache_k, cache_v, rg, x2d, mod, wout_bf, g1, b1, wrt)


def _slot_onehot(mask, pos, cap):
    n = mask.shape[1]
    slot = lax.broadcasted_iota(jnp.int32, (cap, n), 0).astype(F32)
    return [(slot == pos[e:e + 1, :]) & (mask[e:e + 1, :] > 0.5) for e in range(mask.shape[0])]


def _route_kernel(afft_ref, h2_ref, xs_ref, mask_ref, pos_ref, *, n, cap, seqs):
    b = pl.program_id(0)

    @pl.when(b == 0)
    def _():
        aff = afft_ref[...]
        thr = jnp.zeros((aff.shape[0], 1), jnp.int32)
        for bit in range(30, -1, -1):
            cand = thr | (1 << bit)
            cnt = jnp.sum((aff >= lax.bitcast_convert_type(cand, F32)).astype(F32), axis=1, keepdims=True)
            thr = jnp.where(cnt >= cap, cand, thr)
        above = aff >= lax.bitcast_convert_type(thr + 1, F32)
        tied = (aff >= lax.bitcast_convert_type(thr, F32)) & jnp.logical_not(above)
        need = cap - jnp.sum(above.astype(F32), axis=1, keepdims=True)
        before = (lax.broadcasted_iota(jnp.int32, (n, n), 0)
                  < lax.broadcasted_iota(jnp.int32, (n, n), 1)).astype(BF16)
        tie_rank = _dot(tied.astype(BF16), before)
        mask = (above | (tied & (tie_rank < need))).astype(F32)
        mask_ref[...] = mask
        pos_ref[...] = _dot(mask.astype(BF16), before)

    for j in range(seqs):
        r0 = pl.multiple_of((b * seqs + j) * N_EXPERTS, N_EXPERTS)
        onehots = _slot_onehot(mask_ref[pl.ds(r0, N_EXPERTS), :], pos_ref[pl.ds(r0, N_EXPERTS), :], cap)
        sel = jnp.concatenate(onehots, axis=0).astype(BF16)
        xs = _dot(sel, h2_ref[j * n:(j + 1) * n, :]).astype(BF16)
        xs_ref[:, j * cap:(j + 1) * cap, :] = xs.reshape(N_EXPERTS, cap, D_MODEL)


def _route(afft, h2, *, n, nb, cap, seqs):
    rows = nb * N_EXPERTS
    return pl.pallas_call(
        functools.partial(_route_kernel, n=n, cap=cap, seqs=seqs),
        grid=(nb // seqs,),
        in_specs=[_full(afft.shape), pl.BlockSpec((seqs * n, D_MODEL), lambda b: (b, 0))],
        out_specs=[pl.BlockSpec((N_EXPERTS, seqs * cap, D_MODEL), lambda b: (0, b, 0)),
                   _full((rows, n)), _full((rows, n))],
        out_shape=[jax.ShapeDtypeStruct((N_EXPERTS, nb * cap, D_MODEL), BF16),
                   jax.ShapeDtypeStruct((rows, n), F32), jax.ShapeDtypeStruct((rows, n), F32)],
        compiler_params=_params(),
        name="route_n%d" % n,
    )(afft, h2)


def _slot_gates(aff_ref, mask_ref, pos_ref, e, cap):
    cols = []
    for b in range(aff_ref.shape[0] // N_EXPERTS):
        r = b * N_EXPERTS + e
        onehot, = _slot_onehot(mask_ref[pl.ds(r, 1), :], pos_ref[pl.ds(r, 1), :], cap)
        cols.append(jnp.sum(jnp.where(onehot, aff_ref[pl.ds(r, 1), :], 0.0), axis=1, keepdims=True))
    return jnp.concatenate(cols, axis=0)


def _ffn_kernel(xp_ref, xs_ref, ap_ref, mp_ref, pp_ref, as_ref, ms_ref, ps_ref, wgu_ref, wd_ref, yp_ref, ys_ref):
    e = pl.program_id(0)
    rows_p = xp_ref.shape[1]
    xs = jnp.concatenate([xp_ref[0], xs_ref[0]], axis=0)
    gu = _dot(xs, wgu_ref[0].astype(BF16))
    gate = gu[:, :D_EXPERT]
    up = gu[:, D_EXPERT:]
    act = (gate * jax.nn.sigmoid(gate) * up).astype(BF16)
    y = _dot(act, wd_ref[0].astype(BF16))
    g = jnp.concatenate([_slot_gates(ap_ref, mp_ref, pp_ref, e, rows_p * N_EXPERTS // ap_ref.shape[0]),
                         _slot_gates(as_ref, ms_ref, ps_ref, e, xs_ref.shape[1] * N_EXPERTS // as_ref.shape[0])],
                        axis=0)
    y = (y * g).astype(BF16)
    yp_ref[0] = y[:rows_p]
    ys_ref[0] = y[rows_p:]


def _ffn(xs_p, xs_s, route_p, route_s, w_gate_up, w_down):
    per_e = lambda a: pl.BlockSpec((1,) + a.shape[1:], lambda e: (e, 0, 0))
    tables = list(route_p) + list(route_s)
    return pl.pallas_call(
        _ffn_kernel,
        grid=(N_EXPERTS,),
        in_specs=[per_e(xs_p), per_e(xs_s)] + [_full(t.shape) for t in tables] + [per_e(w_gate_up), per_e(w_down)],
        out_specs=[per_e(xs_p), per_e(xs_s)],
        out_shape=[jax.ShapeDtypeStruct(xs_p.shape, BF16), jax.ShapeDtypeStruct(xs_s.shape, BF16)],
        compiler_params=_params(),
        name="expert_ffn",
    )(xs_p, xs_s, *tables, w_gate_up, w_down)


def _combine_kernel(y_ref, mask_ref, pos_ref, x1_ref, mod_ref, g2_ref, b2_ref, o_ref, *, cap, seqs, mod_row0,
                    mod_row_step):
    row = mod_row0 + mod_row_step * pl.program_id(0)
    gate2 = mod_ref[pl.ds(row, 1), 5 * D_MODEL:6 * D_MODEL]
    tn = x1_ref.shape[0] // seqs
    for j in range(seqs):
        e0 = j * N_EXPERTS
        sel = jnp.concatenate(
            _slot_onehot(mask_ref[e0:e0 + N_EXPERTS, :], pos_ref[e0:e0 + N_EXPERTS, :], cap), axis=0).astype(BF16)
        y = y_ref[:, j * cap:(j + 1) * cap, :].reshape(N_EXPERTS * cap, D_MODEL)
        ff = _dot_tn(sel, y)
        r = pl.ds(j * tn, tn)
        o_ref[r, :] = _ln_plain(ALPHA * x1_ref[r, :] + gate2 * ff) * g2_ref[...] + b2_ref[...]


def _combine(y, mask, pos, x1, mod, g2, b2, *, n, nb, cap, seqs, sample):
    tn = min(n, COMBINE_TILE)
    tiles = n // tn
    assert seqs == 1 or tiles == 1
    return pl.pallas_call(
        functools.partial(_combine_kernel, cap=cap, seqs=seqs, mod_row0=1 if sample else 0,
                          mod_row_step=1 if sample else 0),
        grid=(nb // seqs, tiles),
        in_specs=[pl.BlockSpec((N_EXPERTS, seqs * cap, D_MODEL), lambda b, t: (0, b, 0)),
                  pl.BlockSpec((seqs * N_EXPERTS, tn), lambda b, t: (b, t)),
                  pl.BlockSpec((seqs * N_EXPERTS, tn), lambda b, t: (b, t)),
                  pl.BlockSpec((seqs * tn, D_MODEL), lambda b, t: (b * tiles + t, 0)),
                  _full(mod.shape), _full(g2.shape), _full(b2.shape)],
        out_specs=pl.BlockSpec((seqs * tn, D_MODEL), lambda b, t: (b * tiles + t, 0)),
        out_shape=jax.ShapeDtypeStruct((nb * n, D_MODEL), F32),
        compiler_params=_params(2),
        name="combine_sample" if sample else "combine_prompt",
    )(y, mask, pos, x1, mod, g2, b2)


def _block_diag_256(w):
    per_tile = MXU_DIM // RG_BLOCK
    tiles = []
    for c in range(w.shape[0] // per_tile):
        tiles.append(jax.scipy.linalg.block_diag(*[w[c * per_tile + i] for i in range(per_tile)]))
    return jnp.stack(tiles)


def _rope_tables(n):
    lane = np.arange(LANES)
    within = lane % HEAD_DIM
    freq = (within % 16).astype(np.float32)
    inv = np.float32(ROPE_THETA) ** (-freq / np.float32(16.0))
    tok = np.arange(n)
    pos = np.where((within < HEAD_DIM // 2)[None, :], (tok // GRID_W)[:, None], (tok % GRID_W)[:, None])
    ang = pos.astype(np.float32) * inv[None, :]
    sign = np.where(within % 32 < 16, -1.0, 1.0).astype(np.float32)
    return jnp.asarray(np.cos(ang)), jnp.asarray(np.sin(ang) * sign[None, :])


def kernel(x_prompt, x_sample, cache_k, cache_v, state_h, c, c_ctx, w_mod, b_mod, w_in, q_norm_g, k_norm_g,
           conv_w, conv_b, w_rg_a, b_rg_a, w_rg_x, b_rg_x, rg_lambda, w_out, ln1_g, ln1_b, w_router,
           w_gate_up, w_down, ln2_g, ln2_b):
    assert w_mod.shape[0] == DEPTH == 1
    nb_p, n_p, _ = x_prompt.shape
    nb_s, n_s, _ = x_sample.shape
    past = cache_k.shape[2]
    cap_p = CAP_FACTOR * n_p // N_EXPERTS
    cap_s = CAP_FACTOR * n_s // N_EXPERTS

    cs = jnp.concatenate([c_ctx[None, :], c, jnp.zeros((MOD_ROWS - 1 - nb_s, D_MODEL), F32)], axis=0)
    row = lambda v: v.reshape(1, -1)
    head_avg = np.kron(np.eye(ATTN_WIDTH // HEAD_DIM, dtype=np.float32),
                       np.full((HEAD_DIM, HEAD_DIM), 1.0 / HEAD_DIM, np.float32))
    bd = jnp.asarray(head_avg, BF16)
    qg = row(jnp.tile(q_norm_g[0], N_Q_HEADS))
    kg = row(jnp.tile(k_norm_g[0], N_KV_HEADS))
    wg = 0.5 * jnp.concatenate([_block_diag_256(w_rg_a[0, 0]), _block_diag_256(w_rg_x[0, 0]),
                                _block_diag_256(w_rg_a[0, 1]), _block_diag_256(w_rg_x[0, 1])], axis=2)
    bg = 0.5 * jnp.stack([b_rg_a[0, 0], b_rg_x[0, 0], b_rg_a[0, 1], b_rg_x[0, 1]])
    perm = _segment_permutation()
    consts = (bd, qg, kg, conv_w[0], row(conv_b[0]), wg, bg, rg_lambda[0])
    perms = (jnp.asarray(perm, BF16), jnp.asarray(perm.T, BF16))
    wrt = w_router[0].T
    g1, b1, g2, b2 = row(ln1_g[0]), row(ln1_b[0]), row(ln2_g[0]), row(ln2_b[0])

    win_bf, wout_bf = _cast_weights(w_in[0], w_out[0])
    mod = _mod_vectors(cs, w_mod[0], row(b_mod[0]))

    xp = x_prompt.reshape(nb_p * n_p, D_MODEL)
    xs = x_sample.reshape(nb_s * n_s, D_MODEL)

    new_k, new_v, new_h, x1_p, h2_p, aff_p = _mixer_prompt(
        xp, mod, win_bf, wout_bf, consts, g1, b1, wrt, n=n_p, nb=nb_p)

    q_s, k_s, v_s, rg_s = _front_sample(xs, mod, win_bf, consts + perms, state_h[:, 0], _rope_tables(n_s),
                                        n=n_s, nb=nb_s)
    x1_s, h2_s, aff_s = _attn_post_sample(
        q_s, k_s, v_s, cache_k[:, 0].reshape(nb_s * past, KV_WIDTH), cache_v[:, 0].reshape(nb_s * past, KV_WIDTH),
        rg_s, xs, mod, wout_bf, g1, b1, wrt, n=n_s, nb=nb_s, past=past)

    seqs_p = PROMPT_SEQS_PER_STEP if nb_p % PROMPT_SEQS_PER_STEP == 0 else 1
    xs_p, mask_p, pos_p = _route(aff_p, h2_p, n=n_p, nb=nb_p, cap=cap_p, seqs=seqs_p)
    xs_s, mask_s, pos_s = _route(aff_s, h2_s, n=n_s, nb=nb_s, cap=cap_s, seqs=1)

    y_p, y_s = _ffn(xs_p, xs_s, (aff_p, mask_p, pos_p), (aff_s, mask_s, pos_s), w_gate_up[0], w_down[0])

    out_p = _combine(y_p, mask_p, pos_p, x1_p, mod, g2, b2, n=n_p, nb=nb_p, cap=cap_p, seqs=seqs_p, sample=False)
    out_s = _combine(y_s, mask_s, pos_s, x1_s, mod, g2, b2, n=n_s, nb=nb_s, cap=cap_s, seqs=1, sample=True)

    def cache_layout(t):
        t = t.reshape(nb_p, DEPTH, N_KV_HEADS, HEAD_DIM, n_p)
        return jnp.transpose(t, (0, 1, 4, 2, 3))

    return (out_p.reshape(nb_p, n_p, D_MODEL), out_s.reshape(nb_s, n_s, D_MODEL),
            cache_layout(new_k), cache_layout(new_v), new_h.reshape(nb_p, DEPTH, 2, RG_WIDTH))
```

```python
import functools

import numpy as np
import jax
import jax.numpy as jnp
from jax import lax
from jax.experimental import pallas as pl
from jax.experimental.pallas import tpu as pltpu

F32 = jnp.float32
BF16 = jnp.bfloat16

D_MODEL = 1024
HEAD_DIM = 64
N_Q_HEADS = 8
N_KV_HEADS = 2
GROUP = N_Q_HEADS // N_KV_HEADS
ATTN_WIDTH = N_Q_HEADS * HEAD_DIM
KV_WIDTH = N_KV_HEADS * HEAD_DIM
RG_WIDTH = D_MODEL - ATTN_WIDTH
RG_BLOCK = 64
CONV_W = 4
C_LRU = 8.0
N_EXPERTS = 16
CAP_FACTOR = 2
D_EXPERT = 1024
GRID_W = 64
ROPE_THETA = 10000.0
NORM_EPS = 1e-6
DEPTH = 1
ALPHA = (2.0 * DEPTH) ** 0.25
IN_WIDTH = ATTN_WIDTH + 2 * KV_WIDTH + 2 * RG_WIDTH
Q_SCALE = HEAD_DIM ** -0.5 * float(np.log2(np.e))

LANES = 128
SUBLANES = 8
MXU_DIM = 256
VMEM_LIMIT_BYTES = 56 * 1024 * 1024

MOD_ROWS = SUBLANES
MOD_TILE = 256
ATTN_Q_TILE = 512
COMBINE_TILE = 512
MIXER_ORDER = (1, 0, 0, 1, 1, 0, 1, 1, 0, 1, 0, 1, 0, 1, 1, 0, 1, 1, 0, 0, 1, 0, 1, 0, 0, 0, 1, 0, 0, 1, 0)
PERM_BLOCK = MXU_DIM
SEG_LEN = PERM_BLOCK // SUBLANES
TAIL_ROW_STREAMS = 2
PROMPT_SEQS_PER_STEP = 4


def _params(n_axes=1):
    return pltpu.CompilerParams(dimension_semantics=("arbitrary",) * n_axes,
                                vmem_limit_bytes=VMEM_LIMIT_BYTES)


def _full(shape):
    zeros = (0,) * len(shape)
    return pl.BlockSpec(shape, lambda *_: zeros)


def _ln_plain(x):
    mu = jnp.mean(x, -1, keepdims=True)
    xc = x - mu
    var = jnp.mean(xc * xc, -1, keepdims=True)
    return xc * lax.rsqrt(var + NORM_EPS)


def _dot(a, b):
    return jnp.dot(a, b, preferred_element_type=F32)


def _dot_nt(a, b):
    return lax.dot_general(a, b, (((1,), (1,)), ((), ())), preferred_element_type=F32)


def _dot_tn(a, b):
    return lax.dot_general(a, b, (((0,), (0,)), ((), ())), preferred_element_type=F32)


def _split_bf16(x):
    hi = x.astype(BF16)
    lo = (x - hi.astype(F32)).astype(BF16)
    return hi, lo


def _cast_kernel(a_ref, b_ref, ao_ref, bo_ref):
    ao_ref[...] = a_ref[...].astype(BF16)
    bo_ref[...] = b_ref[...].astype(BF16)


def _cast_weights(w_in, w_out):
    return pl.pallas_call(
        _cast_kernel,
        grid=(1,),
        in_specs=[_full(w_in.shape), _full(w_out.shape)],
        out_specs=[_full(w_in.shape), _full(w_out.shape)],
        out_shape=[jax.ShapeDtypeStruct(w_in.shape, BF16), jax.ShapeDtypeStruct(w_out.shape, BF16)],
        compiler_params=_params(),
        name="cast_weights",
    )(w_in, w_out)


def _mod_kernel(cs_ref, w_ref, b_ref, o_ref):
    @pl.when(pl.program_id(0) == 0)
    def _():
        o_ref[...] = jnp.broadcast_to(b_ref[...], o_ref.shape)

    cs = cs_ref[...]
    s = cs * jax.nn.sigmoid(cs)
    o_ref[...] += _dot(s.astype(BF16), w_ref[...].astype(BF16))


def _mod_vectors(cs, w_mod, b_mod):
    width = w_mod.shape[1]
    return pl.pallas_call(
        _mod_kernel,
        grid=(D_MODEL // MOD_TILE,),
        in_specs=[pl.BlockSpec((MOD_ROWS, MOD_TILE), lambda j: (0, j)),
                  pl.BlockSpec((MOD_TILE, width), lambda j: (j, 0)),
                  _full(b_mod.shape)],
        out_specs=_full((MOD_ROWS, width)),
        out_shape=jax.ShapeDtypeStruct((MOD_ROWS, width), F32),
        compiler_params=_params(),
        name="mod_vectors",
    )(cs, w_mod, b_mod)


def _head_mean_sq(x, bd):
    hi, lo = _split_bf16(x * x)
    return _dot(hi, bd) + _dot(lo, bd)


def _rope_lanes(x, cos, sin_signed, hi_half):
    partner = jnp.where(hi_half, pltpu.roll(x, 16, axis=1), pltpu.roll(x, LANES - 16, axis=1))
    return x * cos + partner * sin_signed


def _modulated(x, m):
    return (_ln_plain(x) * (1.0 + m[:, D_MODEL:2 * D_MODEL]) + m[:, 0:D_MODEL]).astype(BF16)


def _qkv(h, win_ref, bd_ref, qg_ref, kg_ref, rope):
    n = h.shape[0]
    bd = bd_ref[...]
    q = _dot(h, win_ref[:, 0:ATTN_WIDTH])
    yield
    q = q * lax.rsqrt(_head_mean_sq(q, bd) + NORM_EPS) * qg_ref[...]
    yield
    k = _dot(h, win_ref[:, ATTN_WIDTH:ATTN_WIDTH + KV_WIDTH])
    k = k * lax.rsqrt(_head_mean_sq(k, bd[:KV_WIDTH, :KV_WIDTH]) + NORM_EPS) * kg_ref[...]
    v = _dot(h, win_ref[:, ATTN_WIDTH + KV_WIDTH:ATTN_WIDTH + 2 * KV_WIDTH])
    yield
    if rope is not None:
        cos, sin_signed = rope
        lane = lax.broadcasted_iota(jnp.int32, (n, LANES), 1)
        hi_half = (lane & 16) != 0
        q = jnp.concatenate(
            [_rope_lanes(q[:, j * LANES:(j + 1) * LANES], cos, sin_signed, hi_half)
             for j in range(ATTN_WIDTH // LANES)], axis=1)
        k = _rope_lanes(k, cos, sin_signed, hi_half)
    return (q * Q_SCALE).astype(BF16), k, v


def _drive(*streams, order=()):
    results = [None] * len(streams)
    live = list(range(len(streams)))
    plan = [j for j in order]
    while live:
        idx = plan.pop(0) if plan else live[0]
        if idx not in live:
            continue
        if not plan:
            live.append(live.pop(0))
        try:
            next(streams[idx])
        except StopIteration as stop:
            results[idx] = stop.value
            live.remove(idx)
    return results


def _attend(q, k, v, lookahead):
    def scores(hq):
        kv = hq // GROUP
        return _dot_nt(q[:, hq * HEAD_DIM:(hq + 1) * HEAD_DIM], k[:, kv * HEAD_DIM:(kv + 1) * HEAD_DIM])

    outs = []
    s_next = scores(0) if lookahead else None
    for hq in range(N_Q_HEADS):
        if lookahead:
            s, s_next = s_next, (scores(hq + 1) if hq + 1 < N_Q_HEADS else None)
        else:
            s = scores(hq)
        kv = hq // GROUP
        e = jnp.exp2(s - jnp.max(s, axis=-1, keepdims=True))
        denom = jnp.sum(e, axis=-1, keepdims=True)
        pv = _dot(e.astype(BF16), v[:, kv * HEAD_DIM:(kv + 1) * HEAD_DIM])
        outs.append(pv / denom)
        yield
    return jnp.concatenate(outs, axis=1).astype(BF16)


def _rg_inputs(h, win_ref):
    rg_lo = ATTN_WIDTH + 2 * KV_WIDTH
    xr = _dot(h, win_ref[:, rg_lo:rg_lo + RG_WIDTH])
    yield
    gr = _dot(h, win_ref[:, rg_lo + RG_WIDTH:IN_WIDTH])
    yield
    return xr, gr


def _rglru_rows(xr, gr, cw_ref, cb_ref, wg_ref, bg_ref, lam_ref, h0, scan_refs):
    af_ref, bf_ref, ab_ref, bb_ref = scan_refs
    n = xr.shape[0]

    t_idx = lax.broadcasted_iota(jnp.int32, (n, 1), 0)
    cw = cw_ref[...]
    xc = jnp.where(t_idx >= 2, pltpu.roll(xr, 2, axis=0), 0.0) * cw[0:1, :]
    xc = xc + jnp.where(t_idx >= 1, pltpu.roll(xr, 1, axis=0), 0.0) * cw[1:2, :]
    xc = xc + xr * cw[2:3, :]
    xc = xc + jnp.where(t_idx < n - 1, pltpu.roll(xr, n - 1, axis=0), 0.0) * cw[3:4, :]
    xc = xc + cb_ref[...]
    yield

    xcb = xc.astype(BF16)
    halves = [_dot(xcb[:, c * MXU_DIM:(c + 1) * MXU_DIM], wg_ref[c].astype(BF16))
              for c in range(RG_WIDTH // MXU_DIM)]

    def gate_pre(idx):
        return jnp.concatenate([hv[:, idx * MXU_DIM:(idx + 1) * MXU_DIM] for hv in halves], axis=1) \
            + bg_ref[idx:idx + 1, :]

    yield
    neg = -lam_ref[...]
    softplus = jnp.maximum(neg, 0.0) + jnp.log1p(jnp.exp(-jnp.abs(neg)))
    decay = (-0.5 * C_LRU) * softplus
    half_xc = 0.5 * xc

    def coeffs(d):
        r2 = jnp.tanh(gate_pre(2 * d)) + 1.0
        i2 = jnp.tanh(gate_pre(2 * d + 1)) + 1.0
        log_a = r2 * decay[d:d + 1, :]
        a = jnp.exp(log_a)
        bx = jnp.sqrt(jnp.tanh(-log_a) * (a * a + 1.0)) * (i2 * half_xc)
        return a, bx

    groups = n // SUBLANES
    rmod = lax.broadcasted_iota(jnp.int32, (1, SUBLANES, 1), 1)

    def tile_scan(d, forward):
        a, bx = coeffs(d)
        yield
        a = a.reshape(groups, SUBLANES, RG_WIDTH)
        bx = bx.reshape(groups, SUBLANES, RG_WIDTH)
        for s in (1, 2, 4):
            ok = (rmod >= s) if forward else (rmod < SUBLANES - s)
            shift = s if forward else SUBLANES - s
            a_sh = jnp.where(ok, pltpu.roll(a, shift, axis=1), 1.0)
            b_sh = jnp.where(ok, pltpu.roll(bx, shift, axis=1), 0.0)
            bx = a * b_sh + bx
            a = a * a_sh
            yield
        return a.reshape(n, RG_WIDTH), bx.reshape(n, RG_WIDTH)

    af_ref[...], bf_ref[...] = yield from tile_scan(0, True)
    ab_ref[...], bb_ref[...] = yield from tile_scan(1, False)

    def step(g, carry):
        hf, hb = carry
        rf = pl.multiple_of(g * SUBLANES, SUBLANES)
        rb = pl.multiple_of((groups - 1 - g) * SUBLANES, SUBLANES)
        new_f = af_ref[pl.ds(rf, SUBLANES), :] * hf + bf_ref[pl.ds(rf, SUBLANES), :]
        new_b = ab_ref[pl.ds(rb, SUBLANES), :] * hb + bb_ref[pl.ds(rb, SUBLANES), :]
        bf_ref[pl.ds(rf, SUBLANES), :] = new_f
        bb_ref[pl.ds(rb, SUBLANES), :] = new_b
        return (jnp.broadcast_to(new_f[SUBLANES - 1:SUBLANES, :], (SUBLANES, RG_WIDTH)),
                jnp.broadcast_to(new_b[0:1, :], (SUBLANES, RG_WIDTH)))

    hf, hb = lax.fori_loop(
        0, groups, step,
        (jnp.broadcast_to(h0[0:1, :], (SUBLANES, RG_WIDTH)), jnp.broadcast_to(h0[1:2, :], (SUBLANES, RG_WIDTH))),
        unroll=True)
    yield

    rg = ((bf_ref[...] + bb_ref[...]) * jax.nn.gelu(gr)).astype(BF16)
    return rg, hf[0:1, :], hb[0:1, :]


def _segment_permutation():
    p = np.zeros((PERM_BLOCK, PERM_BLOCK), np.float32)
    for t in range(SEG_LEN):
        for j in range(SUBLANES):
            p[t * SUBLANES + j, j * SEG_LEN + t] = 1.0
    return p


def _rglru_segments(xr_ref, gr_ref, perm_t_ref, cw_ref, cb_ref, wg_ref, bg_ref, lam_ref, h0, scan_refs, ready):
    af_ref, bf_ref, ab_ref, bb_ref = scan_refs
    n = xr_ref.shape[0]
    blocks = n // PERM_BLOCK
    sub = lax.broadcasted_iota(jnp.int32, (SUBLANES, 1), 0)
    zero_row = jnp.zeros((1, RG_WIDTH), F32)
    cw = cw_ref[...]
    neg = -lam_ref[...]
    softplus = jnp.maximum(neg, 0.0) + jnp.log1p(jnp.exp(-jnp.abs(neg)))
    decay = (-0.5 * C_LRU) * softplus

    def group(b, t):
        return xr_ref[b * PERM_BLOCK + t * SUBLANES:b * PERM_BLOCK + (t + 1) * SUBLANES, :]

    def before(b, t):
        wrap = group(b - 1, t)[SUBLANES - 1:SUBLANES, :] if b > 0 else zero_row
        return jnp.where(sub == 0, wrap, pltpu.roll(group(b, t), 1, axis=0))

    def after(b, t):
        wrap = group(b + 1, t)[0:1, :] if b + 1 < blocks else zero_row
        return jnp.where(sub == SUBLANES - 1, wrap, pltpu.roll(group(b, t), SUBLANES - 1, axis=0))

    ends = []
    for b in range(blocks):
        while not ready(min(b + 1, blocks - 1)):
            yield
        rows = slice(b * PERM_BLOCK, (b + 1) * PERM_BLOCK)
        ext = jnp.concatenate([before(b, SEG_LEN - 2), before(b, SEG_LEN - 1), xr_ref[rows, :], after(b, 0)], axis=0)
        xc = ext[0:PERM_BLOCK] * cw[0:1, :]
        for tap in range(1, CONV_W):
            xc = xc + ext[tap * SUBLANES:tap * SUBLANES + PERM_BLOCK] * cw[tap:tap + 1, :]
        xc = xc + cb_ref[...]
        yield

        xcb = xc.astype(BF16)
        halves = [_dot(xcb[:, c * MXU_DIM:(c + 1) * MXU_DIM], wg_ref[c].astype(BF16))
                  for c in range(RG_WIDTH // MXU_DIM)]
        yield
        half_xc = 0.5 * xc
        for d, (a_ref, b_ref) in enumerate(((af_ref, bf_ref), (ab_ref, bb_ref))):
            pre = [jnp.concatenate([hv[:, i * MXU_DIM:(i + 1) * MXU_DIM] for hv in halves], axis=1)
                   + bg_ref[i:i + 1, :] for i in (2 * d, 2 * d + 1)]
            r2 = jnp.tanh(pre[0]) + 1.0
            i2 = jnp.tanh(pre[1]) + 1.0
            log_a = r2 * decay[d:d + 1, :]
            a = jnp.exp(log_a)
            a_ref[rows, :] = a
            b_ref[rows, :] = jnp.sqrt(jnp.tanh(-log_a) * (a * a + 1.0)) * (i2 * half_xc)
            yield

        for a_ref, b_ref, steps in ((af_ref, bf_ref, range(SEG_LEN)), (ab_ref, bb_ref, range(SEG_LEN - 1, -1, -1))):
            hend = jnp.zeros((SUBLANES, RG_WIDTH), F32)
            pend = jnp.ones((SUBLANES, RG_WIDTH), F32)
            for t in steps:
                r = slice(b * PERM_BLOCK + t * SUBLANES, b * PERM_BLOCK + (t + 1) * SUBLANES)
                a = a_ref[r, :]
                hend = a * hend + b_ref[r, :]
                pend = a * pend
            ends.append((hend, pend))
            yield

    def carries(order, state, which):
        into = {}
        for b, j in order:
            into[b, j] = state
            hend, pend = ends[2 * b + which]
            state = hend[j:j + 1, :] + pend[j:j + 1, :] * state
        return into, state

    segs = [(b, j) for b in range(blocks) for j in range(SUBLANES)]
    into_f, hf = carries(segs, h0[0:1, :], 0)
    into_b, hb = carries(segs[::-1], h0[1:2, :], 1)
    yield

    for b in range(blocks):
        for a_ref, b_ref, into, steps in ((af_ref, bf_ref, into_f, range(SEG_LEN)),
                                          (ab_ref, bb_ref, into_b, range(SEG_LEN - 1, -1, -1))):
            h = jnp.concatenate([into[b, j] for j in range(SUBLANES)], axis=0)
            for t in steps:
                r = slice(b * PERM_BLOCK + t * SUBLANES, b * PERM_BLOCK + (t + 1) * SUBLANES)
                h = a_ref[r, :] * h + b_ref[r, :]
                b_ref[r, :] = h
            yield

    out = []
    for b in range(blocks):
        rows = slice(b * PERM_BLOCK, (b + 1) * PERM_BLOCK)
        rg = ((bf_ref[rows, :] + bb_ref[rows, :]) * jax.nn.gelu(gr_ref[rows, :])).astype(BF16)
        out.append(_dot(perm_t_ref[...], rg).astype(BF16))
        yield
    return jnp.concatenate(out, axis=0), hf, hb


def _out_and_router(attn, rg, x, m, wout_ref, g1_ref, b1_ref, wrt_ref):
    gate1 = m[:, 2 * D_MODEL:3 * D_MODEL]
    shift2 = m[:, 3 * D_MODEL:4 * D_MODEL]
    scale2 = m[:, 4 * D_MODEL:5 * D_MODEL]
    mix = _dot(jnp.concatenate([attn, rg], axis=1), wout_ref[...])
    yield
    x1 = _ln_plain(ALPHA * x + gate1 * mix) * g1_ref[...] + b1_ref[...]
    yield
    h2 = (_ln_plain(x1) * (1.0 + scale2) + shift2).astype(BF16)
    yield
    logits = _dot_nt(wrt_ref[...].astype(BF16), h2)
    e = jnp.exp(logits - jnp.max(logits, axis=0, keepdims=True))
    return x1, h2, e / jnp.sum(e, axis=0, keepdims=True)


def _mixer_kernel(x_ref, mod_ref, win_ref, wout_ref, bd_ref, qg_ref, kg_ref, cw_ref, cb_ref, wg_ref, bg_ref,
                  lam_ref, g1_ref, b1_ref, wrt_ref,
                  nk_ref, nv_ref, nh_ref, x1_ref, h2_ref, afft_ref,
                  xprev_ref, xr_ref, gr_ref, attn_ref, *scan_refs):
    @pl.when(pl.program_id(0) == 0)
    def _():
        xprev_ref[...] = jnp.zeros_like(xprev_ref)
        xr_ref[...] = jnp.zeros_like(xr_ref)
        gr_ref[...] = jnp.zeros_like(gr_ref)
        attn_ref[...] = jnp.zeros_like(attn_ref)

    m = mod_ref[0:1, :]

    def second_half():
        h0 = jnp.zeros((2, RG_WIDTH), F32)
        rg, hf, hb = yield from _rglru_rows(xr_ref[...], gr_ref[...], cw_ref, cb_ref, wg_ref, bg_ref, lam_ref, h0,
                                            scan_refs)
        nh_ref[0] = jnp.concatenate([hf, hb], axis=0)
        x1_ref[...], h2_ref[...], afft_ref[...] = yield from _out_and_router(
            attn_ref[...], rg, xprev_ref[...], m, wout_ref, g1_ref, b1_ref, wrt_ref)

    def first_half():
        x = x_ref[...]
        h = _modulated(x, m)
        yield
        q, k, v = yield from _qkv(h, win_ref, bd_ref, qg_ref, kg_ref, None)
        nk_ref[0] = k.T
        nv_ref[0] = v.T
        yield
        xr, gr = yield from _rg_inputs(h, win_ref)
        attn = yield from _attend(q, k.astype(BF16), v.astype(BF16), False)
        return x, attn, xr, gr

    (x, attn, xr, gr), _ = _drive(first_half(), second_half(), order=MIXER_ORDER)
    xprev_ref[...] = x
    attn_ref[...] = attn
    xr_ref[...] = xr
    gr_ref[...] = gr


def _mixer_prompt(x2d, mod, win_bf, wout_bf, consts, g1, b1, wrt, *, n, nb):
    tokens = nb * n
    first = lambda w: pl.BlockSpec((n, w), lambda i: (jnp.minimum(i, nb - 1), 0))
    second = lambda w: pl.BlockSpec((n, w), lambda i: (jnp.maximum(i - 1, 0), 0))
    small = list(consts) + [g1, b1, wrt]
    return pl.pallas_call(
        _mixer_kernel,
        grid=(nb + 1,),
        in_specs=[first(D_MODEL), _full(mod.shape), _full(win_bf.shape), _full(wout_bf.shape)]
                 + [_full(a.shape) for a in small],
        out_specs=[pl.BlockSpec((1, KV_WIDTH, n), lambda i: (jnp.minimum(i, nb - 1), 0, 0)),
                   pl.BlockSpec((1, KV_WIDTH, n), lambda i: (jnp.minimum(i, nb - 1), 0, 0)),
                   pl.BlockSpec((1, 2, RG_WIDTH), lambda i: (jnp.maximum(i - 1, 0), 0, 0)),
                   second(D_MODEL), second(D_MODEL),
                   pl.BlockSpec((N_EXPERTS, n), lambda i: (jnp.maximum(i - 1, 0), 0))],
        out_shape=[jax.ShapeDtypeStruct((nb, KV_WIDTH, n), F32), jax.ShapeDtypeStruct((nb, KV_WIDTH, n), F32),
                   jax.ShapeDtypeStruct((nb, 2, RG_WIDTH), F32),
                   jax.ShapeDtypeStruct((tokens, D_MODEL), F32), jax.ShapeDtypeStruct((tokens, D_MODEL), BF16),
                   jax.ShapeDtypeStruct((nb * N_EXPERTS, n), F32)],
        scratch_shapes=[pltpu.VMEM((n, D_MODEL), F32), pltpu.VMEM((n, RG_WIDTH), F32),
                        pltpu.VMEM((n, RG_WIDTH), F32), pltpu.VMEM((n, ATTN_WIDTH), BF16)]
                       + [pltpu.VMEM((n, RG_WIDTH), F32)] * 4,
        compiler_params=_params(),
        name="mixer_prompt",
    )(x2d, mod, win_bf, wout_bf, *small)


def _front_kernel(x_ref, mod_ref, win_ref, bd_ref, qg_ref, kg_ref, cw_ref, cb_ref, wg_ref, bg_ref, lam_ref,
                  perm_ref, perm_t_ref, h0_ref, cos_ref, sin_ref,
                  q_ref, k_ref, v_ref, rg_ref, xr_ref, gr_ref, *scan_refs):
    m = mod_ref[pl.ds(1 + pl.program_id(0), 1), :]
    rows = PERM_BLOCK
    streams = x_ref.shape[0] // rows

    def row_block(j):
        r = pl.ds(j * rows, rows)
        h = _modulated(x_ref[r, :], m)
        yield
        q, k, v = yield from _qkv(h, win_ref, bd_ref, qg_ref, kg_ref, (cos_ref[r, :], sin_ref[r, :]))
        q_ref[r, :] = q
        k_ref[r, :] = k.astype(BF16)
        v_ref[r, :] = v.astype(BF16)
        yield
        hp = _dot(perm_ref[...], h).astype(BF16)
        xr_ref[r, :], gr_ref[r, :] = yield from _rg_inputs(hp, win_ref)
        traced[j] = True

    traced = [False] * streams
    recurrent = _rglru_segments(xr_ref, gr_ref, perm_t_ref, cw_ref, cb_ref, wg_ref, bg_ref, lam_ref, h0_ref[0],
                                scan_refs, lambda block: traced[block])
    chunks = 8
    order = [0] * chunks + [1] * chunks + [s for j in range(2, streams) for _ in range(chunks) for s in (j, streams)]
    results = _drive(*[row_block(j) for j in range(streams)], recurrent, order=order)
    rg_ref[...], _, _ = results[-1]


def _front_sample(x2d, mod, win_bf, consts, h0, rope, *, n, nb):
    tokens = nb * n
    seq = lambda w: pl.BlockSpec((n, w), lambda b: (b, 0))
    small = list(consts)
    return pl.pallas_call(
        _front_kernel,
        grid=(nb,),
        in_specs=[seq(D_MODEL), _full(mod.shape), _full(win_bf.shape)] + [_full(a.shape) for a in small]
                 + [pl.BlockSpec((1, 2, RG_WIDTH), lambda b: (b, 0, 0)), _full(rope[0].shape), _full(rope[1].shape)],
        out_specs=[seq(ATTN_WIDTH), seq(KV_WIDTH), seq(KV_WIDTH), seq(RG_WIDTH)],
        out_shape=[jax.ShapeDtypeStruct((tokens, ATTN_WIDTH), BF16), jax.ShapeDtypeStruct((tokens, KV_WIDTH), BF16),
                   jax.ShapeDtypeStruct((tokens, KV_WIDTH), BF16), jax.ShapeDtypeStruct((tokens, RG_WIDTH), BF16)],
        scratch_shapes=[pltpu.VMEM((n, RG_WIDTH), F32)] * 6,
        compiler_params=_params(),
        name="front_sample",
    )(x2d, mod, win_bf, *small, h0, *rope)


def _attn_post_kernel(q_ref, k_ref, v_ref, kc_ref, vc_ref, rg_ref, x_ref, mod_ref, wout_ref, g1_ref, b1_ref, wrt_ref,
                      x1_ref, h2_ref, afft_ref):
    m = mod_ref[pl.ds(1 + pl.program_id(0), 1), :]
    k = jnp.concatenate([k_ref[...], kc_ref[...].astype(BF16)], axis=0)
    v = jnp.concatenate([v_ref[...], vc_ref[...].astype(BF16)], axis=0)
    attn, = _drive(_attend(q_ref[...], k, v, True))
    rows = attn.shape[0] // TAIL_ROW_STREAMS

    def row_block(j):
        r = slice(j * rows, (j + 1) * rows)
        x1_ref[r, :], h2_ref[r, :], afft_ref[:, r] = yield from _out_and_router(
            attn[r, :], rg_ref[r, :], x_ref[r, :], m, wout_ref, g1_ref, b1_ref, wrt_ref)

    _drive(*[row_block(j) for j in range(TAIL_ROW_STREAMS)], order=(0,))


def _attn_post_sample(q, k, v, cache_k, cache_v, rg, x2d, mod, wout_bf, g1, b1, wrt, *, n, nb, past):
    tq = min(n, ATTN_Q_TILE)
    tiles = n // tq
    tokens = nb * n
    tile = lambda w: pl.BlockSpec((tq, w), lambda b, t: (b * tiles + t, 0))
    whole = lambda rows: pl.BlockSpec((rows, KV_WIDTH), lambda b, t: (b, 0))
    return pl.pallas_call(
        _attn_post_kernel,
        grid=(nb, tiles),
        in_specs=[tile(ATTN_WIDTH), whole(n), whole(n), whole(past), whole(past), tile(RG_WIDTH), tile(D_MODEL),
                  _full(mod.shape), _full(wout_bf.shape), _full(g1.shape), _full(b1.shape), _full(wrt.shape)],
        out_specs=[tile(D_MODEL), tile(D_MODEL), pl.BlockSpec((N_EXPERTS, tq), lambda b, t: (b, t))],
        out_shape=[jax.ShapeDtypeStruct((tokens, D_MODEL), F32), jax.ShapeDtypeStruct((tokens, D_MODEL), BF16),
                   jax.ShapeDtypeStruct((nb * N_EXPERTS, n), F32)],
        compiler_params=_params(2),
        name="attn_post_sample",
    )(q, k, v, cache_k, cache_v, rg, x2d, mod, wout_bf, g1, b1, wrt)


def _slot_onehot(mask, pos, cap):
    n = mask.shape[1]
    slot = lax.broadcasted_iota(jnp.int32, (cap, n), 0).astype(F32)
    return [(slot == pos[e:e + 1, :]) & (mask[e:e + 1, :] > 0.5) for e in range(mask.shape[0])]


def _route_kernel(afft_ref, h2_ref, xs_ref, mask_ref, pos_ref, *, n, cap, seqs):
    b = pl.program_id(0)

    @pl.when(b == 0)
    def _():
        aff = afft_ref[...]
        thr = jnp.zeros((aff.shape[0], 1), jnp.int32)
        for bit in range(30, -1, -1):
            cand = thr | (1 << bit)
            cnt = jnp.sum((aff >= lax.bitcast_convert_type(cand, F32)).astype(F32), axis=1, keepdims=True)
            thr = jnp.where(cnt >= cap, cand, thr)
        above = aff >= lax.bitcast_convert_type(thr + 1, F32)
        tied = (aff >= lax.bitcast_convert_type(thr, F32)) & jnp.logical_not(above)
        need = cap - jnp.sum(above.astype(F32), axis=1, keepdims=True)
        before = (lax.broadcasted_iota(jnp.int32, (n, n), 0)
                  < lax.broadcasted_iota(jnp.int32, (n, n), 1)).astype(BF16)
        tie_rank = _dot(tied.astype(BF16), before)
        mask = (above | (tied & (tie_rank < need))).astype(F32)
        mask_ref[...] = mask
        pos_ref[...] = _dot(mask.astype(BF16), before)

    for j in range(seqs):
        r0 = pl.multiple_of((b * seqs + j) * N_EXPERTS, N_EXPERTS)
        onehots = _slot_onehot(mask_ref[pl.ds(r0, N_EXPERTS), :], pos_ref[pl.ds(r0, N_EXPERTS), :], cap)
        sel = jnp.concatenate(onehots, axis=0).astype(BF16)
        xs = _dot(sel, h2_ref[j * n:(j + 1) * n, :]).astype(BF16)
        xs_ref[:, j * cap:(j + 1) * cap, :] = xs.reshape(N_EXPERTS, cap, D_MODEL)


def _route(afft, h2, *, n, nb, cap, seqs):
    rows = nb * N_EXPERTS
    return pl.pallas_call(
        functools.partial(_route_kernel, n=n, cap=cap, seqs=seqs),
        grid=(nb // seqs,),
        in_specs=[_full(afft.shape), pl.BlockSpec((seqs * n, D_MODEL), lambda b: (b, 0))],
        out_specs=[pl.BlockSpec((N_EXPERTS, seqs * cap, D_MODEL), lambda b: (0, b, 0)),
                   _full((rows, n)), _full((rows, n))],
        out_shape=[jax.ShapeDtypeStruct((N_EXPERTS, nb * cap, D_MODEL), BF16),
                   jax.ShapeDtypeStruct((rows, n), F32), jax.ShapeDtypeStruct((rows, n), F32)],
        compiler_params=_params(),
        name="route_n%d" % n,
    )(afft, h2)


def _slot_gates(aff_ref, mask_ref, pos_ref, e, cap):
    cols = []
    for b in range(aff_ref.shape[0] // N_EXPERTS):
        r = b * N_EXPERTS + e
        onehot, = _slot_onehot(mask_ref[pl.ds(r, 1), :], pos_ref[pl.ds(r, 1), :], cap)
        cols.append(jnp.sum(jnp.where(onehot, aff_ref[pl.ds(r, 1), :], 0.0), axis=1, keepdims=True))
    return jnp.concatenate(cols, axis=0)


def _ffn_kernel(xp_ref, xs_ref, ap_ref, mp_ref, pp_ref, as_ref, ms_ref, ps_ref, wgu_ref, wd_ref, yp_ref, ys_ref):
    e = pl.program_id(0)
    rows_p = xp_ref.shape[1]
    xs = jnp.concatenate([xp_ref[0], xs_ref[0]], axis=0)
    gu = _dot(xs, wgu_ref[0].astype(BF16))
    gate = gu[:, :D_EXPERT]
    up = gu[:, D_EXPERT:]
    act = (gate * jax.nn.sigmoid(gate) * up).astype(BF16)
    y = _dot(act, wd_ref[0].astype(BF16))
    g = jnp.concatenate([_slot_gates(ap_ref, mp_ref, pp_ref, e, rows_p * N_EXPERTS // ap_ref.shape[0]),
                         _slot_gates(as_ref, ms_ref, ps_ref, e, xs_ref.shape[1] * N_EXPERTS // as_ref.shape[0])],
                        axis=0)
    y = (y * g).astype(BF16)
    yp_ref[0] = y[:rows_p]
    ys_ref[0] = y[rows_p:]


def _ffn(xs_p, xs_s, route_p, route_s, w_gate_up, w_down):
    per_e = lambda a: pl.BlockSpec((1,) + a.shape[1:], lambda e: (e, 0, 0))
    tables = list(route_p) + list(route_s)
    return pl.pallas_call(
        _ffn_kernel,
        grid=(N_EXPERTS,),
        in_specs=[per_e(xs_p), per_e(xs_s)] + [_full(t.shape) for t in tables] + [per_e(w_gate_up), per_e(w_down)],
        out_specs=[per_e(xs_p), per_e(xs_s)],
        out_shape=[jax.ShapeDtypeStruct(xs_p.shape, BF16), jax.ShapeDtypeStruct(xs_s.shape, BF16)],
        compiler_params=_params(),
        name="expert_ffn",
    )(xs_p, xs_s, *tables, w_gate_up, w_down)


def _combine_kernel(y_ref, mask_ref, pos_ref, x1_ref, mod_ref, g2_ref, b2_ref, o_ref, *, cap, seqs, mod_row0,
                    mod_row_step):
    row = mod_row0 + mod_row_step * pl.program_id(0)
    gate2 = mod_ref[pl.ds(row, 1), 5 * D_MODEL:6 * D_MODEL]
    tn = x1_ref.shape[0] // seqs
    for j in range(seqs):
        e0 = j * N_EXPERTS
        sel = jnp.concatenate(
            _slot_onehot(mask_ref[e0:e0 + N_EXPERTS, :], pos_ref[e0:e0 + N_EXPERTS, :], cap), axis=0).astype(BF16)
        y = y_ref[:, j * cap:(j + 1) * cap, :].reshape(N_EXPERTS * cap, D_MODEL)
        ff = _dot_tn(sel, y)
        r = pl.ds(j * tn, tn)
        o_ref[r, :] = _ln_plain(ALPHA * x1_ref[r, :] + gate2 * ff) * g2_ref[...] + b2_ref[...]


def _combine(y, mask, pos, x1, mod, g2, b2, *, n, nb, cap, seqs, sample):
    tn = min(n, COMBINE_TILE)
    tiles = n // tn
    assert seqs == 1 or tiles == 1
    return pl.pallas_call(
        functools.partial(_combine_kernel, cap=cap, seqs=seqs, mod_row0=1 if sample else 0,
                          mod_row_step=1 if sample else 0),
        grid=(nb // seqs, tiles),
        in_specs=[pl.BlockSpec((N_EXPERTS, seqs * cap, D_MODEL), lambda b, t: (0, b, 0)),
                  pl.BlockSpec((seqs * N_EXPERTS, tn), lambda b, t: (b, t)),
                  pl.BlockSpec((seqs * N_EXPERTS, tn), lambda b, t: (b, t)),
                  pl.BlockSpec((seqs * tn, D_MODEL), lambda b, t: (b * tiles + t, 0)),
                  _full(mod.shape), _full(g2.shape), _full(b2.shape)],
        out_specs=pl.BlockSpec((seqs * tn, D_MODEL), lambda b, t: (b * tiles + t, 0)),
        out_shape=jax.ShapeDtypeStruct((nb * n, D_MODEL), F32),
        compiler_params=_params(2),
        name="combine_sample" if sample else "combine_prompt",
    )(y, mask, pos, x1, mod, g2, b2)


def _block_diag_256(w):
    per_tile = MXU_DIM // RG_BLOCK
    tiles = []
    for c in range(w.shape[0] // per_tile):
        tiles.append(jax.scipy.linalg.block_diag(*[w[c * per_tile + i] for i in range(per_tile)]))
    return jnp.stack(tiles)


def _rope_tables(n):
    lane = np.arange(LANES)
    within = lane % HEAD_DIM
    freq = (within % 16).astype(np.float32)
    inv = np.float32(ROPE_THETA) ** (-freq / np.float32(16.0))
    tok = np.arange(n)
    pos = np.where((within < HEAD_DIM // 2)[None, :], (tok // GRID_W)[:, None], (tok % GRID_W)[:, None])
    ang = pos.astype(np.float32) * inv[None, :]
    sign = np.where(within % 32 < 16, -1.0, 1.0).astype(np.float32)
    return jnp.asarray(np.cos(ang)), jnp.asarray(np.sin(ang) * sign[None, :])


def kernel(x_prompt, x_sample, cache_k, cache_v, state_h, c, c_ctx, w_mod, b_mod, w_in, q_norm_g, k_norm_g,
           conv_w, conv_b, w_rg_a, b_rg_a, w_rg_x, b_rg_x, rg_lambda, w_out, ln1_g, ln1_b, w_router,
           w_gate_up, w_down, ln2_g, ln2_b):
    assert w_mod.shape[0] == DEPTH == 1
    nb_p, n_p, _ = x_prompt.shape
    nb_s, n_s, _ = x_sample.shape
    past = cache_k.shape[2]
    cap_p = CAP_FACTOR * n_p // N_EXPERTS
    cap_s = CAP_FACTOR * n_s // N_EXPERTS

    cs = jnp.concatenate([c_ctx[None, :], c, jnp.zeros((MOD_ROWS - 1 - nb_s, D_MODEL), F32)], axis=0)
    row = lambda v: v.reshape(1, -1)
    head_avg = np.kron(np.eye(ATTN_WIDTH // HEAD_DIM, dtype=np.float32),
                       np.full((HEAD_DIM, HEAD_DIM), 1.0 / HEAD_DIM, np.float32))
    bd = jnp.asarray(head_avg, BF16)
    qg = row(jnp.tile(q_norm_g[0], N_Q_HEADS))
    kg = row(jnp.tile(k_norm_g[0], N_KV_HEADS))
    wg = 0.5 * jnp.concatenate([_block_diag_256(w_rg_a[0, 0]), _block_diag_256(w_rg_x[0, 0]),
                                _block_diag_256(w_rg_a[0, 1]), _block_diag_256(w_rg_x[0, 1])], axis=2)
    bg = 0.5 * jnp.stack([b_rg_a[0, 0], b_rg_x[0, 0], b_rg_a[0, 1], b_rg_x[0, 1]])
    perm = _segment_permutation()
    consts = (bd, qg, kg, conv_w[0], row(conv_b[0]), wg, bg, rg_lambda[0])
    perms = (jnp.asarray(perm, BF16), jnp.asarray(perm.T, BF16))
    wrt = w_router[0].T
    g1, b1, g2, b2 = row(ln1_g[0]), row(ln1_b[0]), row(ln2_g[0]), row(ln2_b[0])

    win_bf, wout_bf = _cast_weights(w_in[0], w_out[0])
    mod = _mod_vectors(cs, w_mod[0], row(b_mod[0]))

    xp = x_prompt.reshape(nb_p * n_p, D_MODEL)
    xs = x_sample.reshape(nb_s * n_s, D_MODEL)

    new_k, new_v, new_h, x1_p, h2_p, aff_p = _mixer_prompt(
        xp, mod, win_bf, wout_bf, consts, g1, b1, wrt, n=n_p, nb=nb_p)

    q_s, k_s, v_s, rg_s = _front_sample(xs, mod, win_bf, consts + perms, state_h[:, 0], _rope_tables(n_s),
                                        n=n_s, nb=nb_s)
    x1_s, h2_s, aff_s = _attn_post_sample(
        q_s, k_s, v_s, cache_k[:, 0].reshape(nb_s * past, KV_WIDTH), cache_v[:, 0].reshape(nb_s * past, KV_WIDTH),
        rg_s, xs, mod, wout_bf, g1, b1, wrt, n=n_s, nb=nb_s, past=past)

    seqs_p = PROMPT_SEQS_PER_STEP if nb_p % PROMPT_SEQS_PER_STEP == 0 else 1
    xs_p, mask_p, pos_p = _route(aff_p, h2_p, n=n_p, nb=nb_p, cap=cap_p, seqs=seqs_p)
    xs_s, mask_s, pos_s = _route(aff_s, h2_s, n=n_s, nb=nb_s, cap=cap_s, seqs=1)

    y_p, y_s = _ffn(xs_p, xs_s, (aff_p, mask_p, pos_p), (aff_s, mask_s, pos_s), w_gate_up[0], w_down[0])

    out_p = _combine(y_p, mask_p, pos_p, x1_p, mod, g2, b2, n=n_p, nb=nb_p, cap=cap_p, seqs=seqs_p, sample=False)
    out_s = _combine(y_s, mask_s, pos_s, x1_s, mod, g2, b2, n=n_s, nb=nb_s, cap=cap_s, seqs=1, sample=True)

    def cache_layout(t):
        t = t.reshape(nb_p, DEPTH, N_KV_HEADS, HEAD_DIM, n_p)
        return jnp.transpose(t, (0, 1, 4, 2, 3))

    return (out_p.reshape(nb_p, n_p, D_MODEL), out_s.reshape(nb_s, n_s, D_MODEL),
            cache_layout(new_k), cache_layout(new_v), new_h.reshape(nb_p, DEPTH, 2, RG_WIDTH))
```

```python
import functools

import numpy as np
import jax
import jax.numpy as jnp
from jax import lax
from jax.experimental import pallas as pl
from jax.experimental.pallas import tpu as pltpu

F32 = jnp.float32
BF16 = jnp.bfloat16

D_MODEL = 1024
HEAD_DIM = 64
N_Q_HEADS = 8
N_KV_HEADS = 2
GROUP = N_Q_HEADS // N_KV_HEADS
ATTN_WIDTH = N_Q_HEADS * HEAD_DIM
KV_WIDTH = N_KV_HEADS * HEAD_DIM
RG_WIDTH = D_MODEL - ATTN_WIDTH
RG_BLOCK = 64
CONV_W = 4
C_LRU = 8.0
N_EXPERTS = 16
CAP_FACTOR = 2
D_EXPERT = 1024
GRID_W = 64
ROPE_THETA = 10000.0
NORM_EPS = 1e-6
DEPTH = 1
ALPHA = (2.0 * DEPTH) ** 0.25
IN_WIDTH = ATTN_WIDTH + 2 * KV_WIDTH + 2 * RG_WIDTH
Q_SCALE = HEAD_DIM ** -0.5 * float(np.log2(np.e))

LANES = 128
SUBLANES = 8
MXU_DIM = 256
VMEM_LIMIT_BYTES = 56 * 1024 * 1024

MOD_ROWS = SUBLANES
MOD_TILE = 256
ATTN_Q_TILE = 512
COMBINE_TILE = 512
MIXER_ORDER = (1, 0, 0, 1, 1, 0, 1, 1, 0, 1, 0, 1, 0, 1, 1, 0, 1, 1, 0, 0, 1, 0, 1, 0, 0, 0, 1, 0, 0, 1, 0)
PERM_BLOCK = MXU_DIM
SEG_LEN = PERM_BLOCK // SUBLANES
TAIL_ROW_STREAMS = 2
PROMPT_SEQS_PER_STEP = 4


def _params(n_axes=1):
    return pltpu.CompilerParams(dimension_semantics=("arbitrary",) * n_axes,
                                vmem_limit_bytes=VMEM_LIMIT_BYTES)


def _full(shape):
    zeros = (0,) * len(shape)
    return pl.BlockSpec(shape, lambda *_: zeros)


def _ln_plain(x):
    mu = jnp.mean(x, -1, keepdims=True)
    xc = x - mu
    var = jnp.mean(xc * xc, -1, keepdims=True)
    return xc * lax.rsqrt(var + NORM_EPS)


def _dot(a, b):
    return jnp.dot(a, b, preferred_element_type=F32)


def _dot_nt(a, b):
    return lax.dot_general(a, b, (((1,), (1,)), ((), ())), preferred_element_type=F32)


def _dot_tn(a, b):
    return lax.dot_general(a, b, (((0,), (0,)), ((), ())), preferred_element_type=F32)


def _split_bf16(x):
    hi = x.astype(BF16)
    lo = (x - hi.astype(F32)).astype(BF16)
    return hi, lo


def _cast_kernel(a_ref, b_ref, ao_ref, bo_ref):
    ao_ref[...] = a_ref[...].astype(BF16)
    bo_ref[...] = b_ref[...].astype(BF16)


def _cast_weights(w_in, w_out):
    return pl.pallas_call(
        _cast_kernel,
        grid=(1,),
        in_specs=[_full(w_in.shape), _full(w_out.shape)],
        out_specs=[_full(w_in.shape), _full(w_out.shape)],
        out_shape=[jax.ShapeDtypeStruct(w_in.shape, BF16), jax.ShapeDtypeStruct(w_out.shape, BF16)],
        compiler_params=_params(),
        name="cast_weights",
    )(w_in, w_out)


def _mod_kernel(ctx_ref, c_ref, w_ref, b_ref, o_ref):
    @pl.when(pl.program_id(0) == 0)
    def _():
        o_ref[...] = jnp.broadcast_to(b_ref[...], o_ref.shape)

    w = w_ref[...].astype(BF16)
    n_c = c_ref.shape[0]
    for rows, ref in ((slice(0, 1), ctx_ref), (slice(1, 1 + n_c), c_ref)):
        cs = ref[...]
        o_ref[rows, :] += _dot((cs * jax.nn.sigmoid(cs)).astype(BF16), w)


def _mod_vectors(c_ctx, c, w_mod, b_mod):
    width = w_mod.shape[1]
    assert 1 + c.shape[0] <= MOD_ROWS
    return pl.pallas_call(
        _mod_kernel,
        grid=(D_MODEL // MOD_TILE,),
        in_specs=[pl.BlockSpec((1, MOD_TILE), lambda j: (0, j)),
                  pl.BlockSpec((c.shape[0], MOD_TILE), lambda j: (0, j)),
                  pl.BlockSpec((MOD_TILE, width), lambda j: (j, 0)),
                  _full(b_mod.shape)],
        out_specs=_full((MOD_ROWS, width)),
        out_shape=jax.ShapeDtypeStruct((MOD_ROWS, width), F32),
        compiler_params=_params(),
        name="mod_vectors",
    )(c_ctx, c, w_mod, b_mod)


def _head_mean_sq(x, bd):
    hi, lo = _split_bf16(x * x)
    return _dot(hi, bd) + _dot(lo, bd)


def _rope_lanes(x, cos, sin_signed, hi_half):
    partner = jnp.where(hi_half, pltpu.roll(x, 16, axis=1), pltpu.roll(x, LANES - 16, axis=1))
    return x * cos + partner * sin_signed


def _modulated(x, m):
    return (_ln_plain(x) * (1.0 + m[:, D_MODEL:2 * D_MODEL]) + m[:, 0:D_MODEL]).astype(BF16)


def _qkv(h, win_ref, bd_ref, qg_ref, kg_ref, rope):
    n = h.shape[0]
    bd = bd_ref[...]
    q = _dot(h, win_ref[:, 0:ATTN_WIDTH])
    yield
    q = q * lax.rsqrt(_head_mean_sq(q, bd) + NORM_EPS) * qg_ref[...]
    yield
    k = _dot(h, win_ref[:, ATTN_WIDTH:ATTN_WIDTH + KV_WIDTH])
    k = k * lax.rsqrt(_head_mean_sq(k, bd[:KV_WIDTH, :KV_WIDTH]) + NORM_EPS) * kg_ref[...]
    v = _dot(h, win_ref[:, ATTN_WIDTH + KV_WIDTH:ATTN_WIDTH + 2 * KV_WIDTH])
    yield
    if rope is not None:
        cos, sin_signed = rope
        lane = lax.broadcasted_iota(jnp.int32, (n, LANES), 1)
        hi_half = (lane & 16) != 0
        q = jnp.concatenate(
            [_rope_lanes(q[:, j * LANES:(j + 1) * LANES], cos, sin_signed, hi_half)
             for j in range(ATTN_WIDTH // LANES)], axis=1)
        k = _rope_lanes(k, cos, sin_signed, hi_half)
    return (q * Q_SCALE).astype(BF16), k, v


def _drive(*streams, order=()):
    results = [None] * len(streams)
    live = list(range(len(streams)))
    plan = [j for j in order]
    while live:
        idx = plan.pop(0) if plan else live[0]
        if idx not in live:
            continue
        if not plan:
            live.append(live.pop(0))
        try:
            next(streams[idx])
        except StopIteration as stop:
            results[idx] = stop.value
            live.remove(idx)
    return results


def _attend(q, k, v, lookahead):
    def scores(hq):
        kv = hq // GROUP
        return _dot_nt(q[:, hq * HEAD_DIM:(hq + 1) * HEAD_DIM], k[:, kv * HEAD_DIM:(kv + 1) * HEAD_DIM])

    outs = []
    s_next = scores(0) if lookahead else None
    for hq in range(N_Q_HEADS):
        if lookahead:
            s, s_next = s_next, (scores(hq + 1) if hq + 1 < N_Q_HEADS else None)
        else:
            s = scores(hq)
        kv = hq // GROUP
        e = jnp.exp2(s - jnp.max(s, axis=-1, keepdims=True))
        denom = jnp.sum(e, axis=-1, keepdims=True)
        pv = _dot(e.astype(BF16), v[:, kv * HEAD_DIM:(kv + 1) * HEAD_DIM])
        outs.append(pv / denom)
        yield
    return jnp.concatenate(outs, axis=1).astype(BF16)


def _rg_inputs(h, win_ref):
    rg_lo = ATTN_WIDTH + 2 * KV_WIDTH
    xr = _dot(h, win_ref[:, rg_lo:rg_lo + RG_WIDTH])
    yield
    gr = _dot(h, win_ref[:, rg_lo + RG_WIDTH:IN_WIDTH])
    yield
    return xr, gr


def _rglru_rows(xr, gr, cw_ref, cb_ref, wg_ref, bg_ref, lam_ref, h0, scan_refs):
    af_ref, bf_ref, ab_ref, bb_ref = scan_refs
    n = xr.shape[0]

    t_idx = lax.broadcasted_iota(jnp.int32, (n, 1), 0)
    cw = cw_ref[...]
    xc = jnp.where(t_idx >= 2, pltpu.roll(xr, 2, axis=0), 0.0) * cw[0:1, :]
    xc = xc + jnp.where(t_idx >= 1, pltpu.roll(xr, 1, axis=0), 0.0) * cw[1:2, :]
    xc = xc + xr * cw[2:3, :]
    xc = xc + jnp.where(t_idx < n - 1, pltpu.roll(xr, n - 1, axis=0), 0.0) * cw[3:4, :]
    xc = xc + cb_ref[...]
    yield

    xcb = xc.astype(BF16)
    halves = [_dot(xcb[:, c * MXU_DIM:(c + 1) * MXU_DIM], wg_ref[c].astype(BF16))
              for c in range(RG_WIDTH // MXU_DIM)]

    def gate_pre(idx):
        return jnp.concatenate([hv[:, idx * MXU_DIM:(idx + 1) * MXU_DIM] for hv in halves], axis=1) \
            + bg_ref[idx:idx + 1, :]

    yield
    neg = -lam_ref[...]
    softplus = jnp.maximum(neg, 0.0) + jnp.log1p(jnp.exp(-jnp.abs(neg)))
    decay = (-0.5 * C_LRU) * softplus
    half_xc = 0.5 * xc

    def coeffs(d):
        r2 = jnp.tanh(gate_pre(2 * d)) + 1.0
        i2 = jnp.tanh(gate_pre(2 * d + 1)) + 1.0
        log_a = r2 * decay[d:d + 1, :]
        a = jnp.exp(log_a)
        bx = jnp.sqrt(jnp.tanh(-log_a) * (a * a + 1.0)) * (i2 * half_xc)
        return a, bx

    groups = n // SUBLANES
    rmod = lax.broadcasted_iota(jnp.int32, (1, SUBLANES, 1), 1)

    def tile_scan(d, forward):
        a, bx = coeffs(d)
        yield
        a = a.reshape(groups, SUBLANES, RG_WIDTH)
        bx = bx.reshape(groups, SUBLANES, RG_WIDTH)
        for s in (1, 2, 4):
            ok = (rmod >= s) if forward else (rmod < SUBLANES - s)
            shift = s if forward else SUBLANES - s
            a_sh = jnp.where(ok, pltpu.roll(a, shift, axis=1), 1.0)
            b_sh = jnp.where(ok, pltpu.roll(bx, shift, axis=1), 0.0)
            bx = a * b_sh + bx
            a = a * a_sh
            yield
        return a.reshape(n, RG_WIDTH), bx.reshape(n, RG_WIDTH)

    af_ref[...], bf_ref[...] = yield from tile_scan(0, True)
    ab_ref[...], bb_ref[...] = yield from tile_scan(1, False)

    def step(g, carry):
        hf, hb = carry
        rf = pl.multiple_of(g * SUBLANES, SUBLANES)
        rb = pl.multiple_of((groups - 1 - g) * SUBLANES, SUBLANES)
        new_f = af_ref[pl.ds(rf, SUBLANES), :] * hf + bf_ref[pl.ds(rf, SUBLANES), :]
        new_b = ab_ref[pl.ds(rb, SUBLANES), :] * hb + bb_ref[pl.ds(rb, SUBLANES), :]
        bf_ref[pl.ds(rf, SUBLANES), :] = new_f
        bb_ref[pl.ds(rb, SUBLANES), :] = new_b
        return (jnp.broadcast_to(new_f[SUBLANES - 1:SUBLANES, :], (SUBLANES, RG_WIDTH)),
                jnp.broadcast_to(new_b[0:1, :], (SUBLANES, RG_WIDTH)))

    hf, hb = lax.fori_loop(
        0, groups, step,
        (jnp.broadcast_to(h0[0:1, :], (SUBLANES, RG_WIDTH)), jnp.broadcast_to(h0[1:2, :], (SUBLANES, RG_WIDTH))),
        unroll=True)
    yield

    rg = ((bf_ref[...] + bb_ref[...]) * jax.nn.gelu(gr)).astype(BF16)
    return rg, hf[0:1, :], hb[0:1, :]


def _segment_permutation():
    p = np.zeros((PERM_BLOCK, PERM_BLOCK), np.float32)
    for t in range(SEG_LEN):
        for j in range(SUBLANES):
            p[t * SUBLANES + j, j * SEG_LEN + t] = 1.0
    return p


def _rglru_segments(xr_ref, gr_ref, perm_t_ref, cw_ref, cb_ref, wg_ref, bg_ref, lam_ref, h0, scan_refs, ready):
    af_ref, bf_ref, ab_ref, bb_ref = scan_refs
    n = xr_ref.shape[0]
    blocks = n // PERM_BLOCK
    sub = lax.broadcasted_iota(jnp.int32, (SUBLANES, 1), 0)
    zero_row = jnp.zeros((1, RG_WIDTH), F32)
    cw = cw_ref[...]
    neg = -lam_ref[...]
    softplus = jnp.maximum(neg, 0.0) + jnp.log1p(jnp.exp(-jnp.abs(neg)))
    decay = (-0.5 * C_LRU) * softplus

    def group(b, t):
        return xr_ref[b * PERM_BLOCK + t * SUBLANES:b * PERM_BLOCK + (t + 1) * SUBLANES, :]

    def before(b, t):
        wrap = group(b - 1, t)[SUBLANES - 1:SUBLANES, :] if b > 0 else zero_row
        return jnp.where(sub == 0, wrap, pltpu.roll(group(b, t), 1, axis=0))

    def after(b, t):
        wrap = group(b + 1, t)[0:1, :] if b + 1 < blocks else zero_row
        return jnp.where(sub == SUBLANES - 1, wrap, pltpu.roll(group(b, t), SUBLANES - 1, axis=0))

    ends = []
    for b in range(blocks):
        while not ready(min(b + 1, blocks - 1)):
            yield
        rows = slice(b * PERM_BLOCK, (b + 1) * PERM_BLOCK)
        ext = jnp.concatenate([before(b, SEG_LEN - 2), before(b, SEG_LEN - 1), xr_ref[rows, :], after(b, 0)], axis=0)
        xc = ext[0:PERM_BLOCK] * cw[0:1, :]
        for tap in range(1, CONV_W):
            xc = xc + ext[tap * SUBLANES:tap * SUBLANES + PERM_BLOCK] * cw[tap:tap + 1, :]
        xc = xc + cb_ref[...]
        yield

        xcb = xc.astype(BF16)
        halves = [_dot(xcb[:, c * MXU_DIM:(c + 1) * MXU_DIM], wg_ref[c].astype(BF16))
                  for c in range(RG_WIDTH // MXU_DIM)]
        yield
        half_xc = 0.5 * xc
        for d, (a_ref, b_ref) in enumerate(((af_ref, bf_ref), (ab_ref, bb_ref))):
            pre = [jnp.concatenate([hv[:, i * MXU_DIM:(i + 1) * MXU_DIM] for hv in halves], axis=1)
                   + bg_ref[i:i + 1, :] for i in (2 * d, 2 * d + 1)]
            r2 = jnp.tanh(pre[0]) + 1.0
            i2 = jnp.tanh(pre[1]) + 1.0
            log_a = r2 * decay[d:d + 1, :]
            a = jnp.exp(log_a)
            a_ref[rows, :] = a
            b_ref[rows, :] = jnp.sqrt(jnp.tanh(-log_a) * (a * a + 1.0)) * (i2 * half_xc)
            yield

        for a_ref, b_ref, steps in ((af_ref, bf_ref, range(SEG_LEN)), (ab_ref, bb_ref, range(SEG_LEN - 1, -1, -1))):
            hend = jnp.zeros((SUBLANES, RG_WIDTH), F32)
            pend = jnp.ones((SUBLANES, RG_WIDTH), F32)
            for t in steps:
                r = slice(b * PERM_BLOCK + t * SUBLANES, b * PERM_BLOCK + (t + 1) * SUBLANES)
                a = a_ref[r, :]
                hend = a * hend + b_ref[r, :]
                pend = a * pend
            ends.append((hend, pend))
            yield

    def carries(order, state, which):
        into = {}
        for b, j in order:
            into[b, j] = state
            hend, pend = ends[2 * b + which]
            state = hend[j:j + 1, :] + pend[j:j + 1, :] * state
        return into, state

    segs = [(b, j) for b in range(blocks) for j in range(SUBLANES)]
    into_f, hf = carries(segs, h0[0:1, :], 0)
    into_b, hb = carries(segs[::-1], h0[1:2, :], 1)
    yield

    for b in range(blocks):
        for a_ref, b_ref, into, steps in ((af_ref, bf_ref, into_f, range(SEG_LEN)),
                                          (ab_ref, bb_ref, into_b, range(SEG_LEN - 1, -1, -1))):
            h = jnp.concatenate([into[b, j] for j in range(SUBLANES)], axis=0)
            for t in steps:
                r = slice(b * PERM_BLOCK + t * SUBLANES, b * PERM_BLOCK + (t + 1) * SUBLANES)
                h = a_ref[r, :] * h + b_ref[r, :]
                b_ref[r, :] = h
            yield

    out = []
    for b in range(blocks):
        rows = slice(b * PERM_BLOCK, (b + 1) * PERM_BLOCK)
        rg = ((bf_ref[rows, :] + bb_ref[rows, :]) * jax.nn.gelu(gr_ref[rows, :])).astype(BF16)
        out.append(_dot(perm_t_ref[...], rg).astype(BF16))
        yield
    return jnp.concatenate(out, axis=0), hf, hb


def _out_and_router(attn, rg, x, m, wout_ref, g1_ref, b1_ref, wrt_ref):
    gate1 = m[:, 2 * D_MODEL:3 * D_MODEL]
    shift2 = m[:, 3 * D_MODEL:4 * D_MODEL]
    scale2 = m[:, 4 * D_MODEL:5 * D_MODEL]
    mix = _dot(jnp.concatenate([attn, rg], axis=1), wout_ref[...])
    yield
    x1 = _ln_plain(ALPHA * x + gate1 * mix) * g1_ref[...] + b1_ref[...]
    yield
    h2 = (_ln_plain(x1) * (1.0 + scale2) + shift2).astype(BF16)
    yield
    logits = _dot_nt(wrt_ref[...].astype(BF16), h2)
    e = jnp.exp(logits - jnp.max(logits, axis=0, keepdims=True))
    return x1, h2, e / jnp.sum(e, axis=0, keepdims=True)


def _mixer_kernel(x_ref, mod_ref, win_ref, wout_ref, bd_ref, qg_ref, kg_ref, cw_ref, cb_ref, wg_ref, bg_ref,
                  lam_ref, g1_ref, b1_ref, wrt_ref,
                  nk_ref, nv_ref, nh_ref, x1_ref, h2_ref, afft_ref,
                  xprev_ref, xr_ref, gr_ref, attn_ref, *scan_refs):
    @pl.when(pl.program_id(0) == 0)
    def _():
        xprev_ref[...] = jnp.zeros_like(xprev_ref)
        xr_ref[...] = jnp.zeros_like(xr_ref)
        gr_ref[...] = jnp.zeros_like(gr_ref)
        attn_ref[...] = jnp.zeros_like(attn_ref)

    m = mod_ref[0:1, :]

    def second_half():
        h0 = jnp.zeros((2, RG_WIDTH), F32)
        rg, hf, hb = yield from _rglru_rows(xr_ref[...], gr_ref[...], cw_ref, cb_ref, wg_ref, bg_ref, lam_ref, h0,
                                            scan_refs)
        nh_ref[0] = jnp.concatenate([hf, hb], axis=0)
        x1_ref[...], h2_ref[...], afft_ref[...] = yield from _out_and_router(
            attn_ref[...], rg, xprev_ref[...], m, wout_ref, g1_ref, b1_ref, wrt_ref)

    def first_half():
        x = x_ref[...]
        h = _modulated(x, m)
        yield
        q, k, v = yield from _qkv(h, win_ref, bd_ref, qg_ref, kg_ref, None)
        nk_ref[0] = k.T
        nv_ref[0] = v.T
        yield
        xr, gr = yield from _rg_inputs(h, win_ref)
        attn = yield from _attend(q, k.astype(BF16), v.astype(BF16), False)
        return x, attn, xr, gr

    (x, attn, xr, gr), _ = _drive(first_half(), second_half(), order=MIXER_ORDER)
    xprev_ref[...] = x
    attn_ref[...] = attn
    xr_ref[...] = xr
    gr_ref[...] = gr


def _mixer_prompt(x2d, mod, win_bf, wout_bf, consts, g1, b1, wrt, *, n, nb):
    tokens = nb * n
    first = lambda w: pl.BlockSpec((n, w), lambda i: (jnp.minimum(i, nb - 1), 0))
    second = lambda w: pl.BlockSpec((n, w), lambda i: (jnp.maximum(i - 1, 0), 0))
    small = list(consts) + [g1, b1, wrt]
    return pl.pallas_call(
        _mixer_kernel,
        grid=(nb + 1,),
        in_specs=[first(D_MODEL), _full(mod.shape), _full(win_bf.shape), _full(wout_bf.shape)]
                 + [_full(a.shape) for a in small],
        out_specs=[pl.BlockSpec((1, KV_WIDTH, n), lambda i: (jnp.minimum(i, nb - 1), 0, 0)),
                   pl.BlockSpec((1, KV_WIDTH, n), lambda i: (jnp.minimum(i, nb - 1), 0, 0)),
                   pl.BlockSpec((1, 2, RG_WIDTH), lambda i: (jnp.maximum(i - 1, 0), 0, 0)),
                   second(D_MODEL), second(D_MODEL),
                   pl.BlockSpec((N_EXPERTS, n), lambda i: (jnp.maximum(i - 1, 0), 0))],
        out_shape=[jax.ShapeDtypeStruct((nb, KV_WIDTH, n), F32), jax.ShapeDtypeStruct((nb, KV_WIDTH, n), F32),
                   jax.ShapeDtypeStruct((nb, 2, RG_WIDTH), F32),
                   jax.ShapeDtypeStruct((tokens, D_MODEL), F32), jax.ShapeDtypeStruct((tokens, D_MODEL), BF16),
                   jax.ShapeDtypeStruct((nb * N_EXPERTS, n), F32)],
        scratch_shapes=[pltpu.VMEM((n, D_MODEL), F32), pltpu.VMEM((n, RG_WIDTH), F32),
                        pltpu.VMEM((n, RG_WIDTH), F32), pltpu.VMEM((n, ATTN_WIDTH), BF16)]
                       + [pltpu.VMEM((n, RG_WIDTH), F32)] * 4,
        compiler_params=_params(),
        name="mixer_prompt",
    )(x2d, mod, win_bf, wout_bf, *small)


def _front_kernel(x_ref, mod_ref, win_ref, bd_ref, qg_ref, kg_ref, cw_ref, cb_ref, wg_ref, bg_ref, lam_ref,
                  perm_ref, perm_t_ref, h0_ref, cos_ref, sin_ref,
                  q_ref, k_ref, v_ref, rg_ref, xr_ref, gr_ref, *scan_refs):
    m = mod_ref[pl.ds(1 + pl.program_id(0), 1), :]
    rows = PERM_BLOCK
    streams = x_ref.shape[0] // rows

    def row_block(j):
        r = pl.ds(j * rows, rows)
        h = _modulated(x_ref[r, :], m)
        yield
        q, k, v = yield from _qkv(h, win_ref, bd_ref, qg_ref, kg_ref, (cos_ref[r, :], sin_ref[r, :]))
        q_ref[r, :] = q
        k_ref[r, :] = k.astype(BF16)
        v_ref[r, :] = v.astype(BF16)
        yield
        hp = _dot(perm_ref[...], h).astype(BF16)
        xr_ref[r, :], gr_ref[r, :] = yield from _rg_inputs(hp, win_ref)
        traced[j] = True

    traced = [False] * streams
    recurrent = _rglru_segments(xr_ref, gr_ref, perm_t_ref, cw_ref, cb_ref, wg_ref, bg_ref, lam_ref, h0_ref[0],
                                scan_refs, lambda block: traced[block])
    chunks = 8
    order = [0] * chunks + [1] * chunks + [s for j in range(2, streams) for _ in range(chunks) for s in (j, streams)]
    results = _drive(*[row_block(j) for j in range(streams)], recurrent, order=order)
    rg_ref[...], _, _ = results[-1]


def _front_sample(x2d, mod, win_bf, consts, h0, rope, *, n, nb):
    tokens = nb * n
    seq = lambda w: pl.BlockSpec((n, w), lambda b: (b, 0))
    small = list(consts)
    return pl.pallas_call(
        _front_kernel,
        grid=(nb,),
        in_specs=[seq(D_MODEL), _full(mod.shape), _full(win_bf.shape)] + [_full(a.shape) for a in small]
                 + [pl.BlockSpec((1, 2, RG_WIDTH), lambda b: (b, 0, 0)), _full(rope[0].shape), _full(rope[1].shape)],
        out_specs=[seq(ATTN_WIDTH), seq(KV_WIDTH), seq(KV_WIDTH), seq(RG_WIDTH)],
        out_shape=[jax.ShapeDtypeStruct((tokens, ATTN_WIDTH), BF16), jax.ShapeDtypeStruct((tokens, KV_WIDTH), BF16),
                   jax.ShapeDtypeStruct((tokens, KV_WIDTH), BF16), jax.ShapeDtypeStruct((tokens, RG_WIDTH), BF16)],
        scratch_shapes=[pltpu.VMEM((n, RG_WIDTH), F32)] * 6,
        compiler_params=_params(),
        name="front_sample",
    )(x2d, mod, win_bf, *small, h0, *rope)


def _attn_post_kernel(q_ref, k_ref, v_ref, kc_ref, vc_ref, rg_ref, x_ref, mod_ref, wout_ref, g1_ref, b1_ref, wrt_ref,
                      x1_ref, h2_ref, afft_ref):
    m = mod_ref[pl.ds(1 + pl.program_id(0), 1), :]
    k = jnp.concatenate([k_ref[...], kc_ref[0].T.astype(BF16)], axis=0)
    v = jnp.concatenate([v_ref[...], vc_ref[0].T.astype(BF16)], axis=0)
    attn, = _drive(_attend(q_ref[...], k, v, True))
    rows = attn.shape[0] // TAIL_ROW_STREAMS

    def row_block(j):
        r = slice(j * rows, (j + 1) * rows)
        x1_ref[r, :], h2_ref[r, :], afft_ref[:, r] = yield from _out_and_router(
            attn[r, :], rg_ref[r, :], x_ref[r, :], m, wout_ref, g1_ref, b1_ref, wrt_ref)

    _drive(*[row_block(j) for j in range(TAIL_ROW_STREAMS)], order=(0,))


def _attn_post_sample(q, k, v, cache_k, cache_v, rg, x2d, mod, wout_bf, g1, b1, wrt, *, n, nb, past):
    tq = min(n, ATTN_Q_TILE)
    tiles = n // tq
    tokens = nb * n
    tile = lambda w: pl.BlockSpec((tq, w), lambda b, t: (b * tiles + t, 0))
    whole = pl.BlockSpec((n, KV_WIDTH), lambda b, t: (b, 0))
    cached = pl.BlockSpec((1, KV_WIDTH, past), lambda b, t: (b, 0, 0))
    return pl.pallas_call(
        _attn_post_kernel,
        grid=(nb, tiles),
        in_specs=[tile(ATTN_WIDTH), whole, whole, cached, cached, tile(RG_WIDTH), tile(D_MODEL),
                  _full(mod.shape), _full(wout_bf.shape), _full(g1.shape), _full(b1.shape), _full(wrt.shape)],
        out_specs=[tile(D_MODEL), tile(D_MODEL), pl.BlockSpec((N_EXPERTS, tq), lambda b, t: (b, t))],
        out_shape=[jax.ShapeDtypeStruct((tokens, D_MODEL), F32), jax.ShapeDtypeStruct((tokens, D_MODEL), BF16),
                   jax.ShapeDtypeStruct((nb * N_EXPERTS, n), F32)],
        compiler_params=_params(2),
        name="attn_post_sample",
    )(q, k, v, cache_k, cache_v, rg, x2d, mod, wout_bf, g1, b1, wrt)


def _slot_onehot(mask, pos, cap):
    n = mask.shape[1]
    slot = lax.broadcasted_iota(jnp.int32, (cap, n), 0).astype(F32)
    return [(slot == pos[e:e + 1, :]) & (mask[e:e + 1, :] > 0.5) for e in range(mask.shape[0])]


def _route_kernel(afft_ref, h2_ref, xs_ref, mask_ref, pos_ref, *, n, cap, seqs):
    b = pl.program_id(0)

    @pl.when(b == 0)
    def _():
        aff = afft_ref[...]
        thr = jnp.zeros((aff.shape[0], 1), jnp.int32)
        for bit in range(30, -1, -1):
            cand = thr | (1 << bit)
            cnt = jnp.sum((aff >= lax.bitcast_convert_type(cand, F32)).astype(F32), axis=1, keepdims=True)
            thr = jnp.where(cnt >= cap, cand, thr)
        above = aff >= lax.bitcast_convert_type(thr + 1, F32)
        tied = (aff >= lax.bitcast_convert_type(thr, F32)) & jnp.logical_not(above)
        need = cap - jnp.sum(above.astype(F32), axis=1, keepdims=True)
        before = (lax.broadcasted_iota(jnp.int32, (n, n), 0)
                  < lax.broadcasted_iota(jnp.int32, (n, n), 1)).astype(BF16)
        tie_rank = _dot(tied.astype(BF16), before)
        mask = (above | (tied & (tie_rank < need))).astype(F32)
        mask_ref[...] = mask
        pos_ref[...] = _dot(mask.astype(BF16), before)

    for j in range(seqs):
        r0 = pl.multiple_of((b * seqs + j) * N_EXPERTS, N_EXPERTS)
        onehots = _slot_onehot(mask_ref[pl.ds(r0, N_EXPERTS), :], pos_ref[pl.ds(r0, N_EXPERTS), :], cap)
        sel = jnp.concatenate(onehots, axis=0).astype(BF16)
        xs = _dot(sel, h2_ref[j * n:(j + 1) * n, :]).astype(BF16)
        xs_ref[:, j * cap:(j + 1) * cap, :] = xs.reshape(N_EXPERTS, cap, D_MODEL)


def _route(afft, h2, *, n, nb, cap, seqs):
    rows = nb * N_EXPERTS
    return pl.pallas_call(
        functools.partial(_route_kernel, n=n, cap=cap, seqs=seqs),
        grid=(nb // seqs,),
        in_specs=[_full(afft.shape), pl.BlockSpec((seqs * n, D_MODEL), lambda b: (b, 0))],
        out_specs=[pl.BlockSpec((N_EXPERTS, seqs * cap, D_MODEL), lambda b: (0, b, 0)),
                   _full((rows, n)), _full((rows, n))],
        out_shape=[jax.ShapeDtypeStruct((N_EXPERTS, nb * cap, D_MODEL), BF16),
                   jax.ShapeDtypeStruct((rows, n), F32), jax.ShapeDtypeStruct((rows, n), F32)],
        compiler_params=_params(),
        name="route_n%d" % n,
    )(afft, h2)


def _slot_gates(aff_ref, mask_ref, pos_ref, e, cap):
    cols = []
    for b in range(aff_ref.shape[0] // N_EXPERTS):
        r = b * N_EXPERTS + e
        onehot, = _slot_onehot(mask_ref[pl.ds(r, 1), :], pos_ref[pl.ds(r, 1), :], cap)
        cols.append(jnp.sum(jnp.where(onehot, aff_ref[pl.ds(r, 1), :], 0.0), axis=1, keepdims=True))
    return jnp.concatenate(cols, axis=0)


def _ffn_kernel(xp_ref, xs_ref, ap_ref, mp_ref, pp_ref, as_ref, ms_ref, ps_ref, wgu_ref, wd_ref, yp_ref, ys_ref):
    e = pl.program_id(0)
    rows_p = xp_ref.shape[1]
    xs = jnp.concatenate([xp_ref[0], xs_ref[0]], axis=0)
    gu = _dot(xs, wgu_ref[0].astype(BF16))
    gate = gu[:, :D_EXPERT]
    up = gu[:, D_EXPERT:]
    act = (gate * jax.nn.sigmoid(gate) * up).astype(BF16)
    y = _dot(act, wd_ref[0].astype(BF16))
    g = jnp.concatenate([_slot_gates(ap_ref, mp_ref, pp_ref, e, rows_p * N_EXPERTS // ap_ref.shape[0]),
                         _slot_gates(as_ref, ms_ref, ps_ref, e, xs_ref.shape[1] * N_EXPERTS // as_ref.shape[0])],
                        axis=0)
    y = (y * g).astype(BF16)
    yp_ref[0] = y[:rows_p]
    ys_ref[0] = y[rows_p:]


def _ffn(xs_p, xs_s, route_p, route_s, w_gate_up, w_down):
    per_e = lambda a: pl.BlockSpec((1,) + a.shape[1:], lambda e: (e, 0, 0))
    tables = list(route_p) + list(route_s)
    return pl.pallas_call(
        _ffn_kernel,
        grid=(N_EXPERTS,),
        in_specs=[per_e(xs_p), per_e(xs_s)] + [_full(t.shape) for t in tables] + [per_e(w_gate_up), per_e(w_down)],
        out_specs=[per_e(xs_p), per_e(xs_s)],
        out_shape=[jax.ShapeDtypeStruct(xs_p.shape, BF16), jax.ShapeDtypeStruct(xs_s.shape, BF16)],
        compiler_params=_params(),
        name="expert_ffn",
    )(xs_p, xs_s, *tables, w_gate_up, w_down)


def _combine_kernel(y_ref, mask_ref, pos_ref, x1_ref, mod_ref, g2_ref, b2_ref, o_ref, *, cap, seqs, mod_row0,
                    mod_row_step):
    row = mod_row0 + mod_row_step * pl.program_id(0)
    gate2 = mod_ref[pl.ds(row, 1), 5 * D_MODEL:6 * D_MODEL]
    tn = x1_ref.shape[0] // seqs
    for j in range(seqs):
        e0 = j * N_EXPERTS
        sel = jnp.concatenate(
            _slot_onehot(mask_ref[e0:e0 + N_EXPERTS, :], pos_ref[e0:e0 + N_EXPERTS, :], cap), axis=0).astype(BF16)
        y = y_ref[:, j * cap:(j + 1) * cap, :].reshape(N_EXPERTS * cap, D_MODEL)
        ff = _dot_tn(sel, y)
        r = pl.ds(j * tn, tn)
        o_ref[r, :] = _ln_plain(ALPHA * x1_ref[r, :] + gate2 * ff) * g2_ref[...] + b2_ref[...]


def _combine(y, mask, pos, x1, mod, g2, b2, *, n, nb, cap, seqs, sample):
    tn = min(n, COMBINE_TILE)
    tiles = n // tn
    assert seqs == 1 or tiles == 1
    return pl.pallas_call(
        functools.partial(_combine_kernel, cap=cap, seqs=seqs, mod_row0=1 if sample else 0,
                          mod_row_step=1 if sample else 0),
        grid=(nb // seqs, tiles),
        in_specs=[pl.BlockSpec((N_EXPERTS, seqs * cap, D_MODEL), lambda b, t: (0, b, 0)),
                  pl.BlockSpec((seqs * N_EXPERTS, tn), lambda b, t: (b, t)),
                  pl.BlockSpec((seqs * N_EXPERTS, tn), lambda b, t: (b, t)),
                  pl.BlockSpec((seqs * tn, D_MODEL), lambda b, t: (b * tiles + t, 0)),
                  _full(mod.shape), _full(g2.shape), _full(b2.shape)],
        out_specs=pl.BlockSpec((seqs * tn, D_MODEL), lambda b, t: (b * tiles + t, 0)),
        out_shape=jax.ShapeDtypeStruct((nb * n, D_MODEL), F32),
        compiler_params=_params(2),
        name="combine_sample" if sample else "combine_prompt",
    )(y, mask, pos, x1, mod, g2, b2)


def _block_diag_256(w):
    per_tile = MXU_DIM // RG_BLOCK
    tiles = []
    for c in range(w.shape[0] // per_tile):
        tiles.append(jax.scipy.linalg.block_diag(*[w[c * per_tile + i] for i in range(per_tile)]))
    return jnp.stack(tiles)


def _rope_tables(n):
    lane = np.arange(LANES)
    within = lane % HEAD_DIM
    freq = (within % 16).astype(np.float32)
    inv = np.float32(ROPE_THETA) ** (-freq / np.float32(16.0))
    tok = np.arange(n)
    pos = np.where((within < HEAD_DIM // 2)[None, :], (tok // GRID_W)[:, None], (tok % GRID_W)[:, None])
    ang = pos.astype(np.float32) * inv[None, :]
    sign = np.where(within % 32 < 16, -1.0, 1.0).astype(np.float32)
    return jnp.asarray(np.cos(ang)), jnp.asarray(np.sin(ang) * sign[None, :])


def kernel(x_prompt, x_sample, cache_k, cache_v, state_h, c, c_ctx, w_mod, b_mod, w_in, q_norm_g, k_norm_g,
           conv_w, conv_b, w_rg_a, b_rg_a, w_rg_x, b_rg_x, rg_lambda, w_out, ln1_g, ln1_b, w_router,
           w_gate_up, w_down, ln2_g, ln2_b):
    assert w_mod.shape[0] == DEPTH == 1
    nb_p, n_p, _ = x_prompt.shape
    nb_s, n_s, _ = x_sample.shape
    past = cache_k.shape[2]
    cap_p = CAP_FACTOR * n_p // N_EXPERTS
    cap_s = CAP_FACTOR * n_s // N_EXPERTS

    row = lambda v: v.reshape(1, -1)
    head_avg = np.kron(np.eye(ATTN_WIDTH // HEAD_DIM, dtype=np.float32),
                       np.full((HEAD_DIM, HEAD_DIM), 1.0 / HEAD_DIM, np.float32))
    bd = jnp.asarray(head_avg, BF16)
    qg = row(jnp.tile(q_norm_g[0], N_Q_HEADS))
    kg = row(jnp.tile(k_norm_g[0], N_KV_HEADS))
    wg = 0.5 * jnp.concatenate([_block_diag_256(w_rg_a[0, 0]), _block_diag_256(w_rg_x[0, 0]),
                                _block_diag_256(w_rg_a[0, 1]), _block_diag_256(w_rg_x[0, 1])], axis=2)
    bg = 0.5 * jnp.stack([b_rg_a[0, 0], b_rg_x[0, 0], b_rg_a[0, 1], b_rg_x[0, 1]])
    perm = _segment_permutation()
    consts = (bd, qg, kg, conv_w[0], row(conv_b[0]), wg, bg, rg_lambda[0])
    perms = (jnp.asarray(perm, BF16), jnp.asarray(perm.T, BF16))
    wrt = w_router[0].T
    g1, b1, g2, b2 = row(ln1_g[0]), row(ln1_b[0]), row(ln2_g[0]), row(ln2_b[0])

    win_bf, wout_bf = _cast_weights(w_in[0], w_out[0])
    mod = _mod_vectors(row(c_ctx), c, w_mod[0], row(b_mod[0]))

    xp = x_prompt.reshape(nb_p * n_p, D_MODEL)
    xs = x_sample.reshape(nb_s * n_s, D_MODEL)

    new_k, new_v, new_h, x1_p, h2_p, aff_p = _mixer_prompt(
        xp, mod, win_bf, wout_bf, consts, g1, b1, wrt, n=n_p, nb=nb_p)

    q_s, k_s, v_s, rg_s = _front_sample(xs, mod, win_bf, consts + perms, state_h[:, 0], _rope_tables(n_s),
                                        n=n_s, nb=nb_s)
    def cached(t):
        return jnp.transpose(t[:, 0], (0, 2, 3, 1)).reshape(nb_s, KV_WIDTH, past)

    x1_s, h2_s, aff_s = _attn_post_sample(
        q_s, k_s, v_s, cached(cache_k), cached(cache_v), rg_s, xs, mod, wout_bf, g1, b1, wrt,
        n=n_s, nb=nb_s, past=past)

    seqs_p = PROMPT_SEQS_PER_STEP if nb_p % PROMPT_SEQS_PER_STEP == 0 else 1
    xs_p, mask_p, pos_p = _route(aff_p, h2_p, n=n_p, nb=nb_p, cap=cap_p, seqs=seqs_p)
    xs_s, mask_s, pos_s = _route(aff_s, h2_s, n=n_s, nb=nb_s, cap=cap_s, seqs=1)

    y_p, y_s = _ffn(xs_p, xs_s, (aff_p, mask_p, pos_p), (aff_s, mask_s, pos_s), w_gate_up[0], w_down[0])

    out_p = _combine(y_p, mask_p, pos_p, x1_p, mod, g2, b2, n=n_p, nb=nb_p, cap=cap_p, seqs=seqs_p, sample=False)
    out_s = _combine(y_s, mask_s, pos_s, x1_s, mod, g2, b2, n=n_s, nb=nb_s, cap=cap_s, seqs=1, sample=True)

    def cache_layout(t):
        t = t.reshape(nb_p, DEPTH, N_KV_HEADS, HEAD_DIM, n_p)
        return jnp.transpose(t, (0, 1, 4, 2, 3))

    return (out_p.reshape(nb_p, n_p, D_MODEL), out_s.reshape(nb_s, n_s, D_MODEL),
            cache_layout(new_k), cache_layout(new_v), new_h.reshape(nb_p, DEPTH, 2, RG_WIDTH))
```

```python
import functools

import numpy as np
import jax
import jax.numpy as jnp
from jax import lax
from jax.experimental import pallas as pl
from jax.experimental.pallas import tpu as pltpu

F32 = jnp.float32
BF16 = jnp.bfloat16

D_MODEL = 1024
HEAD_DIM = 64
N_Q_HEADS = 8
N_KV_HEADS = 2
GROUP = N_Q_HEADS // N_KV_HEADS
ATTN_WIDTH = N_Q_HEADS * HEAD_DIM
KV_WIDTH = N_KV_HEADS * HEAD_DIM
RG_WIDTH = D_MODEL - ATTN_WIDTH
RG_BLOCK = 64
CONV_W = 4
C_LRU = 8.0
N_EXPERTS = 16
CAP_FACTOR = 2
D_EXPERT = 1024
GRID_W = 64
ROPE_THETA = 10000.0
NORM_EPS = 1e-6
DEPTH = 1
ALPHA = (2.0 * DEPTH) ** 0.25
IN_WIDTH = ATTN_WIDTH + 2 * KV_WIDTH + 2 * RG_WIDTH
Q_SCALE = HEAD_DIM ** -0.5 * float(np.log2(np.e))

LANES = 128
SUBLANES = 8
MXU_DIM = 256
VMEM_LIMIT_BYTES = 56 * 1024 * 1024

MOD_ROWS = SUBLANES
MOD_TILE = 256
ATTN_Q_TILE = 512
COMBINE_TILE = 512
MIXER_ORDER = (1, 0, 0, 1, 1, 0, 1, 1, 0, 1, 0, 1, 0, 1, 1, 0, 1, 1, 0, 0, 1, 0, 1, 0, 0, 0, 1, 0, 0, 1, 0)
PERM_BLOCK = MXU_DIM
SEG_LEN = PERM_BLOCK // SUBLANES
TAIL_ROW_STREAMS = 2
PROMPT_SEQS_PER_STEP = 4


def _params(n_axes=1):
    return pltpu.CompilerParams(dimension_semantics=("arbitrary",) * n_axes,
                                vmem_limit_bytes=VMEM_LIMIT_BYTES)


def _full(shape):
    zeros = (0,) * len(shape)
    return pl.BlockSpec(shape, lambda *_: zeros)


def _ln_plain(x):
    mu = jnp.mean(x, -1, keepdims=True)
    xc = x - mu
    var = jnp.mean(xc * xc, -1, keepdims=True)
    return xc * lax.rsqrt(var + NORM_EPS)


def _dot(a, b):
    return jnp.dot(a, b, preferred_element_type=F32)


def _dot_nt(a, b):
    return lax.dot_general(a, b, (((1,), (1,)), ((), ())), preferred_element_type=F32)


def _dot_tn(a, b):
    return lax.dot_general(a, b, (((0,), (0,)), ((), ())), preferred_element_type=F32)


def _split_bf16(x):
    hi = x.astype(BF16)
    lo = (x - hi.astype(F32)).astype(BF16)
    return hi, lo


def _cast_kernel(a_ref, b_ref, wa_ref, wx_ref, ao_ref, bo_ref, wg_ref):
    ao_ref[...] = a_ref[...].astype(BF16)
    bo_ref[...] = b_ref[...].astype(BF16)
    wg_ref[...] = jnp.zeros_like(wg_ref)
    per_tile = MXU_DIM // RG_BLOCK
    for c in range(RG_WIDTH // MXU_DIM):
        for g, (ref, d) in enumerate(((wa_ref, 0), (wx_ref, 0), (wa_ref, 1), (wx_ref, 1))):
            for i in range(per_tile):
                lo = RG_BLOCK * i
                wg_ref[c, lo:lo + RG_BLOCK, MXU_DIM * g + lo:MXU_DIM * g + lo + RG_BLOCK] = \
                    (0.5 * ref[d, per_tile * c + i]).astype(BF16)


def _cast_weights(w_in, w_out, w_rg_a, w_rg_x):
    wg_shape = (RG_WIDTH // MXU_DIM, MXU_DIM, 4 * MXU_DIM)
    ins = [w_in, w_out, w_rg_a, w_rg_x]
    outs = [jax.ShapeDtypeStruct(w_in.shape, BF16), jax.ShapeDtypeStruct(w_out.shape, BF16),
            jax.ShapeDtypeStruct(wg_shape, BF16)]
    return pl.pallas_call(
        _cast_kernel,
        grid=(1,),
        in_specs=[_full(a.shape) for a in ins],
        out_specs=[_full(o.shape) for o in outs],
        out_shape=outs,
        compiler_params=_params(),
        name="cast_weights",
    )(*ins)


def _mod_kernel(ctx_ref, c_ref, w_ref, b_ref, o_ref):
    @pl.when(pl.program_id(0) == 0)
    def _():
        o_ref[...] = jnp.broadcast_to(b_ref[...], o_ref.shape)

    w = w_ref[...].astype(BF16)
    n_c = c_ref.shape[0]
    for rows, ref in ((slice(0, 1), ctx_ref), (slice(1, 1 + n_c), c_ref)):
        cs = ref[...]
        o_ref[rows, :] += _dot((cs * jax.nn.sigmoid(cs)).astype(BF16), w)


def _mod_vectors(c_ctx, c, w_mod, b_mod):
    width = w_mod.shape[1]
    assert 1 + c.shape[0] <= MOD_ROWS
    return pl.pallas_call(
        _mod_kernel,
        grid=(D_MODEL // MOD_TILE,),
        in_specs=[pl.BlockSpec((1, MOD_TILE), lambda j: (0, j)),
                  pl.BlockSpec((c.shape[0], MOD_TILE), lambda j: (0, j)),
                  pl.BlockSpec((MOD_TILE, width), lambda j: (j, 0)),
                  _full(b_mod.shape)],
        out_specs=_full((MOD_ROWS, width)),
        out_shape=jax.ShapeDtypeStruct((MOD_ROWS, width), F32),
        compiler_params=_params(),
        name="mod_vectors",
    )(c_ctx, c, w_mod, b_mod)


def _head_mean_sq(x, bd):
    hi, lo = _split_bf16(x * x)
    return _dot(hi, bd) + _dot(lo, bd)


def _rope_lanes(x, cos, sin_signed, hi_half):
    partner = jnp.where(hi_half, pltpu.roll(x, 16, axis=1), pltpu.roll(x, LANES - 16, axis=1))
    return x * cos + partner * sin_signed


def _modulated(x, m):
    return (_ln_plain(x) * (1.0 + m[:, D_MODEL:2 * D_MODEL]) + m[:, 0:D_MODEL]).astype(BF16)


def _qkv(h, win_ref, bd_ref, qg_ref, kg_ref, rope):
    n = h.shape[0]
    bd = bd_ref[...]
    q = _dot(h, win_ref[:, 0:ATTN_WIDTH])
    yield
    q = q * lax.rsqrt(_head_mean_sq(q, bd) + NORM_EPS) * jnp.concatenate([qg_ref[...]] * N_Q_HEADS, axis=1)
    yield
    k = _dot(h, win_ref[:, ATTN_WIDTH:ATTN_WIDTH + KV_WIDTH])
    k = k * lax.rsqrt(_head_mean_sq(k, bd[:KV_WIDTH, :KV_WIDTH]) + NORM_EPS) \
        * jnp.concatenate([kg_ref[...]] * N_KV_HEADS, axis=1)
    v = _dot(h, win_ref[:, ATTN_WIDTH + KV_WIDTH:ATTN_WIDTH + 2 * KV_WIDTH])
    yield
    if rope is not None:
        cos, sin_signed = rope
        lane = lax.broadcasted_iota(jnp.int32, (n, LANES), 1)
        hi_half = (lane & 16) != 0
        q = jnp.concatenate(
            [_rope_lanes(q[:, j * LANES:(j + 1) * LANES], cos, sin_signed, hi_half)
             for j in range(ATTN_WIDTH // LANES)], axis=1)
        k = _rope_lanes(k, cos, sin_signed, hi_half)
    return (q * Q_SCALE).astype(BF16), k, v


def _drive(*streams, order=()):
    results = [None] * len(streams)
    live = list(range(len(streams)))
    plan = [j for j in order]
    while live:
        idx = plan.pop(0) if plan else live[0]
        if idx not in live:
            continue
        if not plan:
            live.append(live.pop(0))
        try:
            next(streams[idx])
        except StopIteration as stop:
            results[idx] = stop.value
            live.remove(idx)
    return results


def _attend(q, k, v, lookahead):
    def scores(hq):
        kv = hq // GROUP
        return _dot_nt(q[:, hq * HEAD_DIM:(hq + 1) * HEAD_DIM], k[:, kv * HEAD_DIM:(kv + 1) * HEAD_DIM])

    outs = []
    s_next = scores(0) if lookahead else None
    for hq in range(N_Q_HEADS):
        if lookahead:
            s, s_next = s_next, (scores(hq + 1) if hq + 1 < N_Q_HEADS else None)
        else:
            s = scores(hq)
        kv = hq // GROUP
        e = jnp.exp2(s - jnp.max(s, axis=-1, keepdims=True))
        denom = jnp.sum(e, axis=-1, keepdims=True)
        pv = _dot(e.astype(BF16), v[:, kv * HEAD_DIM:(kv + 1) * HEAD_DIM])
        outs.append(pv / denom)
        yield
    return jnp.concatenate(outs, axis=1).astype(BF16)


def _rg_inputs(h, win_ref):
    rg_lo = ATTN_WIDTH + 2 * KV_WIDTH
    xr = _dot(h, win_ref[:, rg_lo:rg_lo + RG_WIDTH])
    yield
    gr = _dot(h, win_ref[:, rg_lo + RG_WIDTH:IN_WIDTH])
    yield
    return xr, gr


def _rglru_rows(xr, gr, cw_ref, cb_ref, wg_ref, bg_ref, lam_ref, h0, scan_refs):
    af_ref, bf_ref, ab_ref, bb_ref = scan_refs
    n = xr.shape[0]

    t_idx = lax.broadcasted_iota(jnp.int32, (n, 1), 0)
    cw = cw_ref[...]
    xc = jnp.where(t_idx >= 2, pltpu.roll(xr, 2, axis=0), 0.0) * cw[0:1, :]
    xc = xc + jnp.where(t_idx >= 1, pltpu.roll(xr, 1, axis=0), 0.0) * cw[1:2, :]
    xc = xc + xr * cw[2:3, :]
    xc = xc + jnp.where(t_idx < n - 1, pltpu.roll(xr, n - 1, axis=0), 0.0) * cw[3:4, :]
    xc = xc + cb_ref[...]
    yield

    xcb = xc.astype(BF16)
    halves = [_dot(xcb[:, c * MXU_DIM:(c + 1) * MXU_DIM], wg_ref[c])
              for c in range(RG_WIDTH // MXU_DIM)]

    def gate_pre(idx):
        return jnp.concatenate([hv[:, idx * MXU_DIM:(idx + 1) * MXU_DIM] for hv in halves], axis=1) \
            + bg_ref[idx:idx + 1, :]

    yield
    neg = -lam_ref[...]
    softplus = jnp.maximum(neg, 0.0) + jnp.log1p(jnp.exp(-jnp.abs(neg)))
    decay = (-0.5 * C_LRU) * softplus
    half_xc = 0.5 * xc

    def coeffs(d):
        r2 = jnp.tanh(gate_pre(2 * d)) + 1.0
        i2 = jnp.tanh(gate_pre(2 * d + 1)) + 1.0
        log_a = r2 * decay[d:d + 1, :]
        a = jnp.exp(log_a)
        bx = jnp.sqrt(jnp.tanh(-log_a) * (a * a + 1.0)) * (i2 * half_xc)
        return a, bx

    groups = n // SUBLANES
    rmod = lax.broadcasted_iota(jnp.int32, (1, SUBLANES, 1), 1)

    def tile_scan(d, forward):
        a, bx = coeffs(d)
        yield
        a = a.reshape(groups, SUBLANES, RG_WIDTH)
        bx = bx.reshape(groups, SUBLANES, RG_WIDTH)
        for s in (1, 2, 4):
            ok = (rmod >= s) if forward else (rmod < SUBLANES - s)
            shift = s if forward else SUBLANES - s
            a_sh = jnp.where(ok, pltpu.roll(a, shift, axis=1), 1.0)
            b_sh = jnp.where(ok, pltpu.roll(bx, shift, axis=1), 0.0)
            bx = a * b_sh + bx
            a = a * a_sh
            yield
        return a.reshape(n, RG_WIDTH), bx.reshape(n, RG_WIDTH)

    af_ref[...], bf_ref[...] = yield from tile_scan(0, True)
    ab_ref[...], bb_ref[...] = yield from tile_scan(1, False)

    def step(g, carry):
        hf, hb = carry
        rf = pl.multiple_of(g * SUBLANES, SUBLANES)
        rb = pl.multiple_of((groups - 1 - g) * SUBLANES, SUBLANES)
        new_f = af_ref[pl.ds(rf, SUBLANES), :] * hf + bf_ref[pl.ds(rf, SUBLANES), :]
        new_b = ab_ref[pl.ds(rb, SUBLANES), :] * hb + bb_ref[pl.ds(rb, SUBLANES), :]
        bf_ref[pl.ds(rf, SUBLANES), :] = new_f
        bb_ref[pl.ds(rb, SUBLANES), :] = new_b
        return (jnp.broadcast_to(new_f[SUBLANES - 1:SUBLANES, :], (SUBLANES, RG_WIDTH)),
                jnp.broadcast_to(new_b[0:1, :], (SUBLANES, RG_WIDTH)))

    hf, hb = lax.fori_loop(
        0, groups, step,
        (jnp.broadcast_to(h0[0:1, :], (SUBLANES, RG_WIDTH)), jnp.broadcast_to(h0[1:2, :], (SUBLANES, RG_WIDTH))),
        unroll=True)
    yield

    rg = ((bf_ref[...] + bb_ref[...]) * jax.nn.gelu(gr)).astype(BF16)
    return rg, hf[0:1, :], hb[0:1, :]


def _segment_permutation():
    p = np.zeros((PERM_BLOCK, PERM_BLOCK), np.float32)
    for t in range(SEG_LEN):
        for j in range(SUBLANES):
            p[t * SUBLANES + j, j * SEG_LEN + t] = 1.0
    return p


def _rglru_segments(xr_ref, gr_ref, perm_t_ref, cw_ref, cb_ref, wg_ref, bg_ref, lam_ref, h0, scan_refs, ready):
    af_ref, bf_ref, ab_ref, bb_ref = scan_refs
    n = xr_ref.shape[0]
    blocks = n // PERM_BLOCK
    sub = lax.broadcasted_iota(jnp.int32, (SUBLANES, 1), 0)
    zero_row = jnp.zeros((1, RG_WIDTH), F32)
    cw = cw_ref[...]
    neg = -lam_ref[...]
    softplus = jnp.maximum(neg, 0.0) + jnp.log1p(jnp.exp(-jnp.abs(neg)))
    decay = (-0.5 * C_LRU) * softplus

    def group(b, t):
        return xr_ref[b * PERM_BLOCK + t * SUBLANES:b * PERM_BLOCK + (t + 1) * SUBLANES, :]

    def before(b, t):
        wrap = group(b - 1, t)[SUBLANES - 1:SUBLANES, :] if b > 0 else zero_row
        return jnp.where(sub == 0, wrap, pltpu.roll(group(b, t), 1, axis=0))

    def after(b, t):
        wrap = group(b + 1, t)[0:1, :] if b + 1 < blocks else zero_row
        return jnp.where(sub == SUBLANES - 1, wrap, pltpu.roll(group(b, t), SUBLANES - 1, axis=0))

    ends = []
    for b in range(blocks):
        while not ready(min(b + 1, blocks - 1)):
            yield
        rows = slice(b * PERM_BLOCK, (b + 1) * PERM_BLOCK)
        ext = jnp.concatenate([before(b, SEG_LEN - 2), before(b, SEG_LEN - 1), xr_ref[rows, :], after(b, 0)], axis=0)
        xc = ext[0:PERM_BLOCK] * cw[0:1, :]
        for tap in range(1, CONV_W):
            xc = xc + ext[tap * SUBLANES:tap * SUBLANES + PERM_BLOCK] * cw[tap:tap + 1, :]
        xc = xc + cb_ref[...]
        yield

        xcb = xc.astype(BF16)
        halves = [_dot(xcb[:, c * MXU_DIM:(c + 1) * MXU_DIM], wg_ref[c])
                  for c in range(RG_WIDTH // MXU_DIM)]
        yield
        half_xc = 0.5 * xc
        for d, (a_ref, b_ref) in enumerate(((af_ref, bf_ref), (ab_ref, bb_ref))):
            pre = [jnp.concatenate([hv[:, i * MXU_DIM:(i + 1) * MXU_DIM] for hv in halves], axis=1)
                   + bg_ref[i:i + 1, :] for i in (2 * d, 2 * d + 1)]
            r2 = jnp.tanh(pre[0]) + 1.0
            i2 = jnp.tanh(pre[1]) + 1.0
            log_a = r2 * decay[d:d + 1, :]
            a = jnp.exp(log_a)
            a_ref[rows, :] = a
            b_ref[rows, :] = jnp.sqrt(jnp.tanh(-log_a) * (a * a + 1.0)) * (i2 * half_xc)
            yield

        for a_ref, b_ref, steps in ((af_ref, bf_ref, range(SEG_LEN)), (ab_ref, bb_ref, range(SEG_LEN - 1, -1, -1))):
            hend = jnp.zeros((SUBLANES, RG_WIDTH), F32)
            pend = jnp.ones((SUBLANES, RG_WIDTH), F32)
            for t in steps:
                r = slice(b * PERM_BLOCK + t * SUBLANES, b * PERM_BLOCK + (t + 1) * SUBLANES)
                a = a_ref[r, :]
                hend = a * hend + b_ref[r, :]
                pend = a * pend
            ends.append((hend, pend))
            yield

    def carries(order, state, which):
        into = {}
        for b, j in order:
            into[b, j] = state
            hend, pend = ends[2 * b + which]
            state = hend[j:j + 1, :] + pend[j:j + 1, :] * state
        return into, state

    segs = [(b, j) for b in range(blocks) for j in range(SUBLANES)]
    into_f, hf = carries(segs, h0[0:1, :], 0)
    into_b, hb = carries(segs[::-1], h0[1:2, :], 1)
    yield

    for b in range(blocks):
        for a_ref, b_ref, into, steps in ((af_ref, bf_ref, into_f, range(SEG_LEN)),
                                          (ab_ref, bb_ref, into_b, range(SEG_LEN - 1, -1, -1))):
            h = jnp.concatenate([into[b, j] for j in range(SUBLANES)], axis=0)
            for t in steps:
                r = slice(b * PERM_BLOCK + t * SUBLANES, b * PERM_BLOCK + (t + 1) * SUBLANES)
                h = a_ref[r, :] * h + b_ref[r, :]
                b_ref[r, :] = h
            yield

    out = []
    for b in range(blocks):
        rows = slice(b * PERM_BLOCK, (b + 1) * PERM_BLOCK)
        rg = ((bf_ref[rows, :] + bb_ref[rows, :]) * jax.nn.gelu(gr_ref[rows, :])).astype(BF16)
        out.append(_dot(perm_t_ref[...], rg).astype(BF16))
        yield
    return jnp.concatenate(out, axis=0), hf, hb


def _out_and_router(attn, rg, x, m, wout_ref, g1_ref, b1_ref, wrt_ref):
    gate1 = m[:, 2 * D_MODEL:3 * D_MODEL]
    shift2 = m[:, 3 * D_MODEL:4 * D_MODEL]
    scale2 = m[:, 4 * D_MODEL:5 * D_MODEL]
    mix = _dot(jnp.concatenate([attn, rg], axis=1), wout_ref[...])
    yield
    x1 = _ln_plain(ALPHA * x + gate1 * mix) * g1_ref[...] + b1_ref[...]
    yield
    h2 = (_ln_plain(x1) * (1.0 + scale2) + shift2).astype(BF16)
    yield
    logits = _dot_nt(wrt_ref[...].astype(BF16), h2)
    e = jnp.exp(logits - jnp.max(logits, axis=0, keepdims=True))
    return x1, h2, e / jnp.sum(e, axis=0, keepdims=True)


def _mixer_kernel(x_ref, mod_ref, win_ref, wout_ref, bd_ref, qg_ref, kg_ref, cw_ref, cb_ref, wg_ref, bg_ref,
                  lam_ref, g1_ref, b1_ref, wrt_ref,
                  nk_ref, nv_ref, nh_ref, x1_ref, h2_ref, afft_ref,
                  xprev_ref, xr_ref, gr_ref, attn_ref, *scan_refs):
    @pl.when(pl.program_id(0) == 0)
    def _():
        xprev_ref[...] = jnp.zeros_like(xprev_ref)
        xr_ref[...] = jnp.zeros_like(xr_ref)
        gr_ref[...] = jnp.zeros_like(gr_ref)
        attn_ref[...] = jnp.zeros_like(attn_ref)

    m = mod_ref[0:1, :]

    def second_half():
        h0 = jnp.zeros((2, RG_WIDTH), F32)
        rg, hf, hb = yield from _rglru_rows(xr_ref[...], gr_ref[...], cw_ref, cb_ref, wg_ref, bg_ref, lam_ref, h0,
                                            scan_refs)
        nh_ref[0] = jnp.concatenate([hf, hb], axis=0)
        x1_ref[...], h2_ref[...], afft_ref[...] = yield from _out_and_router(
            attn_ref[...], rg, xprev_ref[...], m, wout_ref, g1_ref, b1_ref, wrt_ref)

    def first_half():
        x = x_ref[...]
        h = _modulated(x, m)
        yield
        q, k, v = yield from _qkv(h, win_ref, bd_ref, qg_ref, kg_ref, None)
        nk_ref[0] = k.T
        nv_ref[0] = v.T
        yield
        xr, gr = yield from _rg_inputs(h, win_ref)
        attn = yield from _attend(q, k.astype(BF16), v.astype(BF16), False)
        return x, attn, xr, gr

    (x, attn, xr, gr), _ = _drive(first_half(), second_half(), order=MIXER_ORDER)
    xprev_ref[...] = x
    attn_ref[...] = attn
    xr_ref[...] = xr
    gr_ref[...] = gr


def _mixer_prompt(x2d, mod, win_bf, wout_bf, consts, g1, b1, wrt, *, n, nb):
    tokens = nb * n
    first = lambda w: pl.BlockSpec((n, w), lambda i: (jnp.minimum(i, nb - 1), 0))
    second = lambda w: pl.BlockSpec((n, w), lambda i: (jnp.maximum(i - 1, 0), 0))
    small = list(consts) + [g1, b1, wrt]
    return pl.pallas_call(
        _mixer_kernel,
        grid=(nb + 1,),
        in_specs=[first(D_MODEL), _full(mod.shape), _full(win_bf.shape), _full(wout_bf.shape)]
                 + [_full(a.shape) for a in small],
        out_specs=[pl.BlockSpec((1, KV_WIDTH, n), lambda i: (jnp.minimum(i, nb - 1), 0, 0)),
                   pl.BlockSpec((1, KV_WIDTH, n), lambda i: (jnp.minimum(i, nb - 1), 0, 0)),
                   pl.BlockSpec((1, 2, RG_WIDTH), lambda i: (jnp.maximum(i - 1, 0), 0, 0)),
                   second(D_MODEL), second(D_MODEL),
                   pl.BlockSpec((N_EXPERTS, n), lambda i: (jnp.maximum(i - 1, 0), 0))],
        out_shape=[jax.ShapeDtypeStruct((nb, KV_WIDTH, n), F32), jax.ShapeDtypeStruct((nb, KV_WIDTH, n), F32),
                   jax.ShapeDtypeStruct((nb, 2, RG_WIDTH), F32),
                   jax.ShapeDtypeStruct((tokens, D_MODEL), F32), jax.ShapeDtypeStruct((tokens, D_MODEL), BF16),
                   jax.ShapeDtypeStruct((nb * N_EXPERTS, n), F32)],
        scratch_shapes=[pltpu.VMEM((n, D_MODEL), F32), pltpu.VMEM((n, RG_WIDTH), F32),
                        pltpu.VMEM((n, RG_WIDTH), F32), pltpu.VMEM((n, ATTN_WIDTH), BF16)]
                       + [pltpu.VMEM((n, RG_WIDTH), F32)] * 4,
        compiler_params=_params(),
        name="mixer_prompt",
    )(x2d, mod, win_bf, wout_bf, *small)


def _front_kernel(x_ref, mod_ref, win_ref, bd_ref, qg_ref, kg_ref, cw_ref, cb_ref, wg_ref, bg_ref, lam_ref,
                  perm_ref, perm_t_ref, h0_ref, cos_ref, sin_ref,
                  q_ref, k_ref, v_ref, rg_ref, xr_ref, gr_ref, *scan_refs):
    m = mod_ref[pl.ds(1 + pl.program_id(0), 1), :]
    rows = PERM_BLOCK
    streams = x_ref.shape[0] // rows

    def row_block(j):
        r = pl.ds(j * rows, rows)
        h = _modulated(x_ref[r, :], m)
        yield
        q, k, v = yield from _qkv(h, win_ref, bd_ref, qg_ref, kg_ref, (cos_ref[r, :], sin_ref[r, :]))
        q_ref[r, :] = q
        k_ref[r, :] = k.astype(BF16)
        v_ref[r, :] = v.astype(BF16)
        yield
        hp = _dot(perm_ref[...], h).astype(BF16)
        xr_ref[r, :], gr_ref[r, :] = yield from _rg_inputs(hp, win_ref)
        traced[j] = True

    traced = [False] * streams
    recurrent = _rglru_segments(xr_ref, gr_ref, perm_t_ref, cw_ref, cb_ref, wg_ref, bg_ref, lam_ref, h0_ref[0],
                                scan_refs, lambda block: traced[block])
    chunks = 8
    order = [0] * chunks + [1] * chunks + [s for j in range(2, streams) for _ in range(chunks) for s in (j, streams)]
    results = _drive(*[row_block(j) for j in range(streams)], recurrent, order=order)
    rg_ref[...], _, _ = results[-1]


def _front_sample(x2d, mod, win_bf, consts, h0, rope, *, n, nb):
    tokens = nb * n
    seq = lambda w: pl.BlockSpec((n, w), lambda b: (b, 0))
    small = list(consts)
    return pl.pallas_call(
        _front_kernel,
        grid=(nb,),
        in_specs=[seq(D_MODEL), _full(mod.shape), _full(win_bf.shape)] + [_full(a.shape) for a in small]
                 + [pl.BlockSpec((1, 2, RG_WIDTH), lambda b: (b, 0, 0)), _full(rope[0].shape), _full(rope[1].shape)],
        out_specs=[seq(ATTN_WIDTH), seq(KV_WIDTH), seq(KV_WIDTH), seq(RG_WIDTH)],
        out_shape=[jax.ShapeDtypeStruct((tokens, ATTN_WIDTH), BF16), jax.ShapeDtypeStruct((tokens, KV_WIDTH), BF16),
                   jax.ShapeDtypeStruct((tokens, KV_WIDTH), BF16), jax.ShapeDtypeStruct((tokens, RG_WIDTH), BF16)],
        scratch_shapes=[pltpu.VMEM((n, RG_WIDTH), F32)] * 6,
        compiler_params=_params(),
        name="front_sample",
    )(x2d, mod, win_bf, *small, h0, *rope)


def _attn_post_kernel(q_ref, k_ref, v_ref, kc_ref, vc_ref, rg_ref, x_ref, mod_ref, wout_ref, g1_ref, b1_ref, wrt_ref,
                      x1_ref, h2_ref, afft_ref):
    m = mod_ref[pl.ds(1 + pl.program_id(0), 1), :]
    k = jnp.concatenate([k_ref[...], kc_ref[0].T.astype(BF16)], axis=0)
    v = jnp.concatenate([v_ref[...], vc_ref[0].T.astype(BF16)], axis=0)
    attn, = _drive(_attend(q_ref[...], k, v, True))
    rows = attn.shape[0] // TAIL_ROW_STREAMS

    def row_block(j):
        r = slice(j * rows, (j + 1) * rows)
        x1_ref[r, :], h2_ref[r, :], afft_ref[:, r] = yield from _out_and_router(
            attn[r, :], rg_ref[r, :], x_ref[r, :], m, wout_ref, g1_ref, b1_ref, wrt_ref)

    _drive(*[row_block(j) for j in range(TAIL_ROW_STREAMS)], order=(0,))


def _attn_post_sample(q, k, v, cache_k, cache_v, rg, x2d, mod, wout_bf, g1, b1, wrt, *, n, nb, past):
    tq = min(n, ATTN_Q_TILE)
    tiles = n // tq
    tokens = nb * n
    tile = lambda w: pl.BlockSpec((tq, w), lambda b, t: (b * tiles + t, 0))
    whole = pl.BlockSpec((n, KV_WIDTH), lambda b, t: (b, 0))
    cached = pl.BlockSpec((1, KV_WIDTH, past), lambda b, t: (b, 0, 0))
    return pl.pallas_call(
        _attn_post_kernel,
        grid=(nb, tiles),
        in_specs=[tile(ATTN_WIDTH), whole, whole, cached, cached, tile(RG_WIDTH), tile(D_MODEL),
                  _full(mod.shape), _full(wout_bf.shape), _full(g1.shape), _full(b1.shape), _full(wrt.shape)],
        out_specs=[tile(D_MODEL), tile(D_MODEL), pl.BlockSpec((N_EXPERTS, tq), lambda b, t: (b, t))],
        out_shape=[jax.ShapeDtypeStruct((tokens, D_MODEL), F32), jax.ShapeDtypeStruct((tokens, D_MODEL), BF16),
                   jax.ShapeDtypeStruct((nb * N_EXPERTS, n), F32)],
        compiler_params=_params(2),
        name="attn_post_sample",
    )(q, k, v, cache_k, cache_v, rg, x2d, mod, wout_bf, g1, b1, wrt)


def _slot_onehot(mask, pos, cap):
    n = mask.shape[1]
    slot = lax.broadcasted_iota(jnp.int32, (cap, n), 0).astype(F32)
    return [(slot == pos[e:e + 1, :]) & (mask[e:e + 1, :] > 0.5) for e in range(mask.shape[0])]


def _route_kernel(afft_ref, h2_ref, xs_ref, mask_ref, pos_ref, *, n, cap, seqs):
    b = pl.program_id(0)

    @pl.when(b == 0)
    def _():
        aff = afft_ref[...]
        thr = jnp.zeros((aff.shape[0], 1), jnp.int32)
        for bit in range(30, -1, -1):
            cand = thr | (1 << bit)
            cnt = jnp.sum((aff >= lax.bitcast_convert_type(cand, F32)).astype(F32), axis=1, keepdims=True)
            thr = jnp.where(cnt >= cap, cand, thr)
        above = aff >= lax.bitcast_convert_type(thr + 1, F32)
        tied = (aff >= lax.bitcast_convert_type(thr, F32)) & jnp.logical_not(above)
        need = cap - jnp.sum(above.astype(F32), axis=1, keepdims=True)
        before = (lax.broadcasted_iota(jnp.int32, (n, n), 0)
                  < lax.broadcasted_iota(jnp.int32, (n, n), 1)).astype(BF16)
        tie_rank = _dot(tied.astype(BF16), before)
        mask = (above | (tied & (tie_rank < need))).astype(F32)
        mask_ref[...] = mask
        pos_ref[...] = _dot(mask.astype(BF16), before)

    for j in range(seqs):
        r0 = pl.multiple_of((b * seqs + j) * N_EXPERTS, N_EXPERTS)
        onehots = _slot_onehot(mask_ref[pl.ds(r0, N_EXPERTS), :], pos_ref[pl.ds(r0, N_EXPERTS), :], cap)
        sel = jnp.concatenate(onehots, axis=0).astype(BF16)
        xs = _dot(sel, h2_ref[j * n:(j + 1) * n, :]).astype(BF16)
        xs_ref[:, j * cap:(j + 1) * cap, :] = xs.reshape(N_EXPERTS, cap, D_MODEL)


def _route(afft, h2, *, n, nb, cap, seqs):
    rows = nb * N_EXPERTS
    return pl.pallas_call(
        functools.partial(_route_kernel, n=n, cap=cap, seqs=seqs),
        grid=(nb // seqs,),
        in_specs=[_full(afft.shape), pl.BlockSpec((seqs * n, D_MODEL), lambda b: (b, 0))],
        out_specs=[pl.BlockSpec((N_EXPERTS, seqs * cap, D_MODEL), lambda b: (0, b, 0)),
                   _full((rows, n)), _full((rows, n))],
        out_shape=[jax.ShapeDtypeStruct((N_EXPERTS, nb * cap, D_MODEL), BF16),
                   jax.ShapeDtypeStruct((rows, n), F32), jax.ShapeDtypeStruct((rows, n), F32)],
        compiler_params=_params(),
        name="route_n%d" % n,
    )(afft, h2)


def _slot_gates(aff_ref, mask_ref, pos_ref, e, cap):
    cols = []
    for b in range(aff_ref.shape[0] // N_EXPERTS):
        r = b * N_EXPERTS + e
        onehot, = _slot_onehot(mask_ref[pl.ds(r, 1), :], pos_ref[pl.ds(r, 1), :], cap)
        cols.append(jnp.sum(jnp.where(onehot, aff_ref[pl.ds(r, 1), :], 0.0), axis=1, keepdims=True))
    return jnp.concatenate(cols, axis=0)


def _ffn_kernel(xp_ref, xs_ref, ap_ref, mp_ref, pp_ref, as_ref, ms_ref, ps_ref, wgu_ref, wd_ref, yp_ref, ys_ref):
    e = pl.program_id(0)
    rows_p = xp_ref.shape[1]
    xs = jnp.concatenate([xp_ref[0], xs_ref[0]], axis=0)
    gu = _dot(xs, wgu_ref[0].astype(BF16))
    gate = gu[:, :D_EXPERT]
    up = gu[:, D_EXPERT:]
    act = (gate * jax.nn.sigmoid(gate) * up).astype(BF16)
    y = _dot(act, wd_ref[0].astype(BF16))
    g = jnp.concatenate([_slot_gates(ap_ref, mp_ref, pp_ref, e, rows_p * N_EXPERTS // ap_ref.shape[0]),
                         _slot_gates(as_ref, ms_ref, ps_ref, e, xs_ref.shape[1] * N_EXPERTS // as_ref.shape[0])],
                        axis=0)
    y = (y * g).astype(BF16)
    yp_ref[0] = y[:rows_p]
    ys_ref[0] = y[rows_p:]


def _ffn(xs_p, xs_s, route_p, route_s, w_gate_up, w_down):
    per_e = lambda a: pl.BlockSpec((1,) + a.shape[1:], lambda e: (e, 0, 0))
    tables = list(route_p) + list(route_s)
    return pl.pallas_call(
        _ffn_kernel,
        grid=(N_EXPERTS,),
        in_specs=[per_e(xs_p), per_e(xs_s)] + [_full(t.shape) for t in tables] + [per_e(w_gate_up), per_e(w_down)],
        out_specs=[per_e(xs_p), per_e(xs_s)],
        out_shape=[jax.ShapeDtypeStruct(xs_p.shape, BF16), jax.ShapeDtypeStruct(xs_s.shape, BF16)],
        compiler_params=_params(),
        name="expert_ffn",
    )(xs_p, xs_s, *tables, w_gate_up, w_down)


def _combine_kernel(y_ref, mask_ref, pos_ref, x1_ref, mod_ref, g2_ref, b2_ref, o_ref, *, cap, seqs, mod_row0,
                    mod_row_step):
    row = mod_row0 + mod_row_step * pl.program_id(0)
    gate2 = mod_ref[pl.ds(row, 1), 5 * D_MODEL:6 * D_MODEL]
    tn = x1_ref.shape[0] // seqs
    for j in range(seqs):
        e0 = j * N_EXPERTS
        sel = jnp.concatenate(
            _slot_onehot(mask_ref[e0:e0 + N_EXPERTS, :], pos_ref[e0:e0 + N_EXPERTS, :], cap), axis=0).astype(BF16)
        y = y_ref[:, j * cap:(j + 1) * cap, :].reshape(N_EXPERTS * cap, D_MODEL)
        ff = _dot_tn(sel, y)
        r = pl.ds(j * tn, tn)
        o_ref[r, :] = _ln_plain(ALPHA * x1_ref[r, :] + gate2 * ff) * g2_ref[...] + b2_ref[...]


def _combine(y, mask, pos, x1, mod, g2, b2, *, n, nb, cap, seqs, sample):
    tn = min(n, COMBINE_TILE)
    tiles = n // tn
    assert seqs == 1 or tiles == 1
    return pl.pallas_call(
        functools.partial(_combine_kernel, cap=cap, seqs=seqs, mod_row0=1 if sample else 0,
                          mod_row_step=1 if sample else 0),
        grid=(nb // seqs, tiles),
        in_specs=[pl.BlockSpec((N_EXPERTS, seqs * cap, D_MODEL), lambda b, t: (0, b, 0)),
                  pl.BlockSpec((seqs * N_EXPERTS, tn), lambda b, t: (b, t)),
                  pl.BlockSpec((seqs * N_EXPERTS, tn), lambda b, t: (b, t)),
                  pl.BlockSpec((seqs * tn, D_MODEL), lambda b, t: (b * tiles + t, 0)),
                  _full(mod.shape), _full(g2.shape), _full(b2.shape)],
        out_specs=pl.BlockSpec((seqs * tn, D_MODEL), lambda b, t: (b * tiles + t, 0)),
        out_shape=jax.ShapeDtypeStruct((nb * n, D_MODEL), F32),
        compiler_params=_params(2),
        name="combine_sample" if sample else "combine_prompt",
    )(y, mask, pos, x1, mod, g2, b2)


def _rope_tables(n):
    lane = np.arange(LANES)
    within = lane % HEAD_DIM
    freq = (within % 16).astype(np.float32)
    inv = np.float32(ROPE_THETA) ** (-freq / np.float32(16.0))
    tok = np.arange(n)
    pos = np.where((within < HEAD_DIM // 2)[None, :], (tok // GRID_W)[:, None], (tok % GRID_W)[:, None])
    ang = pos.astype(np.float32) * inv[None, :]
    sign = np.where(within % 32 < 16, -1.0, 1.0).astype(np.float32)
    return jnp.asarray(np.cos(ang)), jnp.asarray(np.sin(ang) * sign[None, :])


def kernel(x_prompt, x_sample, cache_k, cache_v, state_h, c, c_ctx, w_mod, b_mod, w_in, q_norm_g, k_norm_g,
           conv_w, conv_b, w_rg_a, b_rg_a, w_rg_x, b_rg_x, rg_lambda, w_out, ln1_g, ln1_b, w_router,
           w_gate_up, w_down, ln2_g, ln2_b):
    assert w_mod.shape[0] == DEPTH == 1
    nb_p, n_p, _ = x_prompt.shape
    nb_s, n_s, _ = x_sample.shape
    past = cache_k.shape[2]
    cap_p = CAP_FACTOR * n_p // N_EXPERTS
    cap_s = CAP_FACTOR * n_s // N_EXPERTS

    row = lambda v: v.reshape(1, -1)
    head_avg = np.kron(np.eye(ATTN_WIDTH // HEAD_DIM, dtype=np.float32),
                       np.full((HEAD_DIM, HEAD_DIM), 1.0 / HEAD_DIM, np.float32))
    bd = jnp.asarray(head_avg, BF16)
    bg = 0.5 * jnp.stack([b_rg_a[0, 0], b_rg_x[0, 0], b_rg_a[0, 1], b_rg_x[0, 1]])
    perm = _segment_permutation()
    perms = (jnp.asarray(perm, BF16), jnp.asarray(perm.T, BF16))
    wrt = w_router[0].T
    g1, b1, g2, b2 = row(ln1_g[0]), row(ln1_b[0]), row(ln2_g[0]), row(ln2_b[0])

    win_bf, wout_bf, wg = _cast_weights(w_in[0], w_out[0], w_rg_a[0], w_rg_x[0])
    consts = (bd, row(q_norm_g[0]), row(k_norm_g[0]), conv_w[0], row(conv_b[0]), wg, bg, rg_lambda[0])
    mod = _mod_vectors(row(c_ctx), c, w_mod[0], row(b_mod[0]))

    xp = x_prompt.reshape(nb_p * n_p, D_MODEL)
    xs = x_sample.reshape(nb_s * n_s, D_MODEL)

    new_k, new_v, new_h, x1_p, h2_p, aff_p = _mixer_prompt(
        xp, mod, win_bf, wout_bf, consts, g1, b1, wrt, n=n_p, nb=nb_p)

    q_s, k_s, v_s, rg_s = _front_sample(xs, mod, win_bf, consts + perms, state_h[:, 0], _rope_tables(n_s),
                                        n=n_s, nb=nb_s)
    def cached(t):
        return jnp.transpose(t[:, 0], (0, 2, 3, 1)).reshape(nb_s, KV_WIDTH, past)

    x1_s, h2_s, aff_s = _attn_post_sample(
        q_s, k_s, v_s, cached(cache_k), cached(cache_v), rg_s, xs, mod, wout_bf, g1, b1, wrt,
        n=n_s, nb=nb_s, past=past)

    seqs_p = PROMPT_SEQS_PER_STEP if nb_p % PROMPT_SEQS_PER_STEP == 0 else 1
    xs_p, mask_p, pos_p = _route(aff_p, h2_p, n=n_p, nb=nb_p, cap=cap_p, seqs=seqs_p)
    xs_s, mask_s, pos_s = _route(aff_s, h2_s, n=n_s, nb=nb_s, cap=cap_s, seqs=1)

    y_p, y_s = _ffn(xs_p, xs_s, (aff_p, mask_p, pos_p), (aff_s, mask_s, pos_s), w_gate_up[0], w_down[0])

    out_p = _combine(y_p, mask_p, pos_p, x1_p, mod, g2, b2, n=n_p, nb=nb_p, cap=cap_p, seqs=seqs_p, sample=False)
    out_s = _combine(y_s, mask_s, pos_s, x1_s, mod, g2, b2, n=n_s, nb=nb_s, cap=cap_s, seqs=1, sample=True)

    def cache_layout(t):
        t = t.reshape(nb_p, DEPTH, N_KV_HEADS, HEAD_DIM, n_p)
        return jnp.transpose(t, (0, 1, 4, 2, 3))

    return (out_p.reshape(nb_p, n_p, D_MODEL), out_s.reshape(nb_s, n_s, D_MODEL),
            cache_layout(new_k), cache_layout(new_v), new_h.reshape(nb_p, DEPTH, 2, RG_WIDTH))
```

```python
import functools

import numpy as np
import jax
import jax.numpy as jnp
from jax import lax
from jax.experimental import pallas as pl
from jax.experimental.pallas import tpu as pltpu

F32 = jnp.float32
BF16 = jnp.bfloat16

D_MODEL = 1024
HEAD_DIM = 64
N_Q_HEADS = 8
N_KV_HEADS = 2
GROUP = N_Q_HEADS // N_KV_HEADS
ATTN_WIDTH = N_Q_HEADS * HEAD_DIM
KV_WIDTH = N_KV_HEADS * HEAD_DIM
RG_WIDTH = D_MODEL - ATTN_WIDTH
RG_BLOCK = 64
CONV_W = 4
C_LRU = 8.0
N_EXPERTS = 16
CAP_FACTOR = 2
D_EXPERT = 1024
GRID_W = 64
ROPE_THETA = 10000.0
ROPE_FREQS = HEAD_DIM // 4
NORM_EPS = 1e-6
DEPTH = 1
ALPHA = (2.0 * DEPTH) ** 0.25
IN_WIDTH = ATTN_WIDTH + 2 * KV_WIDTH + 2 * RG_WIDTH
Q_SCALE = HEAD_DIM ** -0.5 * float(np.log2(np.e))

LANES = 128
SUBLANES = 8
MXU_DIM = 256
VMEM_LIMIT_BYTES = 56 * 1024 * 1024

MOD_ROWS = SUBLANES
MOD_TILE = 256
ATTN_Q_TILE = 512
COMBINE_TILE = 512
MIXER_ORDER = (1, 0, 0, 1, 1, 0, 1, 1, 0, 1, 0, 1, 0, 1, 1, 0, 1, 1, 0, 0, 1, 0, 1, 0, 0, 0, 1, 0, 0, 1, 0)
PERM_BLOCK = MXU_DIM
SEG_LEN = PERM_BLOCK // SUBLANES
TAIL_ROW_STREAMS = 2
PROMPT_SEQS_PER_STEP = 4


def _params(n_axes=1):
    return pltpu.CompilerParams(dimension_semantics=("arbitrary",) * n_axes,
                                vmem_limit_bytes=VMEM_LIMIT_BYTES)


def _full(shape):
    zeros = (0,) * len(shape)
    return pl.BlockSpec(shape, lambda *_: zeros)


def _ln_plain(x):
    mu = jnp.mean(x, -1, keepdims=True)
    xc = x - mu
    var = jnp.mean(xc * xc, -1, keepdims=True)
    return xc * lax.rsqrt(var + NORM_EPS)


def _dot(a, b):
    return jnp.dot(a, b, preferred_element_type=F32)


def _dot_nt(a, b):
    return lax.dot_general(a, b, (((1,), (1,)), ((), ())), preferred_element_type=F32)


def _dot_tn(a, b):
    return lax.dot_general(a, b, (((0,), (0,)), ((), ())), preferred_element_type=F32)


def _split_bf16(x):
    hi = x.astype(BF16)
    lo = (x - hi.astype(F32)).astype(BF16)
    return hi, lo


def _cast_kernel(a_ref, b_ref, wa_ref, wx_ref, ao_ref, bo_ref, wg_ref):
    ao_ref[...] = a_ref[...].astype(BF16)
    bo_ref[...] = b_ref[...].astype(BF16)
    wg_ref[...] = jnp.zeros_like(wg_ref)
    per_tile = MXU_DIM // RG_BLOCK
    for c in range(RG_WIDTH // MXU_DIM):
        for g, (ref, d) in enumerate(((wa_ref, 0), (wx_ref, 0), (wa_ref, 1), (wx_ref, 1))):
            for i in range(per_tile):
                lo = RG_BLOCK * i
                wg_ref[c, lo:lo + RG_BLOCK, MXU_DIM * g + lo:MXU_DIM * g + lo + RG_BLOCK] = \
                    (0.5 * ref[d, per_tile * c + i]).astype(BF16)


def _cast_weights(w_in, w_out, w_rg_a, w_rg_x):
    wg_shape = (RG_WIDTH // MXU_DIM, MXU_DIM, 4 * MXU_DIM)
    ins = [w_in, w_out, w_rg_a, w_rg_x]
    outs = [jax.ShapeDtypeStruct(w_in.shape, BF16), jax.ShapeDtypeStruct(w_out.shape, BF16),
            jax.ShapeDtypeStruct(wg_shape, BF16)]
    return pl.pallas_call(
        _cast_kernel,
        grid=(1,),
        in_specs=[_full(a.shape) for a in ins],
        out_specs=[_full(o.shape) for o in outs],
        out_shape=outs,
        compiler_params=_params(),
        name="cast_weights",
    )(*ins)


def _mod_kernel(ctx_ref, c_ref, w_ref, b_ref, o_ref):
    @pl.when(pl.program_id(0) == 0)
    def _():
        o_ref[...] = jnp.broadcast_to(b_ref[...], o_ref.shape)

    w = w_ref[...].astype(BF16)
    n_c = c_ref.shape[0]
    for rows, ref in ((slice(0, 1), ctx_ref), (slice(1, 1 + n_c), c_ref)):
        cs = ref[...]
        o_ref[rows, :] += _dot((cs * jax.nn.sigmoid(cs)).astype(BF16), w)


def _mod_vectors(c_ctx, c, w_mod, b_mod):
    width = w_mod.shape[1]
    assert 1 + c.shape[0] <= MOD_ROWS
    return pl.pallas_call(
        _mod_kernel,
        grid=(D_MODEL // MOD_TILE,),
        in_specs=[pl.BlockSpec((1, MOD_TILE), lambda j: (0, j)),
                  pl.BlockSpec((c.shape[0], MOD_TILE), lambda j: (0, j)),
                  pl.BlockSpec((MOD_TILE, width), lambda j: (j, 0)),
                  _full(b_mod.shape)],
        out_specs=_full((MOD_ROWS, width)),
        out_shape=jax.ShapeDtypeStruct((MOD_ROWS, width), F32),
        compiler_params=_params(),
        name="mod_vectors",
    )(c_ctx, c, w_mod, b_mod)


def _head_mean_sq(x, bd):
    hi, lo = _split_bf16(x * x)
    return _dot(hi, bd) + _dot(lo, bd)


def _rope_lanes(x, cos, sin_signed, hi_half):
    partner = jnp.where(hi_half, pltpu.roll(x, ROPE_FREQS, axis=1), pltpu.roll(x, LANES - ROPE_FREQS, axis=1))
    return x * cos + partner * sin_signed


def _modulated(x, m):
    return (_ln_plain(x) * (1.0 + m[:, D_MODEL:2 * D_MODEL]) + m[:, 0:D_MODEL]).astype(BF16)


def _qkv(h, win_ref, bd_ref, qg_ref, kg_ref, rope):
    n = h.shape[0]
    bd = bd_ref[...]
    q = _dot(h, win_ref[:, 0:ATTN_WIDTH])
    yield
    q = q * lax.rsqrt(_head_mean_sq(q, bd) + NORM_EPS) * jnp.concatenate([qg_ref[...]] * N_Q_HEADS, axis=1)
    yield
    k = _dot(h, win_ref[:, ATTN_WIDTH:ATTN_WIDTH + KV_WIDTH])
    k = k * lax.rsqrt(_head_mean_sq(k, bd[:KV_WIDTH, :KV_WIDTH]) + NORM_EPS) \
        * jnp.concatenate([kg_ref[...]] * N_KV_HEADS, axis=1)
    v = _dot(h, win_ref[:, ATTN_WIDTH + KV_WIDTH:ATTN_WIDTH + 2 * KV_WIDTH])
    yield
    if rope is not None:
        cos, sin_signed = rope
        lane = lax.broadcasted_iota(jnp.int32, (n, LANES), 1)
        hi_half = (lane & ROPE_FREQS) != 0
        q = jnp.concatenate(
            [_rope_lanes(q[:, j * LANES:(j + 1) * LANES], cos, sin_signed, hi_half)
             for j in range(ATTN_WIDTH // LANES)], axis=1)
        k = _rope_lanes(k, cos, sin_signed, hi_half)
    return (q * Q_SCALE).astype(BF16), k, v


def _drive(*streams, order=()):
    results = [None] * len(streams)
    live = list(range(len(streams)))
    plan = [j for j in order]
    while live:
        idx = plan.pop(0) if plan else live[0]
        if idx not in live:
            continue
        if not plan:
            live.append(live.pop(0))
        try:
            next(streams[idx])
        except StopIteration as stop:
            results[idx] = stop.value
            live.remove(idx)
    return results


def _attend(q, k, v, lookahead):
    def scores(hq):
        kv = hq // GROUP
        return _dot_nt(q[:, hq * HEAD_DIM:(hq + 1) * HEAD_DIM], k[:, kv * HEAD_DIM:(kv + 1) * HEAD_DIM])

    outs = []
    s_next = scores(0) if lookahead else None
    for hq in range(N_Q_HEADS):
        if lookahead:
            s, s_next = s_next, (scores(hq + 1) if hq + 1 < N_Q_HEADS else None)
        else:
            s = scores(hq)
        kv = hq // GROUP
        e = jnp.exp2(s - jnp.max(s, axis=-1, keepdims=True))
        denom = jnp.sum(e, axis=-1, keepdims=True)
        pv = _dot(e.astype(BF16), v[:, kv * HEAD_DIM:(kv + 1) * HEAD_DIM])
        outs.append(pv / denom)
        yield
    return jnp.concatenate(outs, axis=1).astype(BF16)


def _rg_inputs(h, win_ref):
    rg_lo = ATTN_WIDTH + 2 * KV_WIDTH
    xr = _dot(h, win_ref[:, rg_lo:rg_lo + RG_WIDTH])
    yield
    gr = _dot(h, win_ref[:, rg_lo + RG_WIDTH:IN_WIDTH])
    yield
    return xr, gr


def _rglru_rows(xr, gr, cw_ref, cb_ref, wg_ref, bg_ref, lam_ref, h0, scan_refs):
    af_ref, bf_ref, ab_ref, bb_ref = scan_refs
    n = xr.shape[0]

    t_idx = lax.broadcasted_iota(jnp.int32, (n, 1), 0)
    cw = cw_ref[...]
    xc = jnp.where(t_idx >= 2, pltpu.roll(xr, 2, axis=0), 0.0) * cw[0:1, :]
    xc = xc + jnp.where(t_idx >= 1, pltpu.roll(xr, 1, axis=0), 0.0) * cw[1:2, :]
    xc = xc + xr * cw[2:3, :]
    xc = xc + jnp.where(t_idx < n - 1, pltpu.roll(xr, n - 1, axis=0), 0.0) * cw[3:4, :]
    xc = xc + cb_ref[...]
    yield

    xcb = xc.astype(BF16)
    halves = [_dot(xcb[:, c * MXU_DIM:(c + 1) * MXU_DIM], wg_ref[c])
              for c in range(RG_WIDTH // MXU_DIM)]

    def gate_pre(idx):
        return jnp.concatenate([hv[:, idx * MXU_DIM:(idx + 1) * MXU_DIM] for hv in halves], axis=1) \
            + bg_ref[idx:idx + 1, :]

    yield
    neg = -lam_ref[...]
    softplus = jnp.maximum(neg, 0.0) + jnp.log1p(jnp.exp(-jnp.abs(neg)))
    decay = (-0.5 * C_LRU) * softplus
    half_xc = 0.5 * xc

    def coeffs(d):
        r2 = jnp.tanh(gate_pre(2 * d)) + 1.0
        i2 = jnp.tanh(gate_pre(2 * d + 1)) + 1.0
        log_a = r2 * decay[d:d + 1, :]
        a = jnp.exp(log_a)
        bx = jnp.sqrt(jnp.tanh(-log_a) * (a * a + 1.0)) * (i2 * half_xc)
        return a, bx

    groups = n // SUBLANES
    rmod = lax.broadcasted_iota(jnp.int32, (1, SUBLANES, 1), 1)

    def tile_scan(d, forward):
        a, bx = coeffs(d)
        yield
        a = a.reshape(groups, SUBLANES, RG_WIDTH)
        bx = bx.reshape(groups, SUBLANES, RG_WIDTH)
        for s in (1, 2, 4):
            ok = (rmod >= s) if forward else (rmod < SUBLANES - s)
            shift = s if forward else SUBLANES - s
            a_sh = jnp.where(ok, pltpu.roll(a, shift, axis=1), 1.0)
            b_sh = jnp.where(ok, pltpu.roll(bx, shift, axis=1), 0.0)
            bx = a * b_sh + bx
            a = a * a_sh
            yield
        return a.reshape(n, RG_WIDTH), bx.reshape(n, RG_WIDTH)

    af_ref[...], bf_ref[...] = yield from tile_scan(0, True)
    ab_ref[...], bb_ref[...] = yield from tile_scan(1, False)

    def step(g, carry):
        hf, hb = carry
        rf = pl.multiple_of(g * SUBLANES, SUBLANES)
        rb = pl.multiple_of((groups - 1 - g) * SUBLANES, SUBLANES)
        new_f = af_ref[pl.ds(rf, SUBLANES), :] * hf + bf_ref[pl.ds(rf, SUBLANES), :]
        new_b = ab_ref[pl.ds(rb, SUBLANES), :] * hb + bb_ref[pl.ds(rb, SUBLANES), :]
        bf_ref[pl.ds(rf, SUBLANES), :] = new_f
        bb_ref[pl.ds(rb, SUBLANES), :] = new_b
        return (jnp.broadcast_to(new_f[SUBLANES - 1:SUBLANES, :], (SUBLANES, RG_WIDTH)),
                jnp.broadcast_to(new_b[0:1, :], (SUBLANES, RG_WIDTH)))

    hf, hb = lax.fori_loop(
        0, groups, step,
        (jnp.broadcast_to(h0[0:1, :], (SUBLANES, RG_WIDTH)), jnp.broadcast_to(h0[1:2, :], (SUBLANES, RG_WIDTH))),
        unroll=True)
    yield

    rg = ((bf_ref[...] + bb_ref[...]) * jax.nn.gelu(gr)).astype(BF16)
    return rg, hf[0:1, :], hb[0:1, :]


def _segment_permutation():
    p = np.zeros((PERM_BLOCK, PERM_BLOCK), np.float32)
    for t in range(SEG_LEN):
        for j in range(SUBLANES):
            p[t * SUBLANES + j, j * SEG_LEN + t] = 1.0
    return p


def _rglru_segments(xr_ref, gr_ref, perm_t_ref, cw_ref, cb_ref, wg_ref, bg_ref, lam_ref, h0, scan_refs, ready):
    af_ref, bf_ref, ab_ref, bb_ref = scan_refs
    n = xr_ref.shape[0]
    blocks = n // PERM_BLOCK
    sub = lax.broadcasted_iota(jnp.int32, (SUBLANES, 1), 0)
    zero_row = jnp.zeros((1, RG_WIDTH), F32)
    cw = cw_ref[...]
    neg = -lam_ref[...]
    softplus = jnp.maximum(neg, 0.0) + jnp.log1p(jnp.exp(-jnp.abs(neg)))
    decay = (-0.5 * C_LRU) * softplus

    def group(b, t):
        return xr_ref[b * PERM_BLOCK + t * SUBLANES:b * PERM_BLOCK + (t + 1) * SUBLANES, :]

    def before(b, t):
        wrap = group(b - 1, t)[SUBLANES - 1:SUBLANES, :] if b > 0 else zero_row
        return jnp.where(sub == 0, wrap, pltpu.roll(group(b, t), 1, axis=0))

    def after(b, t):
        wrap = group(b + 1, t)[0:1, :] if b + 1 < blocks else zero_row
        return jnp.where(sub == SUBLANES - 1, wrap, pltpu.roll(group(b, t), SUBLANES - 1, axis=0))

    ends = []
    for b in range(blocks):
        while not ready(min(b + 1, blocks - 1)):
            yield
        rows = slice(b * PERM_BLOCK, (b + 1) * PERM_BLOCK)
        ext = jnp.concatenate([before(b, SEG_LEN - 2), before(b, SEG_LEN - 1), xr_ref[rows, :], after(b, 0)], axis=0)
        xc = ext[0:PERM_BLOCK] * cw[0:1, :]
        for tap in range(1, CONV_W):
            xc = xc + ext[tap * SUBLANES:tap * SUBLANES + PERM_BLOCK] * cw[tap:tap + 1, :]
        xc = xc + cb_ref[...]
        yield

        xcb = xc.astype(BF16)
        halves = [_dot(xcb[:, c * MXU_DIM:(c + 1) * MXU_DIM], wg_ref[c])
                  for c in range(RG_WIDTH // MXU_DIM)]
        yield
        half_xc = 0.5 * xc
        for d, (a_ref, b_ref) in enumerate(((af_ref, bf_ref), (ab_ref, bb_ref))):
            pre = [jnp.concatenate([hv[:, i * MXU_DIM:(i + 1) * MXU_DIM] for hv in halves], axis=1)
                   + bg_ref[i:i + 1, :] for i in (2 * d, 2 * d + 1)]
            r2 = jnp.tanh(pre[0]) + 1.0
            i2 = jnp.tanh(pre[1]) + 1.0
            log_a = r2 * decay[d:d + 1, :]
            a = jnp.exp(log_a)
            a_ref[rows, :] = a
            b_ref[rows, :] = jnp.sqrt(jnp.tanh(-log_a) * (a * a + 1.0)) * (i2 * half_xc)
            yield

        for a_ref, b_ref, steps in ((af_ref, bf_ref, range(SEG_LEN)), (ab_ref, bb_ref, range(SEG_LEN - 1, -1, -1))):
            hend = jnp.zeros((SUBLANES, RG_WIDTH), F32)
            pend = jnp.ones((SUBLANES, RG_WIDTH), F32)
            for t in steps:
                r = slice(b * PERM_BLOCK + t * SUBLANES, b * PERM_BLOCK + (t + 1) * SUBLANES)
                a = a_ref[r, :]
                hend = a * hend + b_ref[r, :]
                pend = a * pend
            ends.append((hend, pend))
            yield

    def carries(order, state, which):
        into = {}
        for b, j in order:
            into[b, j] = state
            hend, pend = ends[2 * b + which]
            state = hend[j:j + 1, :] + pend[j:j + 1, :] * state
        return into, state

    segs = [(b, j) for b in range(blocks) for j in range(SUBLANES)]
    into_f, hf = carries(segs, h0[0:1, :], 0)
    into_b, hb = carries(segs[::-1], h0[1:2, :], 1)
    yield

    for b in range(blocks):
        for a_ref, b_ref, into, steps in ((af_ref, bf_ref, into_f, range(SEG_LEN)),
                                          (ab_ref, bb_ref, into_b, range(SEG_LEN - 1, -1, -1))):
            h = jnp.concatenate([into[b, j] for j in range(SUBLANES)], axis=0)
            for t in steps:
                r = slice(b * PERM_BLOCK + t * SUBLANES, b * PERM_BLOCK + (t + 1) * SUBLANES)
                h = a_ref[r, :] * h + b_ref[r, :]
                b_ref[r, :] = h
            yield

    out = []
    for b in range(blocks):
        rows = slice(b * PERM_BLOCK, (b + 1) * PERM_BLOCK)
        rg = ((bf_ref[rows, :] + bb_ref[rows, :]) * jax.nn.gelu(gr_ref[rows, :])).astype(BF16)
        out.append(_dot(perm_t_ref[...], rg).astype(BF16))
        yield
    return jnp.concatenate(out, axis=0), hf, hb


def _out_and_router(attn, rg, x, m, wout_ref, g1_ref, b1_ref, wrt_ref):
    gate1 = m[:, 2 * D_MODEL:3 * D_MODEL]
    shift2 = m[:, 3 * D_MODEL:4 * D_MODEL]
    scale2 = m[:, 4 * D_MODEL:5 * D_MODEL]
    mix = _dot(jnp.concatenate([attn, rg], axis=1), wout_ref[...])
    yield
    x1 = _ln_plain(ALPHA * x + gate1 * mix) * g1_ref[...] + b1_ref[...]
    yield
    h2 = (_ln_plain(x1) * (1.0 + scale2) + shift2).astype(BF16)
    yield
    logits = _dot_nt(wrt_ref[...].astype(BF16), h2)
    e = jnp.exp(logits - jnp.max(logits, axis=0, keepdims=True))
    return x1, h2, e / jnp.sum(e, axis=0, keepdims=True)


def _mixer_kernel(x_ref, mod_ref, win_ref, wout_ref, bd_ref, qg_ref, kg_ref, cw_ref, cb_ref, wg_ref, bg_ref,
                  lam_ref, g1_ref, b1_ref, wrt_ref,
                  nk_ref, nv_ref, nh_ref, x1_ref, h2_ref, afft_ref,
                  xprev_ref, xr_ref, gr_ref, attn_ref, *scan_refs):
    @pl.when(pl.program_id(0) == 0)
    def _():
        xprev_ref[...] = jnp.zeros_like(xprev_ref)
        xr_ref[...] = jnp.zeros_like(xr_ref)
        gr_ref[...] = jnp.zeros_like(gr_ref)
        attn_ref[...] = jnp.zeros_like(attn_ref)

    m = mod_ref[0:1, :]

    def second_half():
        h0 = jnp.zeros((2, RG_WIDTH), F32)
        rg, hf, hb = yield from _rglru_rows(xr_ref[...], gr_ref[...], cw_ref, cb_ref, wg_ref, bg_ref, lam_ref, h0,
                                            scan_refs)
        nh_ref[0] = jnp.concatenate([hf, hb], axis=0)
        x1_ref[...], h2_ref[...], afft_ref[...] = yield from _out_and_router(
            attn_ref[...], rg, xprev_ref[...], m, wout_ref, g1_ref, b1_ref, wrt_ref)

    def first_half():
        x = x_ref[...]
        h = _modulated(x, m)
        yield
        q, k, v = yield from _qkv(h, win_ref, bd_ref, qg_ref, kg_ref, None)
        nk_ref[0] = k.T
        nv_ref[0] = v.T
        yield
        xr, gr = yield from _rg_inputs(h, win_ref)
        attn = yield from _attend(q, k.astype(BF16), v.astype(BF16), False)
        return x, attn, xr, gr

    (x, attn, xr, gr), _ = _drive(first_half(), second_half(), order=MIXER_ORDER)
    xprev_ref[...] = x
    attn_ref[...] = attn
    xr_ref[...] = xr
    gr_ref[...] = gr


def _mixer_prompt(x2d, mod, win_bf, wout_bf, consts, g1, b1, wrt, *, n, nb):
    tokens = nb * n
    first = lambda w: pl.BlockSpec((n, w), lambda i: (jnp.minimum(i, nb - 1), 0))
    second = lambda w: pl.BlockSpec((n, w), lambda i: (jnp.maximum(i - 1, 0), 0))
    small = list(consts) + [g1, b1, wrt]
    return pl.pallas_call(
        _mixer_kernel,
        grid=(nb + 1,),
        in_specs=[first(D_MODEL), _full(mod.shape), _full(win_bf.shape), _full(wout_bf.shape)]
                 + [_full(a.shape) for a in small],
        out_specs=[pl.BlockSpec((1, KV_WIDTH, n), lambda i: (jnp.minimum(i, nb - 1), 0, 0)),
                   pl.BlockSpec((1, KV_WIDTH, n), lambda i: (jnp.minimum(i, nb - 1), 0, 0)),
                   pl.BlockSpec((1, 2, RG_WIDTH), lambda i: (jnp.maximum(i - 1, 0), 0, 0)),
                   second(D_MODEL), second(D_MODEL),
                   pl.BlockSpec((N_EXPERTS, n), lambda i: (jnp.maximum(i - 1, 0), 0))],
        out_shape=[jax.ShapeDtypeStruct((nb, KV_WIDTH, n), F32), jax.ShapeDtypeStruct((nb, KV_WIDTH, n), F32),
                   jax.ShapeDtypeStruct((nb, 2, RG_WIDTH), F32),
                   jax.ShapeDtypeStruct((tokens, D_MODEL), F32), jax.ShapeDtypeStruct((tokens, D_MODEL), BF16),
                   jax.ShapeDtypeStruct((nb * N_EXPERTS, n), F32)],
        scratch_shapes=[pltpu.VMEM((n, D_MODEL), F32), pltpu.VMEM((n, RG_WIDTH), F32),
                        pltpu.VMEM((n, RG_WIDTH), F32), pltpu.VMEM((n, ATTN_WIDTH), BF16)]
                       + [pltpu.VMEM((n, RG_WIDTH), F32)] * 4,
        compiler_params=_params(),
        name="mixer_prompt",
    )(x2d, mod, win_bf, wout_bf, *small)


def _front_kernel(x_ref, mod_ref, win_ref, bd_ref, qg_ref, kg_ref, cw_ref, cb_ref, wg_ref, bg_ref, lam_ref,
                  perm_ref, perm_t_ref, h0_ref, cos_ref, sin_ref,
                  q_ref, k_ref, v_ref, rg_ref, xr_ref, gr_ref, *scan_refs):
    m = mod_ref[pl.ds(1 + pl.program_id(0), 1), :]
    rows = PERM_BLOCK
    streams = x_ref.shape[0] // rows

    def row_block(j):
        r = pl.ds(j * rows, rows)
        h = _modulated(x_ref[r, :], m)
        yield
        q, k, v = yield from _qkv(h, win_ref, bd_ref, qg_ref, kg_ref, (cos_ref[r, :], sin_ref[r, :]))
        q_ref[r, :] = q
        k_ref[r, :] = k.astype(BF16)
        v_ref[r, :] = v.astype(BF16)
        yield
        hp = _dot(perm_ref[...], h).astype(BF16)
        xr_ref[r, :], gr_ref[r, :] = yield from _rg_inputs(hp, win_ref)
        traced[j] = True

    traced = [False] * streams
    recurrent = _rglru_segments(xr_ref, gr_ref, perm_t_ref, cw_ref, cb_ref, wg_ref, bg_ref, lam_ref, h0_ref[0],
                                scan_refs, lambda block: traced[block])
    chunks = 8
    order = [0] * chunks + [1] * chunks + [s for j in range(2, streams) for _ in range(chunks) for s in (j, streams)]
    results = _drive(*[row_block(j) for j in range(streams)], recurrent, order=order)
    rg_ref[...], _, _ = results[-1]


def _front_sample(x2d, mod, win_bf, consts, h0, rope, *, n, nb):
    tokens = nb * n
    seq = lambda w: pl.BlockSpec((n, w), lambda b: (b, 0))
    small = list(consts)
    return pl.pallas_call(
        _front_kernel,
        grid=(nb,),
        in_specs=[seq(D_MODEL), _full(mod.shape), _full(win_bf.shape)] + [_full(a.shape) for a in small]
                 + [pl.BlockSpec((1, 2, RG_WIDTH), lambda b: (b, 0, 0)), _full(rope[0].shape), _full(rope[1].shape)],
        out_specs=[seq(ATTN_WIDTH), seq(KV_WIDTH), seq(KV_WIDTH), seq(RG_WIDTH)],
        out_shape=[jax.ShapeDtypeStruct((tokens, ATTN_WIDTH), BF16), jax.ShapeDtypeStruct((tokens, KV_WIDTH), BF16),
                   jax.ShapeDtypeStruct((tokens, KV_WIDTH), BF16), jax.ShapeDtypeStruct((tokens, RG_WIDTH), BF16)],
        scratch_shapes=[pltpu.VMEM((n, RG_WIDTH), F32)] * 6,
        compiler_params=_params(),
        name="front_sample",
    )(x2d, mod, win_bf, *small, h0, *rope)


def _attn_post_kernel(q_ref, k_ref, v_ref, kc_ref, vc_ref, rg_ref, x_ref, mod_ref, wout_ref, g1_ref, b1_ref, wrt_ref,
                      x1_ref, h2_ref, afft_ref):
    m = mod_ref[pl.ds(1 + pl.program_id(0), 1), :]
    k = jnp.concatenate([k_ref[...], kc_ref[0].T.astype(BF16)], axis=0)
    v = jnp.concatenate([v_ref[...], vc_ref[0].T.astype(BF16)], axis=0)
    attn, = _drive(_attend(q_ref[...], k, v, True))
    rows = attn.shape[0] // TAIL_ROW_STREAMS

    def row_block(j):
        r = slice(j * rows, (j + 1) * rows)
        x1_ref[r, :], h2_ref[r, :], afft_ref[:, r] = yield from _out_and_router(
            attn[r, :], rg_ref[r, :], x_ref[r, :], m, wout_ref, g1_ref, b1_ref, wrt_ref)

    _drive(*[row_block(j) for j in range(TAIL_ROW_STREAMS)], order=(0,))


def _attn_post_sample(q, k, v, cache_k, cache_v, rg, x2d, mod, wout_bf, g1, b1, wrt, *, n, nb, past):
    tq = min(n, ATTN_Q_TILE)
    tiles = n // tq
    tokens = nb * n
    tile = lambda w: pl.BlockSpec((tq, w), lambda b, t: (b * tiles + t, 0))
    whole = pl.BlockSpec((n, KV_WIDTH), lambda b, t: (b, 0))
    cached = pl.BlockSpec((1, KV_WIDTH, past), lambda b, t: (b, 0, 0))
    return pl.pallas_call(
        _attn_post_kernel,
        grid=(nb, tiles),
        in_specs=[tile(ATTN_WIDTH), whole, whole, cached, cached, tile(RG_WIDTH), tile(D_MODEL),
                  _full(mod.shape), _full(wout_bf.shape), _full(g1.shape), _full(b1.shape), _full(wrt.shape)],
        out_specs=[tile(D_MODEL), tile(D_MODEL), pl.BlockSpec((N_EXPERTS, tq), lambda b, t: (b, t))],
        out_shape=[jax.ShapeDtypeStruct((tokens, D_MODEL), F32), jax.ShapeDtypeStruct((tokens, D_MODEL), BF16),
                   jax.ShapeDtypeStruct((nb * N_EXPERTS, n), F32)],
        compiler_params=_params(2),
        name="attn_post_sample",
    )(q, k, v, cache_k, cache_v, rg, x2d, mod, wout_bf, g1, b1, wrt)


def _slot_onehot(mask, pos, cap):
    n = mask.shape[1]
    slot = lax.broadcasted_iota(jnp.int32, (cap, n), 0).astype(F32)
    return [(slot == pos[e:e + 1, :]) & (mask[e:e + 1, :] > 0.5) for e in range(mask.shape[0])]


def _route_kernel(afft_ref, h2_ref, xs_ref, mask_ref, pos_ref, *, n, cap, seqs):
    b = pl.program_id(0)

    @pl.when(b == 0)
    def _():
        aff = afft_ref[...]
        thr = jnp.zeros((aff.shape[0], 1), jnp.int32)
        for bit in range(30, -1, -1):
            cand = thr | (1 << bit)
            cnt = jnp.sum((aff >= lax.bitcast_convert_type(cand, F32)).astype(F32), axis=1, keepdims=True)
            thr = jnp.where(cnt >= cap, cand, thr)
        above = aff >= lax.bitcast_convert_type(thr + 1, F32)
        tied = (aff >= lax.bitcast_convert_type(thr, F32)) & jnp.logical_not(above)
        need = cap - jnp.sum(above.astype(F32), axis=1, keepdims=True)
        before = (lax.broadcasted_iota(jnp.int32, (n, n), 0)
                  < lax.broadcasted_iota(jnp.int32, (n, n), 1)).astype(BF16)
        tie_rank = _dot(tied.astype(BF16), before)
        mask = (above | (tied & (tie_rank < need))).astype(F32)
        mask_ref[...] = mask
        pos_ref[...] = _dot(mask.astype(BF16), before)

    for j in range(seqs):
        r0 = pl.multiple_of((b * seqs + j) * N_EXPERTS, N_EXPERTS)
        onehots = _slot_onehot(mask_ref[pl.ds(r0, N_EXPERTS), :], pos_ref[pl.ds(r0, N_EXPERTS), :], cap)
        sel = jnp.concatenate(onehots, axis=0).astype(BF16)
        xs = _dot(sel, h2_ref[j * n:(j + 1) * n, :]).astype(BF16)
        xs_ref[:, j * cap:(j + 1) * cap, :] = xs.reshape(N_EXPERTS, cap, D_MODEL)


def _route(afft, h2, *, n, nb, cap, seqs):
    rows = nb * N_EXPERTS
    return pl.pallas_call(
        functools.partial(_route_kernel, n=n, cap=cap, seqs=seqs),
        grid=(nb // seqs,),
        in_specs=[_full(afft.shape), pl.BlockSpec((seqs * n, D_MODEL), lambda b: (b, 0))],
        out_specs=[pl.BlockSpec((N_EXPERTS, seqs * cap, D_MODEL), lambda b: (0, b, 0)),
                   _full((rows, n)), _full((rows, n))],
        out_shape=[jax.ShapeDtypeStruct((N_EXPERTS, nb * cap, D_MODEL), BF16),
                   jax.ShapeDtypeStruct((rows, n), F32), jax.ShapeDtypeStruct((rows, n), F32)],
        compiler_params=_params(),
        name="route_n%d" % n,
    )(afft, h2)


def _slot_gates(aff_ref, mask_ref, pos_ref, e, cap):
    cols = []
    for b in range(aff_ref.shape[0] // N_EXPERTS):
        r = b * N_EXPERTS + e
        onehot, = _slot_onehot(mask_ref[pl.ds(r, 1), :], pos_ref[pl.ds(r, 1), :], cap)
        cols.append(jnp.sum(jnp.where(onehot, aff_ref[pl.ds(r, 1), :], 0.0), axis=1, keepdims=True))
    return jnp.concatenate(cols, axis=0)


def _ffn_kernel(xp_ref, xs_ref, ap_ref, mp_ref, pp_ref, as_ref, ms_ref, ps_ref, wgu_ref, wd_ref, yp_ref, ys_ref):
    e = pl.program_id(0)
    rows_p = xp_ref.shape[1]
    xs = jnp.concatenate([xp_ref[0], xs_ref[0]], axis=0)
    gu = _dot(xs, wgu_ref[0].astype(BF16))
    gate = gu[:, :D_EXPERT]
    up = gu[:, D_EXPERT:]
    act = (gate * jax.nn.sigmoid(gate) * up).astype(BF16)
    y = _dot(act, wd_ref[0].astype(BF16))
    g = jnp.concatenate([_slot_gates(ap_ref, mp_ref, pp_ref, e, rows_p * N_EXPERTS // ap_ref.shape[0]),
                         _slot_gates(as_ref, ms_ref, ps_ref, e, xs_ref.shape[1] * N_EXPERTS // as_ref.shape[0])],
                        axis=0)
    y = (y * g).astype(BF16)
    yp_ref[0] = y[:rows_p]
    ys_ref[0] = y[rows_p:]


def _ffn(xs_p, xs_s, route_p, route_s, w_gate_up, w_down):
    per_e = lambda a: pl.BlockSpec((1,) + a.shape[1:], lambda e: (e, 0, 0))
    tables = list(route_p) + list(route_s)
    return pl.pallas_call(
        _ffn_kernel,
        grid=(N_EXPERTS,),
        in_specs=[per_e(xs_p), per_e(xs_s)] + [_full(t.shape) for t in tables] + [per_e(w_gate_up), per_e(w_down)],
        out_specs=[per_e(xs_p), per_e(xs_s)],
        out_shape=[jax.ShapeDtypeStruct(xs_p.shape, BF16), jax.ShapeDtypeStruct(xs_s.shape, BF16)],
        compiler_params=_params(),
        name="expert_ffn",
    )(xs_p, xs_s, *tables, w_gate_up, w_down)


def _combine_kernel(y_ref, mask_ref, pos_ref, x1_ref, mod_ref, g2_ref, b2_ref, o_ref, *, cap, seqs, mod_row0,
                    mod_row_step):
    row = mod_row0 + mod_row_step * pl.program_id(0)
    gate2 = mod_ref[pl.ds(row, 1), 5 * D_MODEL:6 * D_MODEL]
    tn = x1_ref.shape[0] // seqs
    for j in range(seqs):
        e0 = j * N_EXPERTS
        sel = jnp.concatenate(
            _slot_onehot(mask_ref[e0:e0 + N_EXPERTS, :], pos_ref[e0:e0 + N_EXPERTS, :], cap), axis=0).astype(BF16)
        y = y_ref[:, j * cap:(j + 1) * cap, :].reshape(N_EXPERTS * cap, D_MODEL)
        ff = _dot_tn(sel, y)
        r = pl.ds(j * tn, tn)
        o_ref[r, :] = _ln_plain(ALPHA * x1_ref[r, :] + gate2 * ff) * g2_ref[...] + b2_ref[...]


def _combine(y, mask, pos, x1, mod, g2, b2, *, n, nb, cap, seqs, sample):
    tn = min(n, COMBINE_TILE)
    tiles = n // tn
    assert seqs == 1 or tiles == 1
    return pl.pallas_call(
        functools.partial(_combine_kernel, cap=cap, seqs=seqs, mod_row0=1 if sample else 0,
                          mod_row_step=1 if sample else 0),
        grid=(nb // seqs, tiles),
        in_specs=[pl.BlockSpec((N_EXPERTS, seqs * cap, D_MODEL), lambda b, t: (0, b, 0)),
                  pl.BlockSpec((seqs * N_EXPERTS, tn), lambda b, t: (b, t)),
                  pl.BlockSpec((seqs * N_EXPERTS, tn), lambda b, t: (b, t)),
                  pl.BlockSpec((seqs * tn, D_MODEL), lambda b, t: (b * tiles + t, 0)),
                  _full(mod.shape), _full(g2.shape), _full(b2.shape)],
        out_specs=pl.BlockSpec((seqs * tn, D_MODEL), lambda b, t: (b * tiles + t, 0)),
        out_shape=jax.ShapeDtypeStruct((nb * n, D_MODEL), F32),
        compiler_params=_params(2),
        name="combine_sample" if sample else "combine_prompt",
    )(y, mask, pos, x1, mod, g2, b2)


def _rope_tables(n):
    lane = np.arange(LANES)
    within = lane % HEAD_DIM
    freq = (within % ROPE_FREQS).astype(np.float32)
    inv = np.float32(ROPE_THETA) ** (-freq / np.float32(ROPE_FREQS))
    tok = np.arange(n)
    pos = np.where((within < HEAD_DIM // 2)[None, :], (tok // GRID_W)[:, None], (tok % GRID_W)[:, None])
    ang = pos.astype(np.float32) * inv[None, :]
    sign = np.where(within % (2 * ROPE_FREQS) < ROPE_FREQS, -1.0, 1.0).astype(np.float32)
    return jnp.asarray(np.cos(ang)), jnp.asarray(np.sin(ang) * sign[None, :])


def kernel(x_prompt, x_sample, cache_k, cache_v, state_h, c, c_ctx, w_mod, b_mod, w_in, q_norm_g, k_norm_g,
           conv_w, conv_b, w_rg_a, b_rg_a, w_rg_x, b_rg_x, rg_lambda, w_out, ln1_g, ln1_b, w_router,
           w_gate_up, w_down, ln2_g, ln2_b):
    assert w_mod.shape[0] == DEPTH == 1
    nb_p, n_p, _ = x_prompt.shape
    nb_s, n_s, _ = x_sample.shape
    past = cache_k.shape[2]
    cap_p = CAP_FACTOR * n_p // N_EXPERTS
    cap_s = CAP_FACTOR * n_s // N_EXPERTS

    row = lambda v: v.reshape(1, -1)
    head_avg = np.kron(np.eye(ATTN_WIDTH // HEAD_DIM, dtype=np.float32),
                       np.full((HEAD_DIM, HEAD_DIM), 1.0 / HEAD_DIM, np.float32))
    bd = jnp.asarray(head_avg, BF16)
    bg = 0.5 * jnp.stack([b_rg_a[0, 0], b_rg_x[0, 0], b_rg_a[0, 1], b_rg_x[0, 1]])
    perm = _segment_permutation()
    perms = (jnp.asarray(perm, BF16), jnp.asarray(perm.T, BF16))
    wrt = w_router[0].T
    g1, b1, g2, b2 = row(ln1_g[0]), row(ln1_b[0]), row(ln2_g[0]), row(ln2_b[0])

    win_bf, wout_bf, wg = _cast_weights(w_in[0], w_out[0], w_rg_a[0], w_rg_x[0])
    consts = (bd, row(q_norm_g[0]), row(k_norm_g[0]), conv_w[0], row(conv_b[0]), wg, bg, rg_lambda[0])
    mod = _mod_vectors(row(c_ctx), c, w_mod[0], row(b_mod[0]))

    xp = x_prompt.reshape(nb_p * n_p, D_MODEL)
    xs = x_sample.reshape(nb_s * n_s, D_MODEL)

    new_k, new_v, new_h, x1_p, h2_p, aff_p = _mixer_prompt(
        xp, mod, win_bf, wout_bf, consts, g1, b1, wrt, n=n_p, nb=nb_p)

    q_s, k_s, v_s, rg_s = _front_sample(xs, mod, win_bf, consts + perms, state_h[:, 0], _rope_tables(n_s),
                                        n=n_s, nb=nb_s)
    def cached(t):
        return jnp.transpose(t[:, 0], (0, 2, 3, 1)).reshape(nb_s, KV_WIDTH, past)

    x1_s, h2_s, aff_s = _attn_post_sample(
        q_s, k_s, v_s, cached(cache_k), cached(cache_v), rg_s, xs, mod, wout_bf, g1, b1, wrt,
        n=n_s, nb=nb_s, past=past)

    seqs_p = PROMPT_SEQS_PER_STEP if nb_p % PROMPT_SEQS_PER_STEP == 0 else 1
    xs_p, mask_p, pos_p = _route(aff_p, h2_p, n=n_p, nb=nb_p, cap=cap_p, seqs=seqs_p)
    xs_s, mask_s, pos_s = _route(aff_s, h2_s, n=n_s, nb=nb_s, cap=cap_s, seqs=1)

    y_p, y_s = _ffn(xs_p, xs_s, (aff_p, mask_p, pos_p), (aff_s, mask_s, pos_s), w_gate_up[0], w_down[0])

    out_p = _combine(y_p, mask_p, pos_p, x1_p, mod, g2, b2, n=n_p, nb=nb_p, cap=cap_p, seqs=seqs_p, sample=False)
    out_s = _combine(y_s, mask_s, pos_s, x1_s, mod, g2, b2, n=n_s, nb=nb_s, cap=cap_s, seqs=1, sample=True)

    def cache_layout(t):
        t = t.reshape(nb_p, DEPTH, N_KV_HEADS, HEAD_DIM, n_p)
        return jnp.transpose(t, (0, 1, 4, 2, 3))

    return (out_p.reshape(nb_p, n_p, D_MODEL), out_s.reshape(nb_s, n_s, D_MODEL),
            cache_layout(new_k), cache_layout(new_v), new_h.reshape(nb_p, DEPTH, 2, RG_WIDTH))
```

```python
import functools

import numpy as np
import jax
import jax.numpy as jnp
from jax import lax
from jax.experimental import pallas as pl
from jax.experimental.pallas import tpu as pltpu

F32 = jnp.float32
BF16 = jnp.bfloat16

D_MODEL = 1024
HEAD_DIM = 64
N_Q_HEADS = 8
N_KV_HEADS = 2
GROUP = N_Q_HEADS // N_KV_HEADS
ATTN_WIDTH = N_Q_HEADS * HEAD_DIM
KV_WIDTH = N_KV_HEADS * HEAD_DIM
RG_WIDTH = D_MODEL - ATTN_WIDTH
RG_BLOCK = 64
CONV_W = 4
C_LRU = 8.0
N_EXPERTS = 16
CAP_FACTOR = 2
D_EXPERT = 1024
GRID_W = 64
ROPE_THETA = 10000.0
ROPE_FREQS = HEAD_DIM // 4
NORM_EPS = 1e-6
DEPTH = 1
ALPHA = (2.0 * DEPTH) ** 0.25
IN_WIDTH = ATTN_WIDTH + 2 * KV_WIDTH + 2 * RG_WIDTH
Q_SCALE = HEAD_DIM ** -0.5 * float(np.log2(np.e))

LANES = 128
SUBLANES = 8
MXU_DIM = 256
VMEM_LIMIT_BYTES = 56 * 1024 * 1024

MOD_ROWS = SUBLANES
MOD_TILE = 256
ATTN_Q_TILE = 512
COMBINE_TILE = 512
MIXER_ORDER = (1, 0, 0, 1, 1, 0, 1, 1, 0, 1, 0, 1, 0, 1, 1, 0, 1, 1, 0, 0, 1, 0, 1, 0, 0, 0, 1, 0, 0, 1, 0)
PERM_BLOCK = MXU_DIM
SEG_LEN = PERM_BLOCK // SUBLANES
TAIL_ROW_STREAMS = 2
PROMPT_SEQS_PER_STEP = 4


def _params(n_axes=1):
    return pltpu.CompilerParams(dimension_semantics=("arbitrary",) * n_axes,
                                vmem_limit_bytes=VMEM_LIMIT_BYTES)


def _full(shape):
    zeros = (0,) * len(shape)
    return pl.BlockSpec(shape, lambda *_: zeros)


def _ln_plain(x):
    mu = jnp.mean(x, -1, keepdims=True)
    xc = x - mu
    var = jnp.mean(xc * xc, -1, keepdims=True)
    return xc * lax.rsqrt(var + NORM_EPS)


def _dot(a, b):
    return jnp.dot(a, b, preferred_element_type=F32)


def _dot_nt(a, b):
    return lax.dot_general(a, b, (((1,), (1,)), ((), ())), preferred_element_type=F32)


def _dot_tn(a, b):
    return lax.dot_general(a, b, (((0,), (0,)), ((), ())), preferred_element_type=F32)


def _cast_kernel(a_ref, b_ref, wa_ref, wx_ref, ao_ref, bo_ref, wg_ref):
    ao_ref[...] = a_ref[...].astype(BF16)
    bo_ref[...] = b_ref[...].astype(BF16)
    wg_ref[...] = jnp.zeros_like(wg_ref)
    per_tile = MXU_DIM // RG_BLOCK
    for c in range(RG_WIDTH // MXU_DIM):
        for g, (ref, d) in enumerate(((wa_ref, 0), (wx_ref, 0), (wa_ref, 1), (wx_ref, 1))):
            for i in range(per_tile):
                lo = RG_BLOCK * i
                wg_ref[c, lo:lo + RG_BLOCK, MXU_DIM * g + lo:MXU_DIM * g + lo + RG_BLOCK] = \
                    (0.5 * ref[d, per_tile * c + i]).astype(BF16)


def _cast_weights(w_in, w_out, w_rg_a, w_rg_x):
    wg_shape = (RG_WIDTH // MXU_DIM, MXU_DIM, 4 * MXU_DIM)
    ins = [w_in, w_out, w_rg_a, w_rg_x]
    outs = [jax.ShapeDtypeStruct(w_in.shape, BF16), jax.ShapeDtypeStruct(w_out.shape, BF16),
            jax.ShapeDtypeStruct(wg_shape, BF16)]
    return pl.pallas_call(
        _cast_kernel,
        grid=(1,),
        in_specs=[_full(a.shape) for a in ins],
        out_specs=[_full(o.shape) for o in outs],
        out_shape=outs,
        compiler_params=_params(),
        name="cast_weights",
    )(*ins)


def _mod_kernel(ctx_ref, c_ref, w_ref, b_ref, o_ref):
    @pl.when(pl.program_id(0) == 0)
    def _():
        o_ref[...] = jnp.broadcast_to(b_ref[...], o_ref.shape)

    w = w_ref[...].astype(BF16)
    n_c = c_ref.shape[0]
    for rows, ref in ((slice(0, 1), ctx_ref), (slice(1, 1 + n_c), c_ref)):
        cs = ref[...]
        o_ref[rows, :] += _dot((cs * jax.nn.sigmoid(cs)).astype(BF16), w)


def _mod_vectors(c_ctx, c, w_mod, b_mod):
    width = w_mod.shape[1]
    assert 1 + c.shape[0] <= MOD_ROWS
    return pl.pallas_call(
        _mod_kernel,
        grid=(D_MODEL // MOD_TILE,),
        in_specs=[pl.BlockSpec((1, MOD_TILE), lambda j: (0, j)),
                  pl.BlockSpec((c.shape[0], MOD_TILE), lambda j: (0, j)),
                  pl.BlockSpec((MOD_TILE, width), lambda j: (j, 0)),
                  _full(b_mod.shape)],
        out_specs=_full((MOD_ROWS, width)),
        out_shape=jax.ShapeDtypeStruct((MOD_ROWS, width), F32),
        compiler_params=_params(),
        name="mod_vectors",
    )(c_ctx, c, w_mod, b_mod)


def _head_mean_sq(x, bd):
    return _dot((x * x).astype(BF16), bd)


def _rope_lanes(x, cos, sin_signed, hi_half):
    partner = jnp.where(hi_half, pltpu.roll(x, ROPE_FREQS, axis=1), pltpu.roll(x, LANES - ROPE_FREQS, axis=1))
    return x * cos + partner * sin_signed


def _modulated(x, m):
    return (_ln_plain(x) * (1.0 + m[:, D_MODEL:2 * D_MODEL]) + m[:, 0:D_MODEL]).astype(BF16)


def _qkv(h, win_ref, bd_ref, qg_ref, kg_ref, rope):
    n = h.shape[0]
    bd = bd_ref[...]
    q = _dot(h, win_ref[:, 0:ATTN_WIDTH])
    yield
    q = q * lax.rsqrt(_head_mean_sq(q, bd) + NORM_EPS) * jnp.concatenate([qg_ref[...]] * N_Q_HEADS, axis=1)
    yield
    k = _dot(h, win_ref[:, ATTN_WIDTH:ATTN_WIDTH + KV_WIDTH])
    k = k * lax.rsqrt(_head_mean_sq(k, bd[:KV_WIDTH, :KV_WIDTH]) + NORM_EPS) \
        * jnp.concatenate([kg_ref[...]] * N_KV_HEADS, axis=1)
    v = _dot(h, win_ref[:, ATTN_WIDTH + KV_WIDTH:ATTN_WIDTH + 2 * KV_WIDTH])
    yield
    if rope is not None:
        cos, sin_signed = rope
        lane = lax.broadcasted_iota(jnp.int32, (n, LANES), 1)
        hi_half = (lane & ROPE_FREQS) != 0
        q = jnp.concatenate(
            [_rope_lanes(q[:, j * LANES:(j + 1) * LANES], cos, sin_signed, hi_half)
             for j in range(ATTN_WIDTH // LANES)], axis=1)
        k = _rope_lanes(k, cos, sin_signed, hi_half)
    return (q * Q_SCALE).astype(BF16), k, v


def _drive(*streams, order=()):
    results = [None] * len(streams)
    live = list(range(len(streams)))
    plan = [j for j in order]
    while live:
        idx = plan.pop(0) if plan else live[0]
        if idx not in live:
            continue
        if not plan:
            live.append(live.pop(0))
        try:
            next(streams[idx])
        except StopIteration as stop:
            results[idx] = stop.value
            live.remove(idx)
    return results


def _attend(q, k, v, lookahead):
    def scores(hq):
        kv = hq // GROUP
        return _dot_nt(q[:, hq * HEAD_DIM:(hq + 1) * HEAD_DIM], k[:, kv * HEAD_DIM:(kv + 1) * HEAD_DIM])

    outs = []
    s_next = scores(0) if lookahead else None
    for hq in range(N_Q_HEADS):
        if lookahead:
            s, s_next = s_next, (scores(hq + 1) if hq + 1 < N_Q_HEADS else None)
        else:
            s = scores(hq)
        kv = hq // GROUP
        e = jnp.exp2(s - jnp.max(s, axis=-1, keepdims=True))
        denom = jnp.sum(e, axis=-1, keepdims=True)
        pv = _dot(e.astype(BF16), v[:, kv * HEAD_DIM:(kv + 1) * HEAD_DIM])
        outs.append(pv / denom)
        yield
    return jnp.concatenate(outs, axis=1).astype(BF16)


def _rg_inputs(h, win_ref):
    rg_lo = ATTN_WIDTH + 2 * KV_WIDTH
    xr = _dot(h, win_ref[:, rg_lo:rg_lo + RG_WIDTH])
    yield
    gr = _dot(h, win_ref[:, rg_lo + RG_WIDTH:IN_WIDTH])
    yield
    return xr, gr


def _rglru_rows(xr, gr, cw_ref, cb_ref, wg_ref, bg_ref, lam_ref, h0, scan_refs):
    af_ref, bf_ref, ab_ref, bb_ref = scan_refs
    n = xr.shape[0]

    t_idx = lax.broadcasted_iota(jnp.int32, (n, 1), 0)
    cw = cw_ref[...]
    xc = jnp.where(t_idx >= 2, pltpu.roll(xr, 2, axis=0), 0.0) * cw[0:1, :]
    xc = xc + jnp.where(t_idx >= 1, pltpu.roll(xr, 1, axis=0), 0.0) * cw[1:2, :]
    xc = xc + xr * cw[2:3, :]
    xc = xc + jnp.where(t_idx < n - 1, pltpu.roll(xr, n - 1, axis=0), 0.0) * cw[3:4, :]
    xc = xc + cb_ref[...]
    yield

    xcb = xc.astype(BF16)
    halves = [_dot(xcb[:, c * MXU_DIM:(c + 1) * MXU_DIM], wg_ref[c])
              for c in range(RG_WIDTH // MXU_DIM)]

    def gate_pre(idx):
        return jnp.concatenate([hv[:, idx * MXU_DIM:(idx + 1) * MXU_DIM] for hv in halves], axis=1) \
            + bg_ref[idx:idx + 1, :]

    yield
    neg = -lam_ref[...]
    softplus = jnp.maximum(neg, 0.0) + jnp.log1p(jnp.exp(-jnp.abs(neg)))
    decay = (-0.5 * C_LRU) * softplus
    half_xc = 0.5 * xc

    def coeffs(d):
        r2 = jnp.tanh(gate_pre(2 * d)) + 1.0
        i2 = jnp.tanh(gate_pre(2 * d + 1)) + 1.0
        log_a = r2 * decay[d:d + 1, :]
        a = jnp.exp(log_a)
        bx = jnp.sqrt(jnp.tanh(-log_a) * (a * a + 1.0)) * (i2 * half_xc)
        return a, bx

    groups = n // SUBLANES
    rmod = lax.broadcasted_iota(jnp.int32, (1, SUBLANES, 1), 1)

    def tile_scan(d, forward):
        a, bx = coeffs(d)
        yield
        a = a.reshape(groups, SUBLANES, RG_WIDTH)
        bx = bx.reshape(groups, SUBLANES, RG_WIDTH)
        for s in (1, 2, 4):
            ok = (rmod >= s) if forward else (rmod < SUBLANES - s)
            shift = s if forward else SUBLANES - s
            a_sh = jnp.where(ok, pltpu.roll(a, shift, axis=1), 1.0)
            b_sh = jnp.where(ok, pltpu.roll(bx, shift, axis=1), 0.0)
            bx = a * b_sh + bx
            a = a * a_sh
            yield
        return a.reshape(n, RG_WIDTH), bx.reshape(n, RG_WIDTH)

    af_ref[...], bf_ref[...] = yield from tile_scan(0, True)
    ab_ref[...], bb_ref[...] = yield from tile_scan(1, False)

    def step(g, carry):
        hf, hb = carry
        rf = pl.multiple_of(g * SUBLANES, SUBLANES)
        rb = pl.multiple_of((groups - 1 - g) * SUBLANES, SUBLANES)
        new_f = af_ref[pl.ds(rf, SUBLANES), :] * hf + bf_ref[pl.ds(rf, SUBLANES), :]
        new_b = ab_ref[pl.ds(rb, SUBLANES), :] * hb + bb_ref[pl.ds(rb, SUBLANES), :]
        bf_ref[pl.ds(rf, SUBLANES), :] = new_f
        bb_ref[pl.ds(rb, SUBLANES), :] = new_b
        return (jnp.broadcast_to(new_f[SUBLANES - 1:SUBLANES, :], (SUBLANES, RG_WIDTH)),
                jnp.broadcast_to(new_b[0:1, :], (SUBLANES, RG_WIDTH)))

    hf, hb = lax.fori_loop(
        0, groups, step,
        (jnp.broadcast_to(h0[0:1, :], (SUBLANES, RG_WIDTH)), jnp.broadcast_to(h0[1:2, :], (SUBLANES, RG_WIDTH))),
        unroll=True)
    yield

    rg = ((bf_ref[...] + bb_ref[...]) * jax.nn.gelu(gr)).astype(BF16)
    return rg, hf[0:1, :], hb[0:1, :]


def _segment_permutation():
    p = np.zeros((PERM_BLOCK, PERM_BLOCK), np.float32)
    for t in range(SEG_LEN):
        for j in range(SUBLANES):
            p[t * SUBLANES + j, j * SEG_LEN + t] = 1.0
    return p


def _rglru_segments(xr_ref, gr_ref, perm_t_ref, cw_ref, cb_ref, wg_ref, bg_ref, lam_ref, h0, scan_refs, ready):
    af_ref, bf_ref, ab_ref, bb_ref = scan_refs
    n = xr_ref.shape[0]
    blocks = n // PERM_BLOCK
    sub = lax.broadcasted_iota(jnp.int32, (SUBLANES, 1), 0)
    zero_row = jnp.zeros((1, RG_WIDTH), F32)
    cw = cw_ref[...]
    neg = -lam_ref[...]
    softplus = jnp.maximum(neg, 0.0) + jnp.log1p(jnp.exp(-jnp.abs(neg)))
    decay = (-0.5 * C_LRU) * softplus

    def group(b, t):
        return xr_ref[b * PERM_BLOCK + t * SUBLANES:b * PERM_BLOCK + (t + 1) * SUBLANES, :]

    def before(b, t):
        wrap = group(b - 1, t)[SUBLANES - 1:SUBLANES, :] if b > 0 else zero_row
        return jnp.where(sub == 0, wrap, pltpu.roll(group(b, t), 1, axis=0))

    def after(b, t):
        wrap = group(b + 1, t)[0:1, :] if b + 1 < blocks else zero_row
        return jnp.where(sub == SUBLANES - 1, wrap, pltpu.roll(group(b, t), SUBLANES - 1, axis=0))

    ends = []
    for b in range(blocks):
        while not ready(min(b + 1, blocks - 1)):
            yield
        rows = slice(b * PERM_BLOCK, (b + 1) * PERM_BLOCK)
        ext = jnp.concatenate([before(b, SEG_LEN - 2), before(b, SEG_LEN - 1), xr_ref[rows, :], after(b, 0)], axis=0)
        xc = ext[0:PERM_BLOCK] * cw[0:1, :]
        for tap in range(1, CONV_W):
            xc = xc + ext[tap * SUBLANES:tap * SUBLANES + PERM_BLOCK] * cw[tap:tap + 1, :]
        xc = xc + cb_ref[...]
        yield

        xcb = xc.astype(BF16)
        halves = [_dot(xcb[:, c * MXU_DIM:(c + 1) * MXU_DIM], wg_ref[c])
                  for c in range(RG_WIDTH // MXU_DIM)]
        yield
        half_xc = 0.5 * xc
        for d, (a_ref, b_ref) in enumerate(((af_ref, bf_ref), (ab_ref, bb_ref))):
            pre = [jnp.concatenate([hv[:, i * MXU_DIM:(i + 1) * MXU_DIM] for hv in halves], axis=1)
                   + bg_ref[i:i + 1, :] for i in (2 * d, 2 * d + 1)]
            r2 = jnp.tanh(pre[0]) + 1.0
            i2 = jnp.tanh(pre[1]) + 1.0
            log_a = r2 * decay[d:d + 1, :]
            a = jnp.exp(log_a)
            a_ref[rows, :] = a
            b_ref[rows, :] = jnp.sqrt(jnp.tanh(-log_a) * (a * a + 1.0)) * (i2 * half_xc)
            yield

        for a_ref, b_ref, steps in ((af_ref, bf_ref, range(SEG_LEN)), (ab_ref, bb_ref, range(SEG_LEN - 1, -1, -1))):
            hend = jnp.zeros((SUBLANES, RG_WIDTH), F32)
            pend = jnp.ones((SUBLANES, RG_WIDTH), F32)
            for t in steps:
                r = slice(b * PERM_BLOCK + t * SUBLANES, b * PERM_BLOCK + (t + 1) * SUBLANES)
                a = a_ref[r, :]
                hend = a * hend + b_ref[r, :]
                pend = a * pend
            ends.append((hend, pend))
            yield

    def carries(order, state, which):
        into = {}
        for b, j in order:
            into[b, j] = state
            hend, pend = ends[2 * b + which]
            state = hend[j:j + 1, :] + pend[j:j + 1, :] * state
        return into, state

    segs = [(b, j) for b in range(blocks) for j in range(SUBLANES)]
    into_f, hf = carries(segs, h0[0:1, :], 0)
    into_b, hb = carries(segs[::-1], h0[1:2, :], 1)
    yield

    for b in range(blocks):
        for a_ref, b_ref, into, steps in ((af_ref, bf_ref, into_f, range(SEG_LEN)),
                                          (ab_ref, bb_ref, into_b, range(SEG_LEN - 1, -1, -1))):
            h = jnp.concatenate([into[b, j] for j in range(SUBLANES)], axis=0)
            for t in steps:
                r = slice(b * PERM_BLOCK + t * SUBLANES, b * PERM_BLOCK + (t + 1) * SUBLANES)
                h = a_ref[r, :] * h + b_ref[r, :]
                b_ref[r, :] = h
            yield

    out = []
    for b in range(blocks):
        rows = slice(b * PERM_BLOCK, (b + 1) * PERM_BLOCK)
        rg = ((bf_ref[rows, :] + bb_ref[rows, :]) * jax.nn.gelu(gr_ref[rows, :])).astype(BF16)
        out.append(_dot(perm_t_ref[...], rg).astype(BF16))
        yield
    return jnp.concatenate(out, axis=0), hf, hb


def _out_and_router(attn, rg, x, m, wout_ref, g1_ref, b1_ref, wrt_ref):
    gate1 = m[:, 2 * D_MODEL:3 * D_MODEL]
    shift2 = m[:, 3 * D_MODEL:4 * D_MODEL]
    scale2 = m[:, 4 * D_MODEL:5 * D_MODEL]
    mix = _dot(jnp.concatenate([attn, rg], axis=1), wout_ref[...])
    yield
    x1 = _ln_plain(ALPHA * x + gate1 * mix) * g1_ref[...] + b1_ref[...]
    yield
    h2 = (_ln_plain(x1) * (1.0 + scale2) + shift2).astype(BF16)
    yield
    logits = _dot_nt(wrt_ref[...].astype(BF16), h2)
    e = jnp.exp(logits - jnp.max(logits, axis=0, keepdims=True))
    return x1, h2, e / jnp.sum(e, axis=0, keepdims=True)


def _mixer_kernel(x_ref, mod_ref, win_ref, wout_ref, bd_ref, qg_ref, kg_ref, cw_ref, cb_ref, wg_ref, bg_ref,
                  lam_ref, g1_ref, b1_ref, wrt_ref,
                  nk_ref, nv_ref, nh_ref, x1_ref, h2_ref, afft_ref,
                  xprev_ref, xr_ref, gr_ref, attn_ref, *scan_refs):
    @pl.when(pl.program_id(0) == 0)
    def _():
        xprev_ref[...] = jnp.zeros_like(xprev_ref)
        xr_ref[...] = jnp.zeros_like(xr_ref)
        gr_ref[...] = jnp.zeros_like(gr_ref)
        attn_ref[...] = jnp.zeros_like(attn_ref)

    m = mod_ref[0:1, :]

    def second_half():
        h0 = jnp.zeros((2, RG_WIDTH), F32)
        rg, hf, hb = yield from _rglru_rows(xr_ref[...], gr_ref[...], cw_ref, cb_ref, wg_ref, bg_ref, lam_ref, h0,
                                            scan_refs)
        nh_ref[0] = jnp.concatenate([hf, hb], axis=0)
        x1_ref[...], h2_ref[...], afft_ref[...] = yield from _out_and_router(
            attn_ref[...], rg, xprev_ref[...], m, wout_ref, g1_ref, b1_ref, wrt_ref)

    def first_half():
        x = x_ref[...]
        h = _modulated(x, m)
        yield
        q, k, v = yield from _qkv(h, win_ref, bd_ref, qg_ref, kg_ref, None)
        nk_ref[0] = k.T
        nv_ref[0] = v.T
        yield
        xr, gr = yield from _rg_inputs(h, win_ref)
        attn = yield from _attend(q, k.astype(BF16), v.astype(BF16), False)
        return x, attn, xr, gr

    (x, attn, xr, gr), _ = _drive(first_half(), second_half(), order=MIXER_ORDER)
    xprev_ref[...] = x
    attn_ref[...] = attn
    xr_ref[...] = xr
    gr_ref[...] = gr


def _mixer_prompt(x2d, mod, win_bf, wout_bf, consts, g1, b1, wrt, *, n, nb):
    tokens = nb * n
    first = lambda w: pl.BlockSpec((n, w), lambda i: (jnp.minimum(i, nb - 1), 0))
    second = lambda w: pl.BlockSpec((n, w), lambda i: (jnp.maximum(i - 1, 0), 0))
    small = list(consts) + [g1, b1, wrt]
    return pl.pallas_call(
        _mixer_kernel,
        grid=(nb + 1,),
        in_specs=[first(D_MODEL), _full(mod.shape), _full(win_bf.shape), _full(wout_bf.shape)]
                 + [_full(a.shape) for a in small],
        out_specs=[pl.BlockSpec((1, KV_WIDTH, n), lambda i: (jnp.minimum(i, nb - 1), 0, 0)),
                   pl.BlockSpec((1, KV_WIDTH, n), lambda i: (jnp.minimum(i, nb - 1), 0, 0)),
                   pl.BlockSpec((1, 2, RG_WIDTH), lambda i: (jnp.maximum(i - 1, 0), 0, 0)),
                   second(D_MODEL), second(D_MODEL),
                   pl.BlockSpec((N_EXPERTS, n), lambda i: (jnp.maximum(i - 1, 0), 0))],
        out_shape=[jax.ShapeDtypeStruct((nb, KV_WIDTH, n), F32), jax.ShapeDtypeStruct((nb, KV_WIDTH, n), F32),
                   jax.ShapeDtypeStruct((nb, 2, RG_WIDTH), F32),
                   jax.ShapeDtypeStruct((tokens, D_MODEL), F32), jax.ShapeDtypeStruct((tokens, D_MODEL), BF16),
                   jax.ShapeDtypeStruct((nb * N_EXPERTS, n), F32)],
        scratch_shapes=[pltpu.VMEM((n, D_MODEL), F32), pltpu.VMEM((n, RG_WIDTH), F32),
                        pltpu.VMEM((n, RG_WIDTH), F32), pltpu.VMEM((n, ATTN_WIDTH), BF16)]
                       + [pltpu.VMEM((n, RG_WIDTH), F32)] * 4,
        compiler_params=_params(),
        name="mixer_prompt",
    )(x2d, mod, win_bf, wout_bf, *small)


def _front_kernel(x_ref, mod_ref, win_ref, bd_ref, qg_ref, kg_ref, cw_ref, cb_ref, wg_ref, bg_ref, lam_ref,
                  perm_ref, perm_t_ref, h0_ref, cos_ref, sin_ref,
                  q_ref, k_ref, v_ref, rg_ref, xr_ref, gr_ref, *scan_refs):
    m = mod_ref[pl.ds(1 + pl.program_id(0), 1), :]
    rows = PERM_BLOCK
    streams = x_ref.shape[0] // rows

    def row_block(j):
        r = pl.ds(j * rows, rows)
        h = _modulated(x_ref[r, :], m)
        yield
        q, k, v = yield from _qkv(h, win_ref, bd_ref, qg_ref, kg_ref, (cos_ref[r, :], sin_ref[r, :]))
        q_ref[r, :] = q
        k_ref[r, :] = k.astype(BF16)
        v_ref[r, :] = v.astype(BF16)
        yield
        hp = _dot(perm_ref[...], h).astype(BF16)
        xr_ref[r, :], gr_ref[r, :] = yield from _rg_inputs(hp, win_ref)
        traced[j] = True

    traced = [False] * streams
    recurrent = _rglru_segments(xr_ref, gr_ref, perm_t_ref, cw_ref, cb_ref, wg_ref, bg_ref, lam_ref, h0_ref[0],
                                scan_refs, lambda block: traced[block])
    chunks = 8
    order = [0] * chunks + [1] * chunks + [s for j in range(2, streams) for _ in range(chunks) for s in (j, streams)]
    results = _drive(*[row_block(j) for j in range(streams)], recurrent, order=order)
    rg_ref[...], _, _ = results[-1]


def _front_sample(x2d, mod, win_bf, consts, h0, rope, *, n, nb):
    tokens = nb * n
    seq = lambda w: pl.BlockSpec((n, w), lambda b: (b, 0))
    small = list(consts)
    return pl.pallas_call(
        _front_kernel,
        grid=(nb,),
        in_specs=[seq(D_MODEL), _full(mod.shape), _full(win_bf.shape)] + [_full(a.shape) for a in small]
                 + [pl.BlockSpec((1, 2, RG_WIDTH), lambda b: (b, 0, 0)), _full(rope[0].shape), _full(rope[1].shape)],
        out_specs=[seq(ATTN_WIDTH), seq(KV_WIDTH), seq(KV_WIDTH), seq(RG_WIDTH)],
        out_shape=[jax.ShapeDtypeStruct((tokens, ATTN_WIDTH), BF16), jax.ShapeDtypeStruct((tokens, KV_WIDTH), BF16),
                   jax.ShapeDtypeStruct((tokens, KV_WIDTH), BF16), jax.ShapeDtypeStruct((tokens, RG_WIDTH), BF16)],
        scratch_shapes=[pltpu.VMEM((n, RG_WIDTH), F32)] * 6,
        compiler_params=_params(),
        name="front_sample",
    )(x2d, mod, win_bf, *small, h0, *rope)


def _attn_post_kernel(q_ref, k_ref, v_ref, kc_ref, vc_ref, rg_ref, x_ref, mod_ref, wout_ref, g1_ref, b1_ref, wrt_ref,
                      x1_ref, h2_ref, afft_ref):
    m = mod_ref[pl.ds(1 + pl.program_id(0), 1), :]
    k = jnp.concatenate([k_ref[...], kc_ref[0].T.astype(BF16)], axis=0)
    v = jnp.concatenate([v_ref[...], vc_ref[0].T.astype(BF16)], axis=0)
    attn, = _drive(_attend(q_ref[...], k, v, True))
    rows = attn.shape[0] // TAIL_ROW_STREAMS

    def row_block(j):
        r = slice(j * rows, (j + 1) * rows)
        x1_ref[r, :], h2_ref[r, :], afft_ref[:, r] = yield from _out_and_router(
            attn[r, :], rg_ref[r, :], x_ref[r, :], m, wout_ref, g1_ref, b1_ref, wrt_ref)

    _drive(*[row_block(j) for j in range(TAIL_ROW_STREAMS)], order=(0,))


def _attn_post_sample(q, k, v, cache_k, cache_v, rg, x2d, mod, wout_bf, g1, b1, wrt, *, n, nb, past):
    tq = min(n, ATTN_Q_TILE)
    tiles = n // tq
    tokens = nb * n
    tile = lambda w: pl.BlockSpec((tq, w), lambda b, t: (b * tiles + t, 0))
    whole = pl.BlockSpec((n, KV_WIDTH), lambda b, t: (b, 0))
    cached = pl.BlockSpec((1, KV_WIDTH, past), lambda b, t: (b, 0, 0))
    return pl.pallas_call(
        _attn_post_kernel,
        grid=(nb, tiles),
        in_specs=[tile(ATTN_WIDTH), whole, whole, cached, cached, tile(RG_WIDTH), tile(D_MODEL),
                  _full(mod.shape), _full(wout_bf.shape), _full(g1.shape), _full(b1.shape), _full(wrt.shape)],
        out_specs=[tile(D_MODEL), tile(D_MODEL), pl.BlockSpec((N_EXPERTS, tq), lambda b, t: (b, t))],
        out_shape=[jax.ShapeDtypeStruct((tokens, D_MODEL), F32), jax.ShapeDtypeStruct((tokens, D_MODEL), BF16),
                   jax.ShapeDtypeStruct((nb * N_EXPERTS, n), F32)],
        compiler_params=_params(2),
        name="attn_post_sample",
    )(q, k, v, cache_k, cache_v, rg, x2d, mod, wout_bf, g1, b1, wrt)


def _slot_onehot(mask, pos, cap):
    n = mask.shape[1]
    slot = lax.broadcasted_iota(jnp.int32, (cap, n), 0).astype(F32)
    return [(slot == pos[e:e + 1, :]) & (mask[e:e + 1, :] > 0.5) for e in range(mask.shape[0])]


def _route_kernel(afft_ref, h2_ref, xs_ref, mask_ref, pos_ref, *, n, cap, seqs):
    b = pl.program_id(0)

    @pl.when(b == 0)
    def _():
        aff = afft_ref[...]
        thr = jnp.zeros((aff.shape[0], 1), jnp.int32)
        for bit in range(30, -1, -1):
            cand = thr | (1 << bit)
            cnt = jnp.sum((aff >= lax.bitcast_convert_type(cand, F32)).astype(F32), axis=1, keepdims=True)
            thr = jnp.where(cnt >= cap, cand, thr)
        above = aff >= lax.bitcast_convert_type(thr + 1, F32)
        tied = (aff >= lax.bitcast_convert_type(thr, F32)) & jnp.logical_not(above)
        need = cap - jnp.sum(above.astype(F32), axis=1, keepdims=True)
        before = (lax.broadcasted_iota(jnp.int32, (n, n), 0)
                  < lax.broadcasted_iota(jnp.int32, (n, n), 1)).astype(BF16)
        tie_rank = _dot(tied.astype(BF16), before)
        mask = (above | (tied & (tie_rank < need))).astype(F32)
        mask_ref[...] = mask
        pos_ref[...] = _dot(mask.astype(BF16), before)

    for j in range(seqs):
        r0 = pl.multiple_of((b * seqs + j) * N_EXPERTS, N_EXPERTS)
        onehots = _slot_onehot(mask_ref[pl.ds(r0, N_EXPERTS), :], pos_ref[pl.ds(r0, N_EXPERTS), :], cap)
        sel = jnp.concatenate(onehots, axis=0).astype(BF16)
        xs = _dot(sel, h2_ref[j * n:(j + 1) * n, :]).astype(BF16)
        xs_ref[:, j * cap:(j + 1) * cap, :] = xs.reshape(N_EXPERTS, cap, D_MODEL)


def _route(afft, h2, *, n, nb, cap, seqs):
    rows = nb * N_EXPERTS
    return pl.pallas_call(
        functools.partial(_route_kernel, n=n, cap=cap, seqs=seqs),
        grid=(nb // seqs,),
        in_specs=[_full(afft.shape), pl.BlockSpec((seqs * n, D_MODEL), lambda b: (b, 0))],
        out_specs=[pl.BlockSpec((N_EXPERTS, seqs * cap, D_MODEL), lambda b: (0, b, 0)),
                   _full((rows, n)), _full((rows, n))],
        out_shape=[jax.ShapeDtypeStruct((N_EXPERTS, nb * cap, D_MODEL), BF16),
                   jax.ShapeDtypeStruct((rows, n), F32), jax.ShapeDtypeStruct((rows, n), F32)],
        compiler_params=_params(),
        name="route_n%d" % n,
    )(afft, h2)


def _slot_gates(aff_ref, mask_ref, pos_ref, e, cap):
    cols = []
    for b in range(aff_ref.shape[0] // N_EXPERTS):
        r = b * N_EXPERTS + e
        onehot, = _slot_onehot(mask_ref[pl.ds(r, 1), :], pos_ref[pl.ds(r, 1), :], cap)
        cols.append(jnp.sum(jnp.where(onehot, aff_ref[pl.ds(r, 1), :], 0.0), axis=1, keepdims=True))
    return jnp.concatenate(cols, axis=0)


def _ffn_kernel(xp_ref, xs_ref, ap_ref, mp_ref, pp_ref, as_ref, ms_ref, ps_ref, wgu_ref, wd_ref, yp_ref, ys_ref):
    e = pl.program_id(0)
    rows_p = xp_ref.shape[1]
    xs = jnp.concatenate([xp_ref[0], xs_ref[0]], axis=0)
    gu = _dot(xs, wgu_ref[0].astype(BF16))
    gate = gu[:, :D_EXPERT]
    up = gu[:, D_EXPERT:]
    act = (gate * jax.nn.sigmoid(gate) * up).astype(BF16)
    y = _dot(act, wd_ref[0].astype(BF16))
    g = jnp.concatenate([_slot_gates(ap_ref, mp_ref, pp_ref, e, rows_p * N_EXPERTS // ap_ref.shape[0]),
                         _slot_gates(as_ref, ms_ref, ps_ref, e, xs_ref.shape[1] * N_EXPERTS // as_ref.shape[0])],
                        axis=0)
    y = (y * g).astype(BF16)
    yp_ref[0] = y[:rows_p]
    ys_ref[0] = y[rows_p:]


def _ffn(xs_p, xs_s, route_p, route_s, w_gate_up, w_down):
    per_e = lambda a: pl.BlockSpec((1,) + a.shape[1:], lambda e: (e, 0, 0))
    tables = list(route_p) + list(route_s)
    return pl.pallas_call(
        _ffn_kernel,
        grid=(N_EXPERTS,),
        in_specs=[per_e(xs_p), per_e(xs_s)] + [_full(t.shape) for t in tables] + [per_e(w_gate_up), per_e(w_down)],
        out_specs=[per_e(xs_p), per_e(xs_s)],
        out_shape=[jax.ShapeDtypeStruct(xs_p.shape, BF16), jax.ShapeDtypeStruct(xs_s.shape, BF16)],
        compiler_params=_params(),
        name="expert_ffn",
    )(xs_p, xs_s, *tables, w_gate_up, w_down)


def _combine_kernel(y_ref, mask_ref, pos_ref, x1_ref, mod_ref, g2_ref, b2_ref, o_ref, *, cap, seqs, mod_row0,
                    mod_row_step):
    row = mod_row0 + mod_row_step * pl.program_id(0)
    gate2 = mod_ref[pl.ds(row, 1), 5 * D_MODEL:6 * D_MODEL]
    tn = x1_ref.shape[0] // seqs
    for j in range(seqs):
        e0 = j * N_EXPERTS
        sel = jnp.concatenate(
            _slot_onehot(mask_ref[e0:e0 + N_EXPERTS, :], pos_ref[e0:e0 + N_EXPERTS, :], cap), axis=0).astype(BF16)
        y = y_ref[:, j * cap:(j + 1) * cap, :].reshape(N_EXPERTS * cap, D_MODEL)
        ff = _dot_tn(sel, y)
        r = pl.ds(j * tn, tn)
        o_ref[r, :] = _ln_plain(ALPHA * x1_ref[r, :] + gate2 * ff) * g2_ref[...] + b2_ref[...]


def _combine(y, mask, pos, x1, mod, g2, b2, *, n, nb, cap, seqs, sample):
    tn = min(n, COMBINE_TILE)
    tiles = n // tn
    assert seqs == 1 or tiles == 1
    return pl.pallas_call(
        functools.partial(_combine_kernel, cap=cap, seqs=seqs, mod_row0=1 if sample else 0,
                          mod_row_step=1 if sample else 0),
        grid=(nb // seqs, tiles),
        in_specs=[pl.BlockSpec((N_EXPERTS, seqs * cap, D_MODEL), lambda b, t: (0, b, 0)),
                  pl.BlockSpec((seqs * N_EXPERTS, tn), lambda b, t: (b, t)),
                  pl.BlockSpec((seqs * N_EXPERTS, tn), lambda b, t: (b, t)),
                  pl.BlockSpec((seqs * tn, D_MODEL), lambda b, t: (b * tiles + t, 0)),
                  _full(mod.shape), _full(g2.shape), _full(b2.shape)],
        out_specs=pl.BlockSpec((seqs * tn, D_MODEL), lambda b, t: (b * tiles + t, 0)),
        out_shape=jax.ShapeDtypeStruct((nb * n, D_MODEL), F32),
        compiler_params=_params(2),
        name="combine_sample" if sample else "combine_prompt",
    )(y, mask, pos, x1, mod, g2, b2)


def _rope_tables(n):
    lane = np.arange(LANES)
    within = lane % HEAD_DIM
    freq = (within % ROPE_FREQS).astype(np.float32)
    inv = np.float32(ROPE_THETA) ** (-freq / np.float32(ROPE_FREQS))
    tok = np.arange(n)
    pos = np.where((within < HEAD_DIM // 2)[None, :], (tok // GRID_W)[:, None], (tok % GRID_W)[:, None])
    ang = pos.astype(np.float32) * inv[None, :]
    sign = np.where(within % (2 * ROPE_FREQS) < ROPE_FREQS, -1.0, 1.0).astype(np.float32)
    return jnp.asarray(np.cos(ang)), jnp.asarray(np.sin(ang) * sign[None, :])


def kernel(x_prompt, x_sample, cache_k, cache_v, state_h, c, c_ctx, w_mod, b_mod, w_in, q_norm_g, k_norm_g,
           conv_w, conv_b, w_rg_a, b_rg_a, w_rg_x, b_rg_x, rg_lambda, w_out, ln1_g, ln1_b, w_router,
           w_gate_up, w_down, ln2_g, ln2_b):
    assert w_mod.shape[0] == DEPTH == 1
    nb_p, n_p, _ = x_prompt.shape
    nb_s, n_s, _ = x_sample.shape
    past = cache_k.shape[2]
    cap_p = CAP_FACTOR * n_p // N_EXPERTS
    cap_s = CAP_FACTOR * n_s // N_EXPERTS

    row = lambda v: v.reshape(1, -1)
    head_avg = np.kron(np.eye(ATTN_WIDTH // HEAD_DIM, dtype=np.float32),
                       np.full((HEAD_DIM, HEAD_DIM), 1.0 / HEAD_DIM, np.float32))
    bd = jnp.asarray(head_avg, BF16)
    bg = 0.5 * jnp.stack([b_rg_a[0, 0], b_rg_x[0, 0], b_rg_a[0, 1], b_rg_x[0, 1]])
    perm = _segment_permutation()
    perms = (jnp.asarray(perm, BF16), jnp.asarray(perm.T, BF16))
    wrt = w_router[0].T
    g1, b1, g2, b2 = row(ln1_g[0]), row(ln1_b[0]), row(ln2_g[0]), row(ln2_b[0])

    win_bf, wout_bf, wg = _cast_weights(w_in[0], w_out[0], w_rg_a[0], w_rg_x[0])
    consts = (bd, row(q_norm_g[0]), row(k_norm_g[0]), conv_w[0], row(conv_b[0]), wg, bg, rg_lambda[0])
    mod = _mod_vectors(row(c_ctx), c, w_mod[0], row(b_mod[0]))

    xp = x_prompt.reshape(nb_p * n_p, D_MODEL)
    xs = x_sample.reshape(nb_s * n_s, D_MODEL)

    new_k, new_v, new_h, x1_p, h2_p, aff_p = _mixer_prompt(
        xp, mod, win_bf, wout_bf, consts, g1, b1, wrt, n=n_p, nb=nb_p)

    q_s, k_s, v_s, rg_s = _front_sample(xs, mod, win_bf, consts + perms, state_h[:, 0], _rope_tables(n_s),
                                        n=n_s, nb=nb_s)
    def cached(t):
        return jnp.transpose(t[:, 0], (0, 2, 3, 1)).reshape(nb_s, KV_WIDTH, past)

    x1_s, h2_s, aff_s = _attn_post_sample(
        q_s, k_s, v_s, cached(cache_k), cached(cache_v), rg_s, xs, mod, wout_bf, g1, b1, wrt,
        n=n_s, nb=nb_s, past=past)

    seqs_p = PROMPT_SEQS_PER_STEP if nb_p % PROMPT_SEQS_PER_STEP == 0 else 1
    xs_p, mask_p, pos_p = _route(aff_p, h2_p, n=n_p, nb=nb_p, cap=cap_p, seqs=seqs_p)
    xs_s, mask_s, pos_s = _route(aff_s, h2_s, n=n_s, nb=nb_s, cap=cap_s, seqs=1)

    y_p, y_s = _ffn(xs_p, xs_s, (aff_p, mask_p, pos_p), (aff_s, mask_s, pos_s), w_gate_up[0], w_down[0])

    out_p = _combine(y_p, mask_p, pos_p, x1_p, mod, g2, b2, n=n_p, nb=nb_p, cap=cap_p, seqs=seqs_p, sample=False)
    out_s = _combine(y_s, mask_s, pos_s, x1_s, mod, g2, b2, n=n_s, nb=nb_s, cap=cap_s, seqs=1, sample=True)

    def cache_layout(t):
        t = t.reshape(nb_p, DEPTH, N_KV_HEADS, HEAD_DIM, n_p)
        return jnp.transpose(t, (0, 1, 4, 2, 3))

    return (out_p.reshape(nb_p, n_p, D_MODEL), out_s.reshape(nb_s, n_s, D_MODEL),
            cache_layout(new_k), cache_layout(new_v), new_h.reshape(nb_p, DEPTH, 2, RG_WIDTH))
```

```python
import functools

import numpy as np
import jax
import jax.numpy as jnp
from jax import lax
from jax.experimental import pallas as pl
from jax.experimental.pallas import tpu as pltpu

F32 = jnp.float32
BF16 = jnp.bfloat16

D_MODEL = 1024
HEAD_DIM = 64
N_Q_HEADS = 8
N_KV_HEADS = 2
GROUP = N_Q_HEADS // N_KV_HEADS
ATTN_WIDTH = N_Q_HEADS * HEAD_DIM
KV_WIDTH = N_KV_HEADS * HEAD_DIM
RG_WIDTH = D_MODEL - ATTN_WIDTH
RG_BLOCK = 64
CONV_W = 4
C_LRU = 8.0
N_EXPERTS = 16
CAP_FACTOR = 2
D_EXPERT = 1024
GRID_W = 64
ROPE_THETA = 10000.0
ROPE_FREQS = HEAD_DIM // 4
NORM_EPS = 1e-6
DEPTH = 1
ALPHA = (2.0 * DEPTH) ** 0.25
IN_WIDTH = ATTN_WIDTH + 2 * KV_WIDTH + 2 * RG_WIDTH
Q_SCALE = HEAD_DIM ** -0.5 * float(np.log2(np.e))

LANES = 128
SUBLANES = 8
MXU_DIM = 256
VMEM_LIMIT_BYTES = 56 * 1024 * 1024

MOD_ROWS = SUBLANES
MOD_TILE = 256
ATTN_Q_TILE = 512
COMBINE_TILE = 512
MIXER_ORDER = (1, 0, 0, 1, 1, 0, 1, 1, 0, 1, 0, 1, 0, 1, 1, 0, 1, 1, 0, 0, 1, 0, 1, 0, 0, 0, 1, 0, 0, 1, 0)
PERM_BLOCK = MXU_DIM
SEG_LEN = PERM_BLOCK // SUBLANES
TAIL_ROW_STREAMS = 2
PROMPT_SEQS_PER_STEP = 4


def _params(n_axes=1):
    return pltpu.CompilerParams(dimension_semantics=("arbitrary",) * n_axes,
                                vmem_limit_bytes=VMEM_LIMIT_BYTES)


def _full(shape):
    zeros = (0,) * len(shape)
    return pl.BlockSpec(shape, lambda *_: zeros)


def _ln_plain(x):
    mu = jnp.mean(x, -1, keepdims=True)
    xc = x - mu
    var = jnp.mean(xc * xc, -1, keepdims=True)
    return xc * lax.rsqrt(var + NORM_EPS)


def _dot(a, b):
    return jnp.dot(a, b, preferred_element_type=F32)


def _dot_nt(a, b):
    return lax.dot_general(a, b, (((1,), (1,)), ((), ())), preferred_element_type=F32)


def _dot_tn(a, b):
    return lax.dot_general(a, b, (((0,), (0,)), ((), ())), preferred_element_type=F32)


def _cast_kernel(a_ref, b_ref, wa_ref, wx_ref, ao_ref, bo_ref, wg_ref):
    ao_ref[...] = a_ref[...].astype(BF16)
    bo_ref[...] = b_ref[...].astype(BF16)
    wg_ref[...] = jnp.zeros_like(wg_ref)
    per_tile = MXU_DIM // RG_BLOCK
    for c in range(RG_WIDTH // MXU_DIM):
        for g, (ref, d) in enumerate(((wa_ref, 0), (wx_ref, 0), (wa_ref, 1), (wx_ref, 1))):
            for i in range(per_tile):
                lo = RG_BLOCK * i
                wg_ref[c, lo:lo + RG_BLOCK, MXU_DIM * g + lo:MXU_DIM * g + lo + RG_BLOCK] = \
                    (0.5 * ref[d, per_tile * c + i]).astype(BF16)


def _cast_weights(w_in, w_out, w_rg_a, w_rg_x):
    wg_shape = (RG_WIDTH // MXU_DIM, MXU_DIM, 4 * MXU_DIM)
    ins = [w_in, w_out, w_rg_a, w_rg_x]
    outs = [jax.ShapeDtypeStruct(w_in.shape, BF16), jax.ShapeDtypeStruct(w_out.shape, BF16),
            jax.ShapeDtypeStruct(wg_shape, BF16)]
    return pl.pallas_call(
        _cast_kernel,
        grid=(1,),
        in_specs=[_full(a.shape) for a in ins],
        out_specs=[_full(o.shape) for o in outs],
        out_shape=outs,
        compiler_params=_params(),
        name="cast_weights",
    )(*ins)


def _mod_kernel(ctx_ref, c_ref, w_lo_ref, w_hi_ref, b_ref, o_ref):
    @pl.when(pl.program_id(0) == 0)
    def _():
        o_ref[...] = jnp.broadcast_to(b_ref[...], o_ref.shape)

    half = w_lo_ref.shape[0]
    n_c = c_ref.shape[0]
    for w_ref, cols in ((w_lo_ref, slice(0, half)), (w_hi_ref, slice(half, 2 * half))):
        w = w_ref[...].astype(BF16)
        for rows, ref in ((slice(0, 1), ctx_ref), (slice(1, 1 + n_c), c_ref)):
            cs = ref[:, cols]
            o_ref[rows, :] += _dot((cs * jax.nn.sigmoid(cs)).astype(BF16), w)


def _mod_vectors(c_ctx, c, w_mod, b_mod):
    width = w_mod.shape[1]
    assert 1 + c.shape[0] <= MOD_ROWS
    return pl.pallas_call(
        _mod_kernel,
        grid=(D_MODEL // MOD_TILE,),
        in_specs=[pl.BlockSpec((1, MOD_TILE), lambda j: (0, j)),
                  pl.BlockSpec((c.shape[0], MOD_TILE), lambda j: (0, j)),
                  pl.BlockSpec((MOD_TILE // 2, width), lambda j: (2 * j, 0)),
                  pl.BlockSpec((MOD_TILE // 2, width), lambda j: (2 * j + 1, 0)),
                  _full(b_mod.shape)],
        out_specs=_full((MOD_ROWS, width)),
        out_shape=jax.ShapeDtypeStruct((MOD_ROWS, width), F32),
        compiler_params=_params(),
        name="mod_vectors",
    )(c_ctx, c, w_mod, w_mod, b_mod)


def _head_mean_sq(x, bd):
    return _dot((x * x).astype(BF16), bd)


def _rope_lanes(x, cos, sin_signed, hi_half):
    partner = jnp.where(hi_half, pltpu.roll(x, ROPE_FREQS, axis=1), pltpu.roll(x, LANES - ROPE_FREQS, axis=1))
    return x * cos + partner * sin_signed


def _modulated(x, m):
    return (_ln_plain(x) * (1.0 + m[:, D_MODEL:2 * D_MODEL]) + m[:, 0:D_MODEL]).astype(BF16)


def _qkv(h, win_ref, bd_ref, qg_ref, kg_ref, rope):
    n = h.shape[0]
    bd = bd_ref[...]
    q = _dot(h, win_ref[:, 0:ATTN_WIDTH])
    yield
    q = q * lax.rsqrt(_head_mean_sq(q, bd) + NORM_EPS) * jnp.concatenate([qg_ref[...]] * N_Q_HEADS, axis=1)
    yield
    k = _dot(h, win_ref[:, ATTN_WIDTH:ATTN_WIDTH + KV_WIDTH])
    k = k * lax.rsqrt(_head_mean_sq(k, bd[:KV_WIDTH, :KV_WIDTH]) + NORM_EPS) \
        * jnp.concatenate([kg_ref[...]] * N_KV_HEADS, axis=1)
    v = _dot(h, win_ref[:, ATTN_WIDTH + KV_WIDTH:ATTN_WIDTH + 2 * KV_WIDTH])
    yield
    if rope is not None:
        cos, sin_signed = rope
        lane = lax.broadcasted_iota(jnp.int32, (n, LANES), 1)
        hi_half = (lane & ROPE_FREQS) != 0
        q = jnp.concatenate(
            [_rope_lanes(q[:, j * LANES:(j + 1) * LANES], cos, sin_signed, hi_half)
             for j in range(ATTN_WIDTH // LANES)], axis=1)
        k = _rope_lanes(k, cos, sin_signed, hi_half)
    return (q * Q_SCALE).astype(BF16), k, v


def _drive(*streams, order=()):
    results = [None] * len(streams)
    live = list(range(len(streams)))
    plan = [j for j in order]
    while live:
        idx = plan.pop(0) if plan else live[0]
        if idx not in live:
            continue
        if not plan:
            live.append(live.pop(0))
        try:
            next(streams[idx])
        except StopIteration as stop:
            results[idx] = stop.value
            live.remove(idx)
    return results


def _attend(q, k, v, lookahead):
    def scores(hq):
        kv = hq // GROUP
        return _dot_nt(q[:, hq * HEAD_DIM:(hq + 1) * HEAD_DIM], k[:, kv * HEAD_DIM:(kv + 1) * HEAD_DIM])

    outs = []
    s_next = scores(0) if lookahead else None
    for hq in range(N_Q_HEADS):
        if lookahead:
            s, s_next = s_next, (scores(hq + 1) if hq + 1 < N_Q_HEADS else None)
        else:
            s = scores(hq)
        kv = hq // GROUP
        e = jnp.exp2(s - jnp.max(s, axis=-1, keepdims=True))
        denom = jnp.sum(e, axis=-1, keepdims=True)
        pv = _dot(e.astype(BF16), v[:, kv * HEAD_DIM:(kv + 1) * HEAD_DIM])
        outs.append(pv / denom)
        yield
    return jnp.concatenate(outs, axis=1).astype(BF16)


def _rg_inputs(h, win_ref):
    rg_lo = ATTN_WIDTH + 2 * KV_WIDTH
    xr = _dot(h, win_ref[:, rg_lo:rg_lo + RG_WIDTH])
    yield
    gr = _dot(h, win_ref[:, rg_lo + RG_WIDTH:IN_WIDTH])
    yield
    return xr, gr


def _rglru_rows(xr, gr, cw_ref, cb_ref, wg_ref, bg_ref, lam_ref, h0, scan_refs):
    af_ref, bf_ref, ab_ref, bb_ref = scan_refs
    n = xr.shape[0]

    t_idx = lax.broadcasted_iota(jnp.int32, (n, 1), 0)
    cw = cw_ref[...]
    xc = jnp.where(t_idx >= 2, pltpu.roll(xr, 2, axis=0), 0.0) * cw[0:1, :]
    xc = xc + jnp.where(t_idx >= 1, pltpu.roll(xr, 1, axis=0), 0.0) * cw[1:2, :]
    xc = xc + xr * cw[2:3, :]
    xc = xc + jnp.where(t_idx < n - 1, pltpu.roll(xr, n - 1, axis=0), 0.0) * cw[3:4, :]
    xc = xc + cb_ref[...]
    yield

    xcb = xc.astype(BF16)
    halves = [_dot(xcb[:, c * MXU_DIM:(c + 1) * MXU_DIM], wg_ref[c])
              for c in range(RG_WIDTH // MXU_DIM)]

    def gate_pre(idx):
        return jnp.concatenate([hv[:, idx * MXU_DIM:(idx + 1) * MXU_DIM] for hv in halves], axis=1) \
            + bg_ref[idx:idx + 1, :]

    yield
    neg = -lam_ref[...]
    softplus = jnp.maximum(neg, 0.0) + jnp.log1p(jnp.exp(-jnp.abs(neg)))
    decay = (-0.5 * C_LRU) * softplus
    half_xc = 0.5 * xc

    def coeffs(d):
        r2 = jnp.tanh(gate_pre(2 * d)) + 1.0
        i2 = jnp.tanh(gate_pre(2 * d + 1)) + 1.0
        log_a = r2 * decay[d:d + 1, :]
        a = jnp.exp(log_a)
        bx = jnp.sqrt(jnp.tanh(-log_a) * (a * a + 1.0)) * (i2 * half_xc)
        return a, bx

    groups = n // SUBLANES
    rmod = lax.broadcasted_iota(jnp.int32, (1, SUBLANES, 1), 1)

    def tile_scan(d, forward):
        a, bx = coeffs(d)
        yield
        a = a.reshape(groups, SUBLANES, RG_WIDTH)
        bx = bx.reshape(groups, SUBLANES, RG_WIDTH)
        for s in (1, 2, 4):
            ok = (rmod >= s) if forward else (rmod < SUBLANES - s)
            shift = s if forward else SUBLANES - s
            a_sh = jnp.where(ok, pltpu.roll(a, shift, axis=1), 1.0)
            b_sh = jnp.where(ok, pltpu.roll(bx, shift, axis=1), 0.0)
            bx = a * b_sh + bx
            a = a * a_sh
            yield
        return a.reshape(n, RG_WIDTH), bx.reshape(n, RG_WIDTH)

    af_ref[...], bf_ref[...] = yield from tile_scan(0, True)
    ab_ref[...], bb_ref[...] = yield from tile_scan(1, False)

    def step(g, carry):
        hf, hb = carry
        rf = pl.multiple_of(g * SUBLANES, SUBLANES)
        rb = pl.multiple_of((groups - 1 - g) * SUBLANES, SUBLANES)
        new_f = af_ref[pl.ds(rf, SUBLANES), :] * hf + bf_ref[pl.ds(rf, SUBLANES), :]
        new_b = ab_ref[pl.ds(rb, SUBLANES), :] * hb + bb_ref[pl.ds(rb, SUBLANES), :]
        bf_ref[pl.ds(rf, SUBLANES), :] = new_f
        bb_ref[pl.ds(rb, SUBLANES), :] = new_b
        return (jnp.broadcast_to(new_f[SUBLANES - 1:SUBLANES, :], (SUBLANES, RG_WIDTH)),
                jnp.broadcast_to(new_b[0:1, :], (SUBLANES, RG_WIDTH)))

    hf, hb = lax.fori_loop(
        0, groups, step,
        (jnp.broadcast_to(h0[0:1, :], (SUBLANES, RG_WIDTH)), jnp.broadcast_to(h0[1:2, :], (SUBLANES, RG_WIDTH))),
        unroll=True)
    yield

    rg = ((bf_ref[...] + bb_ref[...]) * jax.nn.gelu(gr)).astype(BF16)
    return rg, hf[0:1, :], hb[0:1, :]


def _segment_permutation():
    p = np.zeros((PERM_BLOCK, PERM_BLOCK), np.float32)
    for t in range(SEG_LEN):
        for j in range(SUBLANES):
            p[t * SUBLANES + j, j * SEG_LEN + t] = 1.0
    return p


def _rglru_segments(xr_ref, gr_ref, perm_t_ref, cw_ref, cb_ref, wg_ref, bg_ref, lam_ref, h0, scan_refs, ready):
    af_ref, bf_ref, ab_ref, bb_ref = scan_refs
    n = xr_ref.shape[0]
    blocks = n // PERM_BLOCK
    sub = lax.broadcasted_iota(jnp.int32, (SUBLANES, 1), 0)
    zero_row = jnp.zeros((1, RG_WIDTH), F32)
    cw = cw_ref[...]
    neg = -lam_ref[...]
    softplus = jnp.maximum(neg, 0.0) + jnp.log1p(jnp.exp(-jnp.abs(neg)))
    decay = (-0.5 * C_LRU) * softplus

    def group(b, t):
        return xr_ref[b * PERM_BLOCK + t * SUBLANES:b * PERM_BLOCK + (t + 1) * SUBLANES, :]

    def before(b, t):
        wrap = group(b - 1, t)[SUBLANES - 1:SUBLANES, :] if b > 0 else zero_row
        return jnp.where(sub == 0, wrap, pltpu.roll(group(b, t), 1, axis=0))

    def after(b, t):
        wrap = group(b + 1, t)[0:1, :] if b + 1 < blocks else zero_row
        return jnp.where(sub == SUBLANES - 1, wrap, pltpu.roll(group(b, t), SUBLANES - 1, axis=0))

    ends = []
    for b in range(blocks):
        while not ready(min(b + 1, blocks - 1)):
            yield
        rows = slice(b * PERM_BLOCK, (b + 1) * PERM_BLOCK)
        ext = jnp.concatenate([before(b, SEG_LEN - 2), before(b, SEG_LEN - 1), xr_ref[rows, :], after(b, 0)], axis=0)
        xc = ext[0:PERM_BLOCK] * cw[0:1, :]
        for tap in range(1, CONV_W):
            xc = xc + ext[tap * SUBLANES:tap * SUBLANES + PERM_BLOCK] * cw[tap:tap + 1, :]
        xc = xc + cb_ref[...]
        yield

        xcb = xc.astype(BF16)
        halves = [_dot(xcb[:, c * MXU_DIM:(c + 1) * MXU_DIM], wg_ref[c])
                  for c in range(RG_WIDTH // MXU_DIM)]
        yield
        half_xc = 0.5 * xc
        for d, (a_ref, b_ref) in enumerate(((af_ref, bf_ref), (ab_ref, bb_ref))):
            pre = [jnp.concatenate([hv[:, i * MXU_DIM:(i + 1) * MXU_DIM] for hv in halves], axis=1)
                   + bg_ref[i:i + 1, :] for i in (2 * d, 2 * d + 1)]
            r2 = jnp.tanh(pre[0]) + 1.0
            i2 = jnp.tanh(pre[1]) + 1.0
            log_a = r2 * decay[d:d + 1, :]
            a = jnp.exp(log_a)
            a_ref[rows, :] = a
            b_ref[rows, :] = jnp.sqrt(jnp.tanh(-log_a) * (a * a + 1.0)) * (i2 * half_xc)
            yield

        for a_ref, b_ref, steps in ((af_ref, bf_ref, range(SEG_LEN)), (ab_ref, bb_ref, range(SEG_LEN - 1, -1, -1))):
            hend = jnp.zeros((SUBLANES, RG_WIDTH), F32)
            pend = jnp.ones((SUBLANES, RG_WIDTH), F32)
            for t in steps:
                r = slice(b * PERM_BLOCK + t * SUBLANES, b * PERM_BLOCK + (t + 1) * SUBLANES)
                a = a_ref[r, :]
                hend = a * hend + b_ref[r, :]
                pend = a * pend
            ends.append((hend, pend))
            yield

    def carries(order, state, which):
        into = {}
        for b, j in order:
            into[b, j] = state
            hend, pend = ends[2 * b + which]
            state = hend[j:j + 1, :] + pend[j:j + 1, :] * state
        return into, state

    segs = [(b, j) for b in range(blocks) for j in range(SUBLANES)]
    into_f, hf = carries(segs, h0[0:1, :], 0)
    into_b, hb = carries(segs[::-1], h0[1:2, :], 1)
    yield

    for b in range(blocks):
        for a_ref, b_ref, into, steps in ((af_ref, bf_ref, into_f, range(SEG_LEN)),
                                          (ab_ref, bb_ref, into_b, range(SEG_LEN - 1, -1, -1))):
            h = jnp.concatenate([into[b, j] for j in range(SUBLANES)], axis=0)
            for t in steps:
                r = slice(b * PERM_BLOCK + t * SUBLANES, b * PERM_BLOCK + (t + 1) * SUBLANES)
                h = a_ref[r, :] * h + b_ref[r, :]
                b_ref[r, :] = h
            yield

    out = []
    for b in range(blocks):
        rows = slice(b * PERM_BLOCK, (b + 1) * PERM_BLOCK)
        rg = ((bf_ref[rows, :] + bb_ref[rows, :]) * jax.nn.gelu(gr_ref[rows, :])).astype(BF16)
        out.append(_dot(perm_t_ref[...], rg).astype(BF16))
        yield
    return jnp.concatenate(out, axis=0), hf, hb


def _out_and_router(attn, rg, x, m, wout_ref, g1_ref, b1_ref, wrt_ref):
    gate1 = m[:, 2 * D_MODEL:3 * D_MODEL]
    shift2 = m[:, 3 * D_MODEL:4 * D_MODEL]
    scale2 = m[:, 4 * D_MODEL:5 * D_MODEL]
    mix = _dot(jnp.concatenate([attn, rg], axis=1), wout_ref[...])
    yield
    x1 = _ln_plain(ALPHA * x + gate1 * mix) * g1_ref[...] + b1_ref[...]
    yield
    h2 = (_ln_plain(x1) * (1.0 + scale2) + shift2).astype(BF16)
    yield
    logits = _dot_nt(wrt_ref[...].astype(BF16), h2)
    e = jnp.exp(logits - jnp.max(logits, axis=0, keepdims=True))
    return x1, h2, e / jnp.sum(e, axis=0, keepdims=True)


def _mixer_kernel(x_ref, mod_ref, win_ref, wout_ref, bd_ref, qg_ref, kg_ref, cw_ref, cb_ref, wg_ref, bg_ref,
                  lam_ref, g1_ref, b1_ref, wrt_ref,
                  nk_ref, nv_ref, nh_ref, x1_ref, h2_ref, afft_ref,
                  xprev_ref, xr_ref, gr_ref, attn_ref, *scan_refs):
    @pl.when(pl.program_id(0) == 0)
    def _():
        xprev_ref[...] = jnp.zeros_like(xprev_ref)
        xr_ref[...] = jnp.zeros_like(xr_ref)
        gr_ref[...] = jnp.zeros_like(gr_ref)
        attn_ref[...] = jnp.zeros_like(attn_ref)

    m = mod_ref[0:1, :]

    def second_half():
        h0 = jnp.zeros((2, RG_WIDTH), F32)
        rg, hf, hb = yield from _rglru_rows(xr_ref[...], gr_ref[...], cw_ref, cb_ref, wg_ref, bg_ref, lam_ref, h0,
                                            scan_refs)
        nh_ref[0] = jnp.concatenate([hf, hb], axis=0)
        x1_ref[...], h2_ref[...], afft_ref[...] = yield from _out_and_router(
            attn_ref[...], rg, xprev_ref[...], m, wout_ref, g1_ref, b1_ref, wrt_ref)

    def first_half():
        x = x_ref[...]
        h = _modulated(x, m)
        yield
        q, k, v = yield from _qkv(h, win_ref, bd_ref, qg_ref, kg_ref, None)
        nk_ref[0] = k.T
        nv_ref[0] = v.T
        yield
        xr, gr = yield from _rg_inputs(h, win_ref)
        attn = yield from _attend(q, k.astype(BF16), v.astype(BF16), False)
        return x, attn, xr, gr

    (x, attn, xr, gr), _ = _drive(first_half(), second_half(), order=MIXER_ORDER)
    xprev_ref[...] = x
    attn_ref[...] = attn
    xr_ref[...] = xr
    gr_ref[...] = gr


def _mixer_prompt(x2d, mod, win_bf, wout_bf, consts, g1, b1, wrt, *, n, nb):
    tokens = nb * n
    first = lambda w: pl.BlockSpec((n, w), lambda i: (jnp.minimum(i, nb - 1), 0))
    second = lambda w: pl.BlockSpec((n, w), lambda i: (jnp.maximum(i - 1, 0), 0))
    small = list(consts) + [g1, b1, wrt]
    return pl.pallas_call(
        _mixer_kernel,
        grid=(nb + 1,),
        in_specs=[first(D_MODEL), _full(mod.shape), _full(win_bf.shape), _full(wout_bf.shape)]
                 + [_full(a.shape) for a in small],
        out_specs=[pl.BlockSpec((1, KV_WIDTH, n), lambda i: (jnp.minimum(i, nb - 1), 0, 0)),
                   pl.BlockSpec((1, KV_WIDTH, n), lambda i: (jnp.minimum(i, nb - 1), 0, 0)),
                   pl.BlockSpec((1, 2, RG_WIDTH), lambda i: (jnp.maximum(i - 1, 0), 0, 0)),
                   second(D_MODEL), second(D_MODEL),
                   pl.BlockSpec((N_EXPERTS, n), lambda i: (jnp.maximum(i - 1, 0), 0))],
        out_shape=[jax.ShapeDtypeStruct((nb, KV_WIDTH, n), F32), jax.ShapeDtypeStruct((nb, KV_WIDTH, n), F32),
                   jax.ShapeDtypeStruct((nb, 2, RG_WIDTH), F32),
                   jax.ShapeDtypeStruct((tokens, D_MODEL), F32), jax.ShapeDtypeStruct((tokens, D_MODEL), BF16),
                   jax.ShapeDtypeStruct((nb * N_EXPERTS, n), F32)],
        scratch_shapes=[pltpu.VMEM((n, D_MODEL), F32), pltpu.VMEM((n, RG_WIDTH), F32),
                        pltpu.VMEM((n, RG_WIDTH), F32), pltpu.VMEM((n, ATTN_WIDTH), BF16)]
                       + [pltpu.VMEM((n, RG_WIDTH), F32)] * 4,
        compiler_params=_params(),
        name="mixer_prompt",
    )(x2d, mod, win_bf, wout_bf, *small)


def _front_kernel(x_ref, mod_ref, win_ref, bd_ref, qg_ref, kg_ref, cw_ref, cb_ref, wg_ref, bg_ref, lam_ref,
                  perm_ref, perm_t_ref, h0_ref, cos_ref, sin_ref,
                  q_ref, k_ref, v_ref, rg_ref, xr_ref, gr_ref, *scan_refs):
    m = mod_ref[pl.ds(1 + pl.program_id(0), 1), :]
    rows = PERM_BLOCK
    streams = x_ref.shape[0] // rows

    def row_block(j):
        r = pl.ds(j * rows, rows)
        h = _modulated(x_ref[r, :], m)
        yield
        q, k, v = yield from _qkv(h, win_ref, bd_ref, qg_ref, kg_ref, (cos_ref[r, :], sin_ref[r, :]))
        q_ref[r, :] = q
        k_ref[r, :] = k.astype(BF16)
        v_ref[r, :] = v.astype(BF16)
        yield
        hp = _dot(perm_ref[...], h).astype(BF16)
        xr_ref[r, :], gr_ref[r, :] = yield from _rg_inputs(hp, win_ref)
        traced[j] = True

    traced = [False] * streams
    recurrent = _rglru_segments(xr_ref, gr_ref, perm_t_ref, cw_ref, cb_ref, wg_ref, bg_ref, lam_ref, h0_ref[0],
                                scan_refs, lambda block: traced[block])
    chunks = 8
    order = [0] * chunks + [1] * chunks + [s for j in range(2, streams) for _ in range(chunks) for s in (j, streams)]
    results = _drive(*[row_block(j) for j in range(streams)], recurrent, order=order)
    rg_ref[...], _, _ = results[-1]


def _front_sample(x2d, mod, win_bf, consts, h0, rope, *, n, nb):
    tokens = nb * n
    seq = lambda w: pl.BlockSpec((n, w), lambda b: (b, 0))
    small = list(consts)
    return pl.pallas_call(
        _front_kernel,
        grid=(nb,),
        in_specs=[seq(D_MODEL), _full(mod.shape), _full(win_bf.shape)] + [_full(a.shape) for a in small]
                 + [pl.BlockSpec((1, 2, RG_WIDTH), lambda b: (b, 0, 0)), _full(rope[0].shape), _full(rope[1].shape)],
        out_specs=[seq(ATTN_WIDTH), seq(KV_WIDTH), seq(KV_WIDTH), seq(RG_WIDTH)],
        out_shape=[jax.ShapeDtypeStruct((tokens, ATTN_WIDTH), BF16), jax.ShapeDtypeStruct((tokens, KV_WIDTH), BF16),
                   jax.ShapeDtypeStruct((tokens, KV_WIDTH), BF16), jax.ShapeDtypeStruct((tokens, RG_WIDTH), BF16)],
        scratch_shapes=[pltpu.VMEM((n, RG_WIDTH), F32)] * 6,
        compiler_params=_params(),
        name="front_sample",
    )(x2d, mod, win_bf, *small, h0, *rope)


def _attn_post_kernel(q_ref, k_ref, v_ref, kc_ref, vc_ref, rg_ref, x_ref, mod_ref, wout_ref, g1_ref, b1_ref, wrt_ref,
                      x1_ref, h2_ref, afft_ref):
    m = mod_ref[pl.ds(1 + pl.program_id(0), 1), :]
    k = jnp.concatenate([k_ref[...], kc_ref[0].T.astype(BF16)], axis=0)
    v = jnp.concatenate([v_ref[...], vc_ref[0].T.astype(BF16)], axis=0)
    attn, = _drive(_attend(q_ref[...], k, v, True))
    rows = attn.shape[0] // TAIL_ROW_STREAMS

    def row_block(j):
        r = slice(j * rows, (j + 1) * rows)
        x1_ref[r, :], h2_ref[r, :], afft_ref[:, r] = yield from _out_and_router(
            attn[r, :], rg_ref[r, :], x_ref[r, :], m, wout_ref, g1_ref, b1_ref, wrt_ref)

    _drive(*[row_block(j) for j in range(TAIL_ROW_STREAMS)], order=(0,))


def _attn_post_sample(q, k, v, cache_k, cache_v, rg, x2d, mod, wout_bf, g1, b1, wrt, *, n, nb, past):
    tq = min(n, ATTN_Q_TILE)
    tiles = n // tq
    tokens = nb * n
    tile = lambda w: pl.BlockSpec((tq, w), lambda b, t: (b * tiles + t, 0))
    whole = pl.BlockSpec((n, KV_WIDTH), lambda b, t: (b, 0))
    cached = pl.BlockSpec((1, KV_WIDTH, past), lambda b, t: (b, 0, 0))
    return pl.pallas_call(
        _attn_post_kernel,
        grid=(nb, tiles),
        in_specs=[tile(ATTN_WIDTH), whole, whole, cached, cached, tile(RG_WIDTH), tile(D_MODEL),
                  _full(mod.shape), _full(wout_bf.shape), _full(g1.shape), _full(b1.shape), _full(wrt.shape)],
        out_specs=[tile(D_MODEL), tile(D_MODEL), pl.BlockSpec((N_EXPERTS, tq), lambda b, t: (b, t))],
        out_shape=[jax.ShapeDtypeStruct((tokens, D_MODEL), F32), jax.ShapeDtypeStruct((tokens, D_MODEL), BF16),
                   jax.ShapeDtypeStruct((nb * N_EXPERTS, n), F32)],
        compiler_params=_params(2),
        name="attn_post_sample",
    )(q, k, v, cache_k, cache_v, rg, x2d, mod, wout_bf, g1, b1, wrt)


def _slot_onehot(mask, pos, cap):
    n = mask.shape[1]
    slot = lax.broadcasted_iota(jnp.int32, (cap, n), 0).astype(F32)
    return [(slot == pos[e:e + 1, :]) & (mask[e:e + 1, :] > 0.5) for e in range(mask.shape[0])]


def _route_kernel(afft_ref, h2_ref, xs_ref, mask_ref, pos_ref, *, n, cap, seqs):
    b = pl.program_id(0)

    @pl.when(b == 0)
    def _():
        aff = afft_ref[...]
        thr = jnp.zeros((aff.shape[0], 1), jnp.int32)
        for bit in range(30, -1, -1):
            cand = thr | (1 << bit)
            cnt = jnp.sum((aff >= lax.bitcast_convert_type(cand, F32)).astype(F32), axis=1, keepdims=True)
            thr = jnp.where(cnt >= cap, cand, thr)
        above = aff >= lax.bitcast_convert_type(thr + 1, F32)
        tied = (aff >= lax.bitcast_convert_type(thr, F32)) & jnp.logical_not(above)
        need = cap - jnp.sum(above.astype(F32), axis=1, keepdims=True)
        before = (lax.broadcasted_iota(jnp.int32, (n, n), 0)
                  < lax.broadcasted_iota(jnp.int32, (n, n), 1)).astype(BF16)
        tie_rank = _dot(tied.astype(BF16), before)
        mask = (above | (tied & (tie_rank < need))).astype(F32)
        mask_ref[...] = mask
        pos_ref[...] = _dot(mask.astype(BF16), before)

    for j in range(seqs):
        r0 = pl.multiple_of((b * seqs + j) * N_EXPERTS, N_EXPERTS)
        onehots = _slot_onehot(mask_ref[pl.ds(r0, N_EXPERTS), :], pos_ref[pl.ds(r0, N_EXPERTS), :], cap)
        sel = jnp.concatenate(onehots, axis=0).astype(BF16)
        xs = _dot(sel, h2_ref[j * n:(j + 1) * n, :]).astype(BF16)
        xs_ref[:, j * cap:(j + 1) * cap, :] = xs.reshape(N_EXPERTS, cap, D_MODEL)


def _route(afft, h2, *, n, nb, cap, seqs):
    rows = nb * N_EXPERTS
    return pl.pallas_call(
        functools.partial(_route_kernel, n=n, cap=cap, seqs=seqs),
        grid=(nb // seqs,),
        in_specs=[_full(afft.shape), pl.BlockSpec((seqs * n, D_MODEL), lambda b: (b, 0))],
        out_specs=[pl.BlockSpec((N_EXPERTS, seqs * cap, D_MODEL), lambda b: (0, b, 0)),
                   _full((rows, n)), _full((rows, n))],
        out_shape=[jax.ShapeDtypeStruct((N_EXPERTS, nb * cap, D_MODEL), BF16),
                   jax.ShapeDtypeStruct((rows, n), F32), jax.ShapeDtypeStruct((rows, n), F32)],
        compiler_params=_params(),
        name="route_n%d" % n,
    )(afft, h2)


def _slot_gates(aff_ref, mask_ref, pos_ref, e, cap):
    cols = []
    for b in range(aff_ref.shape[0] // N_EXPERTS):
        r = b * N_EXPERTS + e
        onehot, = _slot_onehot(mask_ref[pl.ds(r, 1), :], pos_ref[pl.ds(r, 1), :], cap)
        cols.append(jnp.sum(jnp.where(onehot, aff_ref[pl.ds(r, 1), :], 0.0), axis=1, keepdims=True))
    return jnp.concatenate(cols, axis=0)


def _ffn_kernel(xp_ref, xs_ref, ap_ref, mp_ref, pp_ref, as_ref, ms_ref, ps_ref, wgu_ref, wd_ref, yp_ref, ys_ref):
    e = pl.program_id(0)
    rows_p = xp_ref.shape[1]
    xs = jnp.concatenate([xp_ref[0], xs_ref[0]], axis=0)
    gu = _dot(xs, wgu_ref[0].astype(BF16))
    gate = gu[:, :D_EXPERT]
    up = gu[:, D_EXPERT:]
    act = (gate * jax.nn.sigmoid(gate) * up).astype(BF16)
    y = _dot(act, wd_ref[0].astype(BF16))
    g = jnp.concatenate([_slot_gates(ap_ref, mp_ref, pp_ref, e, rows_p * N_EXPERTS // ap_ref.shape[0]),
                         _slot_gates(as_ref, ms_ref, ps_ref, e, xs_ref.shape[1] * N_EXPERTS // as_ref.shape[0])],
                        axis=0)
    y = (y * g).astype(BF16)
    yp_ref[0] = y[:rows_p]
    ys_ref[0] = y[rows_p:]


def _ffn(xs_p, xs_s, route_p, route_s, w_gate_up, w_down):
    per_e = lambda a: pl.BlockSpec((1,) + a.shape[1:], lambda e: (e, 0, 0))
    tables = list(route_p) + list(route_s)
    return pl.pallas_call(
        _ffn_kernel,
        grid=(N_EXPERTS,),
        in_specs=[per_e(xs_p), per_e(xs_s)] + [_full(t.shape) for t in tables] + [per_e(w_gate_up), per_e(w_down)],
        out_specs=[per_e(xs_p), per_e(xs_s)],
        out_shape=[jax.ShapeDtypeStruct(xs_p.shape, BF16), jax.ShapeDtypeStruct(xs_s.shape, BF16)],
        compiler_params=_params(),
        name="expert_ffn",
    )(xs_p, xs_s, *tables, w_gate_up, w_down)


def _combine_kernel(y_ref, mask_ref, pos_ref, x1_ref, mod_ref, g2_ref, b2_ref, o_ref, *, cap, seqs, mod_row0,
                    mod_row_step):
    row = mod_row0 + mod_row_step * pl.program_id(0)
    gate2 = mod_ref[pl.ds(row, 1), 5 * D_MODEL:6 * D_MODEL]
    tn = x1_ref.shape[0] // seqs
    for j in range(seqs):
        e0 = j * N_EXPERTS
        sel = jnp.concatenate(
            _slot_onehot(mask_ref[e0:e0 + N_EXPERTS, :], pos_ref[e0:e0 + N_EXPERTS, :], cap), axis=0).astype(BF16)
        y = y_ref[:, j * cap:(j + 1) * cap, :].reshape(N_EXPERTS * cap, D_MODEL)
        ff = _dot_tn(sel, y)
        r = pl.ds(j * tn, tn)
        o_ref[r, :] = _ln_plain(ALPHA * x1_ref[r, :] + gate2 * ff) * g2_ref[...] + b2_ref[...]


def _combine(y, mask, pos, x1, mod, g2, b2, *, n, nb, cap, seqs, sample):
    tn = min(n, COMBINE_TILE)
    tiles = n // tn
    assert seqs == 1 or tiles == 1
    return pl.pallas_call(
        functools.partial(_combine_kernel, cap=cap, seqs=seqs, mod_row0=1 if sample else 0,
                          mod_row_step=1 if sample else 0),
        grid=(nb // seqs, tiles),
        in_specs=[pl.BlockSpec((N_EXPERTS, seqs * cap, D_MODEL), lambda b, t: (0, b, 0)),
                  pl.BlockSpec((seqs * N_EXPERTS, tn), lambda b, t: (b, t)),
                  pl.BlockSpec((seqs * N_EXPERTS, tn), lambda b, t: (b, t)),
                  pl.BlockSpec((seqs * tn, D_MODEL), lambda b, t: (b * tiles + t, 0)),
                  _full(mod.shape), _full(g2.shape), _full(b2.shape)],
        out_specs=pl.BlockSpec((seqs * tn, D_MODEL), lambda b, t: (b * tiles + t, 0)),
        out_shape=jax.ShapeDtypeStruct((nb * n, D_MODEL), F32),
        compiler_params=_params(2),
        name="combine_sample" if sample else "combine_prompt",
    )(y, mask, pos, x1, mod, g2, b2)


def _rope_tables(n):
    lane = np.arange(LANES)
    within = lane % HEAD_DIM
    freq = (within % ROPE_FREQS).astype(np.float32)
    inv = np.float32(ROPE_THETA) ** (-freq / np.float32(ROPE_FREQS))
    tok = np.arange(n)
    pos = np.where((within < HEAD_DIM // 2)[None, :], (tok // GRID_W)[:, None], (tok % GRID_W)[:, None])
    ang = pos.astype(np.float32) * inv[None, :]
    sign = np.where(within % (2 * ROPE_FREQS) < ROPE_FREQS, -1.0, 1.0).astype(np.float32)
    return jnp.asarray(np.cos(ang)), jnp.asarray(np.sin(ang) * sign[None, :])


def kernel(x_prompt, x_sample, cache_k, cache_v, state_h, c, c_ctx, w_mod, b_mod, w_in, q_norm_g, k_norm_g,
           conv_w, conv_b, w_rg_a, b_rg_a, w_rg_x, b_rg_x, rg_lambda, w_out, ln1_g, ln1_b, w_router,
           w_gate_up, w_down, ln2_g, ln2_b):
    assert w_mod.shape[0] == DEPTH == 1
    nb_p, n_p, _ = x_prompt.shape
    nb_s, n_s, _ = x_sample.shape
    past = cache_k.shape[2]
    cap_p = CAP_FACTOR * n_p // N_EXPERTS
    cap_s = CAP_FACTOR * n_s // N_EXPERTS

    row = lambda v: v.reshape(1, -1)
    head_avg = np.kron(np.eye(ATTN_WIDTH // HEAD_DIM, dtype=np.float32),
                       np.full((HEAD_DIM, HEAD_DIM), 1.0 / HEAD_DIM, np.float32))
    bd = jnp.asarray(head_avg, BF16)
    bg = 0.5 * jnp.stack([b_rg_a[0, 0], b_rg_x[0, 0], b_rg_a[0, 1], b_rg_x[0, 1]])
    perm = _segment_permutation()
    perms = (jnp.asarray(perm, BF16), jnp.asarray(perm.T, BF16))
    wrt = w_router[0].T
    g1, b1, g2, b2 = row(ln1_g[0]), row(ln1_b[0]), row(ln2_g[0]), row(ln2_b[0])

    win_bf, wout_bf, wg = _cast_weights(w_in[0], w_out[0], w_rg_a[0], w_rg_x[0])
    consts = (bd, row(q_norm_g[0]), row(k_norm_g[0]), conv_w[0], row(conv_b[0]), wg, bg, rg_lambda[0])
    mod = _mod_vectors(row(c_ctx), c, w_mod[0], row(b_mod[0]))

    xp = x_prompt.reshape(nb_p * n_p, D_MODEL)
    xs = x_sample.reshape(nb_s * n_s, D_MODEL)

    new_k, new_v, new_h, x1_p, h2_p, aff_p = _mixer_prompt(
        xp, mod, win_bf, wout_bf, consts, g1, b1, wrt, n=n_p, nb=nb_p)

    q_s, k_s, v_s, rg_s = _front_sample(xs, mod, win_bf, consts + perms, state_h[:, 0], _rope_tables(n_s),
                                        n=n_s, nb=nb_s)
    def cached(t):
        return jnp.transpose(t[:, 0], (0, 2, 3, 1)).reshape(nb_s, KV_WIDTH, past)

    x1_s, h2_s, aff_s = _attn_post_sample(
        q_s, k_s, v_s, cached(cache_k), cached(cache_v), rg_s, xs, mod, wout_bf, g1, b1, wrt,
        n=n_s, nb=nb_s, past=past)

    seqs_p = PROMPT_SEQS_PER_STEP if nb_p % PROMPT_SEQS_PER_STEP == 0 else 1
    xs_p, mask_p, pos_p = _route(aff_p, h2_p, n=n_p, nb=nb_p, cap=cap_p, seqs=seqs_p)
    xs_s, mask_s, pos_s = _route(aff_s, h2_s, n=n_s, nb=nb_s, cap=cap_s, seqs=1)

    y_p, y_s = _ffn(xs_p, xs_s, (aff_p, mask_p, pos_p), (aff_s, mask_s, pos_s), w_gate_up[0], w_down[0])

    out_p = _combine(y_p, mask_p, pos_p, x1_p, mod, g2, b2, n=n_p, nb=nb_p, cap=cap_p, seqs=seqs_p, sample=False)
    out_s = _combine(y_s, mask_s, pos_s, x1_s, mod, g2, b2, n=n_s, nb=nb_s, cap=cap_s, seqs=1, sample=True)

    def cache_layout(t):
        t = t.reshape(nb_p, DEPTH, N_KV_HEADS, HEAD_DIM, n_p)
        return jnp.transpose(t, (0, 1, 4, 2, 3))

    return (out_p.reshape(nb_p, n_p, D_MODEL), out_s.reshape(nb_s, n_s, D_MODEL),
            cache_layout(new_k), cache_layout(new_v), new_h.reshape(nb_p, DEPTH, 2, RG_WIDTH))
```

```python
import functools

import numpy as np
import jax
import jax.numpy as jnp
from jax import lax
from jax.experimental import pallas as pl
from jax.experimental.pallas import tpu as pltpu

F32 = jnp.float32
BF16 = jnp.bfloat16

D_MODEL = 1024
HEAD_DIM = 64
N_Q_HEADS = 8
N_KV_HEADS = 2
GROUP = N_Q_HEADS // N_KV_HEADS
ATTN_WIDTH = N_Q_HEADS * HEAD_DIM
KV_WIDTH = N_KV_HEADS * HEAD_DIM
RG_WIDTH = D_MODEL - ATTN_WIDTH
RG_BLOCK = 64
CONV_W = 4
C_LRU = 8.0
N_EXPERTS = 16
CAP_FACTOR = 2
D_EXPERT = 1024
GRID_W = 64
ROPE_THETA = 10000.0
ROPE_FREQS = HEAD_DIM // 4
NORM_EPS = 1e-6
DEPTH = 1
ALPHA = (2.0 * DEPTH) ** 0.25
IN_WIDTH = ATTN_WIDTH + 2 * KV_WIDTH + 2 * RG_WIDTH
Q_SCALE = HEAD_DIM ** -0.5 * float(np.log2(np.e))

LANES = 128
SUBLANES = 8
MXU_DIM = 256
VMEM_LIMIT_BYTES = 56 * 1024 * 1024

MOD_ROWS = SUBLANES
MOD_TILE = 256
ATTN_Q_TILE = 512
COMBINE_TILE = 512
MIXER_ORDER = (1, 0, 0, 1, 1, 0, 1, 1, 0, 1, 0, 1, 0, 1, 1, 0, 1, 1, 0, 0, 1, 0, 1, 0, 0, 0, 1, 0, 0, 1, 0)
PERM_BLOCK = MXU_DIM
SEG_LEN = PERM_BLOCK // SUBLANES
TAIL_ROW_STREAMS = 2
PROMPT_SEQS_PER_STEP = 4


def _params(n_axes=1, independent=False):
    return pltpu.CompilerParams(dimension_semantics=("parallel" if independent else "arbitrary",) * n_axes,
                                vmem_limit_bytes=VMEM_LIMIT_BYTES)


def _full(shape):
    zeros = (0,) * len(shape)
    return pl.BlockSpec(shape, lambda *_: zeros)


def _ln_plain(x):
    mu = jnp.mean(x, -1, keepdims=True)
    xc = x - mu
    var = jnp.mean(xc * xc, -1, keepdims=True)
    return xc * lax.rsqrt(var + NORM_EPS)


def _dot(a, b):
    return jnp.dot(a, b, preferred_element_type=F32)


def _dot_nt(a, b):
    return lax.dot_general(a, b, (((1,), (1,)), ((), ())), preferred_element_type=F32)


def _dot_tn(a, b):
    return lax.dot_general(a, b, (((0,), (0,)), ((), ())), preferred_element_type=F32)


def _cast_kernel(a_ref, b_ref, wa_ref, wx_ref, ao_ref, bo_ref, wg_ref):
    ao_ref[...] = a_ref[...].astype(BF16)
    bo_ref[...] = b_ref[...].astype(BF16)
    wg_ref[...] = jnp.zeros_like(wg_ref)
    per_tile = MXU_DIM // RG_BLOCK
    for c in range(RG_WIDTH // MXU_DIM):
        for g, (ref, d) in enumerate(((wa_ref, 0), (wx_ref, 0), (wa_ref, 1), (wx_ref, 1))):
            for i in range(per_tile):
                lo = RG_BLOCK * i
                wg_ref[c, lo:lo + RG_BLOCK, MXU_DIM * g + lo:MXU_DIM * g + lo + RG_BLOCK] = \
                    (0.5 * ref[d, per_tile * c + i]).astype(BF16)


def _cast_weights(w_in, w_out, w_rg_a, w_rg_x):
    wg_shape = (RG_WIDTH // MXU_DIM, MXU_DIM, 4 * MXU_DIM)
    ins = [w_in, w_out, w_rg_a, w_rg_x]
    outs = [jax.ShapeDtypeStruct(w_in.shape, BF16), jax.ShapeDtypeStruct(w_out.shape, BF16),
            jax.ShapeDtypeStruct(wg_shape, BF16)]
    return pl.pallas_call(
        _cast_kernel,
        grid=(1,),
        in_specs=[_full(a.shape) for a in ins],
        out_specs=[_full(o.shape) for o in outs],
        out_shape=outs,
        compiler_params=_params(),
        name="cast_weights",
    )(*ins)


def _mod_kernel(ctx_ref, c_ref, w_ref, b_ref, o_ref):
    @pl.when(pl.program_id(0) == 0)
    def _():
        o_ref[...] = jnp.broadcast_to(b_ref[...], o_ref.shape)

    w = w_ref[...].astype(BF16)
    n_c = c_ref.shape[0]
    for rows, ref in ((slice(0, 1), ctx_ref), (slice(1, 1 + n_c), c_ref)):
        cs = ref[...]
        o_ref[rows, :] += _dot((cs * jax.nn.sigmoid(cs)).astype(BF16), w)


def _mod_vectors(c_ctx, c, w_mod, b_mod):
    width = w_mod.shape[1]
    assert 1 + c.shape[0] <= MOD_ROWS
    return pl.pallas_call(
        _mod_kernel,
        grid=(D_MODEL // MOD_TILE,),
        in_specs=[pl.BlockSpec((1, MOD_TILE), lambda j: (0, j)),
                  pl.BlockSpec((c.shape[0], MOD_TILE), lambda j: (0, j)),
                  pl.BlockSpec((MOD_TILE, width), lambda j: (j, 0)),
                  _full(b_mod.shape)],
        out_specs=_full((MOD_ROWS, width)),
        out_shape=jax.ShapeDtypeStruct((MOD_ROWS, width), F32),
        compiler_params=_params(),
        name="mod_vectors",
    )(c_ctx, c, w_mod, b_mod)


def _head_mean_sq(x, bd):
    return _dot((x * x).astype(BF16), bd)


def _rope_lanes(x, cos, sin_signed, hi_half):
    partner = jnp.where(hi_half, pltpu.roll(x, ROPE_FREQS, axis=1), pltpu.roll(x, LANES - ROPE_FREQS, axis=1))
    return x * cos + partner * sin_signed


def _modulated(x, m):
    return (_ln_plain(x) * (1.0 + m[:, D_MODEL:2 * D_MODEL]) + m[:, 0:D_MODEL]).astype(BF16)


def _qkv(h, win_ref, bd_ref, qg_ref, kg_ref, rope):
    n = h.shape[0]
    bd = bd_ref[...]
    q = _dot(h, win_ref[:, 0:ATTN_WIDTH])
    yield
    q = q * lax.rsqrt(_head_mean_sq(q, bd) + NORM_EPS) * jnp.concatenate([qg_ref[...]] * N_Q_HEADS, axis=1)
    yield
    k = _dot(h, win_ref[:, ATTN_WIDTH:ATTN_WIDTH + KV_WIDTH])
    k = k * lax.rsqrt(_head_mean_sq(k, bd[:KV_WIDTH, :KV_WIDTH]) + NORM_EPS) \
        * jnp.concatenate([kg_ref[...]] * N_KV_HEADS, axis=1)
    v = _dot(h, win_ref[:, ATTN_WIDTH + KV_WIDTH:ATTN_WIDTH + 2 * KV_WIDTH])
    yield
    if rope is not None:
        cos, sin_signed = rope
        lane = lax.broadcasted_iota(jnp.int32, (n, LANES), 1)
        hi_half = (lane & ROPE_FREQS) != 0
        q = jnp.concatenate(
            [_rope_lanes(q[:, j * LANES:(j + 1) * LANES], cos, sin_signed, hi_half)
             for j in range(ATTN_WIDTH // LANES)], axis=1)
        k = _rope_lanes(k, cos, sin_signed, hi_half)
    return (q * Q_SCALE).astype(BF16), k, v


def _drive(*streams, order=()):
    results = [None] * len(streams)
    live = list(range(len(streams)))
    plan = [j for j in order]
    while live:
        idx = plan.pop(0) if plan else live[0]
        if idx not in live:
            continue
        if not plan:
            live.append(live.pop(0))
        try:
            next(streams[idx])
        except StopIteration as stop:
            results[idx] = stop.value
            live.remove(idx)
    return results


def _attend(q, k, v, lookahead):
    def scores(hq):
        kv = hq // GROUP
        return _dot_nt(q[:, hq * HEAD_DIM:(hq + 1) * HEAD_DIM], k[:, kv * HEAD_DIM:(kv + 1) * HEAD_DIM])

    outs = []
    s_next = scores(0) if lookahead else None
    for hq in range(N_Q_HEADS):
        if lookahead:
            s, s_next = s_next, (scores(hq + 1) if hq + 1 < N_Q_HEADS else None)
        else:
            s = scores(hq)
        kv = hq // GROUP
        e = jnp.exp2(s - jnp.max(s, axis=-1, keepdims=True))
        denom = jnp.sum(e, axis=-1, keepdims=True)
        pv = _dot(e.astype(BF16), v[:, kv * HEAD_DIM:(kv + 1) * HEAD_DIM])
        outs.append(pv / denom)
        yield
    return jnp.concatenate(outs, axis=1).astype(BF16)


def _rg_inputs(h, win_ref):
    rg_lo = ATTN_WIDTH + 2 * KV_WIDTH
    xr = _dot(h, win_ref[:, rg_lo:rg_lo + RG_WIDTH])
    yield
    gr = _dot(h, win_ref[:, rg_lo + RG_WIDTH:IN_WIDTH])
    yield
    return xr, gr


def _rglru_rows(xr, gr, cw_ref, cb_ref, wg_ref, bg_ref, lam_ref, h0, scan_refs):
    af_ref, bf_ref, ab_ref, bb_ref = scan_refs
    n = xr.shape[0]

    t_idx = lax.broadcasted_iota(jnp.int32, (n, 1), 0)
    cw = cw_ref[...]
    xc = jnp.where(t_idx >= 2, pltpu.roll(xr, 2, axis=0), 0.0) * cw[0:1, :]
    xc = xc + jnp.where(t_idx >= 1, pltpu.roll(xr, 1, axis=0), 0.0) * cw[1:2, :]
    xc = xc + xr * cw[2:3, :]
    xc = xc + jnp.where(t_idx < n - 1, pltpu.roll(xr, n - 1, axis=0), 0.0) * cw[3:4, :]
    xc = xc + cb_ref[...]
    yield

    xcb = xc.astype(BF16)
    halves = [_dot(xcb[:, c * MXU_DIM:(c + 1) * MXU_DIM], wg_ref[c])
              for c in range(RG_WIDTH // MXU_DIM)]

    def gate_pre(idx):
        return jnp.concatenate([hv[:, idx * MXU_DIM:(idx + 1) * MXU_DIM] for hv in halves], axis=1) \
            + bg_ref[idx:idx + 1, :]

    yield
    neg = -lam_ref[...]
    softplus = jnp.maximum(neg, 0.0) + jnp.log1p(jnp.exp(-jnp.abs(neg)))
    decay = (-0.5 * C_LRU) * softplus
    half_xc = 0.5 * xc

    def coeffs(d):
        r2 = jnp.tanh(gate_pre(2 * d)) + 1.0
        i2 = jnp.tanh(gate_pre(2 * d + 1)) + 1.0
        log_a = r2 * decay[d:d + 1, :]
        a = jnp.exp(log_a)
        bx = jnp.sqrt(jnp.tanh(-log_a) * (a * a + 1.0)) * (i2 * half_xc)
        return a, bx

    groups = n // SUBLANES
    rmod = lax.broadcasted_iota(jnp.int32, (1, SUBLANES, 1), 1)

    def tile_scan(d, forward):
        a, bx = coeffs(d)
        yield
        a = a.reshape(groups, SUBLANES, RG_WIDTH)
        bx = bx.reshape(groups, SUBLANES, RG_WIDTH)
        for s in (1, 2, 4):
            ok = (rmod >= s) if forward else (rmod < SUBLANES - s)
            shift = s if forward else SUBLANES - s
            a_sh = jnp.where(ok, pltpu.roll(a, shift, axis=1), 1.0)
            b_sh = jnp.where(ok, pltpu.roll(bx, shift, axis=1), 0.0)
            bx = a * b_sh + bx
            a = a * a_sh
            yield
        return a.reshape(n, RG_WIDTH), bx.reshape(n, RG_WIDTH)

    af_ref[...], bf_ref[...] = yield from tile_scan(0, True)
    ab_ref[...], bb_ref[...] = yield from tile_scan(1, False)

    def step(g, carry):
        hf, hb = carry
        rf = pl.multiple_of(g * SUBLANES, SUBLANES)
        rb = pl.multiple_of((groups - 1 - g) * SUBLANES, SUBLANES)
        new_f = af_ref[pl.ds(rf, SUBLANES), :] * hf + bf_ref[pl.ds(rf, SUBLANES), :]
        new_b = ab_ref[pl.ds(rb, SUBLANES), :] * hb + bb_ref[pl.ds(rb, SUBLANES), :]
        bf_ref[pl.ds(rf, SUBLANES), :] = new_f
        bb_ref[pl.ds(rb, SUBLANES), :] = new_b
        return (jnp.broadcast_to(new_f[SUBLANES - 1:SUBLANES, :], (SUBLANES, RG_WIDTH)),
                jnp.broadcast_to(new_b[0:1, :], (SUBLANES, RG_WIDTH)))

    hf, hb = lax.fori_loop(
        0, groups, step,
        (jnp.broadcast_to(h0[0:1, :], (SUBLANES, RG_WIDTH)), jnp.broadcast_to(h0[1:2, :], (SUBLANES, RG_WIDTH))),
        unroll=True)
    yield

    rg = ((bf_ref[...] + bb_ref[...]) * jax.nn.gelu(gr)).astype(BF16)
    return rg, hf[0:1, :], hb[0:1, :]


def _segment_permutation():
    p = np.zeros((PERM_BLOCK, PERM_BLOCK), np.float32)
    for t in range(SEG_LEN):
        for j in range(SUBLANES):
            p[t * SUBLANES + j, j * SEG_LEN + t] = 1.0
    return p


def _rglru_segments(xr_ref, gr_ref, perm_t_ref, cw_ref, cb_ref, wg_ref, bg_ref, lam_ref, h0, scan_refs, ready):
    af_ref, bf_ref, ab_ref, bb_ref = scan_refs
    n = xr_ref.shape[0]
    blocks = n // PERM_BLOCK
    sub = lax.broadcasted_iota(jnp.int32, (SUBLANES, 1), 0)
    zero_row = jnp.zeros((1, RG_WIDTH), F32)
    cw = cw_ref[...]
    neg = -lam_ref[...]
    softplus = jnp.maximum(neg, 0.0) + jnp.log1p(jnp.exp(-jnp.abs(neg)))
    decay = (-0.5 * C_LRU) * softplus

    def group(b, t):
        return xr_ref[b * PERM_BLOCK + t * SUBLANES:b * PERM_BLOCK + (t + 1) * SUBLANES, :]

    def before(b, t):
        wrap = group(b - 1, t)[SUBLANES - 1:SUBLANES, :] if b > 0 else zero_row
        return jnp.where(sub == 0, wrap, pltpu.roll(group(b, t), 1, axis=0))

    def after(b, t):
        wrap = group(b + 1, t)[0:1, :] if b + 1 < blocks else zero_row
        return jnp.where(sub == SUBLANES - 1, wrap, pltpu.roll(group(b, t), SUBLANES - 1, axis=0))

    ends = []
    for b in range(blocks):
        while not ready(min(b + 1, blocks - 1)):
            yield
        rows = slice(b * PERM_BLOCK, (b + 1) * PERM_BLOCK)
        ext = jnp.concatenate([before(b, SEG_LEN - 2), before(b, SEG_LEN - 1), xr_ref[rows, :], after(b, 0)], axis=0)
        xc = ext[0:PERM_BLOCK] * cw[0:1, :]
        for tap in range(1, CONV_W):
            xc = xc + ext[tap * SUBLANES:tap * SUBLANES + PERM_BLOCK] * cw[tap:tap + 1, :]
        xc = xc + cb_ref[...]
        yield

        xcb = xc.astype(BF16)
        halves = [_dot(xcb[:, c * MXU_DIM:(c + 1) * MXU_DIM], wg_ref[c])
                  for c in range(RG_WIDTH // MXU_DIM)]
        yield
        half_xc = 0.5 * xc
        for d, (a_ref, b_ref) in enumerate(((af_ref, bf_ref), (ab_ref, bb_ref))):
            pre = [jnp.concatenate([hv[:, i * MXU_DIM:(i + 1) * MXU_DIM] for hv in halves], axis=1)
                   + bg_ref[i:i + 1, :] for i in (2 * d, 2 * d + 1)]
            r2 = jnp.tanh(pre[0]) + 1.0
            i2 = jnp.tanh(pre[1]) + 1.0
            log_a = r2 * decay[d:d + 1, :]
            a = jnp.exp(log_a)
            a_ref[rows, :] = a
            b_ref[rows, :] = jnp.sqrt(jnp.tanh(-log_a) * (a * a + 1.0)) * (i2 * half_xc)
            yield

        for a_ref, b_ref, steps in ((af_ref, bf_ref, range(SEG_LEN)), (ab_ref, bb_ref, range(SEG_LEN - 1, -1, -1))):
            hend = jnp.zeros((SUBLANES, RG_WIDTH), F32)
            pend = jnp.ones((SUBLANES, RG_WIDTH), F32)
            for t in steps:
                r = slice(b * PERM_BLOCK + t * SUBLANES, b * PERM_BLOCK + (t + 1) * SUBLANES)
                a = a_ref[r, :]
                hend = a * hend + b_ref[r, :]
                pend = a * pend
            ends.append((hend, pend))
            yield

    def carries(order, state, which):
        into = {}
        for b, j in order:
            into[b, j] = state
            hend, pend = ends[2 * b + which]
            state = hend[j:j + 1, :] + pend[j:j + 1, :] * state
        return into, state

    segs = [(b, j) for b in range(blocks) for j in range(SUBLANES)]
    into_f, hf = carries(segs, h0[0:1, :], 0)
    into_b, hb = carries(segs[::-1], h0[1:2, :], 1)
    yield

    for b in range(blocks):
        for a_ref, b_ref, into, steps in ((af_ref, bf_ref, into_f, range(SEG_LEN)),
                                          (ab_ref, bb_ref, into_b, range(SEG_LEN - 1, -1, -1))):
            h = jnp.concatenate([into[b, j] for j in range(SUBLANES)], axis=0)
            for t in steps:
                r = slice(b * PERM_BLOCK + t * SUBLANES, b * PERM_BLOCK + (t + 1) * SUBLANES)
                h = a_ref[r, :] * h + b_ref[r, :]
                b_ref[r, :] = h
            yield

    out = []
    for b in range(blocks):
        rows = slice(b * PERM_BLOCK, (b + 1) * PERM_BLOCK)
        rg = ((bf_ref[rows, :] + bb_ref[rows, :]) * jax.nn.gelu(gr_ref[rows, :])).astype(BF16)
        out.append(_dot(perm_t_ref[...], rg).astype(BF16))
        yield
    return jnp.concatenate(out, axis=0), hf, hb


def _out_and_router(attn, rg, x, m, wout_ref, g1_ref, b1_ref, wrt_ref):
    gate1 = m[:, 2 * D_MODEL:3 * D_MODEL]
    shift2 = m[:, 3 * D_MODEL:4 * D_MODEL]
    scale2 = m[:, 4 * D_MODEL:5 * D_MODEL]
    mix = _dot(jnp.concatenate([attn, rg], axis=1), wout_ref[...])
    yield
    x1 = _ln_plain(ALPHA * x + gate1 * mix) * g1_ref[...] + b1_ref[...]
    yield
    h2 = (_ln_plain(x1) * (1.0 + scale2) + shift2).astype(BF16)
    yield
    logits = _dot_nt(wrt_ref[...].astype(BF16), h2)
    e = jnp.exp(logits - jnp.max(logits, axis=0, keepdims=True))
    return x1, h2, e / jnp.sum(e, axis=0, keepdims=True)


def _mixer_kernel(x_ref, mod_ref, win_ref, wout_ref, bd_ref, qg_ref, kg_ref, cw_ref, cb_ref, wg_ref, bg_ref,
                  lam_ref, g1_ref, b1_ref, wrt_ref,
                  nk_ref, nv_ref, nh_ref, x1_ref, h2_ref, afft_ref,
                  xprev_ref, xr_ref, gr_ref, attn_ref, *scan_refs):
    @pl.when(pl.program_id(0) == 0)
    def _():
        xprev_ref[...] = jnp.zeros_like(xprev_ref)
        xr_ref[...] = jnp.zeros_like(xr_ref)
        gr_ref[...] = jnp.zeros_like(gr_ref)
        attn_ref[...] = jnp.zeros_like(attn_ref)

    m = mod_ref[0:1, :]

    def second_half():
        h0 = jnp.zeros((2, RG_WIDTH), F32)
        rg, hf, hb = yield from _rglru_rows(xr_ref[...], gr_ref[...], cw_ref, cb_ref, wg_ref, bg_ref, lam_ref, h0,
                                            scan_refs)
        nh_ref[0] = jnp.concatenate([hf, hb], axis=0)
        x1_ref[...], h2_ref[...], afft_ref[...] = yield from _out_and_router(
            attn_ref[...], rg, xprev_ref[...], m, wout_ref, g1_ref, b1_ref, wrt_ref)

    def first_half():
        x = x_ref[...]
        h = _modulated(x, m)
        yield
        q, k, v = yield from _qkv(h, win_ref, bd_ref, qg_ref, kg_ref, None)
        nk_ref[0] = k.T
        nv_ref[0] = v.T
        yield
        xr, gr = yield from _rg_inputs(h, win_ref)
        attn = yield from _attend(q, k.astype(BF16), v.astype(BF16), False)
        return x, attn, xr, gr

    (x, attn, xr, gr), _ = _drive(first_half(), second_half(), order=MIXER_ORDER)
    xprev_ref[...] = x
    attn_ref[...] = attn
    xr_ref[...] = xr
    gr_ref[...] = gr


def _mixer_prompt(x2d, mod, win_bf, wout_bf, consts, g1, b1, wrt, *, n, nb):
    tokens = nb * n
    first = lambda w: pl.BlockSpec((n, w), lambda i: (jnp.minimum(i, nb - 1), 0))
    second = lambda w: pl.BlockSpec((n, w), lambda i: (jnp.maximum(i - 1, 0), 0))
    small = list(consts) + [g1, b1, wrt]
    return pl.pallas_call(
        _mixer_kernel,
        grid=(nb + 1,),
        in_specs=[first(D_MODEL), _full(mod.shape), _full(win_bf.shape), _full(wout_bf.shape)]
                 + [_full(a.shape) for a in small],
        out_specs=[pl.BlockSpec((1, KV_WIDTH, n), lambda i: (jnp.minimum(i, nb - 1), 0, 0)),
                   pl.BlockSpec((1, KV_WIDTH, n), lambda i: (jnp.minimum(i, nb - 1), 0, 0)),
                   pl.BlockSpec((1, 2, RG_WIDTH), lambda i: (jnp.maximum(i - 1, 0), 0, 0)),
                   second(D_MODEL), second(D_MODEL),
                   pl.BlockSpec((N_EXPERTS, n), lambda i: (jnp.maximum(i - 1, 0), 0))],
        out_shape=[jax.ShapeDtypeStruct((nb, KV_WIDTH, n), F32), jax.ShapeDtypeStruct((nb, KV_WIDTH, n), F32),
                   jax.ShapeDtypeStruct((nb, 2, RG_WIDTH), F32),
                   jax.ShapeDtypeStruct((tokens, D_MODEL), F32), jax.ShapeDtypeStruct((tokens, D_MODEL), BF16),
                   jax.ShapeDtypeStruct((nb * N_EXPERTS, n), F32)],
        scratch_shapes=[pltpu.VMEM((n, D_MODEL), F32), pltpu.VMEM((n, RG_WIDTH), F32),
                        pltpu.VMEM((n, RG_WIDTH), F32), pltpu.VMEM((n, ATTN_WIDTH), BF16)]
                       + [pltpu.VMEM((n, RG_WIDTH), F32)] * 4,
        compiler_params=_params(),
        name="mixer_prompt",
    )(x2d, mod, win_bf, wout_bf, *small)


def _front_kernel(x_ref, mod_ref, win_ref, bd_ref, qg_ref, kg_ref, cw_ref, cb_ref, wg_ref, bg_ref, lam_ref,
                  perm_ref, perm_t_ref, h0_ref, cos_ref, sin_ref,
                  q_ref, k_ref, v_ref, rg_ref, xr_ref, gr_ref, *scan_refs):
    m = mod_ref[pl.ds(1 + pl.program_id(0), 1), :]
    rows = PERM_BLOCK
    streams = x_ref.shape[0] // rows

    def row_block(j):
        r = pl.ds(j * rows, rows)
        h = _modulated(x_ref[r, :], m)
        yield
        q, k, v = yield from _qkv(h, win_ref, bd_ref, qg_ref, kg_ref, (cos_ref[r, :], sin_ref[r, :]))
        q_ref[r, :] = q
        k_ref[r, :] = k.astype(BF16)
        v_ref[r, :] = v.astype(BF16)
        yield
        hp = _dot(perm_ref[...], h).astype(BF16)
        xr_ref[r, :], gr_ref[r, :] = yield from _rg_inputs(hp, win_ref)
        traced[j] = True

    traced = [False] * streams
    recurrent = _rglru_segments(xr_ref, gr_ref, perm_t_ref, cw_ref, cb_ref, wg_ref, bg_ref, lam_ref, h0_ref[0],
                                scan_refs, lambda block: traced[block])
    chunks = 8
    order = [0] * chunks + [1] * chunks + [s for j in range(2, streams) for _ in range(chunks) for s in (j, streams)]
    results = _drive(*[row_block(j) for j in range(streams)], recurrent, order=order)
    rg_ref[...], _, _ = results[-1]


def _front_sample(x2d, mod, win_bf, consts, h0, rope, *, n, nb):
    tokens = nb * n
    seq = lambda w: pl.BlockSpec((n, w), lambda b: (b, 0))
    small = list(consts)
    return pl.pallas_call(
        _front_kernel,
        grid=(nb,),
        in_specs=[seq(D_MODEL), _full(mod.shape), _full(win_bf.shape)] + [_full(a.shape) for a in small]
                 + [pl.BlockSpec((1, 2, RG_WIDTH), lambda b: (b, 0, 0)), _full(rope[0].shape), _full(rope[1].shape)],
        out_specs=[seq(ATTN_WIDTH), seq(KV_WIDTH), seq(KV_WIDTH), seq(RG_WIDTH)],
        out_shape=[jax.ShapeDtypeStruct((tokens, ATTN_WIDTH), BF16), jax.ShapeDtypeStruct((tokens, KV_WIDTH), BF16),
                   jax.ShapeDtypeStruct((tokens, KV_WIDTH), BF16), jax.ShapeDtypeStruct((tokens, RG_WIDTH), BF16)],
        scratch_shapes=[pltpu.VMEM((n, RG_WIDTH), F32)] * 6,
        compiler_params=_params(),
        name="front_sample",
    )(x2d, mod, win_bf, *small, h0, *rope)


def _attn_post_kernel(q_ref, k_ref, v_ref, kc_ref, vc_ref, rg_ref, x_ref, mod_ref, wout_ref, g1_ref, b1_ref, wrt_ref,
                      x1_ref, h2_ref, afft_ref):
    m = mod_ref[pl.ds(1 + pl.program_id(0), 1), :]
    k = jnp.concatenate([k_ref[...], kc_ref[0].T.astype(BF16)], axis=0)
    v = jnp.concatenate([v_ref[...], vc_ref[0].T.astype(BF16)], axis=0)
    attn, = _drive(_attend(q_ref[...], k, v, True))
    rows = attn.shape[0] // TAIL_ROW_STREAMS

    def row_block(j):
        r = slice(j * rows, (j + 1) * rows)
        x1_ref[r, :], h2_ref[r, :], afft_ref[:, r] = yield from _out_and_router(
            attn[r, :], rg_ref[r, :], x_ref[r, :], m, wout_ref, g1_ref, b1_ref, wrt_ref)

    _drive(*[row_block(j) for j in range(TAIL_ROW_STREAMS)], order=(0,))


def _attn_post_sample(q, k, v, cache_k, cache_v, rg, x2d, mod, wout_bf, g1, b1, wrt, *, n, nb, past):
    tq = min(n, ATTN_Q_TILE)
    tiles = n // tq
    tokens = nb * n
    tile = lambda w: pl.BlockSpec((tq, w), lambda b, t: (b * tiles + t, 0))
    whole = pl.BlockSpec((n, KV_WIDTH), lambda b, t: (b, 0))
    cached = pl.BlockSpec((1, KV_WIDTH, past), lambda b, t: (b, 0, 0))
    return pl.pallas_call(
        _attn_post_kernel,
        grid=(nb, tiles),
        in_specs=[tile(ATTN_WIDTH), whole, whole, cached, cached, tile(RG_WIDTH), tile(D_MODEL),
                  _full(mod.shape), _full(wout_bf.shape), _full(g1.shape), _full(b1.shape), _full(wrt.shape)],
        out_specs=[tile(D_MODEL), tile(D_MODEL), pl.BlockSpec((N_EXPERTS, tq), lambda b, t: (b, t))],
        out_shape=[jax.ShapeDtypeStruct((tokens, D_MODEL), F32), jax.ShapeDtypeStruct((tokens, D_MODEL), BF16),
                   jax.ShapeDtypeStruct((nb * N_EXPERTS, n), F32)],
        compiler_params=_params(2),
        name="attn_post_sample",
    )(q, k, v, cache_k, cache_v, rg, x2d, mod, wout_bf, g1, b1, wrt)


def _slot_onehot(mask, pos, cap):
    n = mask.shape[1]
    slot = lax.broadcasted_iota(jnp.int32, (cap, n), 0).astype(F32)
    return [(slot == pos[e:e + 1, :]) & (mask[e:e + 1, :] > 0.5) for e in range(mask.shape[0])]


def _route_kernel(afft_ref, h2_ref, xs_ref, mask_ref, pos_ref, *, n, cap, seqs):
    b = pl.program_id(0)

    @pl.when(b == 0)
    def _():
        aff = afft_ref[...]
        thr = jnp.zeros((aff.shape[0], 1), jnp.int32)
        for bit in range(30, -1, -1):
            cand = thr | (1 << bit)
            cnt = jnp.sum((aff >= lax.bitcast_convert_type(cand, F32)).astype(F32), axis=1, keepdims=True)
            thr = jnp.where(cnt >= cap, cand, thr)
        above = aff >= lax.bitcast_convert_type(thr + 1, F32)
        tied = (aff >= lax.bitcast_convert_type(thr, F32)) & jnp.logical_not(above)
        need = cap - jnp.sum(above.astype(F32), axis=1, keepdims=True)
        before = (lax.broadcasted_iota(jnp.int32, (n, n), 0)
                  < lax.broadcasted_iota(jnp.int32, (n, n), 1)).astype(BF16)
        tie_rank = _dot(tied.astype(BF16), before)
        mask = (above | (tied & (tie_rank < need))).astype(F32)
        mask_ref[...] = mask
        pos_ref[...] = _dot(mask.astype(BF16), before)

    for j in range(seqs):
        r0 = pl.multiple_of((b * seqs + j) * N_EXPERTS, N_EXPERTS)
        onehots = _slot_onehot(mask_ref[pl.ds(r0, N_EXPERTS), :], pos_ref[pl.ds(r0, N_EXPERTS), :], cap)
        sel = jnp.concatenate(onehots, axis=0).astype(BF16)
        xs = _dot(sel, h2_ref[j * n:(j + 1) * n, :]).astype(BF16)
        xs_ref[:, j * cap:(j + 1) * cap, :] = xs.reshape(N_EXPERTS, cap, D_MODEL)


def _route(afft, h2, *, n, nb, cap, seqs):
    rows = nb * N_EXPERTS
    return pl.pallas_call(
        functools.partial(_route_kernel, n=n, cap=cap, seqs=seqs),
        grid=(nb // seqs,),
        in_specs=[_full(afft.shape), pl.BlockSpec((seqs * n, D_MODEL), lambda b: (b, 0))],
        out_specs=[pl.BlockSpec((N_EXPERTS, seqs * cap, D_MODEL), lambda b: (0, b, 0)),
                   _full((rows, n)), _full((rows, n))],
        out_shape=[jax.ShapeDtypeStruct((N_EXPERTS, nb * cap, D_MODEL), BF16),
                   jax.ShapeDtypeStruct((rows, n), F32), jax.ShapeDtypeStruct((rows, n), F32)],
        compiler_params=_params(),
        name="route_n%d" % n,
    )(afft, h2)


def _slot_gates(aff_ref, mask_ref, pos_ref, e, cap):
    cols = []
    for b in range(aff_ref.shape[0] // N_EXPERTS):
        r = b * N_EXPERTS + e
        onehot, = _slot_onehot(mask_ref[pl.ds(r, 1), :], pos_ref[pl.ds(r, 1), :], cap)
        cols.append(jnp.sum(jnp.where(onehot, aff_ref[pl.ds(r, 1), :], 0.0), axis=1, keepdims=True))
    return jnp.concatenate(cols, axis=0)


def _ffn_kernel(xp_ref, xs_ref, ap_ref, mp_ref, pp_ref, as_ref, ms_ref, ps_ref, wgu_ref, wd_ref, yp_ref, ys_ref):
    e = pl.program_id(0)
    rows_p = xp_ref.shape[1]
    xs = jnp.concatenate([xp_ref[0], xs_ref[0]], axis=0)
    gu = _dot(xs, wgu_ref[0].astype(BF16))
    gate = gu[:, :D_EXPERT]
    up = gu[:, D_EXPERT:]
    act = (gate * jax.nn.sigmoid(gate) * up).astype(BF16)
    y = _dot(act, wd_ref[0].astype(BF16))
    g = jnp.concatenate([_slot_gates(ap_ref, mp_ref, pp_ref, e, rows_p * N_EXPERTS // ap_ref.shape[0]),
                         _slot_gates(as_ref, ms_ref, ps_ref, e, xs_ref.shape[1] * N_EXPERTS // as_ref.shape[0])],
                        axis=0)
    y = (y * g).astype(BF16)
    yp_ref[0] = y[:rows_p]
    ys_ref[0] = y[rows_p:]


def _ffn(xs_p, xs_s, route_p, route_s, w_gate_up, w_down):
    per_e = lambda a: pl.BlockSpec((1,) + a.shape[1:], lambda e: (e, 0, 0))
    tables = list(route_p) + list(route_s)
    return pl.pallas_call(
        _ffn_kernel,
        grid=(N_EXPERTS,),
        in_specs=[per_e(xs_p), per_e(xs_s)] + [_full(t.shape) for t in tables] + [per_e(w_gate_up), per_e(w_down)],
        out_specs=[per_e(xs_p), per_e(xs_s)],
        out_shape=[jax.ShapeDtypeStruct(xs_p.shape, BF16), jax.ShapeDtypeStruct(xs_s.shape, BF16)],
        compiler_params=_params(independent=True),
        name="expert_ffn",
    )(xs_p, xs_s, *tables, w_gate_up, w_down)


def _combine_kernel(y_ref, mask_ref, pos_ref, x1_ref, mod_ref, g2_ref, b2_ref, o_ref, *, cap, seqs, mod_row0,
                    mod_row_step):
    row = mod_row0 + mod_row_step * pl.program_id(0)
    gate2 = mod_ref[pl.ds(row, 1), 5 * D_MODEL:6 * D_MODEL]
    tn = x1_ref.shape[0] // seqs
    for j in range(seqs):
        e0 = j * N_EXPERTS
        sel = jnp.concatenate(
            _slot_onehot(mask_ref[e0:e0 + N_EXPERTS, :], pos_ref[e0:e0 + N_EXPERTS, :], cap), axis=0).astype(BF16)
        y = y_ref[:, j * cap:(j + 1) * cap, :].reshape(N_EXPERTS * cap, D_MODEL)
        ff = _dot_tn(sel, y)
        r = pl.ds(j * tn, tn)
        o_ref[r, :] = _ln_plain(ALPHA * x1_ref[r, :] + gate2 * ff) * g2_ref[...] + b2_ref[...]


def _combine(y, mask, pos, x1, mod, g2, b2, *, n, nb, cap, seqs, sample):
    tn = min(n, COMBINE_TILE)
    tiles = n // tn
    assert seqs == 1 or tiles == 1
    return pl.pallas_call(
        functools.partial(_combine_kernel, cap=cap, seqs=seqs, mod_row0=1 if sample else 0,
                          mod_row_step=1 if sample else 0),
        grid=(nb // seqs, tiles),
        in_specs=[pl.BlockSpec((N_EXPERTS, seqs * cap, D_MODEL), lambda b, t: (0, b, 0)),
                  pl.BlockSpec((seqs * N_EXPERTS, tn), lambda b, t: (b, t)),
                  pl.BlockSpec((seqs * N_EXPERTS, tn), lambda b, t: (b, t)),
                  pl.BlockSpec((seqs * tn, D_MODEL), lambda b, t: (b * tiles + t, 0)),
                  _full(mod.shape), _full(g2.shape), _full(b2.shape)],
        out_specs=pl.BlockSpec((seqs * tn, D_MODEL), lambda b, t: (b * tiles + t, 0)),
        out_shape=jax.ShapeDtypeStruct((nb * n, D_MODEL), F32),
        compiler_params=_params(2, independent=True),
        name="combine_sample" if sample else "combine_prompt",
    )(y, mask, pos, x1, mod, g2, b2)


def _rope_tables(n):
    lane = np.arange(LANES)
    within = lane % HEAD_DIM
    freq = (within % ROPE_FREQS).astype(np.float32)
    inv = np.float32(ROPE_THETA) ** (-freq / np.float32(ROPE_FREQS))
    tok = np.arange(n)
    pos = np.where((within < HEAD_DIM // 2)[None, :], (tok // GRID_W)[:, None], (tok % GRID_W)[:, None])
    ang = pos.astype(np.float32) * inv[None, :]
    sign = np.where(within % (2 * ROPE_FREQS) < ROPE_FREQS, -1.0, 1.0).astype(np.float32)
    return jnp.asarray(np.cos(ang)), jnp.asarray(np.sin(ang) * sign[None, :])


def kernel(x_prompt, x_sample, cache_k, cache_v, state_h, c, c_ctx, w_mod, b_mod, w_in, q_norm_g, k_norm_g,
           conv_w, conv_b, w_rg_a, b_rg_a, w_rg_x, b_rg_x, rg_lambda, w_out, ln1_g, ln1_b, w_router,
           w_gate_up, w_down, ln2_g, ln2_b):
    assert w_mod.shape[0] == DEPTH == 1
    nb_p, n_p, _ = x_prompt.shape
    nb_s, n_s, _ = x_sample.shape
    past = cache_k.shape[2]
    cap_p = CAP_FACTOR * n_p // N_EXPERTS
    cap_s = CAP_FACTOR * n_s // N_EXPERTS

    row = lambda v: v.reshape(1, -1)
    head_avg = np.kron(np.eye(ATTN_WIDTH // HEAD_DIM, dtype=np.float32),
                       np.full((HEAD_DIM, HEAD_DIM), 1.0 / HEAD_DIM, np.float32))
    bd = jnp.asarray(head_avg, BF16)
    bg = 0.5 * jnp.stack([b_rg_a[0, 0], b_rg_x[0, 0], b_rg_a[0, 1], b_rg_x[0, 1]])
    perm = _segment_permutation()
    perms = (jnp.asarray(perm, BF16), jnp.asarray(perm.T, BF16))
    wrt = w_router[0].T
    g1, b1, g2, b2 = row(ln1_g[0]), row(ln1_b[0]), row(ln2_g[0]), row(ln2_b[0])

    win_bf, wout_bf, wg = _cast_weights(w_in[0], w_out[0], w_rg_a[0], w_rg_x[0])
    consts = (bd, row(q_norm_g[0]), row(k_norm_g[0]), conv_w[0], row(conv_b[0]), wg, bg, rg_lambda[0])
    mod = _mod_vectors(row(c_ctx), c, w_mod[0], row(b_mod[0]))

    xp = x_prompt.reshape(nb_p * n_p, D_MODEL)
    xs = x_sample.reshape(nb_s * n_s, D_MODEL)

    new_k, new_v, new_h, x1_p, h2_p, aff_p = _mixer_prompt(
        xp, mod, win_bf, wout_bf, consts, g1, b1, wrt, n=n_p, nb=nb_p)

    q_s, k_s, v_s, rg_s = _front_sample(xs, mod, win_bf, consts + perms, state_h[:, 0], _rope_tables(n_s),
                                        n=n_s, nb=nb_s)
    def cached(t):
        return jnp.transpose(t[:, 0], (0, 2, 3, 1)).reshape(nb_s, KV_WIDTH, past)

    x1_s, h2_s, aff_s = _attn_post_sample(
        q_s, k_s, v_s, cached(cache_k), cached(cache_v), rg_s, xs, mod, wout_bf, g1, b1, wrt,
        n=n_s, nb=nb_s, past=past)

    seqs_p = PROMPT_SEQS_PER_STEP if nb_p % PROMPT_SEQS_PER_STEP == 0 else 1
    xs_p, mask_p, pos_p = _route(aff_p, h2_p, n=n_p, nb=nb_p, cap=cap_p, seqs=seqs_p)
    xs_s, mask_s, pos_s = _route(aff_s, h2_s, n=n_s, nb=nb_s, cap=cap_s, seqs=1)

    y_p, y_s = _ffn(xs_p, xs_s, (aff_p, mask_p, pos_p), (aff_s, mask_s, pos_s), w_gate_up[0], w_down[0])

    out_p = _combine(y_p, mask_p, pos_p, x1_p, mod, g2, b2, n=n_p, nb=nb_p, cap=cap_p, seqs=seqs_p, sample=False)
    out_s = _combine(y_s, mask_s, pos_s, x1_s, mod, g2, b2, n=n_s, nb=nb_s, cap=cap_s, seqs=1, sample=True)

    def cache_layout(t):
        t = t.reshape(nb_p, DEPTH, N_KV_HEADS, HEAD_DIM, n_p)
        return jnp.transpose(t, (0, 1, 4, 2, 3))

    return (out_p.reshape(nb_p, n_p, D_MODEL), out_s.reshape(nb_s, n_s, D_MODEL),
            cache_layout(new_k), cache_layout(new_v), new_h.reshape(nb_p, DEPTH, 2, RG_WIDTH))
```
